```python
import math
import jax, jax.numpy as jnp
from jax import lax
import numpy as np

D_MODEL = 2048
BATCH = 1
SEQ = 8192
DEPTH = 4

N_META = 16
LN_EPS = 1e-5
RMS_EPS = 1e-6
DEEPNORM_ALPHA = (2.0 * DEPTH) ** 0.25
DEEPNORM_BETA = (8.0 * DEPTH) ** -0.25
N_EVEN = (DEPTH + 1) // 2
N_ODD = DEPTH // 2

SSD_D_INNER = D_MODEL
SSD_HEAD_DIM = 64
SSD_HEADS = SSD_D_INNER // SSD_HEAD_DIM
SSD_GROUPS = 8
SSD_HPG = SSD_HEADS // SSD_GROUPS
SSD_STATE = 128
SSD_CONV = 4
SSD_CHUNK = 128
SSD_CONV_DIM = SSD_D_INNER + 2 * SSD_GROUPS * SSD_STATE
SSD_DT_MIN = 1e-3
SSD_DT_MAX = 1e-1

GLA_HEADS = 4
GLA_DK = D_MODEL // 2
GLA_DV = D_MODEL
GLA_HEAD_DK = GLA_DK // GLA_HEADS
GLA_HEAD_DV = GLA_DV // GLA_HEADS
GLA_GATE_RANK = 16
GLA_GATE_TAU = 16.0
GLA_CHUNK = 64

HYB_SIZES = (SSD_D_INNER, SSD_CONV_DIM, SSD_HEADS, GLA_DK, GLA_DK, GLA_DV, GLA_GATE_RANK, GLA_DV)
HYB_IN_COLS = sum(HYB_SIZES)
HYB_MIX_WIDTH = SSD_D_INNER + GLA_DV

SWA_HEAD_DIM = 64
SWA_Q_HEADS = D_MODEL // SWA_HEAD_DIM
SWA_KV_HEADS = 8
SWA_GROUP = SWA_Q_HEADS // SWA_KV_HEADS
SWA_WINDOW = 128
SWA_Q_DIM = SWA_Q_HEADS * SWA_HEAD_DIM
SWA_KV_DIM = SWA_KV_HEADS * SWA_HEAD_DIM
SWA_QKV_COLS = SWA_Q_DIM + 2 * SWA_KV_DIM
ROPE_THETA = 500000.0
ROPE_DIM = SWA_HEAD_DIM // 4

D_FF = 5504
FFN_CONV = 3

kernel_name = 'hybrid_ssd_gla_swa_convffn_deepnorm'


def _split_points(sizes):
    pts, acc = [], 0
    for s in sizes[:-1]:
        acc += s
        pts.append(acc)
    return pts


def _pad_front(t, n):
    return jnp.pad(t, ((0, 0), (n, 0)) + ((0, 0),) * (t.ndim - 2))


def layer_norm(x, g, b):
    xf = x.astype(jnp.float32)
    mu = jnp.mean(xf, axis=-1, keepdims=True)
    var = jnp.mean(jnp.square(xf - mu), axis=-1, keepdims=True)
    return ((xf - mu) * lax.rsqrt(var + LN_EPS) * g + b).astype(x.dtype)


def rms_norm_groups(x, w, group):
    shp = x.shape
    xf = x.astype(jnp.float32).reshape(shp[:-1] + (shp[-1] // group, group))
    y = xf * lax.rsqrt(jnp.mean(jnp.square(xf), axis=-1, keepdims=True) + RMS_EPS)
    return (y.reshape(shp) * w).astype(x.dtype)


def causal_depthwise_conv(x, w, b):
    k_size = w.shape[0]
    length = x.shape[1]
    xp = jnp.pad(x, ((0, 0), (k_size - 1, 0), (0, 0)))
    y = b + w[0] * xp[:, 0:length]
    for kk in range(1, k_size):
        y = y + w[kk] * xp[:, kk:kk + length]
    return y


def partial_rotary(x, pos):
    half = ROPE_DIM // 2
    inv_freq = ROPE_THETA ** (-jnp.arange(half, dtype=jnp.float32) / half)
    ang = pos.astype(jnp.float32)[:, None] * inv_freq[None, :]
    cos = jnp.cos(ang)[None, :, None, :]
    sin = jnp.sin(ang)[None, :, None, :]
    x1, x2, rest = x[..., :half], x[..., half:ROPE_DIM], x[..., ROPE_DIM:]
    rot = jnp.concatenate([x1 * cos - x2 * sin, x2 * cos + x1 * sin], axis=-1).astype(x.dtype)
    return jnp.concatenate([rot, rest], axis=-1)


def ssd_chunked_scan(xs, dt, a, bm, cm):
    bsz, length, g, hg, p = xs.shape
    n = bm.shape[-1]
    q = SSD_CHUNK
    nc = length // q
    xd = (xs * dt[..., None]).reshape(bsz, nc, q, g, hg, p)
    a_cs = jnp.cumsum((dt * a).astype(jnp.float32).reshape(bsz, nc, q, g, hg), axis=2)
    bm = bm.reshape(bsz, nc, q, g, n)
    cm = cm.reshape(bsz, nc, q, g, n)
    seg = a_cs[:, :, :, None] - a_cs[:, :, None, :]
    causal = jnp.tril(jnp.ones((q, q), bool))[:, :, None, None]
    decay = jnp.exp(jnp.where(causal, seg, -jnp.inf))
    cb = jnp.einsum('bcign,bcjgn->bcijg', cm, bm)
    y_diag = jnp.einsum('bcijgh,bcjghp->bcighp', cb[..., None] * decay, xd)
    decay_to_end = jnp.exp(a_cs[:, :, -1:] - a_cs)
    states = jnp.einsum('bcjgn,bcjgh,bcjghp->bcghpn', bm, decay_to_end, xd)
    chunk_decay = jnp.exp(a_cs[:, :, -1])

    def step(h, inp):
        s, d = inp
        return h * d[..., None, None] + s, h

    h0 = jnp.zeros_like(states[:, 0])
    _, prev = lax.scan(step, h0, (jnp.moveaxis(states, 1, 0), jnp.moveaxis(chunk_decay, 1, 0)))
    prev = jnp.moveaxis(prev, 0, 1)
    y_off = jnp.einsum('bcign,bcghpn,bcigh->bcighp', cm, prev, jnp.exp(a_cs))
    return (y_diag + y_off).reshape(bsz, length, g, hg, p)


def gla_chunked(q, k, v, g):
    bsz, length, h, dk = q.shape
    dv = v.shape[-1]
    qs = GLA_CHUNK
    nc = length // qs

    def to_chunks(t):
        return jnp.moveaxis(t.reshape((bsz, nc, qs) + t.shape[2:]), 1, 0)

    gc = jnp.cumsum(to_chunks(g.astype(jnp.float32)), axis=2)
    causal = jnp.tril(jnp.ones((qs, qs), bool))[:, :, None, None]

    def step(s_state, inp):
        qi, ki, vi, gi = inp
        rel = gi[:, :, None] - gi[:, None, :]
        wdec = jnp.exp(jnp.where(causal, rel, -jnp.inf))
        scores = jnp.einsum('bihd,bjhd,bijhd->bhij', qi, ki, wdec)
        o = jnp.einsum('bhij,bjhv->bihv', scores, vi) + jnp.einsum('bihd,bhdv->bihv', qi * jnp.exp(gi), s_state)
        g_last = gi[:, -1]
        s_state = s_state * jnp.exp(g_last)[..., None] + jnp.einsum('bjhd,bjhv->bhdv', ki * jnp.exp(g_last[:, None] - gi), vi)
        return s_state, o

    s0 = jnp.zeros((bsz, h, dk, dv), jnp.float32)
    _, o = lax.scan(step, s0, (to_chunks(q), to_chunks(k), to_chunks(v), gc))
    return jnp.moveaxis(o, 0, 1).reshape(bsz, length, h, dv)


def ssd_gla_mixer(x, w_in, conv_w, conv_b, dt_bias, a_log, d_skip, ssd_norm_w, gate_w2, gate_b, gla_norm_w, w_out):
    bsz, length, _ = x.shape
    u = x @ w_in
    z, xbc, dt_raw, q, k, v, g_lr, r = jnp.split(u, _split_points(HYB_SIZES), axis=-1)
    xbc = jax.nn.silu(causal_depthwise_conv(xbc, conv_w, conv_b))
    xs, bm, cm = jnp.split(xbc, [SSD_D_INNER, SSD_D_INNER + SSD_GROUPS * SSD_STATE], axis=-1)
    xs = xs.reshape(bsz, length, SSD_GROUPS, SSD_HPG, SSD_HEAD_DIM)
    bm = bm.reshape(bsz, length, SSD_GROUPS, SSD_STATE)
    cm = cm.reshape(bsz, length, SSD_GROUPS, SSD_STATE)
    dt = jax.nn.softplus((dt_raw + dt_bias).astype(jnp.float32)).reshape(bsz, length, SSD_GROUPS, SSD_HPG)
    a = -jnp.exp(a_log.astype(jnp.float32)).reshape(SSD_GROUPS, SSD_HPG)
    pad = SSD_CHUNK - N_META
    y = ssd_chunked_scan(_pad_front(xs, pad), _pad_front(dt, pad), a, _pad_front(bm, pad), _pad_front(cm, pad))[:, pad:]
    y = y + xs * d_skip.reshape(SSD_GROUPS, SSD_HPG, 1)
    y = y.reshape(bsz, length, SSD_D_INNER).astype(x.dtype)
    y_ssd = rms_norm_groups(y * jax.nn.silu(z), ssd_norm_w, SSD_D_INNER // SSD_GROUPS)
    gk = jax.nn.log_sigmoid((g_lr @ gate_w2 + gate_b).astype(jnp.float32)) / GLA_GATE_TAU
    shp_k = (bsz, length, GLA_HEADS, GLA_HEAD_DK)
    qh = q.reshape(shp_k) * (GLA_HEAD_DK ** -0.5)
    kh = k.reshape(shp_k)
    vh = v.reshape(bsz, length, GLA_HEADS, GLA_HEAD_DV)
    gh = gk.reshape(shp_k)
    pad = GLA_CHUNK - N_META
    o = gla_chunked(_pad_front(qh, pad), _pad_front(kh, pad), _pad_front(vh, pad), _pad_front(gh, pad))[:, pad:]
    o = rms_norm_groups(o.reshape(bsz, length, GLA_DV).astype(x.dtype), gla_norm_w, GLA_HEAD_DV)
    y_gla = o * jax.nn.silu(r)
    return jnp.concatenate([y_ssd, y_gla], axis=-1) @ w_out


def swa_with_sinks(q, k, v, sinks):
    bsz, length = q.shape[:2]
    w = SWA_WINDOW
    nb = (length - N_META) // w
    scale = SWA_HEAD_DIM ** -0.5
    q = q.reshape(bsz, length, SWA_KV_HEADS, SWA_GROUP, SWA_HEAD_DIM)
    sink = sinks.astype(jnp.float32).reshape(SWA_KV_HEADS, SWA_GROUP)
    qm, km, vm = q[:, :N_META], k[:, :N_META], v[:, :N_META]
    s_mm = jnp.einsum('bqkgd,bskd->bkgqs', qm, km).astype(jnp.float32) * scale
    s_mm = jnp.where(jnp.tril(jnp.ones((N_META, N_META), bool)), s_mm, -jnp.inf)
    sink_m = jnp.broadcast_to(sink[None, :, :, None, None], s_mm.shape[:-1] + (1,))
    p_mm = jax.nn.softmax(jnp.concatenate([s_mm, sink_m], axis=-1), axis=-1)[..., :N_META]
    o_meta = jnp.einsum('bkgqs,bskd->bqkgd', p_mm, vm).reshape(bsz, N_META, SWA_Q_DIM)
    qb = q[:, N_META:].reshape(bsz, nb, w, SWA_KV_HEADS, SWA_GROUP, SWA_HEAD_DIM)
    kb = k[:, N_META:].reshape(bsz, nb, w, SWA_KV_HEADS, SWA_HEAD_DIM)
    vb = v[:, N_META:].reshape(bsz, nb, w, SWA_KV_HEADS, SWA_HEAD_DIM)

    def with_prev(t):
        prev = jnp.pad(t, ((0, 0), (1, 0), (0, 0), (0, 0), (0, 0)))[:, :-1]
        return jnp.concatenate([prev, t], axis=2)

    kw, vw = with_prev(kb), with_prev(vb)
    i = jnp.arange(w)[:, None]
    j = jnp.arange(2 * w)[None, :]
    band = (j > i) & (j <= i + w)
    first_ok = (jnp.arange(nb)[:, None, None] > 0) | (j >= w)[None]
    mask = (band[None] & first_ok)[None, :, None, None]
    s_w = jnp.einsum('bnqkgd,bnskd->bnkgqs', qb, kw).astype(jnp.float32) * scale
    s_w = jnp.where(mask, s_w, -jnp.inf)
    s_m = jnp.einsum('bnqkgd,bmkd->bnkgqm', qb, km).astype(jnp.float32) * scale
    sink_b = jnp.broadcast_to(sink[None, None, :, :, None, None], s_m.shape[:-1] + (1,))
    p = jax.nn.softmax(jnp.concatenate([s_m, s_w, sink_b], axis=-1), axis=-1)
    o = (jnp.einsum('bnkgqm,bmkd->bnqkgd', p[..., :N_META], vm)
         + jnp.einsum('bnkgqs,bnskd->bnqkgd', p[..., N_META:N_META + 2 * w], vw))
    o_real = o.reshape(bsz, nb * w, SWA_Q_DIM)
    return jnp.concatenate([o_meta, o_real], axis=1).astype(q.dtype)


def swa_mixer(x, w_qkv, sinks, w_out):
    bsz, length, _ = x.shape
    q, k, v = jnp.split(x @ w_qkv, [SWA_Q_DIM, SWA_Q_DIM + SWA_KV_DIM], axis=-1)
    pos = jnp.arange(length, dtype=jnp.int32)
    q = partial_rotary(q.reshape(bsz, length, SWA_Q_HEADS, SWA_HEAD_DIM), pos)
    k = partial_rotary(k.reshape(bsz, length, SWA_KV_HEADS, SWA_HEAD_DIM), pos)
    v = v.reshape(bsz, length, SWA_KV_HEADS, SWA_HEAD_DIM)
    return swa_with_sinks(q, k, v, sinks) @ w_out


def conv_glu_ffn(x, w_up, conv_w, conv_b, w_down):
    h = causal_depthwise_conv(x @ w_up, conv_w, conv_b)
    gate, val = jnp.split(h, 2, axis=-1)
    return (jax.nn.silu(gate) * val) @ w_down


def setup_inputs(seed: int = 0) -> dict:
    key = jax.random.key(seed)
    ks = jax.random.split(key, 24)

    def nrm(k, shape, s):
        return s * jax.random.normal(k, shape, jnp.float32)

    dt0 = jnp.exp(jax.random.uniform(ks[5], (N_EVEN, SSD_HEADS), jnp.float32, math.log(SSD_DT_MIN), math.log(SSD_DT_MAX)))
    return {
        'x': nrm(ks[0], (BATCH, SEQ, D_MODEL), 1.0),
        'meta_tokens': nrm(ks[1], (N_META, D_MODEL), 1.0),
        'hyb_w_in': nrm(ks[2], (N_EVEN, D_MODEL, HYB_IN_COLS), D_MODEL ** -0.5),
        'hyb_conv_w': nrm(ks[3], (N_EVEN, SSD_CONV, SSD_CONV_DIM), SSD_CONV ** -0.5),
        'hyb_conv_b': nrm(ks[4], (N_EVEN, SSD_CONV_DIM), 0.02),
        'ssd_dt_bias': dt0 + jnp.log(-jnp.expm1(-dt0)),
        'ssd_a_log': jnp.log(jax.random.uniform(ks[6], (N_EVEN, SSD_HEADS), jnp.float32, 1.0, 16.0)),
        'ssd_d': 1.0 + nrm(ks[7], (N_EVEN, SSD_HEADS), 0.1),
        'ssd_norm_w': 1.0 + nrm(ks[8], (N_EVEN, SSD_D_INNER), 0.05),
        'gla_gate_w2': nrm(ks[9], (N_EVEN, GLA_GATE_RANK, GLA_DK), GLA_GATE_RANK ** -0.5),
        'gla_gate_b': nrm(ks[10], (N_EVEN, GLA_DK), 0.1),
        'gla_norm_w': 1.0 + nrm(ks[11], (N_EVEN, GLA_DV), 0.05),
        'hyb_w_out': nrm(ks[12], (N_EVEN, HYB_MIX_WIDTH, D_MODEL), HYB_MIX_WIDTH ** -0.5 * DEEPNORM_BETA),
        'swa_w_qkv': nrm(ks[13], (N_ODD, D_MODEL, SWA_QKV_COLS), D_MODEL ** -0.5),
        'swa_sinks': nrm(ks[14], (N_ODD, SWA_Q_HEADS), 0.5),
        'swa_w_out': nrm(ks[15], (N_ODD, SWA_Q_DIM, D_MODEL), SWA_Q_DIM ** -0.5 * DEEPNORM_BETA),
        'ffn_w_up': nrm(ks[16], (DEPTH, D_MODEL, 2 * D_FF), D_MODEL ** -0.5),
        'ffn_conv_w': nrm(ks[17], (DEPTH, FFN_CONV, 2 * D_FF), FFN_CONV ** -0.5),
        'ffn_conv_b': nrm(ks[18], (DEPTH, 2 * D_FF), 0.02),
        'ffn_w_down': nrm(ks[19], (DEPTH, D_FF, D_MODEL), D_FF ** -0.5 * DEEPNORM_BETA),
        'ln_mix_g': 1.0 + nrm(ks[20], (DEPTH, D_MODEL), 0.05),
        'ln_mix_b': nrm(ks[21], (DEPTH, D_MODEL), 0.02),
        'ln_ffn_g': 1.0 + nrm(ks[22], (DEPTH, D_MODEL), 0.05),
        'ln_ffn_b': nrm(ks[23], (DEPTH, D_MODEL), 0.02),
    }


def reference(x, meta_tokens, hyb_w_in, hyb_conv_w, hyb_conv_b, ssd_dt_bias, ssd_a_log, ssd_d, ssd_norm_w,
              gla_gate_w2, gla_gate_b, gla_norm_w, hyb_w_out, swa_w_qkv, swa_sinks, swa_w_out,
              ffn_w_up, ffn_conv_w, ffn_conv_b, ffn_w_down, ln_mix_g, ln_mix_b, ln_ffn_g, ln_ffn_b):
    bsz = x.shape[0]
    meta = jnp.broadcast_to(meta_tokens.astype(x.dtype)[None], (bsz, N_META, D_MODEL))
    h = jnp.concatenate([meta, x], axis=1)
    for layer in range(DEPTH):
        j = layer // 2
        if layer % 2 == 0:
            mix = ssd_gla_mixer(h, hyb_w_in[j], hyb_conv_w[j], hyb_conv_b[j], ssd_dt_bias[j], ssd_a_log[j], ssd_d[j],
                                ssd_norm_w[j], gla_gate_w2[j], gla_gate_b[j], gla_norm_w[j], hyb_w_out[j])
        else:
            mix = swa_mixer(h, swa_w_qkv[j], swa_sinks[j], swa_w_out[j])
        h = layer_norm(DEEPNORM_ALPHA * h + mix, ln_mix_g[layer], ln_mix_b[layer])
        ffn = conv_glu_ffn(h, ffn_w_up[layer], ffn_conv_w[layer], ffn_conv_b[layer], ffn_w_down[layer])
        h = layer_norm(DEEPNORM_ALPHA * h + ffn, ln_ffn_g[layer], ln_ffn_b[layer])
    return h[:, N_META:]
```

```python
import functools

import jax
import jax.numpy as jnp
from jax import lax
from jax.experimental import pallas as pl
from jax.experimental.pallas import tpu as pltpu

F32 = jnp.float32
BF16 = jnp.bfloat16
HIGHEST = lax.Precision.HIGHEST

D_MODEL = 2048
DEPTH = 4
N_META = 16
LN_EPS = 1e-5
RMS_EPS = 1e-6
DEEPNORM_ALPHA = (2.0 * DEPTH) ** 0.25

SSD_HEAD_DIM = 64
SSD_HEADS = 32
SSD_GROUPS = 8
SSD_HPG = 4
SSD_STATE = 128
SSD_CONV = 4
SSD_CHUNK = 128
SSD_GROUP_W = SSD_HPG * SSD_HEAD_DIM

GLA_HEADS = 4
GLA_HEAD_DK = 256
GLA_HEAD_DV = 512
GLA_GATE_RANK = 16
GLA_GATE_TAU = 16.0
GLA_CHUNK = 64
GLA_SUB = 16

SWA_HEAD_DIM = 64
SWA_Q_HEADS = 32
SWA_KV_HEADS = 8
SWA_GROUP = 4
SWA_WINDOW = 128
ROPE_THETA = 500000.0
ROPE_DIM = 16

D_FF = 5504
FFN_CONV = 3

LANES = 128
SUBLANES = 8
FRONT_PAD = SSD_CHUNK - N_META
D_FF_PAD = 5632
FFN_TN = 512
VMEM_LIMIT = 56 * 1024 * 1024


def _row_tile(rows):
    for t in (640, 512, 384, 256, 128):
        if rows % t == 0:
            return t
    raise ValueError(f"row count {rows} is not a multiple of 128")


def _params(*sem):
    return pltpu.CompilerParams(dimension_semantics=sem, vmem_limit_bytes=VMEM_LIMIT)


def _sigmoid(x):
    return 1.0 / (1.0 + jnp.exp(-x))


def _softplus(x):
    return jnp.maximum(x, 0.0) + jnp.log(1.0 + jnp.exp(-jnp.abs(x)))


def _log_sigmoid(x):
    return jnp.minimum(x, 0.0) - jnp.log(1.0 + jnp.exp(-jnp.abs(x)))


def _causal_conv_rows(y, carry, w_ref, b_row, taps):
    top = jnp.concatenate([carry, y[0:SUBLANES]], axis=0)
    w_last = w_ref[taps - 1:taps, :]
    acc = b_row + w_last * y
    acc_top = b_row + w_last * y[0:SUBLANES]
    for s in range(1, taps):
        wk = w_ref[taps - 1 - s:taps - s, :]
        acc = acc + wk * pltpu.roll(y, s, 0)
        acc_top = acc_top + wk * pltpu.roll(top, s, 0)[SUBLANES:2 * SUBLANES]
    return jnp.concatenate([acc_top, acc[SUBLANES:]], axis=0)


def _mm_kernel(x_ref, w_ref, o_ref):
    o_ref[...] = jnp.dot(x_ref[...], w_ref[...], preferred_element_type=F32).astype(o_ref.dtype)


def _matmul(x, w, tn, out_dtype):
    m, k = x.shape
    n = w.shape[1]
    tm = _row_tile(m)
    return pl.pallas_call(
        _mm_kernel,
        grid=(n // tn, m // tm),
        in_specs=[pl.BlockSpec((tm, k), lambda j, i: (i, 0)),
                  pl.BlockSpec((k, tn), lambda j, i: (0, j))],
        out_specs=pl.BlockSpec((tm, tn), lambda j, i: (i, j)),
        out_shape=jax.ShapeDtypeStruct((m, n), out_dtype),
        compiler_params=_params("arbitrary", "arbitrary"),
        name="proj",
    )(x, w)


def _mm_ln_kernel(x_ref, w_ref, res_ref, g_ref, b_ref, of_ref, ob_ref, acc_ref, *, nk, tm):
    i = pl.program_id(0)
    k = pl.program_id(1)

    @pl.when(k == 0)
    def _():
        acc_ref[...] = jnp.zeros_like(acc_ref)

    acc_ref[...] += jnp.dot(x_ref[...], w_ref[...], preferred_element_type=F32)

    @pl.when(k == nk - 1)
    def _():
        t = DEEPNORM_ALPHA * res_ref[...] + acc_ref[...]
        mu = jnp.mean(t, axis=-1, keepdims=True)
        d = t - mu
        var = jnp.mean(d * d, axis=-1, keepdims=True)
        y = d * lax.rsqrt(var + LN_EPS) * g_ref[...] + b_ref[...]
        row = lax.broadcasted_iota(jnp.int32, (tm, 1), 0) + i * tm
        y = jnp.where(row >= FRONT_PAD, y, 0.0)
        of_ref[...] = y
        ob_ref[...] = y.astype(BF16)


def _matmul_residual_ln(x, w, res, gamma, beta, tk):
    m, kdim = x.shape
    n = w.shape[1]
    tm = _row_tile(m)
    nk = kdim // tk
    return pl.pallas_call(
        functools.partial(_mm_ln_kernel, nk=nk, tm=tm),
        grid=(m // tm, nk),
        in_specs=[pl.BlockSpec((tm, tk), lambda i, k: (i, k)),
                  pl.BlockSpec((tk, n), lambda i, k: (k, 0)),
                  pl.BlockSpec((tm, n), lambda i, k: (i, 0)),
                  pl.BlockSpec((1, n), lambda i, k: (0, 0)),
                  pl.BlockSpec((1, n), lambda i, k: (0, 0))],
        out_specs=[pl.BlockSpec((tm, n), lambda i, k: (i, 0)),
                   pl.BlockSpec((tm, n), lambda i, k: (i, 0))],
        out_shape=[jax.ShapeDtypeStruct((m, n), F32), jax.ShapeDtypeStruct((m, n), BF16)],
        scratch_shapes=[pltpu.VMEM((tm, n), F32)],
        compiler_params=_params("arbitrary", "arbitrary"),
        name="proj_ln",
    )(x, w, res, gamma.reshape(1, n), beta.reshape(1, n))


def _ffn_up_kernel(x_ref, wg_ref, wv_ref, cwg_ref, cwv_ref, cbg_ref, cbv_ref, o_ref, cg_ref, cv_ref, *, tm):
    i = pl.program_id(1)

    @pl.when(i == 0)
    def _():
        cg_ref[...] = jnp.zeros_like(cg_ref)
        cv_ref[...] = jnp.zeros_like(cv_ref)

    x = x_ref[...]
    yg = jnp.dot(x, wg_ref[...], preferred_element_type=F32)
    yv = jnp.dot(x, wv_ref[...], preferred_element_type=F32)
    hg = _causal_conv_rows(yg, cg_ref[...], cwg_ref, cbg_ref[...], FFN_CONV)
    hv = _causal_conv_rows(yv, cv_ref[...], cwv_ref, cbv_ref[...], FFN_CONV)
    cg_ref[...] = yg[tm - SUBLANES:tm]
    cv_ref[...] = yv[tm - SUBLANES:tm]
    o_ref[...] = (hg * _sigmoid(hg) * hv).astype(o_ref.dtype)


def _ffn_up(xb, w_up, conv_w, conv_b):
    m, k = xb.shape
    tm = _row_tile(m)
    nj = D_FF_PAD // FFN_TN
    return pl.pallas_call(
        functools.partial(_ffn_up_kernel, tm=tm),
        grid=(nj, m // tm),
        in_specs=[pl.BlockSpec((tm, k), lambda j, i: (i, 0)),
                  pl.BlockSpec((k, FFN_TN), lambda j, i: (0, j)),
                  pl.BlockSpec((k, FFN_TN), lambda j, i: (0, nj + j)),
                  pl.BlockSpec((FFN_CONV, FFN_TN), lambda j, i: (0, j)),
                  pl.BlockSpec((FFN_CONV, FFN_TN), lambda j, i: (0, nj + j)),
                  pl.BlockSpec((1, FFN_TN), lambda j, i: (0, j)),
                  pl.BlockSpec((1, FFN_TN), lambda j, i: (0, nj + j))],
        out_specs=pl.BlockSpec((tm, FFN_TN), lambda j, i: (i, j)),
        out_shape=jax.ShapeDtypeStruct((m, D_FF_PAD), BF16),
        scratch_shapes=[pltpu.VMEM((SUBLANES, FFN_TN), F32), pltpu.VMEM((SUBLANES, FFN_TN), F32)],
        compiler_params=_params("arbitrary", "arbitrary"),
        name="ffn_up",
    )(xb, w_up, w_up, conv_w, conv_w, conv_b, conv_b)


def _expand_heads(x, rows):
    low = lax.broadcasted_iota(jnp.int32, (1, LANES), 1) < SSD_HEAD_DIM
    b = [jnp.broadcast_to(x[:, h:h + 1], (rows, LANES)) for h in range(SSD_HPG)]
    return jnp.concatenate([jnp.where(low, b[0], b[1]), jnp.where(low, b[2], b[3])], axis=1)


def _ssd_kernel(z_ref, xs_ref, b_ref, c_ref, dt_ref, cwx_ref, cwb_ref, cwc_ref, cbx_ref, cbb_ref, cbc_ref,
                dtb_ref, alog_ref, dsk_ref, nw_ref, o_ref, st_ref, cx_ref, cb_ref, cc_ref):
    g = pl.program_id(0)
    c = pl.program_id(1)
    q = SSD_CHUNK

    @pl.when(c == 0)
    def _():
        st_ref[...] = jnp.zeros_like(st_ref)
        cx_ref[...] = jnp.zeros_like(cx_ref)
        cb_ref[...] = jnp.zeros_like(cb_ref)
        cc_ref[...] = jnp.zeros_like(cc_ref)

    xs_raw = xs_ref[...]
    b_raw = b_ref[...]
    c_raw = c_ref[...]
    xs = _causal_conv_rows(xs_raw, cx_ref[...], cwx_ref, cbx_ref[...], SSD_CONV)
    bm = _causal_conv_rows(b_raw, cb_ref[...], cwb_ref, cbb_ref[...], SSD_CONV)
    cm = _causal_conv_rows(c_raw, cc_ref[...], cwc_ref, cbc_ref[...], SSD_CONV)
    cx_ref[...] = xs_raw[q - SUBLANES:q]
    cb_ref[...] = b_raw[q - SUBLANES:q]
    cc_ref[...] = c_raw[q - SUBLANES:q]
    xs = xs * _sigmoid(xs)
    bm = bm * _sigmoid(bm)
    cm = cm * _sigmoid(cm)

    row = lax.broadcasted_iota(jnp.int32, (q, 1), 0) + c * q
    lane = lax.broadcasted_iota(jnp.int32, (1, LANES), 1)
    dt_raw = pltpu.roll(dt_ref[...], (LANES - SSD_HPG * g) % LANES, 1)
    dt = _softplus(dt_raw + dtb_ref[...])
    dt = jnp.where((row >= FRONT_PAD) & (lane < SSD_HPG), dt, 0.0)
    a = -jnp.exp(alog_ref[...])
    ri = lax.broadcasted_iota(jnp.int32, (q, q), 0)
    ci = lax.broadcasted_iota(jnp.int32, (q, q), 1)
    causal = ci <= ri
    cs = jnp.dot(causal.astype(F32), dt * a, precision=HIGHEST, preferred_element_type=F32)
    cs_t = cs.T
    cs_last = cs[q - 1:q, :]

    xd = xs * _expand_heads(dt, q)
    bm_t = bm.T
    cmb = cm.astype(BF16)
    cb = jnp.dot(cmb, bm_t.astype(BF16), preferred_element_type=F32)
    decayed = []
    for h in range(SSD_HPG):
        seg = cs[:, h:h + 1] - cs_t[h:h + 1, :]
        decayed.append((cb * jnp.exp(jnp.where(causal, seg, -jnp.inf))).astype(BF16))
    lane_w = lax.broadcasted_iota(jnp.int32, (1, SSD_GROUP_W), 1)
    xdb = xd.astype(BF16)
    x_heads = [jnp.where((lane_w >= h * SSD_HEAD_DIM) & (lane_w < (h + 1) * SSD_HEAD_DIM), xdb, jnp.zeros_like(xdb))
               for h in range(SSD_HPG)]
    y = jnp.dot(jnp.concatenate(decayed, axis=1), jnp.concatenate(x_heads, axis=0), preferred_element_type=F32)

    st = st_ref[...]
    y = y + jnp.dot(cmb, st.astype(BF16), preferred_element_type=F32) * _expand_heads(jnp.exp(cs), q)
    to_end = _expand_heads(jnp.exp(cs_last - cs), q)
    st_ref[...] = (st * _expand_heads(jnp.exp(cs_last), 1)
                   + jnp.dot(bm_t.astype(BF16), (xd * to_end).astype(BF16), preferred_element_type=F32))

    y = y + xs * dsk_ref[...]
    z = z_ref[...]
    yg = y * (z * _sigmoid(z))
    yn = yg * lax.rsqrt(jnp.mean(yg * yg, axis=-1, keepdims=True) + RMS_EPS) * nw_ref[...]
    o_ref[...] = yn.astype(o_ref.dtype)


def _ssd_mixer(u, u_small, conv_w, conv_b, dt_bias, a_log, d_skip, norm_w):
    rows = u.shape[0]
    nc = rows // SSD_CHUNK
    q = SSD_CHUNK
    gw = SSD_GROUP_W
    n = SSD_STATE
    zoff = 0
    xoff = 2048 // gw
    boff = (2048 + 2048) // n
    coff = (2048 + 2048 + 1024) // n
    pad_l = lambda v: jnp.pad(v.reshape(SSD_GROUPS, 1, SSD_HPG), ((0, 0), (0, 0), (0, LANES - SSD_HPG)))
    per_group = lambda blk: pl.BlockSpec((None, 1, blk), lambda g, c: (g, 0, 0))
    return pl.pallas_call(
        _ssd_kernel,
        grid=(SSD_GROUPS, nc),
        in_specs=[pl.BlockSpec((q, gw), lambda g, c: (c, zoff + g)),
                  pl.BlockSpec((q, gw), lambda g, c: (c, xoff + g)),
                  pl.BlockSpec((q, n), lambda g, c: (c, boff + g)),
                  pl.BlockSpec((q, n), lambda g, c: (c, coff + g)),
                  pl.BlockSpec((q, LANES), lambda g, c: (c, 0)),
                  pl.BlockSpec((SSD_CONV, gw), lambda g, c: (0, g)),
                  pl.BlockSpec((SSD_CONV, n), lambda g, c: (0, 2048 // n + g)),
                  pl.BlockSpec((SSD_CONV, n), lambda g, c: (0, (2048 + 1024) // n + g)),
                  pl.BlockSpec((1, gw), lambda g, c: (0, g)),
                  pl.BlockSpec((1, n), lambda g, c: (0, 2048 // n + g)),
                  pl.BlockSpec((1, n), lambda g, c: (0, (2048 + 1024) // n + g)),
                  per_group(LANES), per_group(LANES), per_group(gw), per_group(gw)],
        out_specs=pl.BlockSpec((q, gw), lambda g, c: (c, g)),
        out_shape=jax.ShapeDtypeStruct((rows, SSD_GROUPS * gw), BF16),
        scratch_shapes=[pltpu.VMEM((n, gw), F32), pltpu.VMEM((SUBLANES, gw), F32),
                        pltpu.VMEM((SUBLANES, n), F32), pltpu.VMEM((SUBLANES, n), F32)],
        compiler_params=_params("arbitrary", "arbitrary"),
        name="ssd",
    )(u, u, u, u, u_small, conv_w, conv_w, conv_w, conv_b.reshape(1, -1), conv_b.reshape(1, -1),
      conv_b.reshape(1, -1), pad_l(dt_bias), pad_l(a_log),
      jnp.repeat(d_skip, SSD_HEAD_DIM).reshape(SSD_GROUPS, 1, gw), norm_w.reshape(SSD_GROUPS, 1, gw))


def _gla_kernel(q_ref, k_ref, v_ref, r_ref, glr_ref, w2_ref, gb_ref, nw_ref, o_ref, st_ref):
    c = pl.program_id(1)
    qc = GLA_CHUNK
    sub = GLA_SUB

    @pl.when(c == 0)
    def _():
        st_ref[...] = jnp.zeros_like(st_ref)

    q = q_ref[...] * (GLA_HEAD_DK ** -0.5)
    k = k_ref[...]
    v = v_ref[...]
    pre = jnp.dot(glr_ref[...], w2_ref[...], precision=HIGHEST, preferred_element_type=F32) + gb_ref[...]
    g = _log_sigmoid(pre) * (1.0 / GLA_GATE_TAU)
    row = lax.broadcasted_iota(jnp.int32, (qc, 1), 0) + c * qc
    g = jnp.where(row >= FRONT_PAD, g, 0.0)
    ri = lax.broadcasted_iota(jnp.int32, (qc, qc), 0)
    ci = lax.broadcasted_iota(jnp.int32, (qc, qc), 1)
    gc = jnp.dot((ci <= ri).astype(F32), g, precision=HIGHEST, preferred_element_type=F32)
    g_last = gc[qc - 1:qc, :]

    lane_j = lax.broadcasted_iota(jnp.int32, (sub, qc), 1)
    row_i = lax.broadcasted_iota(jnp.int32, (sub, 1), 0)
    nt = (((1,), (1,)), ((), ()))
    a_rows = []
    for blk in range(qc // sub):
        lo = blk * sub
        q_b = q[lo:lo + sub]
        g_b = gc[lo:lo + sub]
        a_blk = jnp.zeros((sub, qc), F32)
        for j in range(sub):
            k_j = k[lo + j:lo + j + 1, :]
            g_j = gc[lo + j:lo + j + 1, :]
            s_j = jnp.sum(q_b * k_j * jnp.exp(jnp.minimum(g_b - g_j, 0.0)), axis=1, keepdims=True)
            a_blk = jnp.where(lane_j == lo + j, jnp.where(row_i >= j, s_j, 0.0), a_blk)
        if blk > 0:
            g_ref0 = gc[lo:lo + 1, :]
            q_t = (q_b * jnp.exp(g_b - g_ref0)).astype(BF16)
            k_t = (k * jnp.exp(jnp.minimum(g_ref0 - gc, 0.0))).astype(BF16)
            off = lax.dot_general(q_t, k_t, nt, preferred_element_type=F32)
            a_blk = jnp.where(lane_j < lo, off, a_blk)
        a_rows.append(a_blk)
    a = jnp.concatenate(a_rows, axis=0)
    vb = v.astype(BF16)
    o = jnp.dot(a.astype(BF16), vb, preferred_element_type=F32)

    st = st_ref[...]
    o = o + lax.dot_general((q * jnp.exp(gc)).astype(BF16), st.astype(BF16), nt, preferred_element_type=F32)
    k_end = (k * jnp.exp(g_last - gc)).astype(BF16)
    st_ref[...] = st * jnp.exp(g_last) + lax.dot_general(vb, k_end, (((0,), (0,)), ((), ())),
                                                         preferred_element_type=F32)

    on = o * lax.rsqrt(jnp.mean(o * o, axis=-1, keepdims=True) + RMS_EPS) * nw_ref[...]
    r = r_ref[...]
    o_ref[...] = (on * (r * _sigmoid(r))).astype(o_ref.dtype)


def _gla_mixer(u, u_small, gate_w2, gate_b, norm_w):
    rows = u.shape[0]
    qc = GLA_CHUNK
    dk, dv = GLA_HEAD_DK, GLA_HEAD_DV
    qoff = 6144 // dk
    koff = (6144 + 1024) // dk
    voff = (6144 + 2048) // dv
    roff = (6144 + 4096) // dv
    w2 = jnp.pad(gate_w2, ((0, LANES - GLA_GATE_RANK), (0, 0)))
    return pl.pallas_call(
        _gla_kernel,
        grid=(GLA_HEADS, rows // qc),
        in_specs=[pl.BlockSpec((qc, dk), lambda h, c: (c, qoff + h)),
                  pl.BlockSpec((qc, dk), lambda h, c: (c, koff + h)),
                  pl.BlockSpec((qc, dv), lambda h, c: (c, voff + h)),
                  pl.BlockSpec((qc, dv), lambda h, c: (c, roff + h)),
                  pl.BlockSpec((qc, LANES), lambda h, c: (c, 1)),
                  pl.BlockSpec((LANES, dk), lambda h, c: (0, h)),
                  pl.BlockSpec((1, dk), lambda h, c: (0, h)),
                  pl.BlockSpec((1, dv), lambda h, c: (0, h))],
        out_specs=pl.BlockSpec((qc, dv), lambda h, c: (c, h)),
        out_shape=jax.ShapeDtypeStruct((rows, GLA_HEADS * dv), BF16),
        scratch_shapes=[pltpu.VMEM((dv, dk), F32)],
        compiler_params=_params("arbitrary", "arbitrary"),
        name="gla",
    )(u, u, u, u, u_small, w2, gate_b.reshape(1, -1), norm_w.reshape(1, -1))


def _rope_kernel(freq_ref, cm_ref, sp_ref, sm_ref, *, tm):
    i = pl.program_id(0)
    row = lax.broadcasted_iota(jnp.int32, (tm, LANES), 0) + i * tm
    lane = lax.broadcasted_iota(jnp.int32, (tm, LANES), 1) % SWA_HEAD_DIM
    ang = (row - FRONT_PAD).astype(F32) * freq_ref[...]
    cos = jnp.cos(ang)
    sin = jnp.sin(ang)
    half = ROPE_DIM // 2
    cm_ref[...] = cos
    sp_ref[...] = jnp.where(lane < half, -sin, 0.0)
    sm_ref[...] = jnp.where((lane >= half) & (lane < ROPE_DIM), sin, 0.0)


def _rope_tables(rows):
    half = ROPE_DIM // 2
    inv_freq = ROPE_THETA ** (-jnp.arange(half, dtype=F32) / half)
    per_head = jnp.concatenate([inv_freq, inv_freq, jnp.zeros((SWA_HEAD_DIM - ROPE_DIM,), F32)])
    freq = jnp.tile(per_head, LANES // SWA_HEAD_DIM).reshape(1, LANES)
    tm = _row_tile(rows)
    shp = jax.ShapeDtypeStruct((rows, LANES), F32)
    spec = pl.BlockSpec((tm, LANES), lambda i: (i, 0))
    return pl.pallas_call(
        functools.partial(_rope_kernel, tm=tm),
        grid=(rows // tm,),
        in_specs=[pl.BlockSpec((1, LANES), lambda i: (0, 0))],
        out_specs=[spec, spec, spec],
        out_shape=[shp, shp, shp],
        compiler_params=_params("arbitrary"),
        name="rope_tables",
    )(freq)


def _qkv_kernel(x_ref, w_ref, cm_ref, sp_ref, sm_ref, o_ref, *, rot_tiles, tn):
    j = pl.program_id(0)
    y = jnp.dot(x_ref[...], w_ref[...], preferred_element_type=F32)

    @pl.when(j < rot_tiles)
    def _():
        cm, sp, sm = cm_ref[...], sp_ref[...], sm_ref[...]
        half = ROPE_DIM // 2
        for cidx in range(tn // LANES):
            yc = y[:, cidx * LANES:(cidx + 1) * LANES]
            rot = yc * cm + pltpu.roll(yc, LANES - half, 1) * sp + pltpu.roll(yc, half, 1) * sm
            o_ref[:, cidx * LANES:(cidx + 1) * LANES] = rot.astype(o_ref.dtype)

    @pl.when(j >= rot_tiles)
    def _():
        o_ref[...] = y.astype(o_ref.dtype)


def _qkv_rope(xb, w, tables, rot_cols, tn=512):
    m, k = xb.shape
    n = w.shape[1]
    tm = _row_tile(m)
    tspec = pl.BlockSpec((tm, LANES), lambda j, i: (i, 0))
    return pl.pallas_call(
        functools.partial(_qkv_kernel, rot_tiles=rot_cols // tn, tn=tn),
        grid=(n // tn, m // tm),
        in_specs=[pl.BlockSpec((tm, k), lambda j, i: (i, 0)),
                  pl.BlockSpec((k, tn), lambda j, i: (0, j)), tspec, tspec, tspec],
        out_specs=pl.BlockSpec((tm, tn), lambda j, i: (i, j)),
        out_shape=jax.ShapeDtypeStruct((m, n), BF16),
        compiler_params=_params("arbitrary", "arbitrary"),
        name="qkv_rope",
    )(xb, w, *tables)


def _swa_kernel(sink_ref, q_ref, kp_ref, kc_ref, km_ref, vp_ref, vc_ref, vm_ref, o_ref):
    g = pl.program_id(0)
    n = pl.program_id(1)
    w = SWA_WINDOW
    meta_lo = FRONT_PAD
    lane = lax.broadcasted_iota(jnp.int32, (1, LANES), 1)
    low = lane < SWA_HEAD_DIM
    q = q_ref[...]
    zero = jnp.zeros((w, LANES), q.dtype)
    qa, qb = q[:, :LANES], q[:, LANES:]
    qs = jnp.concatenate([jnp.where(low, qa, zero), jnp.where(low, zero, qa),
                          jnp.where(low, qb, zero), jnp.where(low, zero, qb)], axis=0)
    kk = jnp.concatenate([kp_ref[...], kc_ref[...], km_ref[...]], axis=0)
    s = lax.dot_general(qs, kk, (((1,), (1,)), ((), ())), preferred_element_type=F32)

    i = lax.broadcasted_iota(jnp.int32, (SWA_GROUP * w, 1), 0) % w
    col = lax.broadcasted_iota(jnp.int32, (1, 3 * w), 1)
    cj = col % w
    seg = col // w
    valid_prev = (seg == 0) & (cj > i) & (n >= 2)
    valid_cur = (seg == 1) & (cj <= i) & ((n >= 1) | (cj >= meta_lo))
    valid_meta = (seg == 2) & (cj >= meta_lo) & (n >= 1)
    s = jnp.where(valid_prev | valid_cur | valid_meta, s, -jnp.inf)

    hrow = lax.broadcasted_iota(jnp.int32, (SWA_GROUP * w, 1), 0) // w
    sink = jnp.zeros((SWA_GROUP * w, 1), F32)
    for h in range(SWA_GROUP):
        sink = jnp.where(hrow == h, sink_ref[g * SWA_GROUP + h], sink)
    m = jnp.maximum(jnp.max(s, axis=-1, keepdims=True), sink)
    p = jnp.exp(s - m)
    denom = jnp.sum(p, axis=-1, keepdims=True) + jnp.exp(sink - m)
    vv = jnp.concatenate([vp_ref[...], vc_ref[...], vm_ref[...]], axis=0)
    o = jnp.dot(p.astype(BF16), vv, preferred_element_type=F32) / denom
    oa = jnp.where(low, o[0:w], o[w:2 * w])
    ob = jnp.where(low, o[2 * w:3 * w], o[3 * w:4 * w])
    o_ref[...] = jnp.concatenate([oa, ob], axis=1).astype(o_ref.dtype)


def _swa_attention(qkv, sinks):
    rows = qkv.shape[0]
    w = SWA_WINDOW
    qw = SWA_GROUP * SWA_HEAD_DIM
    koff = SWA_Q_HEADS * SWA_HEAD_DIM // LANES
    voff = koff + SWA_KV_HEADS
    kv = lambda off, which: pl.BlockSpec(
        (w, LANES), {"prev": lambda g, n: (jnp.maximum(n - 1, 0), off + g),
                     "cur": lambda g, n: (n, off + g),
                     "meta": lambda g, n: (0, off + g)}[which])
    return pl.pallas_call(
        _swa_kernel,
        grid=(SWA_KV_HEADS, rows // w),
        in_specs=[pl.BlockSpec(memory_space=pltpu.SMEM),
                  pl.BlockSpec((w, qw), lambda g, n: (n, g)),
                  kv(koff, "prev"), kv(koff, "cur"), kv(koff, "meta"),
                  kv(voff, "prev"), kv(voff, "cur"), kv(voff, "meta")],
        out_specs=pl.BlockSpec((w, qw), lambda g, n: (n, g)),
        out_shape=jax.ShapeDtypeStruct((rows, SWA_Q_HEADS * SWA_HEAD_DIM), BF16),
        compiler_params=_params("arbitrary", "arbitrary"),
        name="swa",
    )(sinks, qkv, qkv, qkv, qkv, qkv, qkv, qkv)


def _pad_cols(w, n):
    return jnp.pad(w, ((0, 0), (0, n - w.shape[1])))


def _hyb_in_weights(w):
    main = jnp.concatenate([w[:, :6144], w[:, 6176:10272], w[:, 10288:]], axis=1).astype(BF16)
    small = jnp.concatenate([_pad_cols(w[:, 6144:6176], LANES), _pad_cols(w[:, 10272:10288], LANES)], axis=1)
    return main, small.astype(BF16)


def _qkv_weights(w):
    d = w.shape[0]
    qd = SWA_Q_HEADS * SWA_HEAD_DIM
    kvd = SWA_KV_HEADS * SWA_HEAD_DIM
    dup = lambda t: jnp.concatenate([t.reshape(d, SWA_KV_HEADS, SWA_HEAD_DIM)] * 2, axis=-1).reshape(d, 2 * kvd)
    wq = w[:, :qd] * (SWA_HEAD_DIM ** -0.5)
    return jnp.concatenate([wq, dup(w[:, qd:qd + kvd]), dup(w[:, qd + kvd:])], axis=1).astype(BF16)


def _ffn_weights(w_up, conv_w, conv_b, w_down):
    halves = lambda t: jnp.concatenate([_pad_cols(t[:, :D_FF], D_FF_PAD), _pad_cols(t[:, D_FF:], D_FF_PAD)], axis=1)
    return (halves(w_up).astype(BF16), halves(conv_w), halves(conv_b.reshape(1, -1)),
            jnp.pad(w_down, ((0, D_FF_PAD - D_FF), (0, 0))).astype(BF16))


def _trunk(x, meta_tokens, hyb_w_in, hyb_conv_w, hyb_conv_b, ssd_dt_bias, ssd_a_log, ssd_d, ssd_norm_w,
           gla_gate_w2, gla_gate_b, gla_norm_w, hyb_w_out, swa_w_qkv, swa_sinks, swa_w_out,
           ffn_w_up, ffn_conv_w, ffn_conv_b, ffn_w_down, ln_mix_g, ln_mix_b, ln_ffn_g, ln_ffn_b):
    seq = x.shape[0]
    rows = FRONT_PAD + N_META + seq
    h = jnp.concatenate([jnp.zeros((FRONT_PAD, D_MODEL), F32), meta_tokens.astype(F32), x], axis=0)
    hb = h.astype(BF16)
    tables = _rope_tables(rows)
    for layer in range(DEPTH):
        j = layer // 2
        if layer % 2 == 0:
            w_main, w_small = _hyb_in_weights(hyb_w_in[j])
            u = _matmul(hb, w_main, 512, F32)
            u_small = _matmul(hb, w_small, 2 * LANES, F32)
            y_ssd = _ssd_mixer(u, u_small, hyb_conv_w[j], hyb_conv_b[j], ssd_dt_bias[j], ssd_a_log[j],
                               ssd_d[j], ssd_norm_w[j])
            y_gla = _gla_mixer(u, u_small, gla_gate_w2[j], gla_gate_b[j], gla_norm_w[j])
            mix_in = jnp.concatenate([y_ssd, y_gla], axis=1)
            w_out = hyb_w_out[j].astype(BF16)
        else:
            qkv = _qkv_rope(hb, _qkv_weights(swa_w_qkv[j]), tables, 3072)
            mix_in = _swa_attention(qkv, swa_sinks[j])
            w_out = swa_w_out[j].astype(BF16)
        h, hb = _matmul_residual_ln(mix_in, w_out, h, ln_mix_g[layer], ln_mix_b[layer], tk=1024)
        w_up, conv_w, conv_b, w_down = _ffn_weights(ffn_w_up[layer], ffn_conv_w[layer], ffn_conv_b[layer],
                                                    ffn_w_down[layer])
        act = _ffn_up(hb, w_up, conv_w, conv_b)
        h, hb = _matmul_residual_ln(act, w_down, h, ln_ffn_g[layer], ln_ffn_b[layer], tk=512)
    return h[FRONT_PAD + N_META:]


def kernel(x, meta_tokens, hyb_w_in, hyb_conv_w, hyb_conv_b, ssd_dt_bias, ssd_a_log, ssd_d, ssd_norm_w,
           gla_gate_w2, gla_gate_b, gla_norm_w, hyb_w_out, swa_w_qkv, swa_sinks, swa_w_out,
           ffn_w_up, ffn_conv_w, ffn_conv_b, ffn_w_down, ln_mix_g, ln_mix_b, ln_ffn_g, ln_ffn_b):
    params = (meta_tokens, hyb_w_in, hyb_conv_w, hyb_conv_b, ssd_dt_bias, ssd_a_log, ssd_d, ssd_norm_w,
              gla_gate_w2, gla_gate_b, gla_norm_w, hyb_w_out, swa_w_qkv, swa_sinks, swa_w_out,
              ffn_w_up, ffn_conv_w, ffn_conv_b, ffn_w_down, ln_mix_g, ln_mix_b, ln_ffn_g, ln_ffn_b)
    return jnp.stack([_trunk(x[b], *params) for b in range(x.shape[0])], axis=0)
```

```python
import functools

import jax
import jax.numpy as jnp
from jax import lax
from jax.experimental import pallas as pl
from jax.experimental.pallas import tpu as pltpu

F32 = jnp.float32
BF16 = jnp.bfloat16
HIGHEST = lax.Precision.HIGHEST

D_MODEL = 2048
DEPTH = 4
N_META = 16
LN_EPS = 1e-5
RMS_EPS = 1e-6
DEEPNORM_ALPHA = (2.0 * DEPTH) ** 0.25

SSD_HEAD_DIM = 64
SSD_HEADS = 32
SSD_GROUPS = 8
SSD_HPG = 4
SSD_STATE = 128
SSD_CONV = 4
SSD_CHUNK = 128
SSD_GROUP_W = SSD_HPG * SSD_HEAD_DIM

GLA_HEADS = 4
GLA_HEAD_DK = 256
GLA_HEAD_DV = 512
GLA_GATE_RANK = 16
GLA_GATE_TAU = 16.0
GLA_CHUNK = 64
GLA_SUB = 16

SWA_HEAD_DIM = 64
SWA_Q_HEADS = 32
SWA_KV_HEADS = 8
SWA_GROUP = 4
SWA_WINDOW = 128
ROPE_THETA = 500000.0
ROPE_DIM = 16

D_FF = 5504
FFN_CONV = 3

LANES = 128
SUBLANES = 8
FRONT_PAD = SSD_CHUNK - N_META
D_FF_PAD = 5632
FFN_TN = 512
FFN_DOWN_TK = 1408
VMEM_LIMIT = 56 * 1024 * 1024

HYB_DT_COL = 6144
HYB_QKV_COL = 6176
HYB_GLR_COL = 10272
HYB_R_COL = 10288


def _row_tile(rows):
    for t in (640, 512, 384, 256, 128):
        if rows % t == 0:
            return t
    raise ValueError(f"row count {rows} is not a multiple of 128")


def _params(*sem):
    return pltpu.CompilerParams(dimension_semantics=sem, vmem_limit_bytes=VMEM_LIMIT)


def _sigmoid(x):
    return 1.0 / (1.0 + jnp.exp(-x))


def _softplus(x):
    return jnp.maximum(x, 0.0) + jnp.log(1.0 + jnp.exp(-jnp.abs(x)))


def _log_sigmoid(x):
    return jnp.minimum(x, 0.0) - jnp.log(1.0 + jnp.exp(-jnp.abs(x)))


def _causal_conv_rows(y, carry, w_ref, b_row, taps):
    top = jnp.concatenate([carry, y[0:SUBLANES]], axis=0)
    w_last = w_ref[taps - 1:taps, :]
    acc = b_row + w_last * y
    acc_top = b_row + w_last * y[0:SUBLANES]
    for s in range(1, taps):
        wk = w_ref[taps - 1 - s:taps - s, :]
        acc = acc + wk * pltpu.roll(y, s, 0)
        acc_top = acc_top + wk * pltpu.roll(top, s, 0)[SUBLANES:2 * SUBLANES]
    return jnp.concatenate([acc_top, acc[SUBLANES:]], axis=0)


def _assemble_weight(w_refs, wb_ref, shift, scale):
    k, tn = wb_ref.shape
    chunk = 256
    for r in range(0, k, chunk):
        w = jnp.concatenate([wr[r:r + chunk, :] for wr in w_refs], axis=1)
        if shift:
            w = pltpu.roll(w, w.shape[1] - shift, 1)
        w = w[:, :tn]
        if scale is not None:
            w = w * scale
        wb_ref[r:r + chunk, :] = w.astype(BF16)


def _weight_block_specs(k, layer, block_fns):
    return [pl.BlockSpec((None, k, LANES), functools.partial(lambda *ids, fn: (layer, 0, fn(*ids)), fn=fn))
            for fn in block_fns]


def _rotate_heads(y, cm, sp, sm):
    half = ROPE_DIM // 2
    out = []
    for c in range(y.shape[1] // LANES):
        yc = y[:, c * LANES:(c + 1) * LANES]
        out.append(yc * cm + pltpu.roll(yc, LANES - half, 1) * sp + pltpu.roll(yc, half, 1) * sm)
    return jnp.concatenate(out, axis=1)


def _duplicate_heads(y):
    low = lax.broadcasted_iota(jnp.int32, (1, LANES), 1) < SWA_HEAD_DIM
    out = []
    for c in range(y.shape[1] // LANES):
        yc = y[:, c * LANES:(c + 1) * LANES]
        rolled = pltpu.roll(yc, SWA_HEAD_DIM, 1)
        out.append(jnp.where(low, yc, rolled))
        out.append(jnp.where(low, rolled, yc))
    return jnp.concatenate(out, axis=1)


def _proj_kernel(*refs, nblk, shift, scale, mode):
    x_ref = refs[0]
    w_refs = refs[1:1 + nblk]
    rest = refs[1 + nblk:]
    if mode != "plain":
        cm_ref, sp_ref, sm_ref = rest[:3]
        rest = rest[3:]
    o_ref, wb_ref = rest
    j = pl.program_id(0)
    i = pl.program_id(1)

    @pl.when(i == 0)
    def _():
        _assemble_weight(w_refs, wb_ref, shift, scale)

    y = jnp.dot(x_ref[...], wb_ref[...], preferred_element_type=F32)
    if mode == "plain":
        o_ref[...] = y.astype(o_ref.dtype)
    elif mode == "rope":
        o_ref[...] = _rotate_heads(y, cm_ref[...], sp_ref[...], sm_ref[...]).astype(o_ref.dtype)
    else:
        @pl.when(j == 0)
        def _():
            o_ref[...] = _duplicate_heads(_rotate_heads(y, cm_ref[...], sp_ref[...], sm_ref[...])).astype(o_ref.dtype)

        @pl.when(j != 0)
        def _():
            o_ref[...] = _duplicate_heads(y).astype(o_ref.dtype)


def _project(xb, w, layer, *, first_block, block_stride, tn, n_tiles, shift=0, scale=None, mode="plain",
             tables=None, out_dtype=BF16):
    m, k = xb.shape
    tm = _row_tile(m)
    nblk = tn // LANES + (1 if shift else 0)
    tn_out = 2 * tn if mode == "kv" else tn
    w_specs = _weight_block_specs(
        k, layer, [functools.partial(lambda j, i, b: first_block + j * block_stride + b, b=b) for b in range(nblk)])
    in_specs = [pl.BlockSpec((tm, k), lambda j, i: (i, 0))] + w_specs
    args = [xb] + [w] * nblk
    if mode != "plain":
        in_specs += [pl.BlockSpec((tm, LANES), lambda j, i: (i, 0))] * 3
        args += list(tables)
    return pl.pallas_call(
        functools.partial(_proj_kernel, nblk=nblk, shift=shift, scale=scale, mode=mode),
        grid=(n_tiles, m // tm),
        in_specs=in_specs,
        out_specs=pl.BlockSpec((tm, tn_out), lambda j, i: (i, j)),
        out_shape=jax.ShapeDtypeStruct((m, n_tiles * tn_out), out_dtype),
        scratch_shapes=[pltpu.VMEM((k, tn), BF16)],
        compiler_params=_params("arbitrary", "arbitrary"),
        name="proj_" + mode,
    )(*args)


def _mm_ln_kernel(x_ref, w_ref, res_ref, g_ref, b_ref, of_ref, ob_ref, *scratch, nk, tm, tk, k_valid):
    i = pl.program_id(0)
    k = pl.program_id(1)

    def partial_product():
        w = w_ref[...]
        if k_valid % tk:
            row = lax.broadcasted_iota(jnp.int32, (tk, 1), 0) + k * tk
            w = jnp.where(row < k_valid, w, jnp.zeros_like(w))
        return jnp.dot(x_ref[...], w, preferred_element_type=F32)

    def finish(t):
        t = DEEPNORM_ALPHA * res_ref[...] + t
        mu = jnp.mean(t, axis=-1, keepdims=True)
        d = t - mu
        var = jnp.mean(d * d, axis=-1, keepdims=True)
        y = d * lax.rsqrt(var + LN_EPS) * g_ref[...] + b_ref[...]
        row = lax.broadcasted_iota(jnp.int32, (tm, 1), 0) + i * tm
        y = jnp.where(row >= FRONT_PAD, y, 0.0)
        of_ref[...] = y
        ob_ref[...] = y.astype(BF16)

    if nk == 1:
        finish(partial_product())
        return
    acc_ref, = scratch

    @pl.when(k == 0)
    def _():
        acc_ref[...] = partial_product()

    if nk > 2:
        @pl.when((k > 0) & (k < nk - 1))
        def _():
            acc_ref[...] += partial_product()

    @pl.when(k == nk - 1)
    def _():
        finish(acc_ref[...] + partial_product())


def _matmul_residual_ln(x, w, layer, res, gamma, beta, tk):
    m, kdim = x.shape
    _, k_valid, n = w.shape
    tm = _row_tile(m)
    nk = kdim // tk
    return pl.pallas_call(
        functools.partial(_mm_ln_kernel, nk=nk, tm=tm, tk=tk, k_valid=k_valid),
        grid=(m // tm, nk),
        in_specs=[pl.BlockSpec((tm, tk), lambda i, k: (i, k)),
                  pl.BlockSpec((None, tk, n), lambda i, k: (layer, k, 0)),
                  pl.BlockSpec((tm, n), lambda i, k: (i, 0)),
                  pl.BlockSpec((1, n), lambda i, k: (0, 0)),
                  pl.BlockSpec((1, n), lambda i, k: (0, 0))],
        out_specs=[pl.BlockSpec((tm, n), lambda i, k: (i, 0)),
                   pl.BlockSpec((tm, n), lambda i, k: (i, 0))],
        out_shape=[jax.ShapeDtypeStruct((m, n), F32), jax.ShapeDtypeStruct((m, n), BF16)],
        scratch_shapes=[pltpu.VMEM((tm, n), F32)] if nk > 1 else [],
        compiler_params=_params("arbitrary", "arbitrary"),
        name="proj_ln",
    )(x, w, res, gamma.reshape(1, n), beta.reshape(1, n))


def _ffn_up_kernel(*refs, nblk, tm, tn):
    x_ref = refs[0]
    w_refs = refs[1:1 + nblk]
    cw_g, cw_v, cb_g, cb_v, o_ref, wb_ref, cg_ref, cv_ref = refs[1 + nblk:]
    j = pl.program_id(0)
    i = pl.program_id(1)

    @pl.when(i == 0)
    def _():
        _assemble_weight(w_refs, wb_ref, 0, None)
        cg_ref[...] = jnp.zeros_like(cg_ref)
        cv_ref[...] = jnp.zeros_like(cv_ref)

    x = x_ref[...]
    yg = jnp.dot(x, wb_ref[:, :tn], preferred_element_type=F32)
    yv = jnp.dot(x, wb_ref[:, tn:], preferred_element_type=F32)
    hg = _causal_conv_rows(yg, cg_ref[...], cw_g, cb_g[...], FFN_CONV)
    hv = _causal_conv_rows(yv, cv_ref[...], cw_v, cb_v[...], FFN_CONV)
    cg_ref[...] = yg[tm - SUBLANES:tm]
    cv_ref[...] = yv[tm - SUBLANES:tm]
    col = lax.broadcasted_iota(jnp.int32, (1, tn), 1) + j * tn
    o_ref[...] = jnp.where(col < D_FF, hg * _sigmoid(hg) * hv, 0.0).astype(o_ref.dtype)


def _ffn_up(xb, w_up, layer, conv_w, conv_b):
    m, k = xb.shape
    tm = _row_tile(m)
    tn = FFN_TN
    nj = D_FF_PAD // tn
    nb = tn // LANES
    val0 = D_FF // LANES
    last_blk = 2 * D_FF // LANES - 1
    block_fns = ([functools.partial(lambda j, i, b: j * nb + b, b=b) for b in range(nb)]
                 + [functools.partial(lambda j, i, b: jnp.minimum(val0 + j * nb + b, last_blk), b=b)
                    for b in range(nb)])
    nblk = 2 * nb
    return pl.pallas_call(
        functools.partial(_ffn_up_kernel, nblk=nblk, tm=tm, tn=tn),
        grid=(nj, m // tm),
        in_specs=[pl.BlockSpec((tm, k), lambda j, i: (i, 0))]
        + _weight_block_specs(k, layer, block_fns)
        + [pl.BlockSpec((FFN_CONV, tn), lambda j, i: (0, j)),
           pl.BlockSpec((FFN_CONV, tn), lambda j, i: (0, nj + j)),
           pl.BlockSpec((1, tn), lambda j, i: (0, j)),
           pl.BlockSpec((1, tn), lambda j, i: (0, nj + j))],
        out_specs=pl.BlockSpec((tm, tn), lambda j, i: (i, j)),
        out_shape=jax.ShapeDtypeStruct((m, D_FF_PAD), BF16),
        scratch_shapes=[pltpu.VMEM((k, 2 * tn), BF16), pltpu.VMEM((SUBLANES, tn), F32),
                        pltpu.VMEM((SUBLANES, tn), F32)],
        compiler_params=_params("arbitrary", "arbitrary"),
        name="ffn_up",
    )(xb, *([w_up] * nblk), conv_w, conv_w, conv_b, conv_b)


def _expand_heads(x, rows):
    low = lax.broadcasted_iota(jnp.int32, (1, LANES), 1) < SSD_HEAD_DIM
    b = [jnp.broadcast_to(x[:, h:h + 1], (rows, LANES)) for h in range(SSD_HPG)]
    return jnp.concatenate([jnp.where(low, b[0], b[1]), jnp.where(low, b[2], b[3])], axis=1)


def _ssd_kernel(z_ref, xs_ref, b_ref, c_ref, dt_ref, cwx_ref, cwb_ref, cwc_ref, cbx_ref, cbb_ref, cbc_ref,
                dtb_ref, alog_ref, dsk_ref, nw_ref, o_ref, st_ref, cx_ref, cb_ref, cc_ref):
    g = pl.program_id(0)
    c = pl.program_id(1)
    q = SSD_CHUNK

    @pl.when(c == 0)
    def _():
        st_ref[...] = jnp.zeros_like(st_ref)
        cx_ref[...] = jnp.zeros_like(cx_ref)
        cb_ref[...] = jnp.zeros_like(cb_ref)
        cc_ref[...] = jnp.zeros_like(cc_ref)

    xs_raw = xs_ref[...].astype(F32)
    b_raw = b_ref[...].astype(F32)
    c_raw = c_ref[...].astype(F32)
    xs = _causal_conv_rows(xs_raw, cx_ref[...], cwx_ref, cbx_ref[...], SSD_CONV)
    bm = _causal_conv_rows(b_raw, cb_ref[...], cwb_ref, cbb_ref[...], SSD_CONV)
    cm = _causal_conv_rows(c_raw, cc_ref[...], cwc_ref, cbc_ref[...], SSD_CONV)
    cx_ref[...] = xs_raw[q - SUBLANES:q]
    cb_ref[...] = b_raw[q - SUBLANES:q]
    cc_ref[...] = c_raw[q - SUBLANES:q]
    xs = xs * _sigmoid(xs)
    bm = bm * _sigmoid(bm)
    cm = cm * _sigmoid(cm)

    row = lax.broadcasted_iota(jnp.int32, (q, 1), 0) + c * q
    lane = lax.broadcasted_iota(jnp.int32, (1, LANES), 1)
    dt_raw = pltpu.roll(dt_ref[...], (LANES - SSD_HPG * g) % LANES, 1)
    dt = _softplus(dt_raw + dtb_ref[...])
    dt = jnp.where((row >= FRONT_PAD) & (lane < SSD_HPG), dt, 0.0)
    a = -jnp.exp(alog_ref[...])
    ri = lax.broadcasted_iota(jnp.int32, (q, q), 0)
    ci = lax.broadcasted_iota(jnp.int32, (q, q), 1)
    causal = ci <= ri
    cs = jnp.dot(causal.astype(F32), dt * a, precision=HIGHEST, preferred_element_type=F32)
    cs_t = cs.T
    cs_last = cs[q - 1:q, :]

    xd = xs * _expand_heads(dt, q)
    bm_t = bm.T
    cmb = cm.astype(BF16)
    cb = jnp.dot(cmb, bm_t.astype(BF16), preferred_element_type=F32)
    decayed = []
    for h in range(SSD_HPG):
        seg = cs[:, h:h + 1] - cs_t[h:h + 1, :]
        decayed.append((cb * jnp.exp(jnp.where(causal, seg, -jnp.inf))).astype(BF16))
    lane_w = lax.broadcasted_iota(jnp.int32, (1, SSD_GROUP_W), 1)
    xdb = xd.astype(BF16)
    x_heads = [jnp.where((lane_w >= h * SSD_HEAD_DIM) & (lane_w < (h + 1) * SSD_HEAD_DIM), xdb, jnp.zeros_like(xdb))
               for h in range(SSD_HPG)]
    y = jnp.dot(jnp.concatenate(decayed, axis=1), jnp.concatenate(x_heads, axis=0), preferred_element_type=F32)

    st = st_ref[...]
    y = y + jnp.dot(cmb, st.astype(BF16), preferred_element_type=F32) * _expand_heads(jnp.exp(cs), q)
    to_end = _expand_heads(jnp.exp(cs_last - cs), q)
    st_ref[...] = (st * _expand_heads(jnp.exp(cs_last), 1)
                   + jnp.dot(bm_t.astype(BF16), (xd * to_end).astype(BF16), preferred_element_type=F32))

    y = y + xs * dsk_ref[...]
    z = z_ref[...].astype(F32)
    yg = y * (z * _sigmoid(z))
    yn = yg * lax.rsqrt(jnp.mean(yg * yg, axis=-1, keepdims=True) + RMS_EPS) * nw_ref[...]
    o_ref[...] = yn.astype(o_ref.dtype)


def _ssd_mixer(u_zx, u_small, conv_w, conv_b, dt_bias, a_log, d_skip, norm_w):
    rows = u_zx.shape[0]
    nc = rows // SSD_CHUNK
    q = SSD_CHUNK
    gw = SSD_GROUP_W
    n = SSD_STATE
    xoff = 2048 // gw
    boff = (2048 + 2048) // n
    coff = (2048 + 2048 + 1024) // n
    pad_l = lambda v: jnp.pad(v.reshape(SSD_GROUPS, 1, SSD_HPG), ((0, 0), (0, 0), (0, LANES - SSD_HPG)))
    per_group = lambda blk: pl.BlockSpec((None, 1, blk), lambda g, c: (g, 0, 0))
    return pl.pallas_call(
        _ssd_kernel,
        grid=(SSD_GROUPS, nc),
        in_specs=[pl.BlockSpec((q, gw), lambda g, c: (c, g)),
                  pl.BlockSpec((q, gw), lambda g, c: (c, xoff + g)),
                  pl.BlockSpec((q, n), lambda g, c: (c, boff + g)),
                  pl.BlockSpec((q, n), lambda g, c: (c, coff + g)),
                  pl.BlockSpec((q, LANES), lambda g, c: (c, 0)),
                  pl.BlockSpec((SSD_CONV, gw), lambda g, c: (0, g)),
                  pl.BlockSpec((SSD_CONV, n), lambda g, c: (0, 2048 // n + g)),
                  pl.BlockSpec((SSD_CONV, n), lambda g, c: (0, (2048 + 1024) // n + g)),
                  pl.BlockSpec((1, gw), lambda g, c: (0, g)),
                  pl.BlockSpec((1, n), lambda g, c: (0, 2048 // n + g)),
                  pl.BlockSpec((1, n), lambda g, c: (0, (2048 + 1024) // n + g)),
                  per_group(LANES), per_group(LANES), per_group(gw), per_group(gw)],
        out_specs=pl.BlockSpec((q, gw), lambda g, c: (c, g)),
        out_shape=jax.ShapeDtypeStruct((rows, SSD_GROUPS * gw), BF16),
        scratch_shapes=[pltpu.VMEM((n, gw), F32), pltpu.VMEM((SUBLANES, gw), F32),
                        pltpu.VMEM((SUBLANES, n), F32), pltpu.VMEM((SUBLANES, n), F32)],
        compiler_params=_params("arbitrary", "arbitrary"),
        name="ssd",
    )(u_zx, u_zx, u_zx, u_zx, u_small, conv_w, conv_w, conv_w, conv_b.reshape(1, -1), conv_b.reshape(1, -1),
      conv_b.reshape(1, -1), pad_l(dt_bias), pad_l(a_log),
      jnp.repeat(d_skip, SSD_HEAD_DIM).reshape(SSD_GROUPS, 1, gw), norm_w.reshape(SSD_GROUPS, 1, gw))


def _gla_kernel(q_ref, k_ref, v_ref, r_ref, glr_ref, w2_ref, gb_ref, nw_ref, o_ref, st_ref):
    c = pl.program_id(1)
    qc = GLA_CHUNK
    sub = GLA_SUB

    @pl.when(c == 0)
    def _():
        st_ref[...] = jnp.zeros_like(st_ref)

    q = q_ref[...].astype(F32) * (GLA_HEAD_DK ** -0.5)
    k = k_ref[...].astype(F32)
    vb = v_ref[...]
    pre = jnp.dot(glr_ref[...], w2_ref[...], precision=HIGHEST, preferred_element_type=F32) + gb_ref[...]
    g = _log_sigmoid(pre) * (1.0 / GLA_GATE_TAU)
    row = lax.broadcasted_iota(jnp.int32, (qc, 1), 0) + c * qc
    g = jnp.where(row >= FRONT_PAD, g, 0.0)
    ri = lax.broadcasted_iota(jnp.int32, (qc, qc), 0)
    ci = lax.broadcasted_iota(jnp.int32, (qc, qc), 1)
    gc = jnp.dot((ci <= ri).astype(F32), g, precision=HIGHEST, preferred_element_type=F32)
    g_last = gc[qc - 1:qc, :]

    lane_j = lax.broadcasted_iota(jnp.int32, (sub, qc), 1)
    row_i = lax.broadcasted_iota(jnp.int32, (sub, 1), 0)
    nt = (((1,), (1,)), ((), ()))
    a_rows = []
    for blk in range(qc // sub):
        lo = blk * sub
        q_b = q[lo:lo + sub]
        g_b = gc[lo:lo + sub]
        a_blk = jnp.zeros((sub, qc), F32)
        for j in range(sub):
            k_j = k[lo + j:lo + j + 1, :]
            g_j = gc[lo + j:lo + j + 1, :]
            s_j = jnp.sum(q_b * k_j * jnp.exp(jnp.minimum(g_b - g_j, 0.0)), axis=1, keepdims=True)
            a_blk = jnp.where(lane_j == lo + j, jnp.where(row_i >= j, s_j, 0.0), a_blk)
        if blk > 0:
            g_ref0 = gc[lo:lo + 1, :]
            q_t = (q_b * jnp.exp(g_b - g_ref0)).astype(BF16)
            k_t = (k * jnp.exp(jnp.minimum(g_ref0 - gc, 0.0))).astype(BF16)
            off = lax.dot_general(q_t, k_t, nt, preferred_element_type=F32)
            a_blk = jnp.where(lane_j < lo, off, a_blk)
        a_rows.append(a_blk)
    a = jnp.concatenate(a_rows, axis=0)
    o = jnp.dot(a.astype(BF16), vb, preferred_element_type=F32)

    st = st_ref[...]
    o = o + lax.dot_general((q * jnp.exp(gc)).astype(BF16), st.astype(BF16), nt, preferred_element_type=F32)
    k_end = (k * jnp.exp(g_last - gc)).astype(BF16)
    st_ref[...] = st * jnp.exp(g_last) + lax.dot_general(vb, k_end, (((0,), (0,)), ((), ())),
                                                         preferred_element_type=F32)

    on = o * lax.rsqrt(jnp.mean(o * o, axis=-1, keepdims=True) + RMS_EPS) * nw_ref[...]
    r = r_ref[...].astype(F32)
    o_ref[...] = (on * (r * _sigmoid(r))).astype(o_ref.dtype)


def _gla_mixer(u_qkv, u_r, u_small, gate_w2, gate_b, norm_w):
    rows = u_qkv.shape[0]
    qc = GLA_CHUNK
    dk, dv = GLA_HEAD_DK, GLA_HEAD_DV
    koff = 1024 // dk
    voff = 2048 // dv
    glr_lane = HYB_GLR_COL % LANES
    w2 = jnp.pad(gate_w2, ((glr_lane, LANES - GLA_GATE_RANK - glr_lane), (0, 0)))
    return pl.pallas_call(
        _gla_kernel,
        grid=(GLA_HEADS, rows // qc),
        in_specs=[pl.BlockSpec((qc, dk), lambda h, c: (c, h)),
                  pl.BlockSpec((qc, dk), lambda h, c: (c, koff + h)),
                  pl.BlockSpec((qc, dv), lambda h, c: (c, voff + h)),
                  pl.BlockSpec((qc, dv), lambda h, c: (c, h)),
                  pl.BlockSpec((qc, LANES), lambda h, c: (c, 1)),
                  pl.BlockSpec((LANES, dk), lambda h, c: (0, h)),
                  pl.BlockSpec((1, dk), lambda h, c: (0, h)),
                  pl.BlockSpec((1, dv), lambda h, c: (0, h))],
        out_specs=pl.BlockSpec((qc, dv), lambda h, c: (c, h)),
        out_shape=jax.ShapeDtypeStruct((rows, GLA_HEADS * dv), BF16),
        scratch_shapes=[pltpu.VMEM((dv, dk), F32)],
        compiler_params=_params("arbitrary", "arbitrary"),
        name="gla",
    )(u_qkv, u_qkv, u_qkv, u_r, u_small, w2, gate_b.reshape(1, -1), norm_w.reshape(1, -1))


def _rope_kernel(freq_ref, cm_ref, sp_ref, sm_ref, *, tm):
    i = pl.program_id(0)
    row = lax.broadcasted_iota(jnp.int32, (tm, LANES), 0) + i * tm
    lane = lax.broadcasted_iota(jnp.int32, (tm, LANES), 1) % SWA_HEAD_DIM
    ang = (row - FRONT_PAD).astype(F32) * freq_ref[...]
    cos = jnp.cos(ang)
    sin = jnp.sin(ang)
    half = ROPE_DIM // 2
    cm_ref[...] = cos
    sp_ref[...] = jnp.where(lane < half, -sin, 0.0)
    sm_ref[...] = jnp.where((lane >= half) & (lane < ROPE_DIM), sin, 0.0)


def _rope_tables(rows):
    half = ROPE_DIM // 2
    inv_freq = ROPE_THETA ** (-jnp.arange(half, dtype=F32) / half)
    per_head = jnp.concatenate([inv_freq, inv_freq, jnp.zeros((SWA_HEAD_DIM - ROPE_DIM,), F32)])
    freq = jnp.tile(per_head, LANES // SWA_HEAD_DIM).reshape(1, LANES)
    tm = _row_tile(rows)
    shp = jax.ShapeDtypeStruct((rows, LANES), F32)
    spec = pl.BlockSpec((tm, LANES), lambda i: (i, 0))
    return pl.pallas_call(
        functools.partial(_rope_kernel, tm=tm),
        grid=(rows // tm,),
        in_specs=[pl.BlockSpec((1, LANES), lambda i: (0, 0))],
        out_specs=[spec, spec, spec],
        out_shape=[shp, shp, shp],
        compiler_params=_params("arbitrary"),
        name="rope_tables",
    )(freq)


def _swa_kernel(sink_ref, q_ref, kp_ref, kc_ref, km_ref, vp_ref, vc_ref, vm_ref, o_ref):
    g = pl.program_id(0)
    n = pl.program_id(1)
    w = SWA_WINDOW
    meta_lo = FRONT_PAD
    lane = lax.broadcasted_iota(jnp.int32, (1, LANES), 1)
    low = lane < SWA_HEAD_DIM
    q = q_ref[...]
    zero = jnp.zeros((w, LANES), q.dtype)
    qa, qb = q[:, :LANES], q[:, LANES:]
    qs = jnp.concatenate([jnp.where(low, qa, zero), jnp.where(low, zero, qa),
                          jnp.where(low, qb, zero), jnp.where(low, zero, qb)], axis=0)
    kk = jnp.concatenate([kp_ref[...], kc_ref[...], km_ref[...]], axis=0)
    s = lax.dot_general(qs, kk, (((1,), (1,)), ((), ())), preferred_element_type=F32)

    i = lax.broadcasted_iota(jnp.int32, (SWA_GROUP * w, 1), 0) % w
    col = lax.broadcasted_iota(jnp.int32, (1, 3 * w), 1)
    cj = col % w
    seg = col // w
    valid_prev = (seg == 0) & (cj > i) & (n >= 2)
    valid_cur = (seg == 1) & (cj <= i) & ((n >= 1) | (cj >= meta_lo))
    valid_meta = (seg == 2) & (cj >= meta_lo) & (n >= 1)
    s = jnp.where(valid_prev | valid_cur | valid_meta, s, -jnp.inf)

    hrow = lax.broadcasted_iota(jnp.int32, (SWA_GROUP * w, 1), 0) // w
    sink = jnp.zeros((SWA_GROUP * w, 1), F32)
    for h in range(SWA_GROUP):
        sink = jnp.where(hrow == h, sink_ref[g * SWA_GROUP + h], sink)
    m = jnp.maximum(jnp.max(s, axis=-1, keepdims=True), sink)
    p = jnp.exp(s - m)
    denom = jnp.sum(p, axis=-1, keepdims=True) + jnp.exp(sink - m)
    vv = jnp.concatenate([vp_ref[...], vc_ref[...], vm_ref[...]], axis=0)
    o = jnp.dot(p.astype(BF16), vv, preferred_element_type=F32) / denom
    oa = jnp.where(low, o[0:w], o[w:2 * w])
    ob = jnp.where(low, o[2 * w:3 * w], o[3 * w:4 * w])
    o_ref[...] = jnp.concatenate([oa, ob], axis=1).astype(o_ref.dtype)


def _swa_attention(q, kv, sinks):
    rows = q.shape[0]
    w = SWA_WINDOW
    qw = SWA_GROUP * SWA_HEAD_DIM
    voff = SWA_KV_HEADS
    kvspec = lambda off, which: pl.BlockSpec(
        (w, LANES), {"prev": lambda g, n: (jnp.maximum(n - 1, 0), off + g),
                     "cur": lambda g, n: (n, off + g),
                     "meta": lambda g, n: (0, off + g)}[which])
    return pl.pallas_call(
        _swa_kernel,
        grid=(SWA_KV_HEADS, rows // w),
        in_specs=[pl.BlockSpec(memory_space=pltpu.SMEM),
                  pl.BlockSpec((w, qw), lambda g, n: (n, g)),
                  kvspec(0, "prev"), kvspec(0, "cur"), kvspec(0, "meta"),
                  kvspec(voff, "prev"), kvspec(voff, "cur"), kvspec(voff, "meta")],
        out_specs=pl.BlockSpec((w, qw), lambda g, n: (n, g)),
        out_shape=jax.ShapeDtypeStruct((rows, SWA_Q_HEADS * SWA_HEAD_DIM), BF16),
        compiler_params=_params("arbitrary", "arbitrary"),
        name="swa",
    )(sinks, q, kv, kv, kv, kv, kv, kv)


def _pad_halves(t):
    pad = lambda a: jnp.pad(a, ((0, 0), (0, D_FF_PAD - D_FF)))
    return jnp.concatenate([pad(t[:, :D_FF]), pad(t[:, D_FF:])], axis=1)


def _trunk(x, meta_tokens, hyb_w_in, hyb_conv_w, hyb_conv_b, ssd_dt_bias, ssd_a_log, ssd_d, ssd_norm_w,
           gla_gate_w2, gla_gate_b, gla_norm_w, hyb_w_out, swa_w_qkv, swa_sinks, swa_w_out,
           ffn_w_up, ffn_conv_w, ffn_conv_b, ffn_w_down, ln_mix_g, ln_mix_b, ln_ffn_g, ln_ffn_b):
    seq = x.shape[0]
    rows = FRONT_PAD + N_META + seq
    h = jnp.concatenate([jnp.zeros((FRONT_PAD, D_MODEL), F32), meta_tokens.astype(F32), x], axis=0)
    hb = h.astype(BF16)
    tables = _rope_tables(rows)
    hyb_w_out_b = hyb_w_out.astype(BF16)
    swa_w_out_b = swa_w_out.astype(BF16)
    ffn_w_down_b = ffn_w_down.astype(BF16)
    for layer in range(DEPTH):
        j = layer // 2
        if layer % 2 == 0:
            u_zx = _project(hb, hyb_w_in, j, first_block=0, block_stride=4, tn=512, n_tiles=12)
            u_qkv = _project(hb, hyb_w_in, j, first_block=HYB_QKV_COL // LANES, block_stride=4, tn=512, n_tiles=8,
                             shift=HYB_QKV_COL % LANES)
            u_r = _project(hb, hyb_w_in, j, first_block=HYB_R_COL // LANES, block_stride=4, tn=512, n_tiles=4,
                           shift=HYB_R_COL % LANES)
            u_small = _project(hb, hyb_w_in, j, first_block=HYB_DT_COL // LANES,
                               block_stride=(HYB_GLR_COL - HYB_DT_COL) // LANES, tn=LANES, n_tiles=2, out_dtype=F32)
            y_ssd = _ssd_mixer(u_zx, u_small, hyb_conv_w[j], hyb_conv_b[j], ssd_dt_bias[j], ssd_a_log[j],
                               ssd_d[j], ssd_norm_w[j])
            y_gla = _gla_mixer(u_qkv, u_r, u_small, gla_gate_w2[j], gla_gate_b[j], gla_norm_w[j])
            mix_in = jnp.concatenate([y_ssd, y_gla], axis=1)
            h, hb = _matmul_residual_ln(mix_in, hyb_w_out_b, j, h, ln_mix_g[layer], ln_mix_b[layer], tk=1024)
        else:
            q = _project(hb, swa_w_qkv, j, first_block=0, block_stride=4, tn=512, n_tiles=4,
                         scale=SWA_HEAD_DIM ** -0.5, mode="rope", tables=tables)
            kv = _project(hb, swa_w_qkv, j, first_block=SWA_Q_HEADS * SWA_HEAD_DIM // LANES, block_stride=4, tn=512,
                          n_tiles=2, mode="kv", tables=tables)
            mix_in = _swa_attention(q, kv, swa_sinks[j])
            h, hb = _matmul_residual_ln(mix_in, swa_w_out_b, j, h, ln_mix_g[layer], ln_mix_b[layer], tk=1024)
        act = _ffn_up(hb, ffn_w_up, layer, _pad_halves(ffn_conv_w[layer]),
                      _pad_halves(ffn_conv_b[layer].reshape(1, -1)))
        h, hb = _matmul_residual_ln(act, ffn_w_down_b, layer, h, ln_ffn_g[layer], ln_ffn_b[layer], tk=FFN_DOWN_TK)
    return h[FRONT_PAD + N_META:]


def kernel(x, meta_tokens, hyb_w_in, hyb_conv_w, hyb_conv_b, ssd_dt_bias, ssd_a_log, ssd_d, ssd_norm_w,
           gla_gate_w2, gla_gate_b, gla_norm_w, hyb_w_out, swa_w_qkv, swa_sinks, swa_w_out,
           ffn_w_up, ffn_conv_w, ffn_conv_b, ffn_w_down, ln_mix_g, ln_mix_b, ln_ffn_g, ln_ffn_b):
    params = (meta_tokens, hyb_w_in, hyb_conv_w, hyb_conv_b, ssd_dt_bias, ssd_a_log, ssd_d, ssd_norm_w,
              gla_gate_w2, gla_gate_b, gla_norm_w, hyb_w_out, swa_w_qkv, swa_sinks, swa_w_out,
              ffn_w_up, ffn_conv_w, ffn_conv_b, ffn_w_down, ln_mix_g, ln_mix_b, ln_ffn_g, ln_ffn_b)
    return jnp.stack([_trunk(x[b], *params) for b in range(x.shape[0])], axis=0)
```

```python
import functools

import jax
import jax.numpy as jnp
from jax import lax
from jax.experimental import pallas as pl
from jax.experimental.pallas import tpu as pltpu

F32 = jnp.float32
BF16 = jnp.bfloat16
HIGHEST = lax.Precision.HIGHEST

D_MODEL = 2048
DEPTH = 4
N_META = 16
LN_EPS = 1e-5
RMS_EPS = 1e-6
DEEPNORM_ALPHA = (2.0 * DEPTH) ** 0.25

SSD_HEAD_DIM = 64
SSD_HEADS = 32
SSD_GROUPS = 8
SSD_HPG = 4
SSD_STATE = 128
SSD_CONV = 4
SSD_CHUNK = 128
SSD_GROUP_W = SSD_HPG * SSD_HEAD_DIM

GLA_HEADS = 4
GLA_HEAD_DK = 256
GLA_HEAD_DV = 512
GLA_GATE_RANK = 16
GLA_GATE_TAU = 16.0
GLA_CHUNK = 64
GLA_SUB = 16
GLA_SAFE_DECAY = 80.0

SWA_HEAD_DIM = 64
SWA_Q_HEADS = 32
SWA_KV_HEADS = 8
SWA_GROUP = 4
SWA_WINDOW = 128
ROPE_THETA = 500000.0
ROPE_DIM = 16

D_FF = 5504
FFN_CONV = 3

LANES = 128
SUBLANES = 8
FRONT_PAD = SSD_CHUNK - N_META
D_FF_PAD = 5632
FFN_TN = 512
FFN_DOWN_TK = 1408
VMEM_LIMIT = 56 * 1024 * 1024

HYB_DT_COL = 6144
HYB_QKV_COL = 6176
HYB_GLR_COL = 10272
HYB_R_COL = 10288


def _row_tile(rows):
    for t in (640, 512, 384, 256, 128):
        if rows % t == 0:
            return t
    raise ValueError(f"row count {rows} is not a multiple of 128")


def _params(*sem):
    return pltpu.CompilerParams(dimension_semantics=sem, vmem_limit_bytes=VMEM_LIMIT)


def _sigmoid(x):
    return 1.0 / (1.0 + jnp.exp(-x))


def _softplus(x):
    return jnp.maximum(x, 0.0) + jnp.log(1.0 + jnp.exp(-jnp.abs(x)))


def _log_sigmoid(x):
    return jnp.minimum(x, 0.0) - jnp.log(1.0 + jnp.exp(-jnp.abs(x)))


def _causal_conv_rows(y, carry, w_ref, b_row, taps):
    top = jnp.concatenate([carry, y[0:SUBLANES]], axis=0)
    w_last = w_ref[taps - 1:taps, :]
    acc = b_row + w_last * y
    acc_top = b_row + w_last * y[0:SUBLANES]
    for s in range(1, taps):
        wk = w_ref[taps - 1 - s:taps - s, :]
        acc = acc + wk * pltpu.roll(y, s, 0)
        acc_top = acc_top + wk * pltpu.roll(top, s, 0)[SUBLANES:2 * SUBLANES]
    return jnp.concatenate([acc_top, acc[SUBLANES:]], axis=0)


def _assemble_weight(w_refs, wb_ref, shift, scale):
    k, tn = wb_ref.shape
    chunk = 256
    for r in range(0, k, chunk):
        w = jnp.concatenate([wr[r:r + chunk, :] for wr in w_refs], axis=1)
        if shift:
            w = pltpu.roll(w, w.shape[1] - shift, 1)
        w = w[:, :tn]
        if scale is not None:
            w = w * scale
        wb_ref[r:r + chunk, :] = w.astype(BF16)


def _weight_block_specs(k, layer, block_fns):
    return [pl.BlockSpec((None, k, LANES), functools.partial(lambda *ids, fn: (layer, 0, fn(*ids)), fn=fn))
            for fn in block_fns]


def _rotate_heads(y, cm, sp, sm):
    half = ROPE_DIM // 2
    out = []
    for c in range(y.shape[1] // LANES):
        yc = y[:, c * LANES:(c + 1) * LANES]
        out.append(yc * cm + pltpu.roll(yc, LANES - half, 1) * sp + pltpu.roll(yc, half, 1) * sm)
    return jnp.concatenate(out, axis=1)


def _duplicate_heads(y):
    low = lax.broadcasted_iota(jnp.int32, (1, LANES), 1) < SWA_HEAD_DIM
    out = []
    for c in range(y.shape[1] // LANES):
        yc = y[:, c * LANES:(c + 1) * LANES]
        rolled = pltpu.roll(yc, SWA_HEAD_DIM, 1)
        out.append(jnp.where(low, yc, rolled))
        out.append(jnp.where(low, rolled, yc))
    return jnp.concatenate(out, axis=1)


def _proj_kernel(*refs, nblk, shift, scale, mode):
    x_ref = refs[0]
    w_refs = refs[1:1 + nblk]
    rest = refs[1 + nblk:]
    if mode != "plain":
        cm_ref, sp_ref, sm_ref = rest[:3]
        rest = rest[3:]
    o_ref, wb_ref = rest
    j = pl.program_id(0)
    i = pl.program_id(1)

    @pl.when(i == 0)
    def _():
        _assemble_weight(w_refs, wb_ref, shift, scale)

    y = jnp.dot(x_ref[...], wb_ref[...], preferred_element_type=F32)
    if mode == "plain":
        o_ref[...] = y.astype(o_ref.dtype)
    elif mode == "rope":
        o_ref[...] = _rotate_heads(y, cm_ref[...], sp_ref[...], sm_ref[...]).astype(o_ref.dtype)
    else:
        @pl.when(j == 0)
        def _():
            o_ref[...] = _duplicate_heads(_rotate_heads(y, cm_ref[...], sp_ref[...], sm_ref[...])).astype(o_ref.dtype)

        @pl.when(j != 0)
        def _():
            o_ref[...] = _duplicate_heads(y).astype(o_ref.dtype)


def _project(xb, w, layer, *, first_block, block_stride, tn, n_tiles, shift=0, scale=None, mode="plain",
             tables=None, out_dtype=BF16):
    m, k = xb.shape
    tm = _row_tile(m)
    nblk = tn // LANES + (1 if shift else 0)
    tn_out = 2 * tn if mode == "kv" else tn
    w_specs = _weight_block_specs(
        k, layer, [functools.partial(lambda j, i, b: first_block + j * block_stride + b, b=b) for b in range(nblk)])
    in_specs = [pl.BlockSpec((tm, k), lambda j, i: (i, 0))] + w_specs
    args = [xb] + [w] * nblk
    if mode != "plain":
        in_specs += [pl.BlockSpec((tm, LANES), lambda j, i: (i, 0))] * 3
        args += list(tables)
    return pl.pallas_call(
        functools.partial(_proj_kernel, nblk=nblk, shift=shift, scale=scale, mode=mode),
        grid=(n_tiles, m // tm),
        in_specs=in_specs,
        out_specs=pl.BlockSpec((tm, tn_out), lambda j, i: (i, j)),
        out_shape=jax.ShapeDtypeStruct((m, n_tiles * tn_out), out_dtype),
        scratch_shapes=[pltpu.VMEM((k, tn), BF16)],
        compiler_params=_params("arbitrary", "arbitrary"),
        name="proj_" + mode,
    )(*args)


def _mm_ln_kernel(*refs, src_of_step, nk, tm, tk, k_valid):
    nsrc = max(src_of_step) + 1
    x_refs = refs[:nsrc]
    w_ref, res_ref, g_ref, b_ref, of_ref, ob_ref = refs[nsrc:nsrc + 6]
    scratch = refs[nsrc + 6:]
    i = pl.program_id(0)
    k = pl.program_id(1)

    def partial_product(x_ref):
        w = w_ref[...]
        if k_valid % tk:
            row = lax.broadcasted_iota(jnp.int32, (tk, 1), 0) + k * tk
            w = jnp.where(row < k_valid, w, jnp.zeros_like(w))
        return jnp.dot(x_ref[...], w, preferred_element_type=F32)

    def finish(t):
        t = DEEPNORM_ALPHA * res_ref[...] + t
        mu = jnp.mean(t, axis=-1, keepdims=True)
        d = t - mu
        var = jnp.mean(d * d, axis=-1, keepdims=True)
        y = d * lax.rsqrt(var + LN_EPS) * g_ref[...] + b_ref[...]
        row = lax.broadcasted_iota(jnp.int32, (tm, 1), 0) + i * tm
        y = jnp.where(row >= FRONT_PAD, y, 0.0)
        of_ref[...] = y
        ob_ref[...] = y.astype(BF16)

    if nk == 1:
        finish(partial_product(x_refs[0]))
        return
    acc_ref, = scratch
    for step in range(nk):
        @pl.when(k == step)
        def _(step=step):
            part = partial_product(x_refs[src_of_step[step]])
            if step == 0:
                acc_ref[...] = part
            elif step < nk - 1:
                acc_ref[...] += part
            else:
                finish(acc_ref[...] + part)


def _matmul_residual_ln(xs, w, layer, res, gamma, beta, tk):
    m = xs[0].shape[0]
    _, k_valid, n = w.shape
    tm = _row_tile(m)
    steps = [x.shape[1] // tk for x in xs]
    first = [sum(steps[:s]) for s in range(len(xs))]
    nk = sum(steps)
    src_of_step = tuple(s for s, cnt in enumerate(steps) for _ in range(cnt))
    x_specs = [pl.BlockSpec((tm, tk), functools.partial(
        lambda i, k, k0, cnt: (i, jnp.clip(k - k0, 0, cnt - 1)), k0=first[s], cnt=steps[s])) for s in range(len(xs))]
    return pl.pallas_call(
        functools.partial(_mm_ln_kernel, src_of_step=src_of_step, nk=nk, tm=tm, tk=tk, k_valid=k_valid),
        grid=(m // tm, nk),
        in_specs=x_specs + [pl.BlockSpec((None, tk, n), lambda i, k: (layer, k, 0)),
                            pl.BlockSpec((tm, n), lambda i, k: (i, 0)),
                            pl.BlockSpec((1, n), lambda i, k: (0, 0)),
                            pl.BlockSpec((1, n), lambda i, k: (0, 0))],
        out_specs=[pl.BlockSpec((tm, n), lambda i, k: (i, 0)),
                   pl.BlockSpec((tm, n), lambda i, k: (i, 0))],
        out_shape=[jax.ShapeDtypeStruct((m, n), F32), jax.ShapeDtypeStruct((m, n), BF16)],
        scratch_shapes=[pltpu.VMEM((tm, n), F32)] if nk > 1 else [],
        compiler_params=_params("arbitrary", "arbitrary"),
        name="proj_ln",
    )(*xs, w, res, gamma.reshape(1, n), beta.reshape(1, n))


def _ffn_up_kernel(*refs, nblk, tm, tn):
    x_ref = refs[0]
    w_refs = refs[1:1 + nblk]
    cw_g, cw_v, cb_g, cb_v, o_ref, wb_ref, cg_ref, cv_ref = refs[1 + nblk:]
    j = pl.program_id(0)
    i = pl.program_id(1)

    @pl.when(i == 0)
    def _():
        _assemble_weight(w_refs, wb_ref, 0, None)
        cg_ref[...] = jnp.zeros_like(cg_ref)
        cv_ref[...] = jnp.zeros_like(cv_ref)

    x = x_ref[...]
    yg = jnp.dot(x, wb_ref[:, :tn], preferred_element_type=F32)
    yv = jnp.dot(x, wb_ref[:, tn:], preferred_element_type=F32)
    hg = _causal_conv_rows(yg, cg_ref[...], cw_g, cb_g[...], FFN_CONV)
    hv = _causal_conv_rows(yv, cv_ref[...], cw_v, cb_v[...], FFN_CONV)
    cg_ref[...] = yg[tm - SUBLANES:tm]
    cv_ref[...] = yv[tm - SUBLANES:tm]
    col = lax.broadcasted_iota(jnp.int32, (1, tn), 1) + j * tn
    o_ref[...] = jnp.where(col < D_FF, hg * _sigmoid(hg) * hv, 0.0).astype(o_ref.dtype)


def _ffn_up(xb, w_up, layer, conv_w, conv_b):
    m, k = xb.shape
    tm = _row_tile(m)
    tn = FFN_TN
    nj = D_FF_PAD // tn
    nb = tn // LANES
    val0 = D_FF // LANES
    last_blk = 2 * D_FF // LANES - 1
    block_fns = ([functools.partial(lambda j, i, b: j * nb + b, b=b) for b in range(nb)]
                 + [functools.partial(lambda j, i, b: jnp.minimum(val0 + j * nb + b, last_blk), b=b)
                    for b in range(nb)])
    nblk = 2 * nb
    return pl.pallas_call(
        functools.partial(_ffn_up_kernel, nblk=nblk, tm=tm, tn=tn),
        grid=(nj, m // tm),
        in_specs=[pl.BlockSpec((tm, k), lambda j, i: (i, 0))]
        + _weight_block_specs(k, layer, block_fns)
        + [pl.BlockSpec((FFN_CONV, tn), lambda j, i: (0, j)),
           pl.BlockSpec((FFN_CONV, tn), lambda j, i: (0, nj + j)),
           pl.BlockSpec((1, tn), lambda j, i: (0, j)),
           pl.BlockSpec((1, tn), lambda j, i: (0, nj + j))],
        out_specs=pl.BlockSpec((tm, tn), lambda j, i: (i, j)),
        out_shape=jax.ShapeDtypeStruct((m, D_FF_PAD), BF16),
        scratch_shapes=[pltpu.VMEM((k, 2 * tn), BF16), pltpu.VMEM((SUBLANES, tn), F32),
                        pltpu.VMEM((SUBLANES, tn), F32)],
        compiler_params=_params("arbitrary", "arbitrary"),
        name="ffn_up",
    )(xb, *([w_up] * nblk), conv_w, conv_w, conv_b, conv_b)


def _expand_heads(x, rows, g):
    low = lax.broadcasted_iota(jnp.int32, (1, LANES), 1) < SSD_HEAD_DIM
    h0 = g * SSD_HPG
    b = [jnp.broadcast_to(x[:, h0 + h:h0 + h + 1], (rows, LANES)) for h in range(SSD_HPG)]
    return jnp.concatenate([jnp.where(low, b[0], b[1]), jnp.where(low, b[2], b[3])], axis=1)


def _ssd_kernel(z_ref, x_ref, bc_ref, dt_ref, cwx_ref, cwbc_ref, cbx_ref, cbbc_ref, dtb_ref, alog_ref, dsk_ref,
                nw_ref, o_ref, st_ref, cx_ref, cbc_ref):
    c = pl.program_id(0)
    q = SSD_CHUNK
    gw = SSD_GROUP_W
    n = SSD_STATE
    groups = range(SSD_GROUPS)

    @pl.when(c == 0)
    def _():
        st_ref[...] = jnp.zeros_like(st_ref)
        cx_ref[...] = jnp.zeros_like(cx_ref)
        cbc_ref[...] = jnp.zeros_like(cbc_ref)

    row = lax.broadcasted_iota(jnp.int32, (q, 1), 0) + c * q
    lane = lax.broadcasted_iota(jnp.int32, (1, LANES), 1)
    dt = _softplus(dt_ref[...] + dtb_ref[...])
    dt = jnp.where((row >= FRONT_PAD) & (lane < SSD_HEADS), dt, 0.0)
    a = -jnp.exp(alog_ref[...])
    ri = lax.broadcasted_iota(jnp.int32, (q, q), 0)
    ci = lax.broadcasted_iota(jnp.int32, (q, q), 1)
    causal = ci <= ri
    cs = jnp.dot(causal.astype(F32), dt * a, precision=HIGHEST, preferred_element_type=F32)
    cs_t = cs.T
    cs_last = cs[q - 1:q, :]
    from_start = jnp.exp(cs)
    to_end = jnp.exp(cs_last - cs)
    total = jnp.exp(cs_last)

    def conv_silu(raw_ref, carry_ref, w_ref, b_ref, lo, width):
        raw = raw_ref[:, lo:lo + width].astype(F32)
        y = _causal_conv_rows(raw, carry_ref[:, lo:lo + width], w_ref.at[:, lo:lo + width], b_ref[:, lo:lo + width],
                              SSD_CONV)
        carry_ref[:, lo:lo + width] = raw[q - SUBLANES:q]
        return y * _sigmoid(y)

    xs = [conv_silu(x_ref, cx_ref, cwx_ref, cbx_ref, g * gw, gw) for g in groups]
    bm_t = [conv_silu(bc_ref, cbc_ref, cwbc_ref, cbbc_ref, g * n, n).T.astype(BF16) for g in groups]
    cmb = [conv_silu(bc_ref, cbc_ref, cwbc_ref, cbbc_ref, SSD_GROUPS * n + g * n, n).astype(BF16) for g in groups]
    cb = [jnp.dot(cmb[g], bm_t[g], preferred_element_type=F32) for g in groups]

    lane_w = lax.broadcasted_iota(jnp.int32, (1, gw), 1)
    xd = [xs[g] * _expand_heads(dt, q, g) for g in groups]
    ys = []
    for g in groups:
        decayed = []
        for h in range(SSD_HPG):
            hh = g * SSD_HPG + h
            seg = cs[:, hh:hh + 1] - cs_t[hh:hh + 1, :]
            decayed.append((cb[g] * jnp.exp(jnp.where(causal, seg, -jnp.inf))).astype(BF16))
        xdb = xd[g].astype(BF16)
        x_heads = [jnp.where((lane_w >= h * SSD_HEAD_DIM) & (lane_w < (h + 1) * SSD_HEAD_DIM), xdb,
                             jnp.zeros_like(xdb)) for h in range(SSD_HPG)]
        ys.append(jnp.dot(jnp.concatenate(decayed, axis=1), jnp.concatenate(x_heads, axis=0),
                          preferred_element_type=F32))
    for g in groups:
        st = st_ref[g]
        ys[g] = ys[g] + (jnp.dot(cmb[g], st.astype(BF16), preferred_element_type=F32)
                         * _expand_heads(from_start, q, g))
        st_ref[g] = (st * _expand_heads(total, 1, g)
                     + jnp.dot(bm_t[g], (xd[g] * _expand_heads(to_end, q, g)).astype(BF16),
                               preferred_element_type=F32))
    for g in groups:
        cols = slice(g * gw, (g + 1) * gw)
        y = ys[g] + xs[g] * dsk_ref[:, cols]
        z = z_ref[:, cols].astype(F32)
        yg = y * (z * _sigmoid(z))
        yn = yg * lax.rsqrt(jnp.mean(yg * yg, axis=-1, keepdims=True) + RMS_EPS) * nw_ref[:, cols]
        o_ref[:, cols] = yn.astype(o_ref.dtype)


def _ssd_mixer(u_zx, u_small, conv_w, conv_b, dt_bias, a_log, d_skip, norm_w):
    rows = u_zx.shape[0]
    q = SSD_CHUNK
    d = SSD_GROUPS * SSD_GROUP_W
    pad_l = lambda v: jnp.pad(v.reshape(1, SSD_HEADS), ((0, 0), (0, LANES - SSD_HEADS)))
    whole = lambda r, w, j: pl.BlockSpec((r, w), lambda c: (0, j))
    return pl.pallas_call(
        _ssd_kernel,
        grid=(rows // q,),
        in_specs=[pl.BlockSpec((q, d), lambda c: (c, 0)),
                  pl.BlockSpec((q, d), lambda c: (c, 1)),
                  pl.BlockSpec((q, d), lambda c: (c, 2)),
                  pl.BlockSpec((q, LANES), lambda c: (c, 0)),
                  whole(SSD_CONV, d, 0), whole(SSD_CONV, d, 1), whole(1, d, 0), whole(1, d, 1),
                  whole(1, LANES, 0), whole(1, LANES, 0), whole(1, d, 0), whole(1, d, 0)],
        out_specs=pl.BlockSpec((q, d), lambda c: (c, 0)),
        out_shape=jax.ShapeDtypeStruct((rows, d), BF16),
        scratch_shapes=[pltpu.VMEM((SSD_GROUPS, SSD_STATE, SSD_GROUP_W), F32), pltpu.VMEM((SUBLANES, d), F32),
                        pltpu.VMEM((SUBLANES, d), F32)],
        compiler_params=_params("arbitrary"),
        name="ssd",
    )(u_zx, u_zx, u_zx, u_small, conv_w, conv_w, conv_b.reshape(1, -1), conv_b.reshape(1, -1),
      pad_l(dt_bias), pad_l(a_log), jnp.repeat(d_skip, SSD_HEAD_DIM).reshape(1, d), norm_w.reshape(1, d))


def _gla_scores_blocked(q, k, gc):
    qc = q.shape[0]
    sub = GLA_SUB
    lane_j = lax.broadcasted_iota(jnp.int32, (sub, qc), 1)
    row_i = lax.broadcasted_iota(jnp.int32, (sub, 1), 0)
    a_rows = []
    for blk in range(qc // sub):
        lo = blk * sub
        q_b = q[lo:lo + sub]
        g_b = gc[lo:lo + sub]
        a_blk = jnp.zeros((sub, qc), F32)
        for j in range(sub):
            k_j = k[lo + j:lo + j + 1, :]
            g_j = gc[lo + j:lo + j + 1, :]
            s_j = jnp.sum(q_b * k_j * jnp.exp(jnp.minimum(g_b - g_j, 0.0)), axis=1, keepdims=True)
            a_blk = jnp.where(lane_j == lo + j, jnp.where(row_i >= j, s_j, 0.0), a_blk)
        if blk > 0:
            g_ref0 = gc[lo:lo + 1, :]
            q_t = (q_b * jnp.exp(g_b - g_ref0)).astype(BF16)
            k_t = (k * jnp.exp(jnp.minimum(g_ref0 - gc, 0.0))).astype(BF16)
            off = lax.dot_general(q_t, k_t, (((1,), (1,)), ((), ())), preferred_element_type=F32)
            a_blk = jnp.where(lane_j < lo, off, a_blk)
        a_rows.append(a_blk)
    return jnp.concatenate(a_rows, axis=0)


def _gla_kernel(q_ref, k_ref, v_ref, r_ref, glr_ref, w2_ref, gb_ref, nw_ref, o_ref, st_ref, a_ref):
    c = pl.program_id(0)
    qc = GLA_CHUNK
    dk, dv = GLA_HEAD_DK, GLA_HEAD_DV
    heads = range(GLA_HEADS)
    nt = (((1,), (1,)), ((), ()))

    @pl.when(c == 0)
    def _():
        st_ref[...] = jnp.zeros_like(st_ref)

    pre = jnp.dot(glr_ref[...], w2_ref[...], precision=HIGHEST, preferred_element_type=F32) + gb_ref[...]
    g = _log_sigmoid(pre) * (1.0 / GLA_GATE_TAU)
    row = lax.broadcasted_iota(jnp.int32, (qc, 1), 0) + c * qc
    g = jnp.where(row >= FRONT_PAD, g, 0.0)
    ri = lax.broadcasted_iota(jnp.int32, (qc, qc), 0)
    ci = lax.broadcasted_iota(jnp.int32, (qc, qc), 1)
    causal = ci <= ri
    gc = jnp.dot(causal.astype(F32), g, precision=HIGHEST, preferred_element_type=F32)
    g_last = gc[qc - 1:qc, :]

    q = q_ref[...].astype(F32) * (GLA_HEAD_DK ** -0.5)
    k = k_ref[...].astype(F32)
    q_dec = (q * jnp.exp(gc)).astype(BF16)
    k_end = (k * jnp.exp(g_last - gc)).astype(BF16)
    safe = jnp.max(-g_last) <= GLA_SAFE_DECAY

    @pl.when(safe)
    def _():
        k_inv = (k * jnp.exp(-gc)).astype(BF16)
        for h in heads:
            s = lax.dot_general(q_dec[:, h * dk:(h + 1) * dk], k_inv[:, h * dk:(h + 1) * dk], nt,
                                preferred_element_type=F32)
            a_ref[h] = jnp.where(causal, s, 0.0)

    @pl.when(jnp.logical_not(safe))
    def _():
        for h in heads:
            a_ref[h] = _gla_scores_blocked(q[:, h * dk:(h + 1) * dk], k[:, h * dk:(h + 1) * dk],
                                           gc[:, h * dk:(h + 1) * dk])

    vb = v_ref[...]
    decay = jnp.exp(g_last)
    outs = []
    for h in heads:
        st = st_ref[h]
        v_h = vb[:, h * dv:(h + 1) * dv]
        o = jnp.dot(a_ref[h].astype(BF16), v_h, preferred_element_type=F32)
        o = o + lax.dot_general(q_dec[:, h * dk:(h + 1) * dk], st.astype(BF16), nt, preferred_element_type=F32)
        st_ref[h] = st * decay[:, h * dk:(h + 1) * dk] + lax.dot_general(
            v_h, k_end[:, h * dk:(h + 1) * dk], (((0,), (0,)), ((), ())), preferred_element_type=F32)
        outs.append(o)
    for h in heads:
        o = outs[h]
        on = o * lax.rsqrt(jnp.mean(o * o, axis=-1, keepdims=True) + RMS_EPS) * nw_ref[:, h * dv:(h + 1) * dv]
        r = r_ref[:, h * dv:(h + 1) * dv].astype(F32)
        o_ref[:, h * dv:(h + 1) * dv] = (on * (r * _sigmoid(r))).astype(o_ref.dtype)


def _gla_mixer(u_qkv, u_r, u_small, gate_w2, gate_b, norm_w):
    rows = u_qkv.shape[0]
    qc = GLA_CHUNK
    dk, dv = GLA_HEAD_DK, GLA_HEAD_DV
    wk, wv = GLA_HEADS * dk, GLA_HEADS * dv
    glr_lane = HYB_GLR_COL % LANES
    w2 = jnp.pad(gate_w2, ((glr_lane, LANES - GLA_GATE_RANK - glr_lane), (0, 0)))
    return pl.pallas_call(
        _gla_kernel,
        grid=(rows // qc,),
        in_specs=[pl.BlockSpec((qc, wk), lambda c: (c, 0)),
                  pl.BlockSpec((qc, wk), lambda c: (c, 1)),
                  pl.BlockSpec((qc, wv), lambda c: (c, 1)),
                  pl.BlockSpec((qc, wv), lambda c: (c, 0)),
                  pl.BlockSpec((qc, LANES), lambda c: (c, 1)),
                  pl.BlockSpec((LANES, wk), lambda c: (0, 0)),
                  pl.BlockSpec((1, wk), lambda c: (0, 0)),
                  pl.BlockSpec((1, wv), lambda c: (0, 0))],
        out_specs=pl.BlockSpec((qc, wv), lambda c: (c, 0)),
        out_shape=jax.ShapeDtypeStruct((rows, wv), BF16),
        scratch_shapes=[pltpu.VMEM((GLA_HEADS, dv, dk), F32), pltpu.VMEM((GLA_HEADS, qc, qc), F32)],
        compiler_params=_params("arbitrary"),
        name="gla",
    )(u_qkv, u_qkv, u_qkv, u_r, u_small, w2, gate_b.reshape(1, -1), norm_w.reshape(1, -1))


def _rope_kernel(freq_ref, cm_ref, sp_ref, sm_ref, *, tm):
    i = pl.program_id(0)
    row = lax.broadcasted_iota(jnp.int32, (tm, LANES), 0) + i * tm
    lane = lax.broadcasted_iota(jnp.int32, (tm, LANES), 1) % SWA_HEAD_DIM
    ang = (row - FRONT_PAD).astype(F32) * freq_ref[...]
    cos = jnp.cos(ang)
    sin = jnp.sin(ang)
    half = ROPE_DIM // 2
    cm_ref[...] = cos
    sp_ref[...] = jnp.where(lane < half, -sin, 0.0)
    sm_ref[...] = jnp.where((lane >= half) & (lane < ROPE_DIM), sin, 0.0)


def _rope_tables(rows):
    half = ROPE_DIM // 2
    inv_freq = ROPE_THETA ** (-jnp.arange(half, dtype=F32) / half)
    per_head = jnp.concatenate([inv_freq, inv_freq, jnp.zeros((SWA_HEAD_DIM - ROPE_DIM,), F32)])
    freq = jnp.tile(per_head, LANES // SWA_HEAD_DIM).reshape(1, LANES)
    tm = _row_tile(rows)
    shp = jax.ShapeDtypeStruct((rows, LANES), F32)
    spec = pl.BlockSpec((tm, LANES), lambda i: (i, 0))
    return pl.pallas_call(
        functools.partial(_rope_kernel, tm=tm),
        grid=(rows // tm,),
        in_specs=[pl.BlockSpec((1, LANES), lambda i: (0, 0))],
        out_specs=[spec, spec, spec],
        out_shape=[shp, shp, shp],
        compiler_params=_params("arbitrary"),
        name="rope_tables",
    )(freq)


def _swa_kernel(sink_ref, q_ref, prev_ref, cur_ref, meta_ref, o_ref):
    n = pl.program_id(0)
    w = SWA_WINDOW
    meta_lo = FRONT_PAD
    kvw = SWA_KV_HEADS * LANES
    nt = (((1,), (1,)), ((), ()))
    lane = lax.broadcasted_iota(jnp.int32, (1, LANES), 1)
    low = lane < SWA_HEAD_DIM
    i = lax.broadcasted_iota(jnp.int32, (SWA_GROUP * w, 1), 0) % w
    hrow = lax.broadcasted_iota(jnp.int32, (SWA_GROUP * w, 1), 0) // w
    on_cur = lane <= i
    valid_band = (on_cur & ((n >= 1) | (lane >= meta_lo))) | (jnp.logical_not(on_cur) & (n >= 2))
    valid_meta = (lane >= meta_lo) & (n >= 1)
    zero = jnp.zeros((w, LANES), q_ref.dtype)
    kv_heads = range(SWA_KV_HEADS)

    scores, sinks = [], []
    for g in kv_heads:
        qa = q_ref[:, 2 * g * LANES:(2 * g + 1) * LANES]
        qb = q_ref[:, (2 * g + 1) * LANES:(2 * g + 2) * LANES]
        qs = jnp.concatenate([jnp.where(low, qa, zero), jnp.where(low, zero, qa),
                              jnp.where(low, qb, zero), jnp.where(low, zero, qb)], axis=0)
        kcols = slice(g * LANES, (g + 1) * LANES)
        kk = jnp.concatenate([prev_ref[:, kcols], cur_ref[:, kcols], meta_ref[:, kcols]], axis=0)
        s = lax.dot_general(qs, kk, nt, preferred_element_type=F32)
        s_band = jnp.where(on_cur, s[:, w:2 * w], s[:, :w])
        scores.append(jnp.concatenate([jnp.where(valid_band, s_band, -jnp.inf),
                                       jnp.where(valid_meta, s[:, 2 * w:], -jnp.inf)], axis=1))
        sink = jnp.zeros((SWA_GROUP * w, 1), F32)
        for h in range(SWA_GROUP):
            sink = jnp.where(hrow == h, sink_ref[g * SWA_GROUP + h], sink)
        sinks.append(sink)
    for g in kv_heads:
        s, sink = scores[g], sinks[g]
        m = jnp.maximum(jnp.max(s, axis=-1, keepdims=True), sink)
        p = jnp.exp(s - m)
        denom = jnp.sum(p, axis=-1, keepdims=True) + jnp.exp(sink - m)
        p_band = p[:, :w]
        pb = jnp.concatenate([jnp.where(on_cur, 0.0, p_band), jnp.where(on_cur, p_band, 0.0), p[:, w:]],
                             axis=1).astype(BF16)
        vcols = slice(kvw + g * LANES, kvw + (g + 1) * LANES)
        vv = jnp.concatenate([prev_ref[:, vcols], cur_ref[:, vcols], meta_ref[:, vcols]], axis=0)
        o = jnp.dot(pb, vv, preferred_element_type=F32) / denom
        oa = jnp.where(low, o[0:w], o[w:2 * w])
        ob = jnp.where(low, o[2 * w:3 * w], o[3 * w:4 * w])
        o_ref[:, 2 * g * LANES:(2 * g + 2) * LANES] = jnp.concatenate([oa, ob], axis=1).astype(o_ref.dtype)


def _swa_attention(q, kv, sinks):
    rows, qw = q.shape
    w = SWA_WINDOW
    kvw = kv.shape[1]
    return pl.pallas_call(
        _swa_kernel,
        grid=(rows // w,),
        in_specs=[pl.BlockSpec(memory_space=pltpu.SMEM),
                  pl.BlockSpec((w, qw), lambda n: (n, 0)),
                  pl.BlockSpec((w, kvw), lambda n: (jnp.maximum(n - 1, 0), 0)),
                  pl.BlockSpec((w, kvw), lambda n: (n, 0)),
                  pl.BlockSpec((w, kvw), lambda n: (0, 0))],
        out_specs=pl.BlockSpec((w, qw), lambda n: (n, 0)),
        out_shape=jax.ShapeDtypeStruct((rows, qw), BF16),
        compiler_params=_params("arbitrary"),
        name="swa",
    )(sinks, q, kv, kv, kv)


def _pad_halves(t):
    pad = lambda a: jnp.pad(a, ((0, 0), (0, D_FF_PAD - D_FF)))
    return jnp.concatenate([pad(t[:, :D_FF]), pad(t[:, D_FF:])], axis=1)


def _trunk(x, meta_tokens, hyb_w_in, hyb_conv_w, hyb_conv_b, ssd_dt_bias, ssd_a_log, ssd_d, ssd_norm_w,
           gla_gate_w2, gla_gate_b, gla_norm_w, hyb_w_out, swa_w_qkv, swa_sinks, swa_w_out,
           ffn_w_up, ffn_conv_w, ffn_conv_b, ffn_w_down, ln_mix_g, ln_mix_b, ln_ffn_g, ln_ffn_b):
    seq = x.shape[0]
    rows = FRONT_PAD + N_META + seq
    h = jnp.concatenate([jnp.zeros((FRONT_PAD, D_MODEL), F32), meta_tokens.astype(F32), x], axis=0)
    hb = h.astype(BF16)
    tables = _rope_tables(rows)
    hyb_w_out_b = hyb_w_out.astype(BF16)
    swa_w_out_b = swa_w_out.astype(BF16)
    ffn_w_down_b = ffn_w_down.astype(BF16)
    for layer in range(DEPTH):
        j = layer // 2
        if layer % 2 == 0:
            u_zx = _project(hb, hyb_w_in, j, first_block=0, block_stride=4, tn=512, n_tiles=12)
            u_qkv = _project(hb, hyb_w_in, j, first_block=HYB_QKV_COL // LANES, block_stride=4, tn=512, n_tiles=8,
                             shift=HYB_QKV_COL % LANES)
            u_r = _project(hb, hyb_w_in, j, first_block=HYB_R_COL // LANES, block_stride=4, tn=512, n_tiles=4,
                           shift=HYB_R_COL % LANES)
            u_small = _project(hb, hyb_w_in, j, first_block=HYB_DT_COL // LANES,
                               block_stride=(HYB_GLR_COL - HYB_DT_COL) // LANES, tn=LANES, n_tiles=2, out_dtype=F32)
            y_ssd = _ssd_mixer(u_zx, u_small, hyb_conv_w[j], hyb_conv_b[j], ssd_dt_bias[j], ssd_a_log[j],
                               ssd_d[j], ssd_norm_w[j])
            y_gla = _gla_mixer(u_qkv, u_r, u_small, gla_gate_w2[j], gla_gate_b[j], gla_norm_w[j])
            h, hb = _matmul_residual_ln([y_ssd, y_gla], hyb_w_out_b, j, h, ln_mix_g[layer], ln_mix_b[layer],
                                        tk=1024)
        else:
            q = _project(hb, swa_w_qkv, j, first_block=0, block_stride=4, tn=512, n_tiles=4,
                         scale=SWA_HEAD_DIM ** -0.5, mode="rope", tables=tables)
            kv = _project(hb, swa_w_qkv, j, first_block=SWA_Q_HEADS * SWA_HEAD_DIM // LANES, block_stride=4, tn=512,
                          n_tiles=2, mode="kv", tables=tables)
            attn = _swa_attention(q, kv, swa_sinks[j])
            h, hb = _matmul_residual_ln([attn], swa_w_out_b, j, h, ln_mix_g[layer], ln_mix_b[layer], tk=1024)
        act = _ffn_up(hb, ffn_w_up, layer, _pad_halves(ffn_conv_w[layer]),
                      _pad_halves(ffn_conv_b[layer].reshape(1, -1)))
        h, hb = _matmul_residual_ln([act], ffn_w_down_b, layer, h, ln_ffn_g[layer], ln_ffn_b[layer], tk=FFN_DOWN_TK)
    return h[FRONT_PAD + N_META:]


def kernel(x, meta_tokens, hyb_w_in, hyb_conv_w, hyb_conv_b, ssd_dt_bias, ssd_a_log, ssd_d, ssd_norm_w,
           gla_gate_w2, gla_gate_b, gla_norm_w, hyb_w_out, swa_w_qkv, swa_sinks, swa_w_out,
           ffn_w_up, ffn_conv_w, ffn_conv_b, ffn_w_down, ln_mix_g, ln_mix_b, ln_ffn_g, ln_ffn_b):
    params = (meta_tokens, hyb_w_in, hyb_conv_w, hyb_conv_b, ssd_dt_bias, ssd_a_log, ssd_d, ssd_norm_w,
              gla_gate_w2, gla_gate_b, gla_norm_w, hyb_w_out, swa_w_qkv, swa_sinks, swa_w_out,
              ffn_w_up, ffn_conv_w, ffn_conv_b, ffn_w_down, ln_mix_g, ln_mix_b, ln_ffn_g, ln_ffn_b)
    return jnp.stack([_trunk(x[b], *params) for b in range(x.shape[0])], axis=0)
```

```python
import functools

import jax
import jax.numpy as jnp
from jax import lax
from jax.experimental import pallas as pl
from jax.experimental.pallas import tpu as pltpu

F32 = jnp.float32
BF16 = jnp.bfloat16
HIGHEST = lax.Precision.HIGHEST

D_MODEL = 2048
DEPTH = 4
N_META = 16
LN_EPS = 1e-5
RMS_EPS = 1e-6
DEEPNORM_ALPHA = (2.0 * DEPTH) ** 0.25

SSD_HEAD_DIM = 64
SSD_HEADS = 32
SSD_GROUPS = 8
SSD_HPG = 4
SSD_STATE = 128
SSD_CONV = 4
SSD_CHUNK = 128
SSD_GROUP_W = SSD_HPG * SSD_HEAD_DIM

GLA_HEADS = 4
GLA_HEAD_DK = 256
GLA_HEAD_DV = 512
GLA_GATE_RANK = 16
GLA_GATE_TAU = 16.0
GLA_CHUNK = 64
GLA_SUB = 16
GLA_SAFE_DECAY = 80.0

SWA_HEAD_DIM = 64
SWA_Q_HEADS = 32
SWA_KV_HEADS = 8
SWA_GROUP = 4
SWA_WINDOW = 128
ROPE_THETA = 500000.0
ROPE_DIM = 16

D_FF = 5504
FFN_CONV = 3

LANES = 128
SUBLANES = 8
FRONT_PAD = SSD_CHUNK - N_META
D_FF_PAD = 5632
FFN_TN = 512
FFN_SUBTILES = 4
PROJ_SUBTILES = 4
FFN_DOWN_TK = 1408
VMEM_LIMIT = 56 * 1024 * 1024

HYB_DT_COL = 6144
HYB_QKV_COL = 6176
HYB_GLR_COL = 10272
HYB_R_COL = 10288


def _row_tile(rows):
    for t in (640, 512, 384, 256, 128):
        if rows % t == 0:
            return t
    raise ValueError(f"row count {rows} is not a multiple of 128")


def _row_tile_big(rows):
    return 1664 if rows % 1664 == 0 else _row_tile(rows)


def _params(*sem):
    return pltpu.CompilerParams(dimension_semantics=sem, vmem_limit_bytes=VMEM_LIMIT)


def _sigmoid(x):
    return 1.0 / (1.0 + jnp.exp(-x))


def _softplus(x):
    return jnp.maximum(x, 0.0) + jnp.log(1.0 + jnp.exp(-jnp.abs(x)))


def _log_sigmoid(x):
    return jnp.minimum(x, 0.0) - jnp.log(1.0 + jnp.exp(-jnp.abs(x)))


def _causal_conv_rows(y, carry, w_ref, b_row, taps):
    top = jnp.concatenate([carry, y[0:SUBLANES]], axis=0)
    w_last = w_ref[taps - 1:taps, :]
    acc = b_row + w_last * y
    acc_top = b_row + w_last * y[0:SUBLANES]
    for s in range(1, taps):
        wk = w_ref[taps - 1 - s:taps - s, :]
        acc = acc + wk * pltpu.roll(y, s, 0)
        acc_top = acc_top + wk * pltpu.roll(top, s, 0)[SUBLANES:2 * SUBLANES]
    return jnp.concatenate([acc_top, acc[SUBLANES:]], axis=0)


def _assemble_weight(w_refs, wb_ref, shift, scale):
    k, tn = wb_ref.shape
    chunk = 256
    for r in range(0, k, chunk):
        w = jnp.concatenate([wr[r:r + chunk, :] for wr in w_refs], axis=1)
        if shift:
            w = pltpu.roll(w, w.shape[1] - shift, 1)
        w = w[:, :tn]
        if scale is not None:
            w = w * scale
        wb_ref[r:r + chunk, :] = w.astype(BF16)


def _weight_block_specs(k, layer, block_fns):
    return [pl.BlockSpec((None, k, LANES), functools.partial(lambda *ids, fn: (layer, 0, fn(*ids)), fn=fn))
            for fn in block_fns]


def _rotate_heads(y, cm, sp, sm):
    half = ROPE_DIM // 2
    out = []
    for c in range(y.shape[1] // LANES):
        yc = y[:, c * LANES:(c + 1) * LANES]
        out.append(yc * cm + pltpu.roll(yc, LANES - half, 1) * sp + pltpu.roll(yc, half, 1) * sm)
    return jnp.concatenate(out, axis=1)


def _duplicate_heads(y):
    low = lax.broadcasted_iota(jnp.int32, (1, LANES), 1) < SWA_HEAD_DIM
    out = []
    for c in range(y.shape[1] // LANES):
        yc = y[:, c * LANES:(c + 1) * LANES]
        rolled = pltpu.roll(yc, SWA_HEAD_DIM, 1)
        out.append(jnp.where(low, yc, rolled))
        out.append(jnp.where(low, rolled, yc))
    return jnp.concatenate(out, axis=1)


def _proj_kernel(*refs, nblk, shift, scale, mode):
    x_ref = refs[0]
    w_refs = refs[1:1 + nblk]
    rest = refs[1 + nblk:]
    if mode != "plain":
        cm_ref, sp_ref, sm_ref = rest[:3]
        rest = rest[3:]
    o_ref, wb_ref = rest
    j = pl.program_id(0)
    i = pl.program_id(1)

    @pl.when(i == 0)
    def _():
        _assemble_weight(w_refs, wb_ref, shift, scale)

    if mode == "plain":
        o_ref[...] = jnp.dot(x_ref[...], wb_ref[...], preferred_element_type=F32).astype(o_ref.dtype)
        return
    tm = x_ref.shape[0]
    ts = tm // PROJ_SUBTILES
    ys = [jnp.dot(x_ref[s * ts:(s + 1) * ts, :], wb_ref[...], preferred_element_type=F32)
          for s in range(PROJ_SUBTILES)]
    rotate = (j == 0) if mode == "kv" else True
    for s, y in enumerate(ys):
        rows = slice(s * ts, (s + 1) * ts)
        cm = jnp.where(rotate, cm_ref[rows, :], 1.0)
        sp = jnp.where(rotate, sp_ref[rows, :], 0.0)
        sm = jnp.where(rotate, sm_ref[rows, :], 0.0)
        y = _rotate_heads(y, cm, sp, sm)
        if mode == "kv":
            y = _duplicate_heads(y)
        o_ref[rows, :] = y.astype(o_ref.dtype)


def _project(xb, w, layer, *, first_block, block_stride, tn, n_tiles, shift=0, scale=None, mode="plain",
             tables=None, out_dtype=BF16):
    m, k = xb.shape
    tm = _row_tile_big(m)
    nb = tn // LANES
    tn_out = 2 * tn if mode == "kv" else tn
    if shift == 0 and block_stride == nb and first_block % nb == 0:
        nblk = 1
        w_specs = [pl.BlockSpec((None, k, tn), lambda j, i: (layer, 0, first_block // nb + j))]
    else:
        nblk = nb + (1 if shift else 0)
        w_specs = _weight_block_specs(
            k, layer, [functools.partial(lambda j, i, b: first_block + j * block_stride + b, b=b) for b in range(nblk)])
    in_specs = [pl.BlockSpec((tm, k), lambda j, i: (i, 0))] + w_specs
    args = [xb] + [w] * nblk
    if mode != "plain":
        in_specs += [pl.BlockSpec((tm, LANES), lambda j, i: (i, 0))] * 3
        args += list(tables)
    return pl.pallas_call(
        functools.partial(_proj_kernel, nblk=nblk, shift=shift, scale=scale, mode=mode),
        grid=(n_tiles, m // tm),
        in_specs=in_specs,
        out_specs=pl.BlockSpec((tm, tn_out), lambda j, i: (i, j)),
        out_shape=jax.ShapeDtypeStruct((m, n_tiles * tn_out), out_dtype),
        scratch_shapes=[pltpu.VMEM((k, tn), BF16)],
        compiler_params=_params("arbitrary", "arbitrary"),
        name="proj_" + mode,
    )(*args)


def _mm_ln_kernel(*refs, src_of_step, nk, tm, tk, k_valid):
    nsrc = max(src_of_step) + 1
    x_refs = refs[:nsrc]
    w_ref, res_ref, g_ref, b_ref, of_ref, ob_ref = refs[nsrc:nsrc + 6]
    scratch = refs[nsrc + 6:]
    i = pl.program_id(0)
    k = pl.program_id(1)

    def partial_product(x_ref):
        w = w_ref[...]
        if k_valid % tk:
            row = lax.broadcasted_iota(jnp.int32, (tk, 1), 0) + k * tk
            w = jnp.where(row < k_valid, w, jnp.zeros_like(w))
        return jnp.dot(x_ref[...], w, preferred_element_type=F32)

    def finish(t):
        t = DEEPNORM_ALPHA * res_ref[...] + t
        mu = jnp.mean(t, axis=-1, keepdims=True)
        d = t - mu
        var = jnp.mean(d * d, axis=-1, keepdims=True)
        y = d * lax.rsqrt(var + LN_EPS) * g_ref[...] + b_ref[...]
        row = lax.broadcasted_iota(jnp.int32, (tm, 1), 0) + i * tm
        y = jnp.where(row >= FRONT_PAD, y, 0.0)
        of_ref[...] = y
        ob_ref[...] = y.astype(BF16)

    if nk == 1:
        finish(partial_product(x_refs[0]))
        return
    acc_ref, = scratch
    for step in range(nk):
        @pl.when(k == step)
        def _(step=step):
            part = partial_product(x_refs[src_of_step[step]])
            if step == 0:
                acc_ref[...] = part
            elif step < nk - 1:
                acc_ref[...] += part
            else:
                finish(acc_ref[...] + part)


def _matmul_residual_ln(xs, w, layer, res, gamma, beta, tk):
    m = xs[0].shape[0]
    _, k_valid, n = w.shape
    tm = _row_tile(m)
    steps = [x.shape[1] // tk for x in xs]
    first = [sum(steps[:s]) for s in range(len(xs))]
    nk = sum(steps)
    src_of_step = tuple(s for s, cnt in enumerate(steps) for _ in range(cnt))
    x_specs = [pl.BlockSpec((tm, tk), functools.partial(
        lambda i, k, k0, cnt: (i, jnp.clip(k - k0, 0, cnt - 1)), k0=first[s], cnt=steps[s])) for s in range(len(xs))]
    return pl.pallas_call(
        functools.partial(_mm_ln_kernel, src_of_step=src_of_step, nk=nk, tm=tm, tk=tk, k_valid=k_valid),
        grid=(m // tm, nk),
        in_specs=x_specs + [pl.BlockSpec((None, tk, n), lambda i, k: (layer, k, 0)),
                            pl.BlockSpec((tm, n), lambda i, k: (i, 0)),
                            pl.BlockSpec((1, n), lambda i, k: (0, 0)),
                            pl.BlockSpec((1, n), lambda i, k: (0, 0))],
        out_specs=[pl.BlockSpec((tm, n), lambda i, k: (i, 0)),
                   pl.BlockSpec((tm, n), lambda i, k: (i, 0))],
        out_shape=[jax.ShapeDtypeStruct((m, n), F32), jax.ShapeDtypeStruct((m, n), BF16)],
        scratch_shapes=[pltpu.VMEM((tm, n), F32)] if nk > 1 else [],
        compiler_params=_params("arbitrary", "arbitrary"),
        name="proj_ln",
    )(*xs, w, res, gamma.reshape(1, n), beta.reshape(1, n))


def _ffn_up_kernel(*refs, nblk, tm, tn):
    x_ref = refs[0]
    w_refs = refs[1:1 + nblk]
    cw_g, cw_v, cb_g, cb_v, o_ref, wb_ref, cg_ref, cv_ref = refs[1 + nblk:]
    j = pl.program_id(0)
    i = pl.program_id(1)

    @pl.when(i == 0)
    def _():
        _assemble_weight(w_refs, wb_ref, 0, None)
        cg_ref[...] = jnp.zeros_like(cg_ref)
        cv_ref[...] = jnp.zeros_like(cv_ref)

    col = lax.broadcasted_iota(jnp.int32, (1, tn), 1) + j * tn
    ts = tm // FFN_SUBTILES
    ys = []
    for s in range(FFN_SUBTILES):
        x = x_ref[s * ts:(s + 1) * ts, :]
        ys.append((jnp.dot(x, wb_ref[:, :tn], preferred_element_type=F32),
                   jnp.dot(x, wb_ref[:, tn:], preferred_element_type=F32)))
    carry_g, carry_v = cg_ref[...], cv_ref[...]
    for s, (yg, yv) in enumerate(ys):
        hg = _causal_conv_rows(yg, carry_g, cw_g, cb_g[...], FFN_CONV)
        hv = _causal_conv_rows(yv, carry_v, cw_v, cb_v[...], FFN_CONV)
        carry_g, carry_v = yg[ts - SUBLANES:ts], yv[ts - SUBLANES:ts]
        o_ref[s * ts:(s + 1) * ts, :] = jnp.where(col < D_FF, hg * _sigmoid(hg) * hv, 0.0).astype(o_ref.dtype)
    cg_ref[...] = carry_g
    cv_ref[...] = carry_v


def _ffn_up(xb, w_up, layer, conv_w, conv_b):
    m, k = xb.shape
    tm = _row_tile_big(m)
    assert (tm // FFN_SUBTILES) % 16 == 0, tm
    tn = FFN_TN
    nj = D_FF_PAD // tn
    nb = tn // LANES
    val0 = D_FF // LANES
    last_blk = 2 * D_FF // LANES - 1
    w_specs = [pl.BlockSpec((None, k, tn), lambda j, i: (layer, 0, j))] + _weight_block_specs(
        k, layer, [functools.partial(lambda j, i, b: jnp.minimum(val0 + j * nb + b, last_blk), b=b) for b in range(nb)])
    nblk = 1 + nb
    return pl.pallas_call(
        functools.partial(_ffn_up_kernel, nblk=nblk, tm=tm, tn=tn),
        grid=(nj, m // tm),
        in_specs=[pl.BlockSpec((tm, k), lambda j, i: (i, 0))] + w_specs
        + [pl.BlockSpec((FFN_CONV, tn), lambda j, i: (0, j)),
           pl.BlockSpec((FFN_CONV, tn), lambda j, i: (0, nj + j)),
           pl.BlockSpec((1, tn), lambda j, i: (0, j)),
           pl.BlockSpec((1, tn), lambda j, i: (0, nj + j))],
        out_specs=pl.BlockSpec((tm, tn), lambda j, i: (i, j)),
        out_shape=jax.ShapeDtypeStruct((m, D_FF_PAD), BF16),
        scratch_shapes=[pltpu.VMEM((k, 2 * tn), BF16), pltpu.VMEM((SUBLANES, tn), F32),
                        pltpu.VMEM((SUBLANES, tn), F32)],
        compiler_params=_params("arbitrary", "arbitrary"),
        name="ffn_up",
    )(xb, *([w_up] * nblk), conv_w, conv_w, conv_b, conv_b)


def _expand_heads(x, rows, g):
    low = lax.broadcasted_iota(jnp.int32, (1, LANES), 1) < SSD_HEAD_DIM
    h0 = g * SSD_HPG
    b = [jnp.broadcast_to(x[:, h0 + h:h0 + h + 1], (rows, LANES)) for h in range(SSD_HPG)]
    return jnp.concatenate([jnp.where(low, b[0], b[1]), jnp.where(low, b[2], b[3])], axis=1)


def _ssd_kernel(z_ref, x_ref, bc_ref, dt_ref, cwx_ref, cwbc_ref, cbx_ref, cbbc_ref, dtb_ref, alog_ref, dsk_ref,
                nw_ref, o_ref, st_ref, cx_ref, cbc_ref):
    c = pl.program_id(0)
    q = SSD_CHUNK
    gw = SSD_GROUP_W
    n = SSD_STATE
    groups = range(SSD_GROUPS)

    @pl.when(c == 0)
    def _():
        st_ref[...] = jnp.zeros_like(st_ref)
        cx_ref[...] = jnp.zeros_like(cx_ref)
        cbc_ref[...] = jnp.zeros_like(cbc_ref)

    row = lax.broadcasted_iota(jnp.int32, (q, 1), 0) + c * q
    lane = lax.broadcasted_iota(jnp.int32, (1, LANES), 1)
    dt = _softplus(dt_ref[...] + dtb_ref[...])
    dt = jnp.where((row >= FRONT_PAD) & (lane < SSD_HEADS), dt, 0.0)
    a = -jnp.exp(alog_ref[...])
    ri = lax.broadcasted_iota(jnp.int32, (q, q), 0)
    ci = lax.broadcasted_iota(jnp.int32, (q, q), 1)
    causal = ci <= ri
    cs = jnp.dot(causal.astype(F32), dt * a, precision=HIGHEST, preferred_element_type=F32)
    cs_t = cs.T
    cs_last = cs[q - 1:q, :]
    from_start = jnp.exp(cs)
    to_end = jnp.exp(cs_last - cs)
    total = jnp.exp(cs_last)

    def conv_silu(raw_ref, carry_ref, w_ref, b_ref, lo, width):
        raw = raw_ref[:, lo:lo + width].astype(F32)
        y = _causal_conv_rows(raw, carry_ref[:, lo:lo + width], w_ref.at[:, lo:lo + width], b_ref[:, lo:lo + width],
                              SSD_CONV)
        carry_ref[:, lo:lo + width] = raw[q - SUBLANES:q]
        return y * _sigmoid(y)

    xs = [conv_silu(x_ref, cx_ref, cwx_ref, cbx_ref, g * gw, gw) for g in groups]
    bm_t = [conv_silu(bc_ref, cbc_ref, cwbc_ref, cbbc_ref, g * n, n).T.astype(BF16) for g in groups]
    cmb = [conv_silu(bc_ref, cbc_ref, cwbc_ref, cbbc_ref, SSD_GROUPS * n + g * n, n).astype(BF16) for g in groups]
    cb = [jnp.dot(cmb[g], bm_t[g], preferred_element_type=F32) for g in groups]

    lane_w = lax.broadcasted_iota(jnp.int32, (1, gw), 1)
    xd = [xs[g] * _expand_heads(dt, q, g) for g in groups]
    ys = []
    for g in groups:
        decayed = []
        for h in range(SSD_HPG):
            hh = g * SSD_HPG + h
            seg = cs[:, hh:hh + 1] - cs_t[hh:hh + 1, :]
            decayed.append((cb[g] * jnp.exp(jnp.where(causal, seg, -jnp.inf))).astype(BF16))
        xdb = xd[g].astype(BF16)
        x_heads = [jnp.where((lane_w >= h * SSD_HEAD_DIM) & (lane_w < (h + 1) * SSD_HEAD_DIM), xdb,
                             jnp.zeros_like(xdb)) for h in range(SSD_HPG)]
        ys.append(jnp.dot(jnp.concatenate(decayed, axis=1), jnp.concatenate(x_heads, axis=0),
                          preferred_element_type=F32))
    for g in groups:
        st = st_ref[g]
        ys[g] = ys[g] + (jnp.dot(cmb[g], st.astype(BF16), preferred_element_type=F32)
                         * _expand_heads(from_start, q, g))
        st_ref[g] = (st * _expand_heads(total, 1, g)
                     + jnp.dot(bm_t[g], (xd[g] * _expand_heads(to_end, q, g)).astype(BF16),
                               preferred_element_type=F32))
    for g in groups:
        cols = slice(g * gw, (g + 1) * gw)
        y = ys[g] + xs[g] * dsk_ref[:, cols]
        z = z_ref[:, cols].astype(F32)
        yg = y * (z * _sigmoid(z))
        yn = yg * lax.rsqrt(jnp.mean(yg * yg, axis=-1, keepdims=True) + RMS_EPS) * nw_ref[:, cols]
        o_ref[:, cols] = yn.astype(o_ref.dtype)


def _ssd_mixer(u_zx, u_small, conv_w, conv_b, dt_bias, a_log, d_skip, norm_w):
    rows = u_zx.shape[0]
    q = SSD_CHUNK
    d = SSD_GROUPS * SSD_GROUP_W
    pad_l = lambda v: jnp.pad(v.reshape(1, SSD_HEADS), ((0, 0), (0, LANES - SSD_HEADS)))
    whole = lambda r, w, j: pl.BlockSpec((r, w), lambda c: (0, j))
    return pl.pallas_call(
        _ssd_kernel,
        grid=(rows // q,),
        in_specs=[pl.BlockSpec((q, d), lambda c: (c, 0)),
                  pl.BlockSpec((q, d), lambda c: (c, 1)),
                  pl.BlockSpec((q, d), lambda c: (c, 2)),
                  pl.BlockSpec((q, LANES), lambda c: (c, 0)),
                  whole(SSD_CONV, d, 0), whole(SSD_CONV, d, 1), whole(1, d, 0), whole(1, d, 1),
                  whole(1, LANES, 0), whole(1, LANES, 0), whole(1, d, 0), whole(1, d, 0)],
        out_specs=pl.BlockSpec((q, d), lambda c: (c, 0)),
        out_shape=jax.ShapeDtypeStruct((rows, d), BF16),
        scratch_shapes=[pltpu.VMEM((SSD_GROUPS, SSD_STATE, SSD_GROUP_W), F32), pltpu.VMEM((SUBLANES, d), F32),
                        pltpu.VMEM((SUBLANES, d), F32)],
        compiler_params=_params("arbitrary"),
        name="ssd",
    )(u_zx, u_zx, u_zx, u_small, conv_w, conv_w, conv_b.reshape(1, -1), conv_b.reshape(1, -1),
      pad_l(dt_bias), pad_l(a_log), jnp.repeat(d_skip, SSD_HEAD_DIM).reshape(1, d), norm_w.reshape(1, d))


def _gla_scores_blocked(q, k, gc):
    qc = q.shape[0]
    sub = GLA_SUB
    lane_j = lax.broadcasted_iota(jnp.int32, (sub, qc), 1)
    row_i = lax.broadcasted_iota(jnp.int32, (sub, 1), 0)
    a_rows = []
    for blk in range(qc // sub):
        lo = blk * sub
        q_b = q[lo:lo + sub]
        g_b = gc[lo:lo + sub]
        a_blk = jnp.zeros((sub, qc), F32)
        for j in range(sub):
            k_j = k[lo + j:lo + j + 1, :]
            g_j = gc[lo + j:lo + j + 1, :]
            s_j = jnp.sum(q_b * k_j * jnp.exp(jnp.minimum(g_b - g_j, 0.0)), axis=1, keepdims=True)
            a_blk = jnp.where(lane_j == lo + j, jnp.where(row_i >= j, s_j, 0.0), a_blk)
        if blk > 0:
            g_ref0 = gc[lo:lo + 1, :]
            q_t = (q_b * jnp.exp(g_b - g_ref0)).astype(BF16)
            k_t = (k * jnp.exp(jnp.minimum(g_ref0 - gc, 0.0))).astype(BF16)
            off = lax.dot_general(q_t, k_t, (((1,), (1,)), ((), ())), preferred_element_type=F32)
            a_blk = jnp.where(lane_j < lo, off, a_blk)
        a_rows.append(a_blk)
    return jnp.concatenate(a_rows, axis=0)


def _gla_kernel(q_ref, k_ref, v_ref, r_ref, glr_ref, w2_ref, gb_ref, nw_ref, o_ref, st_ref, a_ref):
    c = pl.program_id(0)
    qc = GLA_CHUNK
    dk, dv = GLA_HEAD_DK, GLA_HEAD_DV
    heads = range(GLA_HEADS)
    nt = (((1,), (1,)), ((), ()))

    @pl.when(c == 0)
    def _():
        st_ref[...] = jnp.zeros_like(st_ref)

    pre = jnp.dot(glr_ref[...], w2_ref[...], precision=HIGHEST, preferred_element_type=F32) + gb_ref[...]
    g = _log_sigmoid(pre) * (1.0 / GLA_GATE_TAU)
    row = lax.broadcasted_iota(jnp.int32, (qc, 1), 0) + c * qc
    g = jnp.where(row >= FRONT_PAD, g, 0.0)
    ri = lax.broadcasted_iota(jnp.int32, (qc, qc), 0)
    ci = lax.broadcasted_iota(jnp.int32, (qc, qc), 1)
    causal = ci <= ri
    gc = jnp.dot(causal.astype(F32), g, precision=HIGHEST, preferred_element_type=F32)
    g_last = gc[qc - 1:qc, :]

    q = q_ref[...].astype(F32) * (GLA_HEAD_DK ** -0.5)
    k = k_ref[...].astype(F32)
    q_dec = (q * jnp.exp(gc)).astype(BF16)
    k_end = (k * jnp.exp(g_last - gc)).astype(BF16)
    safe = jnp.max(-g_last) <= GLA_SAFE_DECAY

    @pl.when(safe)
    def _():
        k_inv = (k * jnp.exp(-gc)).astype(BF16)
        for h in heads:
            s = lax.dot_general(q_dec[:, h * dk:(h + 1) * dk], k_inv[:, h * dk:(h + 1) * dk], nt,
                                preferred_element_type=F32)
            a_ref[h] = jnp.where(causal, s, 0.0)

    @pl.when(jnp.logical_not(safe))
    def _():
        for h in heads:
            a_ref[h] = _gla_scores_blocked(q[:, h * dk:(h + 1) * dk], k[:, h * dk:(h + 1) * dk],
                                           gc[:, h * dk:(h + 1) * dk])

    vb = v_ref[...]
    decay = jnp.exp(g_last)
    outs = []
    for h in heads:
        st = st_ref[h]
        v_h = vb[:, h * dv:(h + 1) * dv]
        o = jnp.dot(a_ref[h].astype(BF16), v_h, preferred_element_type=F32)
        o = o + lax.dot_general(q_dec[:, h * dk:(h + 1) * dk], st.astype(BF16), nt, preferred_element_type=F32)
        st_ref[h] = st * decay[:, h * dk:(h + 1) * dk] + lax.dot_general(
            v_h, k_end[:, h * dk:(h + 1) * dk], (((0,), (0,)), ((), ())), preferred_element_type=F32)
        outs.append(o)
    for h in heads:
        o = outs[h]
        on = o * lax.rsqrt(jnp.mean(o * o, axis=-1, keepdims=True) + RMS_EPS) * nw_ref[:, h * dv:(h + 1) * dv]
        r = r_ref[:, h * dv:(h + 1) * dv].astype(F32)
        o_ref[:, h * dv:(h + 1) * dv] = (on * (r * _sigmoid(r))).astype(o_ref.dtype)


def _gla_mixer(u_qkv, u_r, u_small, gate_w2, gate_b, norm_w):
    rows = u_qkv.shape[0]
    qc = GLA_CHUNK
    dk, dv = GLA_HEAD_DK, GLA_HEAD_DV
    wk, wv = GLA_HEADS * dk, GLA_HEADS * dv
    glr_lane = HYB_GLR_COL % LANES
    w2 = jnp.pad(gate_w2, ((glr_lane, LANES - GLA_GATE_RANK - glr_lane), (0, 0)))
    return pl.pallas_call(
        _gla_kernel,
        grid=(rows // qc,),
        in_specs=[pl.BlockSpec((qc, wk), lambda c: (c, 0)),
                  pl.BlockSpec((qc, wk), lambda c: (c, 1)),
                  pl.BlockSpec((qc, wv), lambda c: (c, 1)),
                  pl.BlockSpec((qc, wv), lambda c: (c, 0)),
                  pl.BlockSpec((qc, LANES), lambda c: (c, 1)),
                  pl.BlockSpec((LANES, wk), lambda c: (0, 0)),
                  pl.BlockSpec((1, wk), lambda c: (0, 0)),
                  pl.BlockSpec((1, wv), lambda c: (0, 0))],
        out_specs=pl.BlockSpec((qc, wv), lambda c: (c, 0)),
        out_shape=jax.ShapeDtypeStruct((rows, wv), BF16),
        scratch_shapes=[pltpu.VMEM((GLA_HEADS, dv, dk), F32), pltpu.VMEM((GLA_HEADS, qc, qc), F32)],
        compiler_params=_params("arbitrary"),
        name="gla",
    )(u_qkv, u_qkv, u_qkv, u_r, u_small, w2, gate_b.reshape(1, -1), norm_w.reshape(1, -1))


def _rope_kernel(freq_ref, cm_ref, sp_ref, sm_ref, *, tm):
    i = pl.program_id(0)
    row = lax.broadcasted_iota(jnp.int32, (tm, LANES), 0) + i * tm
    lane = lax.broadcasted_iota(jnp.int32, (tm, LANES), 1) % SWA_HEAD_DIM
    ang = (row - FRONT_PAD).astype(F32) * freq_ref[...]
    cos = jnp.cos(ang)
    sin = jnp.sin(ang)
    half = ROPE_DIM // 2
    cm_ref[...] = cos
    sp_ref[...] = jnp.where(lane < half, -sin, 0.0)
    sm_ref[...] = jnp.where((lane >= half) & (lane < ROPE_DIM), sin, 0.0)


def _rope_tables(rows):
    half = ROPE_DIM // 2
    inv_freq = ROPE_THETA ** (-jnp.arange(half, dtype=F32) / half)
    per_head = jnp.concatenate([inv_freq, inv_freq, jnp.zeros((SWA_HEAD_DIM - ROPE_DIM,), F32)])
    freq = jnp.tile(per_head, LANES // SWA_HEAD_DIM).reshape(1, LANES)
    tm = _row_tile(rows)
    shp = jax.ShapeDtypeStruct((rows, LANES), F32)
    spec = pl.BlockSpec((tm, LANES), lambda i: (i, 0))
    return pl.pallas_call(
        functools.partial(_rope_kernel, tm=tm),
        grid=(rows // tm,),
        in_specs=[pl.BlockSpec((1, LANES), lambda i: (0, 0))],
        out_specs=[spec, spec, spec],
        out_shape=[shp, shp, shp],
        compiler_params=_params("arbitrary"),
        name="rope_tables",
    )(freq)


def _swa_kernel(sink_ref, q_ref, prev_ref, cur_ref, meta_ref, o_ref):
    n = pl.program_id(0)
    w = SWA_WINDOW
    meta_lo = FRONT_PAD
    kvw = SWA_KV_HEADS * LANES
    nt = (((1,), (1,)), ((), ()))
    lane = lax.broadcasted_iota(jnp.int32, (1, LANES), 1)
    low = lane < SWA_HEAD_DIM
    i = lax.broadcasted_iota(jnp.int32, (SWA_GROUP * w, 1), 0) % w
    hrow = lax.broadcasted_iota(jnp.int32, (SWA_GROUP * w, 1), 0) // w
    on_cur = lane <= i
    valid_band = (on_cur & ((n >= 1) | (lane >= meta_lo))) | (jnp.logical_not(on_cur) & (n >= 2))
    valid_meta = (lane >= meta_lo) & (n >= 1)
    zero = jnp.zeros((w, LANES), q_ref.dtype)
    kv_heads = range(SWA_KV_HEADS)

    scores, sinks = [], []
    for g in kv_heads:
        qa = q_ref[:, 2 * g * LANES:(2 * g + 1) * LANES]
        qb = q_ref[:, (2 * g + 1) * LANES:(2 * g + 2) * LANES]
        qs = jnp.concatenate([jnp.where(low, qa, zero), jnp.where(low, zero, qa),
                              jnp.where(low, qb, zero), jnp.where(low, zero, qb)], axis=0)
        kcols = slice(g * LANES, (g + 1) * LANES)
        kk = jnp.concatenate([prev_ref[:, kcols], cur_ref[:, kcols], meta_ref[:, kcols]], axis=0)
        s = lax.dot_general(qs, kk, nt, preferred_element_type=F32)
        s_band = jnp.where(on_cur, s[:, w:2 * w], s[:, :w])
        scores.append(jnp.concatenate([jnp.where(valid_band, s_band, -jnp.inf),
                                       jnp.where(valid_meta, s[:, 2 * w:], -jnp.inf)], axis=1))
        sink = jnp.zeros((SWA_GROUP * w, 1), F32)
        for h in range(SWA_GROUP):
            sink = jnp.where(hrow == h, sink_ref[g * SWA_GROUP + h], sink)
        sinks.append(sink)
    for g in kv_heads:
        s, sink = scores[g], sinks[g]
        m = jnp.maximum(jnp.max(s, axis=-1, keepdims=True), sink)
        p = jnp.exp(s - m)
        denom = jnp.sum(p, axis=-1, keepdims=True) + jnp.exp(sink - m)
        p_band = p[:, :w]
        pb = jnp.concatenate([jnp.where(on_cur, 0.0, p_band), jnp.where(on_cur, p_band, 0.0), p[:, w:]],
                             axis=1).astype(BF16)
        vcols = slice(kvw + g * LANES, kvw + (g + 1) * LANES)
        vv = jnp.concatenate([prev_ref[:, vcols], cur_ref[:, vcols], meta_ref[:, vcols]], axis=0)
        o = jnp.dot(pb, vv, preferred_element_type=F32) / denom
        oa = jnp.where(low, o[0:w], o[w:2 * w])
        ob = jnp.where(low, o[2 * w:3 * w], o[3 * w:4 * w])
        o_ref[:, 2 * g * LANES:(2 * g + 2) * LANES] = jnp.concatenate([oa, ob], axis=1).astype(o_ref.dtype)


def _swa_attention(q, kv, sinks):
    rows, qw = q.shape
    w = SWA_WINDOW
    kvw = kv.shape[1]
    return pl.pallas_call(
        _swa_kernel,
        grid=(rows // w,),
        in_specs=[pl.BlockSpec(memory_space=pltpu.SMEM),
                  pl.BlockSpec((w, qw), lambda n: (n, 0)),
                  pl.BlockSpec((w, kvw), lambda n: (jnp.maximum(n - 1, 0), 0)),
                  pl.BlockSpec((w, kvw), lambda n: (n, 0)),
                  pl.BlockSpec((w, kvw), lambda n: (0, 0))],
        out_specs=pl.BlockSpec((w, qw), lambda n: (n, 0)),
        out_shape=jax.ShapeDtypeStruct((rows, qw), BF16),
        compiler_params=_params("arbitrary"),
        name="swa",
    )(sinks, q, kv, kv, kv)


def _pad_halves(t):
    pad = lambda a: jnp.pad(a, ((0, 0), (0, D_FF_PAD - D_FF)))
    return jnp.concatenate([pad(t[:, :D_FF]), pad(t[:, D_FF:])], axis=1)


def _trunk(x, meta_tokens, hyb_w_in, hyb_conv_w, hyb_conv_b, ssd_dt_bias, ssd_a_log, ssd_d, ssd_norm_w,
           gla_gate_w2, gla_gate_b, gla_norm_w, hyb_w_out, swa_w_qkv, swa_sinks, swa_w_out,
           ffn_w_up, ffn_conv_w, ffn_conv_b, ffn_w_down, ln_mix_g, ln_mix_b, ln_ffn_g, ln_ffn_b):
    seq = x.shape[0]
    rows = FRONT_PAD + N_META + seq
    h = jnp.concatenate([jnp.zeros((FRONT_PAD, D_MODEL), F32), meta_tokens.astype(F32), x], axis=0)
    hb = h.astype(BF16)
    tables = _rope_tables(rows)
    hyb_w_out_b = hyb_w_out.astype(BF16)
    swa_w_out_b = swa_w_out.astype(BF16)
    ffn_w_down_b = ffn_w_down.astype(BF16)
    for layer in range(DEPTH):
        j = layer // 2
        if layer % 2 == 0:
            u_zx = _project(hb, hyb_w_in, j, first_block=0, block_stride=4, tn=512, n_tiles=12)
            u_qkv = _project(hb, hyb_w_in, j, first_block=HYB_QKV_COL // LANES, block_stride=4, tn=512, n_tiles=8,
                             shift=HYB_QKV_COL % LANES)
            u_r = _project(hb, hyb_w_in, j, first_block=HYB_R_COL // LANES, block_stride=4, tn=512, n_tiles=4,
                           shift=HYB_R_COL % LANES)
            u_small = _project(hb, hyb_w_in, j, first_block=HYB_DT_COL // LANES,
                               block_stride=(HYB_GLR_COL - HYB_DT_COL) // LANES, tn=LANES, n_tiles=2, out_dtype=F32)
            y_ssd = _ssd_mixer(u_zx, u_small, hyb_conv_w[j], hyb_conv_b[j], ssd_dt_bias[j], ssd_a_log[j],
                               ssd_d[j], ssd_norm_w[j])
            y_gla = _gla_mixer(u_qkv, u_r, u_small, gla_gate_w2[j], gla_gate_b[j], gla_norm_w[j])
            h, hb = _matmul_residual_ln([y_ssd, y_gla], hyb_w_out_b, j, h, ln_mix_g[layer], ln_mix_b[layer],
                                        tk=1024)
        else:
            q = _project(hb, swa_w_qkv, j, first_block=0, block_stride=4, tn=512, n_tiles=4,
                         scale=SWA_HEAD_DIM ** -0.5, mode="rope", tables=tables)
            kv = _project(hb, swa_w_qkv, j, first_block=SWA_Q_HEADS * SWA_HEAD_DIM // LANES, block_stride=4, tn=512,
                          n_tiles=2, mode="kv", tables=tables)
            attn = _swa_attention(q, kv, swa_sinks[j])
            h, hb = _matmul_residual_ln([attn], swa_w_out_b, j, h, ln_mix_g[layer], ln_mix_b[layer], tk=1024)
        act = _ffn_up(hb, ffn_w_up, layer, _pad_halves(ffn_conv_w[layer]),
                      _pad_halves(ffn_conv_b[layer].reshape(1, -1)))
        h, hb = _matmul_residual_ln([act], ffn_w_down_b, layer, h, ln_ffn_g[layer], ln_ffn_b[layer], tk=FFN_DOWN_TK)
    return h[FRONT_PAD + N_META:]


def kernel(x, meta_tokens, hyb_w_in, hyb_conv_w, hyb_conv_b, ssd_dt_bias, ssd_a_log, ssd_d, ssd_norm_w,
           gla_gate_w2, gla_gate_b, gla_norm_w, hyb_w_out, swa_w_qkv, swa_sinks, swa_w_out,
           ffn_w_up, ffn_conv_w, ffn_conv_b, ffn_w_down, ln_mix_g, ln_mix_b, ln_ffn_g, ln_ffn_b):
    params = (meta_tokens, hyb_w_in, hyb_conv_w, hyb_conv_b, ssd_dt_bias, ssd_a_log, ssd_d, ssd_norm_w,
              gla_gate_w2, gla_gate_b, gla_norm_w, hyb_w_out, swa_w_qkv, swa_sinks, swa_w_out,
              ffn_w_up, ffn_conv_w, ffn_conv_b, ffn_w_down, ln_mix_g, ln_mix_b, ln_ffn_g, ln_ffn_b)
    return jnp.stack([_trunk(x[b], *params) for b in range(x.shape[0])], axis=0)
```

```python
import functools

import jax
import jax.numpy as jnp
from jax import lax
from jax.experimental import pallas as pl
from jax.experimental.pallas import tpu as pltpu

F32 = jnp.float32
BF16 = jnp.bfloat16
HIGHEST = lax.Precision.HIGHEST

D_MODEL = 2048
DEPTH = 4
N_META = 16
LN_EPS = 1e-5
RMS_EPS = 1e-6
DEEPNORM_ALPHA = (2.0 * DEPTH) ** 0.25

SSD_HEAD_DIM = 64
SSD_HEADS = 32
SSD_GROUPS = 8
SSD_HPG = 4
SSD_STATE = 128
SSD_CONV = 4
SSD_CHUNK = 128
SSD_GROUP_W = SSD_HPG * SSD_HEAD_DIM

GLA_HEADS = 4
GLA_HEAD_DK = 256
GLA_HEAD_DV = 512
GLA_GATE_RANK = 16
GLA_GATE_TAU = 16.0
GLA_CHUNK = 64
GLA_SUB = 16
GLA_SAFE_DECAY = 80.0

SWA_HEAD_DIM = 64
SWA_Q_HEADS = 32
SWA_KV_HEADS = 8
SWA_GROUP = 4
SWA_WINDOW = 128
ROPE_THETA = 500000.0
ROPE_DIM = 16

D_FF = 5504
FFN_CONV = 3

LANES = 128
SUBLANES = 8
MXU_DEPTH = 256
FRONT_PAD = SSD_CHUNK - N_META
D_FF_PAD = 5632
FFN_TN = 512
FFN_SUBTILES = 4
PROJ_SUBTILES = 4
LN_ROW_TILE = 320
LN_SUBTILES = 2
LN_WEIGHT_COPY_BUDGET = 36 * 1024 * 1024
VMEM_LIMIT = 56 * 1024 * 1024

HYB_DT_COL = 6144
HYB_QKV_COL = 6176
HYB_GLR_COL = 10272
HYB_R_COL = 10288


def _row_tile(rows):
    for t in (640, 512, 384, 256, 128):
        if rows % t == 0:
            return t
    raise ValueError(f"row count {rows} is not a multiple of 128")


def _row_tile_big(rows):
    return 1664 if rows % 1664 == 0 else _row_tile(rows)


def _params(*sem):
    return pltpu.CompilerParams(dimension_semantics=sem, vmem_limit_bytes=VMEM_LIMIT)


def _sigmoid(x):
    return 1.0 / (1.0 + jnp.exp(-x))


def _softplus(x):
    return jnp.maximum(x, 0.0) + jnp.log(1.0 + jnp.exp(-jnp.abs(x)))


def _log_sigmoid(x):
    return jnp.minimum(x, 0.0) - jnp.log(1.0 + jnp.exp(-jnp.abs(x)))


def _causal_conv_rows(y, carry, w_ref, b_row, taps):
    top = jnp.concatenate([carry, y[0:SUBLANES]], axis=0)
    w_last = w_ref[taps - 1:taps, :]
    acc = b_row + w_last * y
    acc_top = b_row + w_last * y[0:SUBLANES]
    for s in range(1, taps):
        wk = w_ref[taps - 1 - s:taps - s, :]
        acc = acc + wk * pltpu.roll(y, s, 0)
        acc_top = acc_top + wk * pltpu.roll(top, s, 0)[SUBLANES:2 * SUBLANES]
    return jnp.concatenate([acc_top, acc[SUBLANES:]], axis=0)


def _assemble_weight(w_refs, wb_ref, shift, scale):
    k, tn = wb_ref.shape
    chunk = 256
    for r in range(0, k, chunk):
        w = jnp.concatenate([wr[r:r + chunk, :] for wr in w_refs], axis=1)
        if shift:
            w = pltpu.roll(w, w.shape[1] - shift, 1)
        w = w[:, :tn]
        if scale is not None:
            w = w * scale
        wb_ref[r:r + chunk, :] = w.astype(BF16)


def _weight_block_specs(k, layer, block_fns):
    return [pl.BlockSpec((None, k, LANES), functools.partial(lambda *ids, fn: (layer, 0, fn(*ids)), fn=fn))
            for fn in block_fns]


def _rotate_heads(y, cm, sp, sm):
    half = ROPE_DIM // 2
    out = []
    for c in range(y.shape[1] // LANES):
        yc = y[:, c * LANES:(c + 1) * LANES]
        out.append(yc * cm + pltpu.roll(yc, LANES - half, 1) * sp + pltpu.roll(yc, half, 1) * sm)
    return jnp.concatenate(out, axis=1)


def _duplicate_heads(y):
    low = lax.broadcasted_iota(jnp.int32, (1, LANES), 1) < SWA_HEAD_DIM
    out = []
    for c in range(y.shape[1] // LANES):
        yc = y[:, c * LANES:(c + 1) * LANES]
        rolled = pltpu.roll(yc, SWA_HEAD_DIM, 1)
        out.append(jnp.where(low, yc, rolled))
        out.append(jnp.where(low, rolled, yc))
    return jnp.concatenate(out, axis=1)


def _proj_kernel(*refs, nblk, shift, scale, mode):
    x_ref = refs[0]
    w_refs = refs[1:1 + nblk]
    rest = refs[1 + nblk:]
    if mode != "plain":
        cm_ref, sp_ref, sm_ref = rest[:3]
        rest = rest[3:]
    o_ref, wb_ref = rest
    j = pl.program_id(0)
    i = pl.program_id(1)

    @pl.when(i == 0)
    def _():
        _assemble_weight(w_refs, wb_ref, shift, scale)

    if mode == "plain":
        o_ref[...] = jnp.dot(x_ref[...], wb_ref[...], preferred_element_type=F32).astype(o_ref.dtype)
        return
    tm = x_ref.shape[0]
    ts = tm // PROJ_SUBTILES
    ys = [jnp.dot(x_ref[s * ts:(s + 1) * ts, :], wb_ref[...], preferred_element_type=F32)
          for s in range(PROJ_SUBTILES)]
    rotate = (j == 0) if mode == "kv" else True
    for s, y in enumerate(ys):
        rows = slice(s * ts, (s + 1) * ts)
        cm = jnp.where(rotate, cm_ref[rows, :], 1.0)
        sp = jnp.where(rotate, sp_ref[rows, :], 0.0)
        sm = jnp.where(rotate, sm_ref[rows, :], 0.0)
        y = _rotate_heads(y, cm, sp, sm)
        if mode == "kv":
            y = _duplicate_heads(y)
        o_ref[rows, :] = y.astype(o_ref.dtype)


def _project(xb, w, layer, *, first_block, block_stride, tn, n_tiles, shift=0, scale=None, mode="plain",
             tables=None, out_dtype=BF16):
    m, k = xb.shape
    tm = _row_tile_big(m)
    nb = tn // LANES
    tn_out = 2 * tn if mode == "kv" else tn
    if shift == 0 and block_stride == nb and first_block % nb == 0:
        nblk = 1
        w_specs = [pl.BlockSpec((None, k, tn), lambda j, i: (layer, 0, first_block // nb + j))]
    else:
        nblk = nb + (1 if shift else 0)
        w_specs = _weight_block_specs(
            k, layer, [functools.partial(lambda j, i, b: first_block + j * block_stride + b, b=b) for b in range(nblk)])
    in_specs = [pl.BlockSpec((tm, k), lambda j, i: (i, 0))] + w_specs
    args = [xb] + [w] * nblk
    if mode != "plain":
        in_specs += [pl.BlockSpec((tm, LANES), lambda j, i: (i, 0))] * 3
        args += list(tables)
    return pl.pallas_call(
        functools.partial(_proj_kernel, nblk=nblk, shift=shift, scale=scale, mode=mode),
        grid=(n_tiles, m // tm),
        in_specs=in_specs,
        out_specs=pl.BlockSpec((tm, tn_out), lambda j, i: (i, j)),
        out_shape=jax.ShapeDtypeStruct((m, n_tiles * tn_out), out_dtype),
        scratch_shapes=[pltpu.VMEM((k, tn), BF16)],
        compiler_params=_params("arbitrary", "arbitrary"),
        name="proj_" + mode,
    )(*args)


def _mm_ln_kernel(*refs, nsrc, tm, subtiles):
    x_refs = refs[:nsrc]
    w_ref, res_ref, g_ref, b_ref, of_ref, ob_ref = refs[nsrc:]
    i = pl.program_id(0)
    ts = tm // subtiles
    sums = []
    for s in range(subtiles):
        rows = slice(s * ts, (s + 1) * ts)
        acc, k0 = None, 0
        for x_ref in x_refs:
            kw = x_ref.shape[1]
            part = jnp.dot(x_ref[rows, :], w_ref[k0:k0 + kw, :], preferred_element_type=F32)
            acc = part if acc is None else acc + part
            k0 += kw
        sums.append(acc)
    for s, acc in enumerate(sums):
        rows = slice(s * ts, (s + 1) * ts)
        t = DEEPNORM_ALPHA * res_ref[rows, :] + acc
        mu = jnp.mean(t, axis=-1, keepdims=True)
        d = t - mu
        var = jnp.mean(d * d, axis=-1, keepdims=True)
        y = d * lax.rsqrt(var + LN_EPS) * g_ref[...] + b_ref[...]
        row = lax.broadcasted_iota(jnp.int32, (ts, 1), 0) + (i * tm + s * ts)
        y = jnp.where(row >= FRONT_PAD, y, 0.0)
        of_ref[rows, :] = y
        ob_ref[rows, :] = y.astype(BF16)


def _matmul_residual_ln(xs, k_widths, w, layer, res, gamma, beta):
    m = xs[0].shape[0]
    _, kdim, n = w.shape
    assert sum(k_widths) == kdim and m % LN_ROW_TILE == 0
    tm = LN_ROW_TILE
    subtiles = LN_SUBTILES if 2 * kdim * n * 2 <= LN_WEIGHT_COPY_BUDGET else 1
    x_specs = [pl.BlockSpec((tm, kw), lambda i: (i, 0)) for kw in k_widths]
    return pl.pallas_call(
        functools.partial(_mm_ln_kernel, nsrc=len(xs), tm=tm, subtiles=subtiles),
        grid=(m // tm,),
        in_specs=x_specs + [pl.BlockSpec((None, kdim, n), lambda i: (layer, 0, 0), pipeline_mode=pl.Buffered(1)),
                            pl.BlockSpec((tm, n), lambda i: (i, 0)),
                            pl.BlockSpec((1, n), lambda i: (0, 0)),
                            pl.BlockSpec((1, n), lambda i: (0, 0))],
        out_specs=[pl.BlockSpec((tm, n), lambda i: (i, 0)),
                   pl.BlockSpec((tm, n), lambda i: (i, 0))],
        out_shape=[jax.ShapeDtypeStruct((m, n), F32), jax.ShapeDtypeStruct((m, n), BF16)],
        compiler_params=_params("arbitrary"),
        name="proj_ln",
    )(*xs, w, res, gamma.reshape(1, n), beta.reshape(1, n))


def _ffn_up_kernel(*refs, nblk, tm, tn):
    x_ref = refs[0]
    w_refs = refs[1:1 + nblk]
    cw_g, cw_v, cb_g, cb_v, o_ref, wb_ref, cg_ref, cv_ref = refs[1 + nblk:]
    j = pl.program_id(0)
    i = pl.program_id(1)

    @pl.when(i == 0)
    def _():
        _assemble_weight(w_refs, wb_ref, 0, None)
        cg_ref[...] = jnp.zeros_like(cg_ref)
        cv_ref[...] = jnp.zeros_like(cv_ref)

    col = lax.broadcasted_iota(jnp.int32, (1, tn), 1) + j * tn
    ts = tm // FFN_SUBTILES
    ys = []
    for s in range(FFN_SUBTILES):
        x = x_ref[s * ts:(s + 1) * ts, :]
        ys.append((jnp.dot(x, wb_ref[:, :tn], preferred_element_type=F32),
                   jnp.dot(x, wb_ref[:, tn:], preferred_element_type=F32)))
    carry_g, carry_v = cg_ref[...], cv_ref[...]
    for s, (yg, yv) in enumerate(ys):
        hg = _causal_conv_rows(yg, carry_g, cw_g, cb_g[...], FFN_CONV)
        hv = _causal_conv_rows(yv, carry_v, cw_v, cb_v[...], FFN_CONV)
        carry_g, carry_v = yg[ts - SUBLANES:ts], yv[ts - SUBLANES:ts]
        o_ref[s * ts:(s + 1) * ts, :] = jnp.where(col < D_FF, hg * _sigmoid(hg) * hv, 0.0).astype(o_ref.dtype)
    cg_ref[...] = carry_g
    cv_ref[...] = carry_v


def _ffn_up(xb, w_up, layer, conv_w, conv_b):
    m, k = xb.shape
    tm = _row_tile_big(m)
    assert (tm // FFN_SUBTILES) % 16 == 0, tm
    tn = FFN_TN
    nj = D_FF_PAD // tn
    nb = tn // LANES
    val0 = D_FF // LANES
    last_blk = 2 * D_FF // LANES - 1
    w_specs = [pl.BlockSpec((None, k, tn), lambda j, i: (layer, 0, j))] + _weight_block_specs(
        k, layer, [functools.partial(lambda j, i, b: jnp.minimum(val0 + j * nb + b, last_blk), b=b) for b in range(nb)])
    nblk = 1 + nb
    return pl.pallas_call(
        functools.partial(_ffn_up_kernel, nblk=nblk, tm=tm, tn=tn),
        grid=(nj, m // tm),
        in_specs=[pl.BlockSpec((tm, k), lambda j, i: (i, 0))] + w_specs
        + [pl.BlockSpec((FFN_CONV, tn), lambda j, i: (0, j)),
           pl.BlockSpec((FFN_CONV, tn), lambda j, i: (0, nj + j)),
           pl.BlockSpec((1, tn), lambda j, i: (0, j)),
           pl.BlockSpec((1, tn), lambda j, i: (0, nj + j))],
        out_specs=pl.BlockSpec((tm, tn), lambda j, i: (i, j)),
        out_shape=jax.ShapeDtypeStruct((m, D_FF_PAD), BF16),
        scratch_shapes=[pltpu.VMEM((k, 2 * tn), BF16), pltpu.VMEM((SUBLANES, tn), F32),
                        pltpu.VMEM((SUBLANES, tn), F32)],
        compiler_params=_params("arbitrary", "arbitrary"),
        name="ffn_up",
    )(xb, *([w_up] * nblk), conv_w, conv_w, conv_b, conv_b)


def _expand_heads(x, rows, g):
    low = lax.broadcasted_iota(jnp.int32, (1, LANES), 1) < SSD_HEAD_DIM
    h0 = g * SSD_HPG
    b = [jnp.broadcast_to(x[:, h0 + h:h0 + h + 1], (rows, LANES)) for h in range(SSD_HPG)]
    return jnp.concatenate([jnp.where(low, b[0], b[1]), jnp.where(low, b[2], b[3])], axis=1)


def _ssd_kernel(z_ref, x_ref, bc_ref, dt_ref, cwx_ref, cwbc_ref, cbx_ref, cbbc_ref, dtb_ref, alog_ref, dsk_ref,
                nw_ref, o_ref, st_ref, cx_ref, cbc_ref):
    c = pl.program_id(0)
    q = SSD_CHUNK
    gw = SSD_GROUP_W
    n = SSD_STATE
    groups = range(SSD_GROUPS)

    @pl.when(c == 0)
    def _():
        st_ref[...] = jnp.zeros_like(st_ref)
        cx_ref[...] = jnp.zeros_like(cx_ref)
        cbc_ref[...] = jnp.zeros_like(cbc_ref)

    row = lax.broadcasted_iota(jnp.int32, (q, 1), 0) + c * q
    lane = lax.broadcasted_iota(jnp.int32, (1, LANES), 1)
    dt = _softplus(dt_ref[...] + dtb_ref[...])
    dt = jnp.where((row >= FRONT_PAD) & (lane < SSD_HEADS), dt, 0.0)
    a = -jnp.exp(alog_ref[...])
    ri = lax.broadcasted_iota(jnp.int32, (q, q), 0)
    ci = lax.broadcasted_iota(jnp.int32, (q, q), 1)
    causal = ci <= ri
    cs = jnp.dot(causal.astype(F32), dt * a, precision=HIGHEST, preferred_element_type=F32)
    cs_t = cs.T
    cs_last = cs[q - 1:q, :]
    from_start = jnp.exp(cs)
    to_end = jnp.exp(cs_last - cs)
    total = jnp.exp(cs_last)

    def conv_silu(raw_ref, carry_ref, w_ref, b_ref, lo, width):
        raw = raw_ref[:, lo:lo + width].astype(F32)
        y = _causal_conv_rows(raw, carry_ref[:, lo:lo + width], w_ref.at[:, lo:lo + width], b_ref[:, lo:lo + width],
                              SSD_CONV)
        carry_ref[:, lo:lo + width] = raw[q - SUBLANES:q]
        return y * _sigmoid(y)

    xs = [conv_silu(x_ref, cx_ref, cwx_ref, cbx_ref, g * gw, gw) for g in groups]
    bm_t = [conv_silu(bc_ref, cbc_ref, cwbc_ref, cbbc_ref, g * n, n).T.astype(BF16) for g in groups]
    cmb = [conv_silu(bc_ref, cbc_ref, cwbc_ref, cbbc_ref, SSD_GROUPS * n + g * n, n).astype(BF16) for g in groups]
    cb = [jnp.dot(cmb[g], bm_t[g], preferred_element_type=F32) for g in groups]

    lane_w = lax.broadcasted_iota(jnp.int32, (1, gw), 1)
    xd = [xs[g] * _expand_heads(dt, q, g) for g in groups]
    ys = []
    for g in groups:
        decayed = []
        for h in range(SSD_HPG):
            hh = g * SSD_HPG + h
            seg = cs[:, hh:hh + 1] - cs_t[hh:hh + 1, :]
            decayed.append((cb[g] * jnp.exp(jnp.where(causal, seg, -jnp.inf))).astype(BF16))
        xdb = xd[g].astype(BF16)
        x_heads = [jnp.where((lane_w >= h * SSD_HEAD_DIM) & (lane_w < (h + 1) * SSD_HEAD_DIM), xdb,
                             jnp.zeros_like(xdb)) for h in range(SSD_HPG)]
        ys.append(jnp.dot(jnp.concatenate(decayed, axis=1), jnp.concatenate(x_heads, axis=0),
                          preferred_element_type=F32))
    for g in groups:
        st = st_ref[g]
        ys[g] = ys[g] + (jnp.dot(cmb[g], st.astype(BF16), preferred_element_type=F32)
                         * _expand_heads(from_start, q, g))
        st_ref[g] = (st * _expand_heads(total, 1, g)
                     + jnp.dot(bm_t[g], (xd[g] * _expand_heads(to_end, q, g)).astype(BF16),
                               preferred_element_type=F32))
    for g in groups:
        cols = slice(g * gw, (g + 1) * gw)
        y = ys[g] + xs[g] * dsk_ref[:, cols]
        z = z_ref[:, cols].astype(F32)
        yg = y * (z * _sigmoid(z))
        yn = yg * lax.rsqrt(jnp.mean(yg * yg, axis=-1, keepdims=True) + RMS_EPS) * nw_ref[:, cols]
        o_ref[:, cols] = yn.astype(o_ref.dtype)


def _ssd_mixer(u_zx, u_small, conv_w, conv_b, dt_bias, a_log, d_skip, norm_w):
    rows = u_zx.shape[0]
    q = SSD_CHUNK
    d = SSD_GROUPS * SSD_GROUP_W
    pad_l = lambda v: jnp.pad(v.reshape(1, SSD_HEADS), ((0, 0), (0, LANES - SSD_HEADS)))
    whole = lambda r, w, j: pl.BlockSpec((r, w), lambda c: (0, j))
    return pl.pallas_call(
        _ssd_kernel,
        grid=(rows // q,),
        in_specs=[pl.BlockSpec((q, d), lambda c: (c, 0)),
                  pl.BlockSpec((q, d), lambda c: (c, 1)),
                  pl.BlockSpec((q, d), lambda c: (c, 2)),
                  pl.BlockSpec((q, LANES), lambda c: (c, 0)),
                  whole(SSD_CONV, d, 0), whole(SSD_CONV, d, 1), whole(1, d, 0), whole(1, d, 1),
                  whole(1, LANES, 0), whole(1, LANES, 0), whole(1, d, 0), whole(1, d, 0)],
        out_specs=pl.BlockSpec((q, d), lambda c: (c, 0)),
        out_shape=jax.ShapeDtypeStruct((rows, d), BF16),
        scratch_shapes=[pltpu.VMEM((SSD_GROUPS, SSD_STATE, SSD_GROUP_W), F32), pltpu.VMEM((SUBLANES, d), F32),
                        pltpu.VMEM((SUBLANES, d), F32)],
        compiler_params=_params("arbitrary"),
        name="ssd",
    )(u_zx, u_zx, u_zx, u_small, conv_w, conv_w, conv_b.reshape(1, -1), conv_b.reshape(1, -1),
      pad_l(dt_bias), pad_l(a_log), jnp.repeat(d_skip, SSD_HEAD_DIM).reshape(1, d), norm_w.reshape(1, d))


def _gla_scores_blocked(q, k, gc):
    qc = q.shape[0]
    sub = GLA_SUB
    lane_j = lax.broadcasted_iota(jnp.int32, (sub, qc), 1)
    row_i = lax.broadcasted_iota(jnp.int32, (sub, 1), 0)
    a_rows = []
    for blk in range(qc // sub):
        lo = blk * sub
        q_b = q[lo:lo + sub]
        g_b = gc[lo:lo + sub]
        a_blk = jnp.zeros((sub, qc), F32)
        for j in range(sub):
            k_j = k[lo + j:lo + j + 1, :]
            g_j = gc[lo + j:lo + j + 1, :]
            s_j = jnp.sum(q_b * k_j * jnp.exp(jnp.minimum(g_b - g_j, 0.0)), axis=1, keepdims=True)
            a_blk = jnp.where(lane_j == lo + j, jnp.where(row_i >= j, s_j, 0.0), a_blk)
        if blk > 0:
            g_ref0 = gc[lo:lo + 1, :]
            q_t = (q_b * jnp.exp(g_b - g_ref0)).astype(BF16)
            k_t = (k * jnp.exp(jnp.minimum(g_ref0 - gc, 0.0))).astype(BF16)
            off = lax.dot_general(q_t, k_t, (((1,), (1,)), ((), ())), preferred_element_type=F32)
            a_blk = jnp.where(lane_j < lo, off, a_blk)
        a_rows.append(a_blk)
    return jnp.concatenate(a_rows, axis=0)


def _gla_kernel(q_ref, k_ref, v_ref, r_ref, glr_ref, w2_ref, gb_ref, nw_ref, o_ref, st_ref, a_ref):
    c = pl.program_id(0)
    qc = GLA_CHUNK
    dk, dv = GLA_HEAD_DK, GLA_HEAD_DV
    heads = range(GLA_HEADS)
    nt = (((1,), (1,)), ((), ()))

    @pl.when(c == 0)
    def _():
        st_ref[...] = jnp.zeros_like(st_ref)

    pre = jnp.dot(glr_ref[...], w2_ref[...], precision=HIGHEST, preferred_element_type=F32) + gb_ref[...]
    g = _log_sigmoid(pre) * (1.0 / GLA_GATE_TAU)
    row = lax.broadcasted_iota(jnp.int32, (qc, 1), 0) + c * qc
    g = jnp.where(row >= FRONT_PAD, g, 0.0)
    ri = lax.broadcasted_iota(jnp.int32, (qc, qc), 0)
    ci = lax.broadcasted_iota(jnp.int32, (qc, qc), 1)
    causal = ci <= ri
    gc = jnp.dot(causal.astype(F32), g, precision=HIGHEST, preferred_element_type=F32)
    g_last = gc[qc - 1:qc, :]

    q = q_ref[...].astype(F32) * (GLA_HEAD_DK ** -0.5)
    k = k_ref[...].astype(F32)
    q_dec = (q * jnp.exp(gc)).astype(BF16)
    k_end = (k * jnp.exp(g_last - gc)).astype(BF16)
    safe = jnp.max(-g_last) <= GLA_SAFE_DECAY

    @pl.when(safe)
    def _():
        k_inv = (k * jnp.exp(-gc)).astype(BF16)
        for h in heads:
            s = lax.dot_general(q_dec[:, h * dk:(h + 1) * dk], k_inv[:, h * dk:(h + 1) * dk], nt,
                                preferred_element_type=F32)
            a_ref[h] = jnp.where(causal, s, 0.0)

    @pl.when(jnp.logical_not(safe))
    def _():
        for h in heads:
            a_ref[h] = _gla_scores_blocked(q[:, h * dk:(h + 1) * dk], k[:, h * dk:(h + 1) * dk],
                                           gc[:, h * dk:(h + 1) * dk])

    vb = v_ref[...]
    decay = jnp.exp(g_last)
    outs = []
    for h in heads:
        st = st_ref[h]
        v_h = vb[:, h * dv:(h + 1) * dv]
        o = jnp.dot(a_ref[h].astype(BF16), v_h, preferred_element_type=F32)
        o = o + lax.dot_general(q_dec[:, h * dk:(h + 1) * dk], st.astype(BF16), nt, preferred_element_type=F32)
        st_ref[h] = st * decay[:, h * dk:(h + 1) * dk] + lax.dot_general(
            v_h, k_end[:, h * dk:(h + 1) * dk], (((0,), (0,)), ((), ())), preferred_element_type=F32)
        outs.append(o)
    for h in heads:
        o = outs[h]
        on = o * lax.rsqrt(jnp.mean(o * o, axis=-1, keepdims=True) + RMS_EPS) * nw_ref[:, h * dv:(h + 1) * dv]
        r = r_ref[:, h * dv:(h + 1) * dv].astype(F32)
        o_ref[:, h * dv:(h + 1) * dv] = (on * (r * _sigmoid(r))).astype(o_ref.dtype)


def _gla_mixer(u_qkv, u_r, u_small, gate_w2, gate_b, norm_w):
    rows = u_qkv.shape[0]
    qc = GLA_CHUNK
    dk, dv = GLA_HEAD_DK, GLA_HEAD_DV
    wk, wv = GLA_HEADS * dk, GLA_HEADS * dv
    glr_lane = HYB_GLR_COL % LANES
    w2 = jnp.pad(gate_w2, ((glr_lane, LANES - GLA_GATE_RANK - glr_lane), (0, 0)))
    return pl.pallas_call(
        _gla_kernel,
        grid=(rows // qc,),
        in_specs=[pl.BlockSpec((qc, wk), lambda c: (c, 0)),
                  pl.BlockSpec((qc, wk), lambda c: (c, 1)),
                  pl.BlockSpec((qc, wv), lambda c: (c, 1)),
                  pl.BlockSpec((qc, wv), lambda c: (c, 0)),
                  pl.BlockSpec((qc, LANES), lambda c: (c, 1)),
                  pl.BlockSpec((LANES, wk), lambda c: (0, 0)),
                  pl.BlockSpec((1, wk), lambda c: (0, 0)),
                  pl.BlockSpec((1, wv), lambda c: (0, 0))],
        out_specs=pl.BlockSpec((qc, wv), lambda c: (c, 0)),
        out_shape=jax.ShapeDtypeStruct((rows, wv), BF16),
        scratch_shapes=[pltpu.VMEM((GLA_HEADS, dv, dk), F32), pltpu.VMEM((GLA_HEADS, qc, qc), F32)],
        compiler_params=_params("arbitrary"),
        name="gla",
    )(u_qkv, u_qkv, u_qkv, u_r, u_small, w2, gate_b.reshape(1, -1), norm_w.reshape(1, -1))


def _rope_kernel(freq_ref, cm_ref, sp_ref, sm_ref, *, tm):
    i = pl.program_id(0)
    row = lax.broadcasted_iota(jnp.int32, (tm, LANES), 0) + i * tm
    lane = lax.broadcasted_iota(jnp.int32, (tm, LANES), 1) % SWA_HEAD_DIM
    ang = (row - FRONT_PAD).astype(F32) * freq_ref[...]
    cos = jnp.cos(ang)
    sin = jnp.sin(ang)
    half = ROPE_DIM // 2
    cm_ref[...] = cos
    sp_ref[...] = jnp.where(lane < half, -sin, 0.0)
    sm_ref[...] = jnp.where((lane >= half) & (lane < ROPE_DIM), sin, 0.0)


def _rope_tables(rows):
    half = ROPE_DIM // 2
    inv_freq = ROPE_THETA ** (-jnp.arange(half, dtype=F32) / half)
    per_head = jnp.concatenate([inv_freq, inv_freq, jnp.zeros((SWA_HEAD_DIM - ROPE_DIM,), F32)])
    freq = jnp.tile(per_head, LANES // SWA_HEAD_DIM).reshape(1, LANES)
    tm = _row_tile(rows)
    shp = jax.ShapeDtypeStruct((rows, LANES), F32)
    spec = pl.BlockSpec((tm, LANES), lambda i: (i, 0))
    return pl.pallas_call(
        functools.partial(_rope_kernel, tm=tm),
        grid=(rows // tm,),
        in_specs=[pl.BlockSpec((1, LANES), lambda i: (0, 0))],
        out_specs=[spec, spec, spec],
        out_shape=[shp, shp, shp],
        compiler_params=_params("arbitrary"),
        name="rope_tables",
    )(freq)


def _swa_kernel(sink_ref, q_ref, prev_ref, cur_ref, meta_ref, o_ref):
    n = pl.program_id(0)
    w = SWA_WINDOW
    meta_lo = FRONT_PAD
    kvw = SWA_KV_HEADS * LANES
    nt = (((1,), (1,)), ((), ()))
    lane = lax.broadcasted_iota(jnp.int32, (1, LANES), 1)
    low = lane < SWA_HEAD_DIM
    i = lax.broadcasted_iota(jnp.int32, (SWA_GROUP * w, 1), 0) % w
    hrow = lax.broadcasted_iota(jnp.int32, (SWA_GROUP * w, 1), 0) // w
    on_cur = lane <= i
    valid_band = (on_cur & ((n >= 1) | (lane >= meta_lo))) | (jnp.logical_not(on_cur) & (n >= 2))
    valid_meta = (lane >= meta_lo) & (n >= 1)
    zero = jnp.zeros((w, LANES), q_ref.dtype)
    kv_heads = range(SWA_KV_HEADS)

    scores, sinks = [], []
    for g in kv_heads:
        qa = q_ref[:, 2 * g * LANES:(2 * g + 1) * LANES]
        qb = q_ref[:, (2 * g + 1) * LANES:(2 * g + 2) * LANES]
        qs = jnp.concatenate([jnp.where(low, qa, zero), jnp.where(low, zero, qa),
                              jnp.where(low, qb, zero), jnp.where(low, zero, qb)], axis=0)
        kcols = slice(g * LANES, (g + 1) * LANES)
        kk = jnp.concatenate([prev_ref[:, kcols], cur_ref[:, kcols], meta_ref[:, kcols]], axis=0)
        s = lax.dot_general(qs, kk, nt, preferred_element_type=F32)
        s_band = jnp.where(on_cur, s[:, w:2 * w], s[:, :w])
        scores.append(jnp.concatenate([jnp.where(valid_band, s_band, -jnp.inf),
                                       jnp.where(valid_meta, s[:, 2 * w:], -jnp.inf)], axis=1))
        sink = jnp.zeros((SWA_GROUP * w, 1), F32)
        for h in range(SWA_GROUP):
            sink = jnp.where(hrow == h, sink_ref[g * SWA_GROUP + h], sink)
        sinks.append(sink)
    for g in kv_heads:
        s, sink = scores[g], sinks[g]
        m = jnp.maximum(jnp.max(s, axis=-1, keepdims=True), sink)
        p = jnp.exp(s - m)
        denom = jnp.sum(p, axis=-1, keepdims=True) + jnp.exp(sink - m)
        p_band = p[:, :w]
        pb = jnp.concatenate([jnp.where(on_cur, 0.0, p_band), jnp.where(on_cur, p_band, 0.0), p[:, w:]],
                             axis=1).astype(BF16)
        vcols = slice(kvw + g * LANES, kvw + (g + 1) * LANES)
        vv = jnp.concatenate([prev_ref[:, vcols], cur_ref[:, vcols], meta_ref[:, vcols]], axis=0)
        o = jnp.dot(pb, vv, preferred_element_type=F32) / denom
        oa = jnp.where(low, o[0:w], o[w:2 * w])
        ob = jnp.where(low, o[2 * w:3 * w], o[3 * w:4 * w])
        o_ref[:, 2 * g * LANES:(2 * g + 2) * LANES] = jnp.concatenate([oa, ob], axis=1).astype(o_ref.dtype)


def _swa_attention(q, kv, sinks):
    rows, qw = q.shape
    w = SWA_WINDOW
    kvw = kv.shape[1]
    return pl.pallas_call(
        _swa_kernel,
        grid=(rows // w,),
        in_specs=[pl.BlockSpec(memory_space=pltpu.SMEM),
                  pl.BlockSpec((w, qw), lambda n: (n, 0)),
                  pl.BlockSpec((w, kvw), lambda n: (jnp.maximum(n - 1, 0), 0)),
                  pl.BlockSpec((w, kvw), lambda n: (n, 0)),
                  pl.BlockSpec((w, kvw), lambda n: (0, 0))],
        out_specs=pl.BlockSpec((w, qw), lambda n: (n, 0)),
        out_shape=jax.ShapeDtypeStruct((rows, qw), BF16),
        compiler_params=_params("arbitrary"),
        name="swa",
    )(sinks, q, kv, kv, kv)


def _pad_halves(t):
    pad = lambda a: jnp.pad(a, ((0, 0), (0, D_FF_PAD - D_FF)))
    return jnp.concatenate([pad(t[:, :D_FF]), pad(t[:, D_FF:])], axis=1)


def _trunk(x, meta_tokens, hyb_w_in, hyb_conv_w, hyb_conv_b, ssd_dt_bias, ssd_a_log, ssd_d, ssd_norm_w,
           gla_gate_w2, gla_gate_b, gla_norm_w, hyb_w_out, swa_w_qkv, swa_sinks, swa_w_out,
           ffn_w_up, ffn_conv_w, ffn_conv_b, ffn_w_down, ln_mix_g, ln_mix_b, ln_ffn_g, ln_ffn_b):
    seq = x.shape[0]
    rows = FRONT_PAD + N_META + seq
    h = jnp.concatenate([jnp.zeros((FRONT_PAD, D_MODEL), F32), meta_tokens.astype(F32), x], axis=0)
    hb = h.astype(BF16)
    tables = _rope_tables(rows)
    hyb_w_out_b = hyb_w_out.astype(BF16)
    swa_w_out_b = swa_w_out.astype(BF16)
    ffn_w_down_b = ffn_w_down.astype(BF16)
    for layer in range(DEPTH):
        j = layer // 2
        if layer % 2 == 0:
            u_zx = _project(hb, hyb_w_in, j, first_block=0, block_stride=4, tn=512, n_tiles=12)
            u_qkv = _project(hb, hyb_w_in, j, first_block=HYB_QKV_COL // LANES, block_stride=4, tn=512, n_tiles=8,
                             shift=HYB_QKV_COL % LANES)
            u_r = _project(hb, hyb_w_in, j, first_block=HYB_R_COL // LANES, block_stride=4, tn=512, n_tiles=4,
                           shift=HYB_R_COL % LANES)
            u_small = _project(hb, hyb_w_in, j, first_block=HYB_DT_COL // LANES,
                               block_stride=(HYB_GLR_COL - HYB_DT_COL) // LANES, tn=LANES, n_tiles=2, out_dtype=F32)
            y_ssd = _ssd_mixer(u_zx, u_small, hyb_conv_w[j], hyb_conv_b[j], ssd_dt_bias[j], ssd_a_log[j],
                               ssd_d[j], ssd_norm_w[j])
            y_gla = _gla_mixer(u_qkv, u_r, u_small, gla_gate_w2[j], gla_gate_b[j], gla_norm_w[j])
            h, hb = _matmul_residual_ln([y_ssd, y_gla], [y_ssd.shape[1], y_gla.shape[1]], hyb_w_out_b, j, h,
                                        ln_mix_g[layer], ln_mix_b[layer])
        else:
            q = _project(hb, swa_w_qkv, j, first_block=0, block_stride=4, tn=512, n_tiles=4,
                         scale=SWA_HEAD_DIM ** -0.5, mode="rope", tables=tables)
            kv = _project(hb, swa_w_qkv, j, first_block=SWA_Q_HEADS * SWA_HEAD_DIM // LANES, block_stride=4, tn=512,
                          n_tiles=2, mode="kv", tables=tables)
            attn = _swa_attention(q, kv, swa_sinks[j])
            h, hb = _matmul_residual_ln([attn], [attn.shape[1]], swa_w_out_b, j, h, ln_mix_g[layer], ln_mix_b[layer])
        act = _ffn_up(hb, ffn_w_up, layer, _pad_halves(ffn_conv_w[layer]),
                      _pad_halves(ffn_conv_b[layer].reshape(1, -1)))
        h, hb = _matmul_residual_ln([act], [D_FF], ffn_w_down_b, layer, h, ln_ffn_g[layer], ln_ffn_b[layer])
    return h[FRONT_PAD + N_META:]


def kernel(x, meta_tokens, hyb_w_in, hyb_conv_w, hyb_conv_b, ssd_dt_bias, ssd_a_log, ssd_d, ssd_norm_w,
           gla_gate_w2, gla_gate_b, gla_norm_w, hyb_w_out, swa_w_qkv, swa_sinks, swa_w_out,
           ffn_w_up, ffn_conv_w, ffn_conv_b, ffn_w_down, ln_mix_g, ln_mix_b, ln_ffn_g, ln_ffn_b):
    params = (meta_tokens, hyb_w_in, hyb_conv_w, hyb_conv_b, ssd_dt_bias, ssd_a_log, ssd_d, ssd_norm_w,
              gla_gate_w2, gla_gate_b, gla_norm_w, hyb_w_out, swa_w_qkv, swa_sinks, swa_w_out,
              ffn_w_up, ffn_conv_w, ffn_conv_b, ffn_w_down, ln_mix_g, ln_mix_b, ln_ffn_g, ln_ffn_b)
    return jnp.stack([_trunk(x[b], *params) for b in range(x.shape[0])], axis=0)
```

```python
import functools

import jax
import jax.numpy as jnp
from jax import lax
from jax.experimental import pallas as pl
from jax.experimental.pallas import tpu as pltpu

F32 = jnp.float32
BF16 = jnp.bfloat16
HIGHEST = lax.Precision.HIGHEST

D_MODEL = 2048
DEPTH = 4
N_META = 16
LN_EPS = 1e-5
RMS_EPS = 1e-6
DEEPNORM_ALPHA = (2.0 * DEPTH) ** 0.25

SSD_HEAD_DIM = 64
SSD_HEADS = 32
SSD_GROUPS = 8
SSD_HPG = 4
SSD_STATE = 128
SSD_CONV = 4
SSD_CHUNK = 128
SSD_GROUP_W = SSD_HPG * SSD_HEAD_DIM

GLA_HEADS = 4
GLA_HEAD_DK = 256
GLA_HEAD_DV = 512
GLA_GATE_RANK = 16
GLA_GATE_TAU = 16.0
GLA_CHUNK = 128
GLA_SUB = 16
GLA_SAFE_DECAY = 80.0

SWA_HEAD_DIM = 64
SWA_Q_HEADS = 32
SWA_KV_HEADS = 8
SWA_GROUP = 4
SWA_WINDOW = 128
ROPE_THETA = 500000.0
ROPE_DIM = 16

D_FF = 5504
FFN_CONV = 3

LANES = 128
SUBLANES = 8
MXU_DEPTH = 256
FRONT_PAD = SSD_CHUNK - N_META
D_FF_PAD = 5632
FFN_TN = 512
FFN_SUBTILES = 4
PROJ_SUBTILES = 4
LN_ROW_TILE = 320
LN_SUBTILES = 2
LN_WEIGHT_COPY_BUDGET = 36 * 1024 * 1024
VMEM_LIMIT = 56 * 1024 * 1024

HYB_DT_COL = 6144
HYB_QKV_COL = 6176
HYB_GLR_COL = 10272
HYB_R_COL = 10288


def _row_tile(rows):
    for t in (640, 512, 384, 256, 128):
        if rows % t == 0:
            return t
    raise ValueError(f"row count {rows} is not a multiple of 128")


def _row_tile_big(rows):
    return 1664 if rows % 1664 == 0 else _row_tile(rows)


def _params(*sem):
    return pltpu.CompilerParams(dimension_semantics=sem, vmem_limit_bytes=VMEM_LIMIT)


def _sigmoid(x):
    return 1.0 / (1.0 + jnp.exp(-x))


def _softplus(x):
    return jnp.maximum(x, 0.0) + jnp.log(1.0 + jnp.exp(-jnp.abs(x)))


def _log_sigmoid(x):
    return jnp.minimum(x, 0.0) - jnp.log(1.0 + jnp.exp(-jnp.abs(x)))


def _causal_conv_rows(y, carry, w_ref, b_row, taps):
    top = jnp.concatenate([carry, y[0:SUBLANES]], axis=0)
    w_last = w_ref[taps - 1:taps, :]
    acc = b_row + w_last * y
    acc_top = b_row + w_last * y[0:SUBLANES]
    for s in range(1, taps):
        wk = w_ref[taps - 1 - s:taps - s, :]
        acc = acc + wk * pltpu.roll(y, s, 0)
        acc_top = acc_top + wk * pltpu.roll(top, s, 0)[SUBLANES:2 * SUBLANES]
    return jnp.concatenate([acc_top, acc[SUBLANES:]], axis=0)


def _assemble_weight(w_refs, wb_ref, shift, scale):
    k, tn = wb_ref.shape
    chunk = 256
    for r in range(0, k, chunk):
        w = jnp.concatenate([wr[r:r + chunk, :] for wr in w_refs], axis=1)
        if shift:
            w = pltpu.roll(w.astype(F32), w.shape[1] - shift, 1)
        w = w[:, :tn]
        if scale is not None:
            w = w * scale
        wb_ref[r:r + chunk, :] = w.astype(BF16)


def _weight_block_specs(k, layer, block_fns):
    return [pl.BlockSpec((None, k, LANES), functools.partial(lambda *ids, fn: (layer, 0, fn(*ids)), fn=fn))
            for fn in block_fns]


def _rotate_heads(y, cm, sp, sm):
    half = ROPE_DIM // 2
    out = []
    for c in range(y.shape[1] // LANES):
        yc = y[:, c * LANES:(c + 1) * LANES]
        out.append(yc * cm + pltpu.roll(yc, LANES - half, 1) * sp + pltpu.roll(yc, half, 1) * sm)
    return jnp.concatenate(out, axis=1)


def _duplicate_heads(y):
    low = lax.broadcasted_iota(jnp.int32, (1, LANES), 1) < SWA_HEAD_DIM
    out = []
    for c in range(y.shape[1] // LANES):
        yc = y[:, c * LANES:(c + 1) * LANES]
        rolled = pltpu.roll(yc, SWA_HEAD_DIM, 1)
        out.append(jnp.where(low, yc, rolled))
        out.append(jnp.where(low, rolled, yc))
    return jnp.concatenate(out, axis=1)


def _proj_kernel(*refs, nblk, shift, scale, mode):
    x_ref = refs[0]
    w_refs = refs[1:1 + nblk]
    rest = refs[1 + nblk:]
    if mode != "plain":
        cm_ref, sp_ref, sm_ref = rest[:3]
        rest = rest[3:]
    o_ref, wb_ref = rest
    j = pl.program_id(0)
    i = pl.program_id(1)

    @pl.when(i == 0)
    def _():
        _assemble_weight(w_refs, wb_ref, shift, scale)

    if mode == "plain":
        o_ref[...] = jnp.dot(x_ref[...], wb_ref[...], preferred_element_type=F32).astype(o_ref.dtype)
        return
    tm = x_ref.shape[0]
    ts = tm // PROJ_SUBTILES
    ys = [jnp.dot(x_ref[s * ts:(s + 1) * ts, :], wb_ref[...], preferred_element_type=F32)
          for s in range(PROJ_SUBTILES)]
    rotate = (j == 0) if mode == "kv" else True
    for s, y in enumerate(ys):
        rows = slice(s * ts, (s + 1) * ts)
        cm = jnp.where(rotate, cm_ref[rows, :], 1.0)
        sp = jnp.where(rotate, sp_ref[rows, :], 0.0)
        sm = jnp.where(rotate, sm_ref[rows, :], 0.0)
        y = _rotate_heads(y, cm, sp, sm)
        if mode == "kv":
            y = _duplicate_heads(y)
        o_ref[rows, :] = y.astype(o_ref.dtype)


def _project(xb, w, layer, *, first_block, block_stride, tn, n_tiles, shift=0, scale=None, mode="plain",
             tables=None, out_dtype=BF16):
    m, k = xb.shape
    tm = _row_tile_big(m)
    nb = tn // LANES
    tn_out = 2 * tn if mode == "kv" else tn
    if shift == 0 and block_stride == nb and first_block % nb == 0:
        nblk = 1
        w_specs = [pl.BlockSpec((None, k, tn), lambda j, i: (layer, 0, first_block // nb + j))]
    else:
        nblk = nb + (1 if shift else 0)
        w_specs = _weight_block_specs(
            k, layer, [functools.partial(lambda j, i, b: first_block + j * block_stride + b, b=b) for b in range(nblk)])
    in_specs = [pl.BlockSpec((tm, k), lambda j, i: (i, 0))] + w_specs
    args = [xb] + [w] * nblk
    if mode != "plain":
        in_specs += [pl.BlockSpec((tm, LANES), lambda j, i: (i, 0))] * 3
        args += list(tables)
    return pl.pallas_call(
        functools.partial(_proj_kernel, nblk=nblk, shift=shift, scale=scale, mode=mode),
        grid=(n_tiles, m // tm),
        in_specs=in_specs,
        out_specs=pl.BlockSpec((tm, tn_out), lambda j, i: (i, j)),
        out_shape=jax.ShapeDtypeStruct((m, n_tiles * tn_out), out_dtype),
        scratch_shapes=[pltpu.VMEM((k, tn), BF16)],
        compiler_params=_params("arbitrary", "arbitrary"),
        name="proj_" + mode,
    )(*args)


def _mm_ln_kernel(*refs, nsrc, tm, subtiles):
    x_refs = refs[:nsrc]
    w_ref, res_ref, g_ref, b_ref, of_ref, ob_ref = refs[nsrc:]
    i = pl.program_id(0)
    ts = tm // subtiles
    sums = []
    for s in range(subtiles):
        rows = slice(s * ts, (s + 1) * ts)
        acc, k0 = None, 0
        for x_ref in x_refs:
            kw = x_ref.shape[1]
            part = jnp.dot(x_ref[rows, :], w_ref[k0:k0 + kw, :], preferred_element_type=F32)
            acc = part if acc is None else acc + part
            k0 += kw
        sums.append(acc)
    for s, acc in enumerate(sums):
        rows = slice(s * ts, (s + 1) * ts)
        t = DEEPNORM_ALPHA * res_ref[rows, :] + acc
        mu = jnp.mean(t, axis=-1, keepdims=True)
        d = t - mu
        var = jnp.mean(d * d, axis=-1, keepdims=True)
        y = d * lax.rsqrt(var + LN_EPS) * g_ref[...] + b_ref[...]
        row = lax.broadcasted_iota(jnp.int32, (ts, 1), 0) + (i * tm + s * ts)
        y = jnp.where(row >= FRONT_PAD, y, 0.0)
        of_ref[rows, :] = y
        ob_ref[rows, :] = y.astype(BF16)


def _matmul_residual_ln(xs, k_widths, w, layer, res, gamma, beta):
    m = xs[0].shape[0]
    _, kdim, n = w.shape
    assert sum(k_widths) == kdim and m % LN_ROW_TILE == 0
    tm = LN_ROW_TILE
    subtiles = LN_SUBTILES if 2 * kdim * n * 2 <= LN_WEIGHT_COPY_BUDGET else 1
    x_specs = [pl.BlockSpec((tm, kw), lambda i: (i, 0)) for kw in k_widths]
    return pl.pallas_call(
        functools.partial(_mm_ln_kernel, nsrc=len(xs), tm=tm, subtiles=subtiles),
        grid=(m // tm,),
        in_specs=x_specs + [pl.BlockSpec((None, kdim, n), lambda i: (layer, 0, 0), pipeline_mode=pl.Buffered(1)),
                            pl.BlockSpec((tm, n), lambda i: (i, 0)),
                            pl.BlockSpec((1, n), lambda i: (0, 0)),
                            pl.BlockSpec((1, n), lambda i: (0, 0))],
        out_specs=[pl.BlockSpec((tm, n), lambda i: (i, 0)),
                   pl.BlockSpec((tm, n), lambda i: (i, 0))],
        out_shape=[jax.ShapeDtypeStruct((m, n), F32), jax.ShapeDtypeStruct((m, n), BF16)],
        compiler_params=_params("arbitrary"),
        name="proj_ln",
    )(*xs, w, res, gamma.reshape(1, n), beta.reshape(1, n))


def _ffn_up_kernel(*refs, nblk, tm, tn):
    x_ref = refs[0]
    w_refs = refs[1:1 + nblk]
    cw_g, cw_v, cb_g, cb_v, o_ref, wb_ref, cg_ref, cv_ref = refs[1 + nblk:]
    j = pl.program_id(0)
    i = pl.program_id(1)

    @pl.when(i == 0)
    def _():
        _assemble_weight(w_refs, wb_ref, 0, None)
        cg_ref[...] = jnp.zeros_like(cg_ref)
        cv_ref[...] = jnp.zeros_like(cv_ref)

    col = lax.broadcasted_iota(jnp.int32, (1, tn), 1) + j * tn
    ts = tm // FFN_SUBTILES
    ys = []
    for s in range(FFN_SUBTILES):
        x = x_ref[s * ts:(s + 1) * ts, :]
        ys.append((jnp.dot(x, wb_ref[:, :tn], preferred_element_type=F32),
                   jnp.dot(x, wb_ref[:, tn:], preferred_element_type=F32)))
    carry_g, carry_v = cg_ref[...], cv_ref[...]
    for s, (yg, yv) in enumerate(ys):
        hg = _causal_conv_rows(yg, carry_g, cw_g, cb_g[...], FFN_CONV)
        hv = _causal_conv_rows(yv, carry_v, cw_v, cb_v[...], FFN_CONV)
        carry_g, carry_v = yg[ts - SUBLANES:ts], yv[ts - SUBLANES:ts]
        o_ref[s * ts:(s + 1) * ts, :] = jnp.where(col < D_FF, hg * _sigmoid(hg) * hv, 0.0).astype(o_ref.dtype)
    cg_ref[...] = carry_g
    cv_ref[...] = carry_v


def _ffn_up(xb, w_up, layer, conv_w, conv_b):
    m, k = xb.shape
    tm = _row_tile_big(m)
    assert (tm // FFN_SUBTILES) % 16 == 0, tm
    tn = FFN_TN
    nj = D_FF_PAD // tn
    nb = tn // LANES
    val0 = D_FF // LANES
    last_blk = 2 * D_FF // LANES - 1
    w_specs = [pl.BlockSpec((None, k, tn), lambda j, i: (layer, 0, j))] + _weight_block_specs(
        k, layer, [functools.partial(lambda j, i, b: jnp.minimum(val0 + j * nb + b, last_blk), b=b) for b in range(nb)])
    nblk = 1 + nb
    return pl.pallas_call(
        functools.partial(_ffn_up_kernel, nblk=nblk, tm=tm, tn=tn),
        grid=(nj, m // tm),
        in_specs=[pl.BlockSpec((tm, k), lambda j, i: (i, 0))] + w_specs
        + [pl.BlockSpec((FFN_CONV, tn), lambda j, i: (0, j)),
           pl.BlockSpec((FFN_CONV, tn), lambda j, i: (0, nj + j)),
           pl.BlockSpec((1, tn), lambda j, i: (0, j)),
           pl.BlockSpec((1, tn), lambda j, i: (0, nj + j))],
        out_specs=pl.BlockSpec((tm, tn), lambda j, i: (i, j)),
        out_shape=jax.ShapeDtypeStruct((m, D_FF_PAD), BF16),
        scratch_shapes=[pltpu.VMEM((k, 2 * tn), BF16), pltpu.VMEM((SUBLANES, tn), F32),
                        pltpu.VMEM((SUBLANES, tn), F32)],
        compiler_params=_params("arbitrary", "arbitrary"),
        name="ffn_up",
    )(xb, *([w_up] * nblk), conv_w, conv_w, conv_b, conv_b)


def _expand_heads(x, rows, g):
    low = lax.broadcasted_iota(jnp.int32, (1, LANES), 1) < SSD_HEAD_DIM
    h0 = g * SSD_HPG
    b = [jnp.broadcast_to(x[:, h0 + h:h0 + h + 1], (rows, LANES)) for h in range(SSD_HPG)]
    return jnp.concatenate([jnp.where(low, b[0], b[1]), jnp.where(low, b[2], b[3])], axis=1)


def _ssd_kernel(z_ref, x_ref, bc_ref, dt_ref, cwx_ref, cwbc_ref, cbx_ref, cbbc_ref, dtb_ref, alog_ref, dsk_ref,
                nw_ref, o_ref, st_ref, cx_ref, cbc_ref):
    c = pl.program_id(0)
    q = SSD_CHUNK
    gw = SSD_GROUP_W
    n = SSD_STATE
    groups = range(SSD_GROUPS)

    @pl.when(c == 0)
    def _():
        st_ref[...] = jnp.zeros_like(st_ref)
        cx_ref[...] = jnp.zeros_like(cx_ref)
        cbc_ref[...] = jnp.zeros_like(cbc_ref)

    row = lax.broadcasted_iota(jnp.int32, (q, 1), 0) + c * q
    lane = lax.broadcasted_iota(jnp.int32, (1, LANES), 1)
    dt = _softplus(dt_ref[...] + dtb_ref[...])
    dt = jnp.where((row >= FRONT_PAD) & (lane < SSD_HEADS), dt, 0.0)
    a = -jnp.exp(alog_ref[...])
    ri = lax.broadcasted_iota(jnp.int32, (q, q), 0)
    ci = lax.broadcasted_iota(jnp.int32, (q, q), 1)
    causal = ci <= ri
    cs = jnp.dot(causal.astype(F32), dt * a, precision=HIGHEST, preferred_element_type=F32)
    cs_t = cs.T
    cs_last = cs[q - 1:q, :]
    from_start = jnp.exp(cs)
    to_end = jnp.exp(cs_last - cs)
    total = jnp.exp(cs_last)

    def conv_silu(raw_ref, carry_ref, w_ref, b_ref, lo, width):
        raw = raw_ref[:, lo:lo + width].astype(F32)
        y = _causal_conv_rows(raw, carry_ref[:, lo:lo + width], w_ref.at[:, lo:lo + width], b_ref[:, lo:lo + width],
                              SSD_CONV)
        carry_ref[:, lo:lo + width] = raw[q - SUBLANES:q]
        return y * _sigmoid(y)

    xs = [conv_silu(x_ref, cx_ref, cwx_ref, cbx_ref, g * gw, gw) for g in groups]
    bm_t = [conv_silu(bc_ref, cbc_ref, cwbc_ref, cbbc_ref, g * n, n).T.astype(BF16) for g in groups]
    cmb = [conv_silu(bc_ref, cbc_ref, cwbc_ref, cbbc_ref, SSD_GROUPS * n + g * n, n).astype(BF16) for g in groups]
    cb = [jnp.dot(cmb[g], bm_t[g], preferred_element_type=F32) for g in groups]

    lane_w = lax.broadcasted_iota(jnp.int32, (1, gw), 1)
    xd = [xs[g] * _expand_heads(dt, q, g) for g in groups]
    ys = []
    for g in groups:
        decayed = []
        for h in range(SSD_HPG):
            hh = g * SSD_HPG + h
            seg = cs[:, hh:hh + 1] - cs_t[hh:hh + 1, :]
            decayed.append((cb[g] * jnp.exp(jnp.where(causal, seg, -jnp.inf))).astype(BF16))
        xdb = xd[g].astype(BF16)
        x_heads = [jnp.where((lane_w >= h * SSD_HEAD_DIM) & (lane_w < (h + 1) * SSD_HEAD_DIM), xdb,
                             jnp.zeros_like(xdb)) for h in range(SSD_HPG)]
        ys.append(jnp.dot(jnp.concatenate(decayed, axis=1), jnp.concatenate(x_heads, axis=0),
                          preferred_element_type=F32))
    for g in groups:
        st = st_ref[g]
        ys[g] = ys[g] + (jnp.dot(cmb[g], st.astype(BF16), preferred_element_type=F32)
                         * _expand_heads(from_start, q, g))
        st_ref[g] = (st * _expand_heads(total, 1, g)
                     + jnp.dot(bm_t[g], (xd[g] * _expand_heads(to_end, q, g)).astype(BF16),
                               preferred_element_type=F32))
    for g in groups:
        cols = slice(g * gw, (g + 1) * gw)
        y = ys[g] + xs[g] * dsk_ref[:, cols]
        z = z_ref[:, cols].astype(F32)
        yg = y * (z * _sigmoid(z))
        yn = yg * lax.rsqrt(jnp.mean(yg * yg, axis=-1, keepdims=True) + RMS_EPS) * nw_ref[:, cols]
        o_ref[:, cols] = yn.astype(o_ref.dtype)


def _ssd_mixer(u_zx, u_small, conv_w, conv_b, dt_bias, a_log, d_skip, norm_w):
    rows = u_zx.shape[0]
    q = SSD_CHUNK
    d = SSD_GROUPS * SSD_GROUP_W
    pad_l = lambda v: jnp.pad(v.reshape(1, SSD_HEADS), ((0, 0), (0, LANES - SSD_HEADS)))
    whole = lambda r, w, j: pl.BlockSpec((r, w), lambda c: (0, j))
    return pl.pallas_call(
        _ssd_kernel,
        grid=(rows // q,),
        in_specs=[pl.BlockSpec((q, d), lambda c: (c, 0)),
                  pl.BlockSpec((q, d), lambda c: (c, 1)),
                  pl.BlockSpec((q, d), lambda c: (c, 2)),
                  pl.BlockSpec((q, LANES), lambda c: (c, 0)),
                  whole(SSD_CONV, d, 0), whole(SSD_CONV, d, 1), whole(1, d, 0), whole(1, d, 1),
                  whole(1, LANES, 0), whole(1, LANES, 0), whole(1, d, 0), whole(1, d, 0)],
        out_specs=pl.BlockSpec((q, d), lambda c: (c, 0)),
        out_shape=jax.ShapeDtypeStruct((rows, d), BF16),
        scratch_shapes=[pltpu.VMEM((SSD_GROUPS, SSD_STATE, SSD_GROUP_W), F32), pltpu.VMEM((SUBLANES, d), F32),
                        pltpu.VMEM((SUBLANES, d), F32)],
        compiler_params=_params("arbitrary"),
        name="ssd",
    )(u_zx, u_zx, u_zx, u_small, conv_w, conv_w, conv_b.reshape(1, -1), conv_b.reshape(1, -1),
      pad_l(dt_bias), pad_l(a_log), jnp.repeat(d_skip, SSD_HEAD_DIM).reshape(1, d), norm_w.reshape(1, d))


def _gla_scores_blocked(q, k, gc):
    qc = q.shape[0]
    sub = GLA_SUB
    lane_j = lax.broadcasted_iota(jnp.int32, (sub, qc), 1)
    row_i = lax.broadcasted_iota(jnp.int32, (sub, 1), 0)
    a_rows = []
    for blk in range(qc // sub):
        lo = blk * sub
        q_b = q[lo:lo + sub]
        g_b = gc[lo:lo + sub]
        a_blk = jnp.zeros((sub, qc), F32)
        for j in range(sub):
            k_j = k[lo + j:lo + j + 1, :]
            g_j = gc[lo + j:lo + j + 1, :]
            s_j = jnp.sum(q_b * k_j * jnp.exp(jnp.minimum(g_b - g_j, 0.0)), axis=1, keepdims=True)
            a_blk = jnp.where(lane_j == lo + j, jnp.where(row_i >= j, s_j, 0.0), a_blk)
        if blk > 0:
            g_ref0 = gc[lo:lo + 1, :]
            q_t = (q_b * jnp.exp(g_b - g_ref0)).astype(BF16)
            k_t = (k * jnp.exp(jnp.minimum(g_ref0 - gc, 0.0))).astype(BF16)
            off = lax.dot_general(q_t, k_t, (((1,), (1,)), ((), ())), preferred_element_type=F32)
            a_blk = jnp.where(lane_j < lo, off, a_blk)
        a_rows.append(a_blk)
    return jnp.concatenate(a_rows, axis=0)


def _gla_kernel(q_ref, k_ref, v_ref, r_ref, glr_ref, w2_ref, gb_ref, nw_ref, o_ref, st_ref, a_ref):
    c = pl.program_id(0)
    qc = GLA_CHUNK
    dk, dv = GLA_HEAD_DK, GLA_HEAD_DV
    heads = range(GLA_HEADS)
    nt = (((1,), (1,)), ((), ()))

    @pl.when(c == 0)
    def _():
        st_ref[...] = jnp.zeros_like(st_ref)

    pre = jnp.dot(glr_ref[...], w2_ref[...], precision=HIGHEST, preferred_element_type=F32) + gb_ref[...]
    g = _log_sigmoid(pre) * (1.0 / GLA_GATE_TAU)
    row = lax.broadcasted_iota(jnp.int32, (qc, 1), 0) + c * qc
    g = jnp.where(row >= FRONT_PAD, g, 0.0)
    ri = lax.broadcasted_iota(jnp.int32, (qc, qc), 0)
    ci = lax.broadcasted_iota(jnp.int32, (qc, qc), 1)
    causal = ci <= ri
    gc = jnp.dot(causal.astype(F32), g, precision=HIGHEST, preferred_element_type=F32)
    g_last = gc[qc - 1:qc, :]

    q = q_ref[...].astype(F32) * (GLA_HEAD_DK ** -0.5)
    k = k_ref[...].astype(F32)
    q_dec = (q * jnp.exp(gc)).astype(BF16)
    k_end = (k * jnp.exp(g_last - gc)).astype(BF16)
    safe = jnp.max(-g_last) <= GLA_SAFE_DECAY

    @pl.when(safe)
    def _():
        k_inv = (k * jnp.exp(-gc)).astype(BF16)
        for h in heads:
            s = lax.dot_general(q_dec[:, h * dk:(h + 1) * dk], k_inv[:, h * dk:(h + 1) * dk], nt,
                                preferred_element_type=F32)
            a_ref[h] = jnp.where(causal, s, 0.0)

    @pl.when(jnp.logical_not(safe))
    def _():
        for h in heads:
            a_ref[h] = _gla_scores_blocked(q[:, h * dk:(h + 1) * dk], k[:, h * dk:(h + 1) * dk],
                                           gc[:, h * dk:(h + 1) * dk])

    vb = v_ref[...]
    decay = jnp.exp(g_last)
    outs = []
    for h in heads:
        st = st_ref[h]
        v_h = vb[:, h * dv:(h + 1) * dv]
        o = jnp.dot(a_ref[h].astype(BF16), v_h, preferred_element_type=F32)
        o = o + lax.dot_general(q_dec[:, h * dk:(h + 1) * dk], st.astype(BF16), nt, preferred_element_type=F32)
        st_ref[h] = st * decay[:, h * dk:(h + 1) * dk] + lax.dot_general(
            v_h, k_end[:, h * dk:(h + 1) * dk], (((0,), (0,)), ((), ())), preferred_element_type=F32)
        outs.append(o)
    for h in heads:
        o = outs[h]
        on = o * lax.rsqrt(jnp.mean(o * o, axis=-1, keepdims=True) + RMS_EPS) * nw_ref[:, h * dv:(h + 1) * dv]
        r = r_ref[:, h * dv:(h + 1) * dv].astype(F32)
        o_ref[:, h * dv:(h + 1) * dv] = (on * (r * _sigmoid(r))).astype(o_ref.dtype)


def _gla_mixer(u_qkv, u_r, u_small, gate_w2, gate_b, norm_w):
    rows = u_qkv.shape[0]
    qc = GLA_CHUNK
    dk, dv = GLA_HEAD_DK, GLA_HEAD_DV
    wk, wv = GLA_HEADS * dk, GLA_HEADS * dv
    glr_lane = HYB_GLR_COL % LANES
    w2 = jnp.pad(gate_w2, ((glr_lane, LANES - GLA_GATE_RANK - glr_lane), (0, 0)))
    return pl.pallas_call(
        _gla_kernel,
        grid=(rows // qc,),
        in_specs=[pl.BlockSpec((qc, wk), lambda c: (c, 0)),
                  pl.BlockSpec((qc, wk), lambda c: (c, 1)),
                  pl.BlockSpec((qc, wv), lambda c: (c, 1)),
                  pl.BlockSpec((qc, wv), lambda c: (c, 0)),
                  pl.BlockSpec((qc, LANES), lambda c: (c, 1)),
                  pl.BlockSpec((LANES, wk), lambda c: (0, 0)),
                  pl.BlockSpec((1, wk), lambda c: (0, 0)),
                  pl.BlockSpec((1, wv), lambda c: (0, 0))],
        out_specs=pl.BlockSpec((qc, wv), lambda c: (c, 0)),
        out_shape=jax.ShapeDtypeStruct((rows, wv), BF16),
        scratch_shapes=[pltpu.VMEM((GLA_HEADS, dv, dk), F32), pltpu.VMEM((GLA_HEADS, qc, qc), F32)],
        compiler_params=_params("arbitrary"),
        name="gla",
    )(u_qkv, u_qkv, u_qkv, u_r, u_small, w2, gate_b.reshape(1, -1), norm_w.reshape(1, -1))


def _rope_kernel(freq_ref, cm_ref, sp_ref, sm_ref, *, tm):
    i = pl.program_id(0)
    row = lax.broadcasted_iota(jnp.int32, (tm, LANES), 0) + i * tm
    lane = lax.broadcasted_iota(jnp.int32, (tm, LANES), 1) % SWA_HEAD_DIM
    ang = (row - FRONT_PAD).astype(F32) * freq_ref[...]
    cos = jnp.cos(ang)
    sin = jnp.sin(ang)
    half = ROPE_DIM // 2
    cm_ref[...] = cos
    sp_ref[...] = jnp.where(lane < half, -sin, 0.0)
    sm_ref[...] = jnp.where((lane >= half) & (lane < ROPE_DIM), sin, 0.0)


def _rope_tables(rows):
    half = ROPE_DIM // 2
    inv_freq = ROPE_THETA ** (-jnp.arange(half, dtype=F32) / half)
    per_head = jnp.concatenate([inv_freq, inv_freq, jnp.zeros((SWA_HEAD_DIM - ROPE_DIM,), F32)])
    freq = jnp.tile(per_head, LANES // SWA_HEAD_DIM).reshape(1, LANES)
    tm = _row_tile(rows)
    shp = jax.ShapeDtypeStruct((rows, LANES), F32)
    spec = pl.BlockSpec((tm, LANES), lambda i: (i, 0))
    return pl.pallas_call(
        functools.partial(_rope_kernel, tm=tm),
        grid=(rows // tm,),
        in_specs=[pl.BlockSpec((1, LANES), lambda i: (0, 0))],
        out_specs=[spec, spec, spec],
        out_shape=[shp, shp, shp],
        compiler_params=_params("arbitrary"),
        name="rope_tables",
    )(freq)


def _swa_kernel(sink_ref, q_ref, prev_ref, cur_ref, meta_ref, o_ref):
    n = pl.program_id(0)
    w = SWA_WINDOW
    meta_lo = FRONT_PAD
    kvw = SWA_KV_HEADS * LANES
    nt = (((1,), (1,)), ((), ()))
    lane = lax.broadcasted_iota(jnp.int32, (1, LANES), 1)
    low = lane < SWA_HEAD_DIM
    i = lax.broadcasted_iota(jnp.int32, (SWA_GROUP * w, 1), 0) % w
    hrow = lax.broadcasted_iota(jnp.int32, (SWA_GROUP * w, 1), 0) // w
    on_cur = lane <= i
    valid_band = (on_cur & ((n >= 1) | (lane >= meta_lo))) | (jnp.logical_not(on_cur) & (n >= 2))
    valid_meta = (lane >= meta_lo) & (n >= 1)
    zero = jnp.zeros((w, LANES), q_ref.dtype)
    kv_heads = range(SWA_KV_HEADS)

    scores, sinks = [], []
    for g in kv_heads:
        qa = q_ref[:, 2 * g * LANES:(2 * g + 1) * LANES]
        qb = q_ref[:, (2 * g + 1) * LANES:(2 * g + 2) * LANES]
        qs = jnp.concatenate([jnp.where(low, qa, zero), jnp.where(low, zero, qa),
                              jnp.where(low, qb, zero), jnp.where(low, zero, qb)], axis=0)
        kcols = slice(g * LANES, (g + 1) * LANES)
        kk = jnp.concatenate([prev_ref[:, kcols], cur_ref[:, kcols], meta_ref[:, kcols]], axis=0)
        s = lax.dot_general(qs, kk, nt, preferred_element_type=F32)
        s_band = jnp.where(on_cur, s[:, w:2 * w], s[:, :w])
        scores.append(jnp.concatenate([jnp.where(valid_band, s_band, -jnp.inf),
                                       jnp.where(valid_meta, s[:, 2 * w:], -jnp.inf)], axis=1))
        sink = jnp.zeros((SWA_GROUP * w, 1), F32)
        for h in range(SWA_GROUP):
            sink = jnp.where(hrow == h, sink_ref[g * SWA_GROUP + h], sink)
        sinks.append(sink)
    for g in kv_heads:
        s, sink = scores[g], sinks[g]
        m = jnp.maximum(jnp.max(s, axis=-1, keepdims=True), sink)
        p = jnp.exp(s - m)
        denom = jnp.sum(p, axis=-1, keepdims=True) + jnp.exp(sink - m)
        p_band = p[:, :w]
        pb = jnp.concatenate([jnp.where(on_cur, 0.0, p_band), jnp.where(on_cur, p_band, 0.0), p[:, w:]],
                             axis=1).astype(BF16)
        vcols = slice(kvw + g * LANES, kvw + (g + 1) * LANES)
        vv = jnp.concatenate([prev_ref[:, vcols], cur_ref[:, vcols], meta_ref[:, vcols]], axis=0)
        o = jnp.dot(pb, vv, preferred_element_type=F32) / denom
        oa = jnp.where(low, o[0:w], o[w:2 * w])
        ob = jnp.where(low, o[2 * w:3 * w], o[3 * w:4 * w])
        o_ref[:, 2 * g * LANES:(2 * g + 2) * LANES] = jnp.concatenate([oa, ob], axis=1).astype(o_ref.dtype)


def _swa_attention(q, kv, sinks):
    rows, qw = q.shape
    w = SWA_WINDOW
    kvw = kv.shape[1]
    return pl.pallas_call(
        _swa_kernel,
        grid=(rows // w,),
        in_specs=[pl.BlockSpec(memory_space=pltpu.SMEM),
                  pl.BlockSpec((w, qw), lambda n: (n, 0)),
                  pl.BlockSpec((w, kvw), lambda n: (jnp.maximum(n - 1, 0), 0)),
                  pl.BlockSpec((w, kvw), lambda n: (n, 0)),
                  pl.BlockSpec((w, kvw), lambda n: (0, 0))],
        out_specs=pl.BlockSpec((w, qw), lambda n: (n, 0)),
        out_shape=jax.ShapeDtypeStruct((rows, qw), BF16),
        compiler_params=_params("arbitrary"),
        name="swa",
    )(sinks, q, kv, kv, kv)


def _pad_halves(t):
    pad = lambda a: jnp.pad(a, ((0, 0), (0, D_FF_PAD - D_FF)))
    return jnp.concatenate([pad(t[:, :D_FF]), pad(t[:, D_FF:])], axis=1)


def _trunk(x, meta_tokens, hyb_w_in, hyb_conv_w, hyb_conv_b, ssd_dt_bias, ssd_a_log, ssd_d, ssd_norm_w,
           gla_gate_w2, gla_gate_b, gla_norm_w, hyb_w_out, swa_w_qkv, swa_sinks, swa_w_out,
           ffn_w_up, ffn_conv_w, ffn_conv_b, ffn_w_down, ln_mix_g, ln_mix_b, ln_ffn_g, ln_ffn_b):
    seq = x.shape[0]
    rows = FRONT_PAD + N_META + seq
    h = jnp.concatenate([jnp.zeros((FRONT_PAD, D_MODEL), F32), meta_tokens.astype(F32), x], axis=0)
    hb = h.astype(BF16)
    tables = _rope_tables(rows)
    hyb_w_out_b = hyb_w_out.astype(BF16)
    swa_w_out_b = swa_w_out.astype(BF16)
    ffn_w_down_b = ffn_w_down.astype(BF16)
    hyb_w_in = jnp.pad(hyb_w_in.astype(BF16), ((0, 0), (0, 0), (0, -hyb_w_in.shape[2] % LANES)))
    for layer in range(DEPTH):
        j = layer // 2
        if layer % 2 == 0:
            u_zx = _project(hb, hyb_w_in, j, first_block=0, block_stride=4, tn=512, n_tiles=12)
            u_qkv = _project(hb, hyb_w_in, j, first_block=HYB_QKV_COL // LANES, block_stride=4, tn=512, n_tiles=8,
                             shift=HYB_QKV_COL % LANES)
            u_r = _project(hb, hyb_w_in, j, first_block=HYB_R_COL // LANES, block_stride=4, tn=512, n_tiles=4,
                           shift=HYB_R_COL % LANES)
            u_small = _project(hb, hyb_w_in, j, first_block=HYB_DT_COL // LANES,
                               block_stride=(HYB_GLR_COL - HYB_DT_COL) // LANES, tn=LANES, n_tiles=2, out_dtype=F32)
            y_ssd = _ssd_mixer(u_zx, u_small, hyb_conv_w[j], hyb_conv_b[j], ssd_dt_bias[j], ssd_a_log[j],
                               ssd_d[j], ssd_norm_w[j])
            y_gla = _gla_mixer(u_qkv, u_r, u_small, gla_gate_w2[j], gla_gate_b[j], gla_norm_w[j])
            h, hb = _matmul_residual_ln([y_ssd, y_gla], [y_ssd.shape[1], y_gla.shape[1]], hyb_w_out_b, j, h,
                                        ln_mix_g[layer], ln_mix_b[layer])
        else:
            q = _project(hb, swa_w_qkv, j, first_block=0, block_stride=4, tn=512, n_tiles=4,
                         scale=SWA_HEAD_DIM ** -0.5, mode="rope", tables=tables)
            kv = _project(hb, swa_w_qkv, j, first_block=SWA_Q_HEADS * SWA_HEAD_DIM // LANES, block_stride=4, tn=512,
                          n_tiles=2, mode="kv", tables=tables)
            attn = _swa_attention(q, kv, swa_sinks[j])
            h, hb = _matmul_residual_ln([attn], [attn.shape[1]], swa_w_out_b, j, h, ln_mix_g[layer], ln_mix_b[layer])
        act = _ffn_up(hb, ffn_w_up, layer, _pad_halves(ffn_conv_w[layer]),
                      _pad_halves(ffn_conv_b[layer].reshape(1, -1)))
        h, hb = _matmul_residual_ln([act], [D_FF], ffn_w_down_b, layer, h, ln_ffn_g[layer], ln_ffn_b[layer])
    return h[FRONT_PAD + N_META:]


def kernel(x, meta_tokens, hyb_w_in, hyb_conv_w, hyb_conv_b, ssd_dt_bias, ssd_a_log, ssd_d, ssd_norm_w,
           gla_gate_w2, gla_gate_b, gla_norm_w, hyb_w_out, swa_w_qkv, swa_sinks, swa_w_out,
           ffn_w_up, ffn_conv_w, ffn_conv_b, ffn_w_down, ln_mix_g, ln_mix_b, ln_ffn_g, ln_ffn_b):
    params = (meta_tokens, hyb_w_in, hyb_conv_w, hyb_conv_b, ssd_dt_bias, ssd_a_log, ssd_d, ssd_norm_w,
              gla_gate_w2, gla_gate_b, gla_norm_w, hyb_w_out, swa_w_qkv, swa_sinks, swa_w_out,
              ffn_w_up, ffn_conv_w, ffn_conv_b, ffn_w_down, ln_mix_g, ln_mix_b, ln_ffn_g, ln_ffn_b)
    return jnp.stack([_trunk(x[b], *params) for b in range(x.shape[0])], axis=0)
```

```python
import functools

import jax
import jax.numpy as jnp
from jax import lax
from jax.experimental import pallas as pl
from jax.experimental.pallas import tpu as pltpu

F32 = jnp.float32
BF16 = jnp.bfloat16
HIGHEST = lax.Precision.HIGHEST

D_MODEL = 2048
DEPTH = 4
N_META = 16
LN_EPS = 1e-5
RMS_EPS = 1e-6
DEEPNORM_ALPHA = (2.0 * DEPTH) ** 0.25

SSD_HEAD_DIM = 64
SSD_HEADS = 32
SSD_GROUPS = 8
SSD_HPG = 4
SSD_STATE = 128
SSD_CONV = 4
SSD_CHUNK = 128
SSD_GROUP_W = SSD_HPG * SSD_HEAD_DIM

GLA_HEADS = 4
GLA_HEAD_DK = 256
GLA_HEAD_DV = 512
GLA_GATE_RANK = 16
GLA_GATE_TAU = 16.0
GLA_CHUNK = 128
GLA_SUB = 16
GLA_SAFE_DECAY = 80.0

SWA_HEAD_DIM = 64
SWA_Q_HEADS = 32
SWA_KV_HEADS = 8
SWA_GROUP = 4
SWA_WINDOW = 128
ROPE_THETA = 500000.0
ROPE_DIM = 16

D_FF = 5504
FFN_CONV = 3

LANES = 128
SUBLANES = 8
MXU_DEPTH = 256
FRONT_PAD = SSD_CHUNK - N_META
D_FF_PAD = 5632
FFN_TN = 512
FFN_SUBTILES = 8
PROJ_SUBTILES = 4
LN_ROW_TILE = 320
LN_SUBTILES = 2
LN_WEIGHT_COPY_BUDGET = 20 * 1024 * 1024
VMEM_LIMIT = 56 * 1024 * 1024

HYB_DT_COL = 6144
HYB_QKV_COL = 6176
HYB_GLR_COL = 10272
HYB_R_COL = 10288


def _row_tile(rows):
    for t in (640, 512, 384, 256, 128):
        if rows % t == 0:
            return t
    raise ValueError(f"row count {rows} is not a multiple of 128")


def _row_tile_big(rows):
    return 1664 if rows % 1664 == 0 else _row_tile(rows)


def _params(*sem):
    return pltpu.CompilerParams(dimension_semantics=sem, vmem_limit_bytes=VMEM_LIMIT)


def _sigmoid(x):
    return 1.0 / (1.0 + jnp.exp(-x))


def _softplus(x):
    return jnp.maximum(x, 0.0) + jnp.log(1.0 + jnp.exp(-jnp.abs(x)))


def _log_sigmoid(x):
    return jnp.minimum(x, 0.0) - jnp.log(1.0 + jnp.exp(-jnp.abs(x)))


def _causal_conv_rows(y, carry, w_ref, b_row, taps):
    top = jnp.concatenate([carry, y[0:SUBLANES]], axis=0)
    w_last = w_ref[taps - 1:taps, :]
    acc = b_row + w_last * y
    acc_top = b_row + w_last * y[0:SUBLANES]
    for s in range(1, taps):
        wk = w_ref[taps - 1 - s:taps - s, :]
        acc = acc + wk * pltpu.roll(y, s, 0)
        acc_top = acc_top + wk * pltpu.roll(top, s, 0)[SUBLANES:2 * SUBLANES]
    return jnp.concatenate([acc_top, acc[SUBLANES:]], axis=0)


def _assemble_weight(w_refs, wb_ref, shift, scale):
    k, tn = wb_ref.shape
    chunk = 256
    for r in range(0, k, chunk):
        w = jnp.concatenate([wr[r:r + chunk, :] for wr in w_refs], axis=1)
        if shift:
            w = pltpu.roll(w.astype(F32), w.shape[1] - shift, 1)
        w = w[:, :tn]
        if scale is not None:
            w = w * scale
        wb_ref[r:r + chunk, :] = w.astype(BF16)


def _weight_block_specs(k, layer, block_fns):
    return [pl.BlockSpec((None, k, LANES), functools.partial(lambda *ids, fn: (layer, 0, fn(*ids)), fn=fn))
            for fn in block_fns]


def _rotate_heads(y, cm, sp, sm):
    half = ROPE_DIM // 2
    out = []
    for c in range(y.shape[1] // LANES):
        yc = y[:, c * LANES:(c + 1) * LANES]
        out.append(yc * cm + pltpu.roll(yc, LANES - half, 1) * sp + pltpu.roll(yc, half, 1) * sm)
    return jnp.concatenate(out, axis=1)


def _duplicate_heads(y):
    low = lax.broadcasted_iota(jnp.int32, (1, LANES), 1) < SWA_HEAD_DIM
    out = []
    for c in range(y.shape[1] // LANES):
        yc = y[:, c * LANES:(c + 1) * LANES]
        rolled = pltpu.roll(yc, SWA_HEAD_DIM, 1)
        out.append(jnp.where(low, yc, rolled))
        out.append(jnp.where(low, rolled, yc))
    return jnp.concatenate(out, axis=1)


def _proj_kernel(*refs, nblk, shift, scale, mode):
    x_ref = refs[0]
    w_refs = refs[1:1 + nblk]
    rest = refs[1 + nblk:]
    if mode != "plain":
        cm_ref, sp_ref, sm_ref = rest[:3]
        rest = rest[3:]
    o_ref, wb_ref = rest
    j = pl.program_id(0)
    i = pl.program_id(1)

    @pl.when(i == 0)
    def _():
        _assemble_weight(w_refs, wb_ref, shift, scale)

    if mode == "plain":
        o_ref[...] = jnp.dot(x_ref[...], wb_ref[...], preferred_element_type=F32).astype(o_ref.dtype)
        return
    tm = x_ref.shape[0]
    ts = tm // PROJ_SUBTILES
    ys = [jnp.dot(x_ref[s * ts:(s + 1) * ts, :], wb_ref[...], preferred_element_type=F32)
          for s in range(PROJ_SUBTILES)]
    rotate = (j == 0) if mode == "kv" else True
    for s, y in enumerate(ys):
        rows = slice(s * ts, (s + 1) * ts)
        cm = jnp.where(rotate, cm_ref[rows, :], 1.0)
        sp = jnp.where(rotate, sp_ref[rows, :], 0.0)
        sm = jnp.where(rotate, sm_ref[rows, :], 0.0)
        y = _rotate_heads(y, cm, sp, sm)
        if mode == "kv":
            y = _duplicate_heads(y)
        o_ref[rows, :] = y.astype(o_ref.dtype)


def _project(xb, w, layer, *, first_block, block_stride, tn, n_tiles, shift=0, scale=None, mode="plain",
             tables=None, out_dtype=BF16):
    m, k = xb.shape
    tm = _row_tile_big(m)
    nb = tn // LANES
    tn_out = 2 * tn if mode == "kv" else tn
    if shift == 0 and block_stride == nb and first_block % nb == 0:
        nblk = 1
        w_specs = [pl.BlockSpec((None, k, tn), lambda j, i: (layer, 0, first_block // nb + j))]
    else:
        nblk = nb + (1 if shift else 0)
        w_specs = _weight_block_specs(
            k, layer, [functools.partial(lambda j, i, b: first_block + j * block_stride + b, b=b) for b in range(nblk)])
    in_specs = [pl.BlockSpec((tm, k), lambda j, i: (i, 0))] + w_specs
    args = [xb] + [w] * nblk
    if mode != "plain":
        in_specs += [pl.BlockSpec((tm, LANES), lambda j, i: (i, 0))] * 3
        args += list(tables)
    return pl.pallas_call(
        functools.partial(_proj_kernel, nblk=nblk, shift=shift, scale=scale, mode=mode),
        grid=(n_tiles, m // tm),
        in_specs=in_specs,
        out_specs=pl.BlockSpec((tm, tn_out), lambda j, i: (i, j)),
        out_shape=jax.ShapeDtypeStruct((m, n_tiles * tn_out), out_dtype),
        scratch_shapes=[pltpu.VMEM((k, tn), BF16)],
        compiler_params=_params("arbitrary", "arbitrary"),
        name="proj_" + mode,
    )(*args)


def _mm_ln_kernel(*refs, nsrc, tm, subtiles, nchunks, ck):
    x_refs = refs[:nsrc]
    w_ref, res_ref, g_ref, b_ref, of_ref, ob_ref, wb_ref = refs[nsrc:]
    step = pl.program_id(0)

    @pl.when(step < nchunks)
    def _():
        wb_ref[pl.ds(pl.multiple_of(step * ck, ck), ck), :] = w_ref[...].astype(BF16)

    @pl.when(step >= nchunks)
    def _():
        i = step - nchunks
        ts = tm // subtiles
        sums = []
        for s in range(subtiles):
            rows = slice(s * ts, (s + 1) * ts)
            acc, k0 = None, 0
            for x_ref in x_refs:
                kw = x_ref.shape[1]
                part = jnp.dot(x_ref[rows, :], wb_ref[k0:k0 + kw, :], preferred_element_type=F32)
                acc = part if acc is None else acc + part
                k0 += kw
            sums.append(acc)
        for s, acc in enumerate(sums):
            rows = slice(s * ts, (s + 1) * ts)
            t = DEEPNORM_ALPHA * res_ref[rows, :] + acc
            mu = jnp.mean(t, axis=-1, keepdims=True)
            d = t - mu
            var = jnp.mean(d * d, axis=-1, keepdims=True)
            y = d * lax.rsqrt(var + LN_EPS) * g_ref[...] + b_ref[...]
            row = lax.broadcasted_iota(jnp.int32, (ts, 1), 0) + (i * tm + s * ts)
            y = jnp.where(row >= FRONT_PAD, y, 0.0)
            of_ref[rows, :] = y
            ob_ref[rows, :] = y.astype(BF16)


def _matmul_residual_ln(xs, k_widths, w, layer, res, gamma, beta, nchunks):
    m = xs[0].shape[0]
    _, kdim, n = w.shape
    assert sum(k_widths) == kdim and m % LN_ROW_TILE == 0 and kdim % (16 * nchunks) == 0
    tm = LN_ROW_TILE
    ck = kdim // nchunks
    subtiles = LN_SUBTILES if 2 * kdim * n * 2 <= LN_WEIGHT_COPY_BUDGET else 1
    tile = lambda s: jnp.maximum(s - nchunks, 0)
    x_specs = [pl.BlockSpec((tm, kw), lambda s: (tile(s), 0)) for kw in k_widths]
    return pl.pallas_call(
        functools.partial(_mm_ln_kernel, nsrc=len(xs), tm=tm, subtiles=subtiles, nchunks=nchunks, ck=ck),
        grid=(nchunks + m // tm,),
        in_specs=x_specs + [pl.BlockSpec((None, ck, n), lambda s: (layer, jnp.minimum(s, nchunks - 1), 0)),
                            pl.BlockSpec((tm, n), lambda s: (tile(s), 0)),
                            pl.BlockSpec((1, n), lambda s: (0, 0)),
                            pl.BlockSpec((1, n), lambda s: (0, 0))],
        out_specs=[pl.BlockSpec((tm, n), lambda s: (tile(s), 0)),
                   pl.BlockSpec((tm, n), lambda s: (tile(s), 0))],
        out_shape=[jax.ShapeDtypeStruct((m, n), F32), jax.ShapeDtypeStruct((m, n), BF16)],
        scratch_shapes=[pltpu.VMEM((kdim, n), BF16)],
        compiler_params=_params("arbitrary"),
        name="proj_ln",
    )(*xs, w, res, gamma.reshape(1, n), beta.reshape(1, n))


def _ffn_up_kernel(*refs, nblk, tm, tn):
    x_ref = refs[0]
    w_refs = refs[1:1 + nblk]
    cw_g, cw_v, cb_g, cb_v, o_ref, wb_ref, cg_ref, cv_ref = refs[1 + nblk:]
    j = pl.program_id(0)
    i = pl.program_id(1)

    @pl.when(i == 0)
    def _():
        _assemble_weight(w_refs, wb_ref, 0, None)
        cg_ref[...] = jnp.zeros_like(cg_ref)
        cv_ref[...] = jnp.zeros_like(cv_ref)

    col = lax.broadcasted_iota(jnp.int32, (1, tn), 1) + j * tn
    ts = tm // FFN_SUBTILES
    ys = []
    for s in range(FFN_SUBTILES):
        y = jnp.dot(x_ref[s * ts:(s + 1) * ts, :], wb_ref[...], preferred_element_type=F32)
        ys.append((y[:, :tn], y[:, tn:]))
    carry_g, carry_v = cg_ref[...], cv_ref[...]
    for s, (yg, yv) in enumerate(ys):
        hg = _causal_conv_rows(yg, carry_g, cw_g, cb_g[...], FFN_CONV)
        hv = _causal_conv_rows(yv, carry_v, cw_v, cb_v[...], FFN_CONV)
        carry_g, carry_v = yg[ts - SUBLANES:ts], yv[ts - SUBLANES:ts]
        o_ref[s * ts:(s + 1) * ts, :] = jnp.where(col < D_FF, hg * _sigmoid(hg) * hv, 0.0).astype(o_ref.dtype)
    cg_ref[...] = carry_g
    cv_ref[...] = carry_v


def _ffn_up(xb, w_up, layer, conv_w, conv_b):
    m, k = xb.shape
    tm = _row_tile_big(m)
    assert (tm // FFN_SUBTILES) % 16 == 0, tm
    tn = FFN_TN
    nj = D_FF_PAD // tn
    nb = tn // LANES
    val0 = D_FF // LANES
    last_blk = 2 * D_FF // LANES - 1
    w_specs = [pl.BlockSpec((None, k, tn), lambda j, i: (layer, 0, j))] + _weight_block_specs(
        k, layer, [functools.partial(lambda j, i, b: jnp.minimum(val0 + j * nb + b, last_blk), b=b) for b in range(nb)])
    nblk = 1 + nb
    return pl.pallas_call(
        functools.partial(_ffn_up_kernel, nblk=nblk, tm=tm, tn=tn),
        grid=(nj, m // tm),
        in_specs=[pl.BlockSpec((tm, k), lambda j, i: (i, 0))] + w_specs
        + [pl.BlockSpec((FFN_CONV, tn), lambda j, i: (0, j)),
           pl.BlockSpec((FFN_CONV, tn), lambda j, i: (0, nj + j)),
           pl.BlockSpec((1, tn), lambda j, i: (0, j)),
           pl.BlockSpec((1, tn), lambda j, i: (0, nj + j))],
        out_specs=pl.BlockSpec((tm, tn), lambda j, i: (i, j)),
        out_shape=jax.ShapeDtypeStruct((m, D_FF_PAD), BF16),
        scratch_shapes=[pltpu.VMEM((k, 2 * tn), BF16), pltpu.VMEM((SUBLANES, tn), F32),
                        pltpu.VMEM((SUBLANES, tn), F32)],
        compiler_params=_params("arbitrary", "arbitrary"),
        name="ffn_up",
    )(xb, *([w_up] * nblk), conv_w, conv_w, conv_b, conv_b)


def _expand_heads(x, rows, g):
    low = lax.broadcasted_iota(jnp.int32, (1, LANES), 1) < SSD_HEAD_DIM
    h0 = g * SSD_HPG
    b = [jnp.broadcast_to(x[:, h0 + h:h0 + h + 1], (rows, LANES)) for h in range(SSD_HPG)]
    return jnp.concatenate([jnp.where(low, b[0], b[1]), jnp.where(low, b[2], b[3])], axis=1)


def _ssd_kernel(z_ref, x_ref, bc_ref, dt_ref, cwx_ref, cwbc_ref, cbx_ref, cbbc_ref, dtb_ref, alog_ref, dsk_ref,
                nw_ref, o_ref, st_ref, cx_ref, cbc_ref):
    c = pl.program_id(0)
    q = SSD_CHUNK
    gw = SSD_GROUP_W
    n = SSD_STATE
    groups = range(SSD_GROUPS)

    @pl.when(c == 0)
    def _():
        st_ref[...] = jnp.zeros_like(st_ref)
        cx_ref[...] = jnp.zeros_like(cx_ref)
        cbc_ref[...] = jnp.zeros_like(cbc_ref)

    row = lax.broadcasted_iota(jnp.int32, (q, 1), 0) + c * q
    lane = lax.broadcasted_iota(jnp.int32, (1, LANES), 1)
    dt = _softplus(dt_ref[...] + dtb_ref[...])
    dt = jnp.where((row >= FRONT_PAD) & (lane < SSD_HEADS), dt, 0.0)
    a = -jnp.exp(alog_ref[...])
    ri = lax.broadcasted_iota(jnp.int32, (q, q), 0)
    ci = lax.broadcasted_iota(jnp.int32, (q, q), 1)
    causal = ci <= ri
    cs = jnp.dot(causal.astype(F32), dt * a, precision=HIGHEST, preferred_element_type=F32)
    cs_t = cs.T
    cs_last = cs[q - 1:q, :]
    from_start = jnp.exp(cs)
    to_end = jnp.exp(cs_last - cs)
    total = jnp.exp(cs_last)

    def conv_silu(raw_ref, carry_ref, w_ref, b_ref, lo, width):
        raw = raw_ref[:, lo:lo + width].astype(F32)
        y = _causal_conv_rows(raw, carry_ref[:, lo:lo + width], w_ref.at[:, lo:lo + width], b_ref[:, lo:lo + width],
                              SSD_CONV)
        carry_ref[:, lo:lo + width] = raw[q - SUBLANES:q]
        return y * _sigmoid(y)

    xs = [conv_silu(x_ref, cx_ref, cwx_ref, cbx_ref, g * gw, gw) for g in groups]
    bm_t = [conv_silu(bc_ref, cbc_ref, cwbc_ref, cbbc_ref, g * n, n).T.astype(BF16) for g in groups]
    cmb = [conv_silu(bc_ref, cbc_ref, cwbc_ref, cbbc_ref, SSD_GROUPS * n + g * n, n).astype(BF16) for g in groups]
    cb = [jnp.dot(cmb[g], bm_t[g], preferred_element_type=F32) for g in groups]

    lane_w = lax.broadcasted_iota(jnp.int32, (1, gw), 1)
    xd = [xs[g] * _expand_heads(dt, q, g) for g in groups]
    ys = []
    for g in groups:
        decayed = []
        for h in range(SSD_HPG):
            hh = g * SSD_HPG + h
            seg = cs[:, hh:hh + 1] - cs_t[hh:hh + 1, :]
            decayed.append((cb[g] * jnp.exp(jnp.where(causal, seg, -jnp.inf))).astype(BF16))
        xdb = xd[g].astype(BF16)
        x_heads = [jnp.where((lane_w >= h * SSD_HEAD_DIM) & (lane_w < (h + 1) * SSD_HEAD_DIM), xdb,
                             jnp.zeros_like(xdb)) for h in range(SSD_HPG)]
        ys.append(jnp.dot(jnp.concatenate(decayed, axis=1), jnp.concatenate(x_heads, axis=0),
                          preferred_element_type=F32))
    for g in groups:
        st = st_ref[g]
        ys[g] = ys[g] + (jnp.dot(cmb[g], st.astype(BF16), preferred_element_type=F32)
                         * _expand_heads(from_start, q, g))
        st_ref[g] = (st * _expand_heads(total, 1, g)
                     + jnp.dot(bm_t[g], (xd[g] * _expand_heads(to_end, q, g)).astype(BF16),
                               preferred_element_type=F32))
    for g in groups:
        cols = slice(g * gw, (g + 1) * gw)
        y = ys[g] + xs[g] * dsk_ref[:, cols]
        z = z_ref[:, cols].astype(F32)
        yg = y * (z * _sigmoid(z))
        yn = yg * lax.rsqrt(jnp.mean(yg * yg, axis=-1, keepdims=True) + RMS_EPS) * nw_ref[:, cols]
        o_ref[:, cols] = yn.astype(o_ref.dtype)


def _ssd_mixer(u_zx, u_small, conv_w, conv_b, dt_bias, a_log, d_skip, norm_w):
    rows = u_zx.shape[0]
    q = SSD_CHUNK
    d = SSD_GROUPS * SSD_GROUP_W
    pad_l = lambda v: jnp.pad(v.reshape(1, SSD_HEADS), ((0, 0), (0, LANES - SSD_HEADS)))
    whole = lambda r, w, j: pl.BlockSpec((r, w), lambda c: (0, j))
    return pl.pallas_call(
        _ssd_kernel,
        grid=(rows // q,),
        in_specs=[pl.BlockSpec((q, d), lambda c: (c, 0)),
                  pl.BlockSpec((q, d), lambda c: (c, 1)),
                  pl.BlockSpec((q, d), lambda c: (c, 2)),
                  pl.BlockSpec((q, LANES), lambda c: (c, 0)),
                  whole(SSD_CONV, d, 0), whole(SSD_CONV, d, 1), whole(1, d, 0), whole(1, d, 1),
                  whole(1, LANES, 0), whole(1, LANES, 0), whole(1, d, 0), whole(1, d, 0)],
        out_specs=pl.BlockSpec((q, d), lambda c: (c, 0)),
        out_shape=jax.ShapeDtypeStruct((rows, d), BF16),
        scratch_shapes=[pltpu.VMEM((SSD_GROUPS, SSD_STATE, SSD_GROUP_W), F32), pltpu.VMEM((SUBLANES, d), F32),
                        pltpu.VMEM((SUBLANES, d), F32)],
        compiler_params=_params("arbitrary"),
        name="ssd",
    )(u_zx, u_zx, u_zx, u_small, conv_w, conv_w, conv_b.reshape(1, -1), conv_b.reshape(1, -1),
      pad_l(dt_bias), pad_l(a_log), jnp.repeat(d_skip, SSD_HEAD_DIM).reshape(1, d), norm_w.reshape(1, d))


def _gla_scores_blocked(q, k, gc):
    qc = q.shape[0]
    sub = GLA_SUB
    lane_j = lax.broadcasted_iota(jnp.int32, (sub, qc), 1)
    row_i = lax.broadcasted_iota(jnp.int32, (sub, 1), 0)
    a_rows = []
    for blk in range(qc // sub):
        lo = blk * sub
        q_b = q[lo:lo + sub]
        g_b = gc[lo:lo + sub]
        a_blk = jnp.zeros((sub, qc), F32)
        for j in range(sub):
            k_j = k[lo + j:lo + j + 1, :]
            g_j = gc[lo + j:lo + j + 1, :]
            s_j = jnp.sum(q_b * k_j * jnp.exp(jnp.minimum(g_b - g_j, 0.0)), axis=1, keepdims=True)
            a_blk = jnp.where(lane_j == lo + j, jnp.where(row_i >= j, s_j, 0.0), a_blk)
        if blk > 0:
            g_ref0 = gc[lo:lo + 1, :]
            q_t = (q_b * jnp.exp(g_b - g_ref0)).astype(BF16)
            k_t = (k * jnp.exp(jnp.minimum(g_ref0 - gc, 0.0))).astype(BF16)
            off = lax.dot_general(q_t, k_t, (((1,), (1,)), ((), ())), preferred_element_type=F32)
            a_blk = jnp.where(lane_j < lo, off, a_blk)
        a_rows.append(a_blk)
    return jnp.concatenate(a_rows, axis=0)


def _gla_kernel(q_ref, k_ref, v_ref, r_ref, glr_ref, w2_ref, gb_ref, nw_ref, o_ref, st_ref, a_ref):
    c = pl.program_id(0)
    qc = GLA_CHUNK
    dk, dv = GLA_HEAD_DK, GLA_HEAD_DV
    heads = range(GLA_HEADS)
    nt = (((1,), (1,)), ((), ()))

    @pl.when(c == 0)
    def _():
        st_ref[...] = jnp.zeros_like(st_ref)

    pre = jnp.dot(glr_ref[...], w2_ref[...], precision=HIGHEST, preferred_element_type=F32) + gb_ref[...]
    g = _log_sigmoid(pre) * (1.0 / GLA_GATE_TAU)
    row = lax.broadcasted_iota(jnp.int32, (qc, 1), 0) + c * qc
    g = jnp.where(row >= FRONT_PAD, g, 0.0)
    ri = lax.broadcasted_iota(jnp.int32, (qc, qc), 0)
    ci = lax.broadcasted_iota(jnp.int32, (qc, qc), 1)
    causal = ci <= ri
    gc = jnp.dot(causal.astype(F32), g, precision=HIGHEST, preferred_element_type=F32)
    g_last = gc[qc - 1:qc, :]

    q = q_ref[...].astype(F32) * (GLA_HEAD_DK ** -0.5)
    k = k_ref[...].astype(F32)
    q_dec = (q * jnp.exp(gc)).astype(BF16)
    k_end = (k * jnp.exp(g_last - gc)).astype(BF16)
    safe = jnp.max(-g_last) <= GLA_SAFE_DECAY

    @pl.when(safe)
    def _():
        k_inv = (k * jnp.exp(-gc)).astype(BF16)
        for h in heads:
            s = lax.dot_general(q_dec[:, h * dk:(h + 1) * dk], k_inv[:, h * dk:(h + 1) * dk], nt,
                                preferred_element_type=F32)
            a_ref[h] = jnp.where(causal, s, 0.0)

    @pl.when(jnp.logical_not(safe))
    def _():
        for h in heads:
            a_ref[h] = _gla_scores_blocked(q[:, h * dk:(h + 1) * dk], k[:, h * dk:(h + 1) * dk],
                                           gc[:, h * dk:(h + 1) * dk])

    vb = v_ref[...]
    decay = jnp.exp(g_last)
    outs = []
    for h in heads:
        st = st_ref[h]
        v_h = vb[:, h * dv:(h + 1) * dv]
        o = jnp.dot(a_ref[h].astype(BF16), v_h, preferred_element_type=F32)
        o = o + lax.dot_general(q_dec[:, h * dk:(h + 1) * dk], st.astype(BF16), nt, preferred_element_type=F32)
        st_ref[h] = st * decay[:, h * dk:(h + 1) * dk] + lax.dot_general(
            v_h, k_end[:, h * dk:(h + 1) * dk], (((0,), (0,)), ((), ())), preferred_element_type=F32)
        outs.append(o)
    for h in heads:
        o = outs[h]
        on = o * lax.rsqrt(jnp.mean(o * o, axis=-1, keepdims=True) + RMS_EPS) * nw_ref[:, h * dv:(h + 1) * dv]
        r = r_ref[:, h * dv:(h + 1) * dv].astype(F32)
        o_ref[:, h * dv:(h + 1) * dv] = (on * (r * _sigmoid(r))).astype(o_ref.dtype)


def _gla_mixer(u_qkv, u_r, u_small, gate_w2, gate_b, norm_w):
    rows = u_qkv.shape[0]
    qc = GLA_CHUNK
    dk, dv = GLA_HEAD_DK, GLA_HEAD_DV
    wk, wv = GLA_HEADS * dk, GLA_HEADS * dv
    glr_lane = HYB_GLR_COL % LANES
    w2 = jnp.pad(gate_w2, ((glr_lane, LANES - GLA_GATE_RANK - glr_lane), (0, 0)))
    return pl.pallas_call(
        _gla_kernel,
        grid=(rows // qc,),
        in_specs=[pl.BlockSpec((qc, wk), lambda c: (c, 0)),
                  pl.BlockSpec((qc, wk), lambda c: (c, 1)),
                  pl.BlockSpec((qc, wv), lambda c: (c, 1)),
                  pl.BlockSpec((qc, wv), lambda c: (c, 0)),
                  pl.BlockSpec((qc, LANES), lambda c: (c, 1)),
                  pl.BlockSpec((LANES, wk), lambda c: (0, 0)),
                  pl.BlockSpec((1, wk), lambda c: (0, 0)),
                  pl.BlockSpec((1, wv), lambda c: (0, 0))],
        out_specs=pl.BlockSpec((qc, wv), lambda c: (c, 0)),
        out_shape=jax.ShapeDtypeStruct((rows, wv), BF16),
        scratch_shapes=[pltpu.VMEM((GLA_HEADS, dv, dk), F32), pltpu.VMEM((GLA_HEADS, qc, qc), F32)],
        compiler_params=_params("arbitrary"),
        name="gla",
    )(u_qkv, u_qkv, u_qkv, u_r, u_small, w2, gate_b.reshape(1, -1), norm_w.reshape(1, -1))


def _rope_kernel(freq_ref, cm_ref, sp_ref, sm_ref, *, tm):
    i = pl.program_id(0)
    row = lax.broadcasted_iota(jnp.int32, (tm, LANES), 0) + i * tm
    lane = lax.broadcasted_iota(jnp.int32, (tm, LANES), 1) % SWA_HEAD_DIM
    ang = (row - FRONT_PAD).astype(F32) * freq_ref[...]
    cos = jnp.cos(ang)
    sin = jnp.sin(ang)
    half = ROPE_DIM // 2
    cm_ref[...] = cos
    sp_ref[...] = jnp.where(lane < half, -sin, 0.0)
    sm_ref[...] = jnp.where((lane >= half) & (lane < ROPE_DIM), sin, 0.0)


def _rope_tables(rows):
    half = ROPE_DIM // 2
    inv_freq = ROPE_THETA ** (-jnp.arange(half, dtype=F32) / half)
    per_head = jnp.concatenate([inv_freq, inv_freq, jnp.zeros((SWA_HEAD_DIM - ROPE_DIM,), F32)])
    freq = jnp.tile(per_head, LANES // SWA_HEAD_DIM).reshape(1, LANES)
    tm = _row_tile(rows)
    shp = jax.ShapeDtypeStruct((rows, LANES), F32)
    spec = pl.BlockSpec((tm, LANES), lambda i: (i, 0))
    return pl.pallas_call(
        functools.partial(_rope_kernel, tm=tm),
        grid=(rows // tm,),
        in_specs=[pl.BlockSpec((1, LANES), lambda i: (0, 0))],
        out_specs=[spec, spec, spec],
        out_shape=[shp, shp, shp],
        compiler_params=_params("arbitrary"),
        name="rope_tables",
    )(freq)


def _swa_kernel(sink_ref, q_ref, prev_ref, cur_ref, meta_ref, o_ref):
    n = pl.program_id(0)
    w = SWA_WINDOW
    meta_lo = FRONT_PAD
    kvw = SWA_KV_HEADS * LANES
    nt = (((1,), (1,)), ((), ()))
    lane = lax.broadcasted_iota(jnp.int32, (1, LANES), 1)
    low = lane < SWA_HEAD_DIM
    i = lax.broadcasted_iota(jnp.int32, (SWA_GROUP * w, 1), 0) % w
    hrow = lax.broadcasted_iota(jnp.int32, (SWA_GROUP * w, 1), 0) // w
    on_cur = lane <= i
    valid_band = (on_cur & ((n >= 1) | (lane >= meta_lo))) | (jnp.logical_not(on_cur) & (n >= 2))
    valid_meta = (lane >= meta_lo) & (n >= 1)
    zero = jnp.zeros((w, LANES), q_ref.dtype)
    kv_heads = range(SWA_KV_HEADS)

    def masked_scores(g):
        qa = q_ref[:, 2 * g * LANES:(2 * g + 1) * LANES]
        qb = q_ref[:, (2 * g + 1) * LANES:(2 * g + 2) * LANES]
        qs = jnp.concatenate([jnp.where(low, qa, zero), jnp.where(low, zero, qa),
                              jnp.where(low, qb, zero), jnp.where(low, zero, qb)], axis=0)
        kcols = slice(g * LANES, (g + 1) * LANES)
        kk = jnp.concatenate([prev_ref[:, kcols], cur_ref[:, kcols], meta_ref[:, kcols]], axis=0)
        s = lax.dot_general(qs, kk, nt, preferred_element_type=F32)
        s_band = jnp.where(on_cur, s[:, w:2 * w], s[:, :w])
        return jnp.concatenate([jnp.where(valid_band, s_band, -jnp.inf),
                                jnp.where(valid_meta, s[:, 2 * w:], -jnp.inf)], axis=1)

    s_next = masked_scores(0)
    for g in kv_heads:
        s = s_next
        if g + 1 < SWA_KV_HEADS:
            s_next = masked_scores(g + 1)
        sink = jnp.zeros((SWA_GROUP * w, 1), F32)
        for h in range(SWA_GROUP):
            sink = jnp.where(hrow == h, sink_ref[g * SWA_GROUP + h], sink)
        m = jnp.maximum(jnp.max(s, axis=-1, keepdims=True), sink)
        p = jnp.exp(s - m)
        denom = jnp.sum(p, axis=-1, keepdims=True) + jnp.exp(sink - m)
        p_band = p[:, :w]
        pb = jnp.concatenate([jnp.where(on_cur, 0.0, p_band), jnp.where(on_cur, p_band, 0.0), p[:, w:]],
                             axis=1).astype(BF16)
        vcols = slice(kvw + g * LANES, kvw + (g + 1) * LANES)
        vv = jnp.concatenate([prev_ref[:, vcols], cur_ref[:, vcols], meta_ref[:, vcols]], axis=0)
        o = jnp.dot(pb, vv, preferred_element_type=F32) / denom
        oa = jnp.where(low, o[0:w], o[w:2 * w])
        ob = jnp.where(low, o[2 * w:3 * w], o[3 * w:4 * w])
        o_ref[:, 2 * g * LANES:(2 * g + 2) * LANES] = jnp.concatenate([oa, ob], axis=1).astype(o_ref.dtype)


def _swa_attention(q, kv, sinks):
    rows, qw = q.shape
    w = SWA_WINDOW
    kvw = kv.shape[1]
    return pl.pallas_call(
        _swa_kernel,
        grid=(rows // w,),
        in_specs=[pl.BlockSpec(memory_space=pltpu.SMEM),
                  pl.BlockSpec((w, qw), lambda n: (n, 0)),
                  pl.BlockSpec((w, kvw), lambda n: (jnp.maximum(n - 1, 0), 0)),
                  pl.BlockSpec((w, kvw), lambda n: (n, 0)),
                  pl.BlockSpec((w, kvw), lambda n: (0, 0))],
        out_specs=pl.BlockSpec((w, qw), lambda n: (n, 0)),
        out_shape=jax.ShapeDtypeStruct((rows, qw), BF16),
        compiler_params=_params("arbitrary"),
        name="swa",
    )(sinks, q, kv, kv, kv)


def _pad_halves(t):
    pad = lambda a: jnp.pad(a, ((0, 0), (0, D_FF_PAD - D_FF)))
    return jnp.concatenate([pad(t[:, :D_FF]), pad(t[:, D_FF:])], axis=1)


def _trunk(x, meta_tokens, hyb_w_in, hyb_conv_w, hyb_conv_b, ssd_dt_bias, ssd_a_log, ssd_d, ssd_norm_w,
           gla_gate_w2, gla_gate_b, gla_norm_w, hyb_w_out, swa_w_qkv, swa_sinks, swa_w_out,
           ffn_w_up, ffn_conv_w, ffn_conv_b, ffn_w_down, ln_mix_g, ln_mix_b, ln_ffn_g, ln_ffn_b):
    seq = x.shape[0]
    rows = FRONT_PAD + N_META + seq
    h = jnp.concatenate([jnp.zeros((FRONT_PAD, D_MODEL), F32), meta_tokens.astype(F32), x], axis=0)
    hb = h.astype(BF16)
    tables = _rope_tables(rows)
    hyb_w_in = jnp.pad(hyb_w_in.astype(BF16), ((0, 0), (0, 0), (0, -hyb_w_in.shape[2] % LANES)))
    for layer in range(DEPTH):
        j = layer // 2
        if layer % 2 == 0:
            u_zx = _project(hb, hyb_w_in, j, first_block=0, block_stride=4, tn=512, n_tiles=12)
            u_qkv = _project(hb, hyb_w_in, j, first_block=HYB_QKV_COL // LANES, block_stride=4, tn=512, n_tiles=8,
                             shift=HYB_QKV_COL % LANES)
            u_r = _project(hb, hyb_w_in, j, first_block=HYB_R_COL // LANES, block_stride=4, tn=512, n_tiles=4,
                           shift=HYB_R_COL % LANES)
            u_small = _project(hb, hyb_w_in, j, first_block=HYB_DT_COL // LANES,
                               block_stride=(HYB_GLR_COL - HYB_DT_COL) // LANES, tn=LANES, n_tiles=2, out_dtype=F32)
            y_ssd = _ssd_mixer(u_zx, u_small, hyb_conv_w[j], hyb_conv_b[j], ssd_dt_bias[j], ssd_a_log[j],
                               ssd_d[j], ssd_norm_w[j])
            y_gla = _gla_mixer(u_qkv, u_r, u_small, gla_gate_w2[j], gla_gate_b[j], gla_norm_w[j])
            h, hb = _matmul_residual_ln([y_ssd, y_gla], [y_ssd.shape[1], y_gla.shape[1]], hyb_w_out, j, h,
                                        ln_mix_g[layer], ln_mix_b[layer], nchunks=8)
        else:
            q = _project(hb, swa_w_qkv, j, first_block=0, block_stride=4, tn=512, n_tiles=4,
                         scale=SWA_HEAD_DIM ** -0.5, mode="rope", tables=tables)
            kv = _project(hb, swa_w_qkv, j, first_block=SWA_Q_HEADS * SWA_HEAD_DIM // LANES, block_stride=4, tn=512,
                          n_tiles=2, mode="kv", tables=tables)
            attn = _swa_attention(q, kv, swa_sinks[j])
            h, hb = _matmul_residual_ln([attn], [attn.shape[1]], swa_w_out, j, h, ln_mix_g[layer], ln_mix_b[layer],
                                        nchunks=4)
        act = _ffn_up(hb, ffn_w_up, layer, _pad_halves(ffn_conv_w[layer]),
                      _pad_halves(ffn_conv_b[layer].reshape(1, -1)))
        h, hb = _matmul_residual_ln([act], [D_FF], ffn_w_down, layer, h, ln_ffn_g[layer], ln_ffn_b[layer], nchunks=8)
    return h[FRONT_PAD + N_META:]


def kernel(x, meta_tokens, hyb_w_in, hyb_conv_w, hyb_conv_b, ssd_dt_bias, ssd_a_log, ssd_d, ssd_norm_w,
           gla_gate_w2, gla_gate_b, gla_norm_w, hyb_w_out, swa_w_qkv, swa_sinks, swa_w_out,
           ffn_w_up, ffn_conv_w, ffn_conv_b, ffn_w_down, ln_mix_g, ln_mix_b, ln_ffn_g, ln_ffn_b):
    params = (meta_tokens, hyb_w_in, hyb_conv_w, hyb_conv_b, ssd_dt_bias, ssd_a_log, ssd_d, ssd_norm_w,
              gla_gate_w2, gla_gate_b, gla_norm_w, hyb_w_out, swa_w_qkv, swa_sinks, swa_w_out,
              ffn_w_up, ffn_conv_w, ffn_conv_b, ffn_w_down, ln_mix_g, ln_mix_b, ln_ffn_g, ln_ffn_b)
    return jnp.stack([_trunk(x[b], *params) for b in range(x.shape[0])], axis=0)
```

```python
import functools

import jax
import jax.numpy as jnp
from jax import lax
from jax.experimental import pallas as pl
from jax.experimental.pallas import tpu as pltpu

F32 = jnp.float32
BF16 = jnp.bfloat16
HIGHEST = lax.Precision.HIGHEST

D_MODEL = 2048
DEPTH = 4
N_META = 16
LN_EPS = 1e-5
RMS_EPS = 1e-6
DEEPNORM_ALPHA = (2.0 * DEPTH) ** 0.25

SSD_HEAD_DIM = 64
SSD_HEADS = 32
SSD_GROUPS = 8
SSD_HPG = 4
SSD_STATE = 128
SSD_CONV = 4
SSD_CHUNK = 128
SSD_GROUP_W = SSD_HPG * SSD_HEAD_DIM

GLA_HEADS = 4
GLA_HEAD_DK = 256
GLA_HEAD_DV = 512
GLA_GATE_RANK = 16
GLA_GATE_TAU = 16.0
GLA_CHUNK = 128
GLA_SUB = 16
GLA_SAFE_DECAY = 80.0

SWA_HEAD_DIM = 64
SWA_Q_HEADS = 32
SWA_KV_HEADS = 8
SWA_GROUP = 4
SWA_WINDOW = 128
ROPE_THETA = 500000.0
ROPE_DIM = 16

D_FF = 5504
FFN_CONV = 3

LANES = 128
SUBLANES = 8
MXU_DEPTH = 256
FRONT_PAD = SSD_CHUNK - N_META
D_FF_PAD = 5632
FFN_TN = 512
FFN_SUBTILES = 4
PROJ_SUBTILES = 4
LN_ROW_TILE = 320
LN_SUBTILES = 2
LN_WEIGHT_COPY_BUDGET = 20 * 1024 * 1024
VMEM_LIMIT = 56 * 1024 * 1024

HYB_DT_COL = 6144
HYB_QKV_COL = 6176
HYB_GLR_COL = 10272
HYB_R_COL = 10288


def _row_tile(rows):
    for t in (640, 512, 384, 256, 128):
        if rows % t == 0:
            return t
    raise ValueError(f"row count {rows} is not a multiple of 128")


def _row_tile_big(rows):
    return 1664 if rows % 1664 == 0 else _row_tile(rows)


def _params(*sem):
    return pltpu.CompilerParams(dimension_semantics=sem, vmem_limit_bytes=VMEM_LIMIT)


def _sigmoid(x):
    return 1.0 / (1.0 + jnp.exp(-x))


def _softplus(x):
    return jnp.maximum(x, 0.0) + jnp.log(1.0 + jnp.exp(-jnp.abs(x)))


def _log_sigmoid(x):
    return jnp.minimum(x, 0.0) - jnp.log(1.0 + jnp.exp(-jnp.abs(x)))


def _causal_conv_rows(y, carry, w_ref, b_row, taps):
    top = jnp.concatenate([carry, y[0:SUBLANES]], axis=0)
    w_last = w_ref[taps - 1:taps, :]
    acc = b_row + w_last * y
    acc_top = b_row + w_last * y[0:SUBLANES]
    for s in range(1, taps):
        wk = w_ref[taps - 1 - s:taps - s, :]
        acc = acc + wk * pltpu.roll(y, s, 0)
        acc_top = acc_top + wk * pltpu.roll(top, s, 0)[SUBLANES:2 * SUBLANES]
    return jnp.concatenate([acc_top, acc[SUBLANES:]], axis=0)


def _assemble_weight(w_refs, wb_ref, shift, scale):
    k, tn = wb_ref.shape
    chunk = 256
    for r in range(0, k, chunk):
        w = jnp.concatenate([wr[r:r + chunk, :] for wr in w_refs], axis=1)
        if shift:
            w = pltpu.roll(w.astype(F32), w.shape[1] - shift, 1)
        w = w[:, :tn]
        if scale is not None:
            w = w * scale
        wb_ref[r:r + chunk, :] = w.astype(BF16)


def _weight_block_specs(k, layer, block_fns):
    return [pl.BlockSpec((None, k, LANES), functools.partial(lambda *ids, fn: (layer, 0, fn(*ids)), fn=fn))
            for fn in block_fns]


def _rotate_heads(y, cm, sp, sm):
    half = ROPE_DIM // 2
    out = []
    for c in range(y.shape[1] // LANES):
        yc = y[:, c * LANES:(c + 1) * LANES]
        out.append(yc * cm + pltpu.roll(yc, LANES - half, 1) * sp + pltpu.roll(yc, half, 1) * sm)
    return jnp.concatenate(out, axis=1)


def _duplicate_heads(y):
    low = lax.broadcasted_iota(jnp.int32, (1, LANES), 1) < SWA_HEAD_DIM
    out = []
    for c in range(y.shape[1] // LANES):
        yc = y[:, c * LANES:(c + 1) * LANES]
        rolled = pltpu.roll(yc, SWA_HEAD_DIM, 1)
        out.append(jnp.where(low, yc, rolled))
        out.append(jnp.where(low, rolled, yc))
    return jnp.concatenate(out, axis=1)


def _proj_kernel(*refs, nblk, shift, scale, mode):
    x_ref = refs[0]
    w_refs = refs[1:1 + nblk]
    rest = refs[1 + nblk:]
    if mode != "plain":
        cm_ref, sp_ref, sm_ref = rest[:3]
        rest = rest[3:]
    o_ref, wb_ref = rest
    j = pl.program_id(0)
    i = pl.program_id(1)

    @pl.when(i == 0)
    def _():
        _assemble_weight(w_refs, wb_ref, shift, scale)

    if mode == "plain":
        o_ref[...] = jnp.dot(x_ref[...], wb_ref[...], preferred_element_type=F32).astype(o_ref.dtype)
        return
    tm = x_ref.shape[0]
    ts = tm // PROJ_SUBTILES
    ys = [jnp.dot(x_ref[s * ts:(s + 1) * ts, :], wb_ref[...], preferred_element_type=F32)
          for s in range(PROJ_SUBTILES)]
    rotate = (j == 0) if mode == "kv" else True
    for s, y in enumerate(ys):
        rows = slice(s * ts, (s + 1) * ts)
        cm = jnp.where(rotate, cm_ref[rows, :], 1.0)
        sp = jnp.where(rotate, sp_ref[rows, :], 0.0)
        sm = jnp.where(rotate, sm_ref[rows, :], 0.0)
        y = _rotate_heads(y, cm, sp, sm)
        if mode == "kv":
            y = _duplicate_heads(y)
        o_ref[rows, :] = y.astype(o_ref.dtype)


def _project(xb, w, layer, *, first_block, block_stride, tn, n_tiles, shift=0, scale=None, mode="plain",
             tables=None, out_dtype=BF16):
    m, k = xb.shape
    tm = _row_tile_big(m)
    nb = tn // LANES
    tn_out = 2 * tn if mode == "kv" else tn
    if shift == 0 and block_stride == nb and first_block % nb == 0:
        nblk = 1
        w_specs = [pl.BlockSpec((None, k, tn), lambda j, i: (layer, 0, first_block // nb + j))]
    else:
        nblk = nb + (1 if shift else 0)
        w_specs = _weight_block_specs(
            k, layer, [functools.partial(lambda j, i, b: first_block + j * block_stride + b, b=b) for b in range(nblk)])
    in_specs = [pl.BlockSpec((tm, k), lambda j, i: (i, 0))] + w_specs
    args = [xb] + [w] * nblk
    if mode != "plain":
        in_specs += [pl.BlockSpec((tm, LANES), lambda j, i: (i, 0))] * 3
        args += list(tables)
    return pl.pallas_call(
        functools.partial(_proj_kernel, nblk=nblk, shift=shift, scale=scale, mode=mode),
        grid=(n_tiles, m // tm),
        in_specs=in_specs,
        out_specs=pl.BlockSpec((tm, tn_out), lambda j, i: (i, j)),
        out_shape=jax.ShapeDtypeStruct((m, n_tiles * tn_out), out_dtype),
        scratch_shapes=[pltpu.VMEM((k, tn), BF16)],
        compiler_params=_params("arbitrary", "arbitrary"),
        name="proj_" + mode,
    )(*args)


def _mm_ln_kernel(*refs, nsrc, tm, subtiles, nchunks, ck):
    x_refs = refs[:nsrc]
    w_ref, res_ref, g_ref, b_ref, of_ref, ob_ref, wb_ref = refs[nsrc:]
    step = pl.program_id(0)

    @pl.when(step < nchunks)
    def _():
        wb_ref[pl.ds(pl.multiple_of(step * ck, ck), ck), :] = w_ref[...].astype(BF16)

    @pl.when(step >= nchunks)
    def _():
        i = step - nchunks
        ts = tm // subtiles
        sums = []
        for s in range(subtiles):
            rows = slice(s * ts, (s + 1) * ts)
            acc, k0 = None, 0
            for x_ref in x_refs:
                kw = x_ref.shape[1]
                part = jnp.dot(x_ref[rows, :], wb_ref[k0:k0 + kw, :], preferred_element_type=F32)
                acc = part if acc is None else acc + part
                k0 += kw
            sums.append(acc)
        for s, acc in enumerate(sums):
            rows = slice(s * ts, (s + 1) * ts)
            t = DEEPNORM_ALPHA * res_ref[rows, :] + acc
            mu = jnp.mean(t, axis=-1, keepdims=True)
            d = t - mu
            var = jnp.mean(d * d, axis=-1, keepdims=True)
            y = d * lax.rsqrt(var + LN_EPS) * g_ref[...] + b_ref[...]
            row = lax.broadcasted_iota(jnp.int32, (ts, 1), 0) + (i * tm + s * ts)
            y = jnp.where(row >= FRONT_PAD, y, 0.0)
            of_ref[rows, :] = y
            ob_ref[rows, :] = y.astype(BF16)


def _matmul_residual_ln(xs, k_widths, w, layer, res, gamma, beta, nchunks):
    m = xs[0].shape[0]
    _, kdim, n = w.shape
    assert sum(k_widths) == kdim and m % LN_ROW_TILE == 0 and kdim % (16 * nchunks) == 0
    tm = LN_ROW_TILE
    ck = kdim // nchunks
    subtiles = LN_SUBTILES if 2 * kdim * n * 2 <= LN_WEIGHT_COPY_BUDGET else 1
    tile = lambda s: jnp.maximum(s - nchunks, 0)
    x_specs = [pl.BlockSpec((tm, kw), lambda s: (tile(s), 0)) for kw in k_widths]
    return pl.pallas_call(
        functools.partial(_mm_ln_kernel, nsrc=len(xs), tm=tm, subtiles=subtiles, nchunks=nchunks, ck=ck),
        grid=(nchunks + m // tm,),
        in_specs=x_specs + [pl.BlockSpec((None, ck, n), lambda s: (layer, jnp.minimum(s, nchunks - 1), 0)),
                            pl.BlockSpec((tm, n), lambda s: (tile(s), 0)),
                            pl.BlockSpec((1, n), lambda s: (0, 0)),
                            pl.BlockSpec((1, n), lambda s: (0, 0))],
        out_specs=[pl.BlockSpec((tm, n), lambda s: (tile(s), 0)),
                   pl.BlockSpec((tm, n), lambda s: (tile(s), 0))],
        out_shape=[jax.ShapeDtypeStruct((m, n), F32), jax.ShapeDtypeStruct((m, n), BF16)],
        scratch_shapes=[pltpu.VMEM((kdim, n), BF16)],
        compiler_params=_params("arbitrary"),
        name="proj_ln",
    )(*xs, w, res, gamma.reshape(1, n), beta.reshape(1, n))


def _ffn_up_kernel(*refs, nblk, tm, tn):
    x_ref = refs[0]
    w_refs = refs[1:1 + nblk]
    cw_g, cw_v, cb_g, cb_v, o_ref, wb_ref, cg_ref, cv_ref = refs[1 + nblk:]
    j = pl.program_id(0)
    i = pl.program_id(1)

    @pl.when(i == 0)
    def _():
        _assemble_weight(w_refs, wb_ref, 0, None)
        cg_ref[...] = jnp.zeros_like(cg_ref)
        cv_ref[...] = jnp.zeros_like(cv_ref)

    col = lax.broadcasted_iota(jnp.int32, (1, tn), 1) + j * tn
    ts = tm // FFN_SUBTILES
    ys = []
    for s in range(FFN_SUBTILES):
        y = jnp.dot(x_ref[s * ts:(s + 1) * ts, :], wb_ref[...], preferred_element_type=F32)
        ys.append((y[:, :tn], y[:, tn:]))
    carry_g, carry_v = cg_ref[...], cv_ref[...]
    for s, (yg, yv) in enumerate(ys):
        hg = _causal_conv_rows(yg, carry_g, cw_g, cb_g[...], FFN_CONV)
        hv = _causal_conv_rows(yv, carry_v, cw_v, cb_v[...], FFN_CONV)
        carry_g, carry_v = yg[ts - SUBLANES:ts], yv[ts - SUBLANES:ts]
        o_ref[s * ts:(s + 1) * ts, :] = jnp.where(col < D_FF, hg * _sigmoid(hg) * hv, 0.0).astype(o_ref.dtype)
    cg_ref[...] = carry_g
    cv_ref[...] = carry_v


def _ffn_up(xb, w_up, layer, conv_w, conv_b):
    m, k = xb.shape
    tm = _row_tile_big(m)
    assert (tm // FFN_SUBTILES) % 16 == 0, tm
    tn = FFN_TN
    nj = D_FF_PAD // tn
    nb = tn // LANES
    val0 = D_FF // LANES
    last_blk = 2 * D_FF // LANES - 1
    w_specs = [pl.BlockSpec((None, k, tn), lambda j, i: (layer, 0, j))] + _weight_block_specs(
        k, layer, [functools.partial(lambda j, i, b: jnp.minimum(val0 + j * nb + b, last_blk), b=b) for b in range(nb)])
    nblk = 1 + nb
    return pl.pallas_call(
        functools.partial(_ffn_up_kernel, nblk=nblk, tm=tm, tn=tn),
        grid=(nj, m // tm),
        in_specs=[pl.BlockSpec((tm, k), lambda j, i: (i, 0))] + w_specs
        + [pl.BlockSpec((FFN_CONV, tn), lambda j, i: (0, j)),
           pl.BlockSpec((FFN_CONV, tn), lambda j, i: (0, nj + j)),
           pl.BlockSpec((1, tn), lambda j, i: (0, j)),
           pl.BlockSpec((1, tn), lambda j, i: (0, nj + j))],
        out_specs=pl.BlockSpec((tm, tn), lambda j, i: (i, j)),
        out_shape=jax.ShapeDtypeStruct((m, D_FF_PAD), BF16),
        scratch_shapes=[pltpu.VMEM((k, 2 * tn), BF16), pltpu.VMEM((SUBLANES, tn), F32),
                        pltpu.VMEM((SUBLANES, tn), F32)],
        compiler_params=_params("arbitrary", "arbitrary"),
        name="ffn_up",
    )(xb, *([w_up] * nblk), conv_w, conv_w, conv_b, conv_b)


def _expand_heads(x, rows, g):
    low = lax.broadcasted_iota(jnp.int32, (1, LANES), 1) < SSD_HEAD_DIM
    h0 = g * SSD_HPG
    b = [jnp.broadcast_to(x[:, h0 + h:h0 + h + 1], (rows, LANES)) for h in range(SSD_HPG)]
    return jnp.concatenate([jnp.where(low, b[0], b[1]), jnp.where(low, b[2], b[3])], axis=1)


def _ssd_kernel(z_ref, x_ref, bc_ref, dt_ref, cwx_ref, cwbc_ref, cbx_ref, cbbc_ref, dtb_ref, alog_ref, dsk_ref,
                nw_ref, o_ref, st_ref, cx_ref, cbc_ref):
    c = pl.program_id(0)
    q = SSD_CHUNK
    gw = SSD_GROUP_W
    n = SSD_STATE
    groups = range(SSD_GROUPS)

    @pl.when(c == 0)
    def _():
        st_ref[...] = jnp.zeros_like(st_ref)
        cx_ref[...] = jnp.zeros_like(cx_ref)
        cbc_ref[...] = jnp.zeros_like(cbc_ref)

    row = lax.broadcasted_iota(jnp.int32, (q, 1), 0) + c * q
    lane = lax.broadcasted_iota(jnp.int32, (1, LANES), 1)
    dt = _softplus(dt_ref[...] + dtb_ref[...])
    dt = jnp.where((row >= FRONT_PAD) & (lane < SSD_HEADS), dt, 0.0)
    a = -jnp.exp(alog_ref[...])
    ri = lax.broadcasted_iota(jnp.int32, (q, q), 0)
    ci = lax.broadcasted_iota(jnp.int32, (q, q), 1)
    causal = ci <= ri
    cs = jnp.dot(causal.astype(F32), dt * a, precision=HIGHEST, preferred_element_type=F32)
    cs_t = cs.T
    cs_last = cs[q - 1:q, :]
    from_start = jnp.exp(cs)
    to_end = jnp.exp(cs_last - cs)
    total = jnp.exp(cs_last)

    def conv_silu(raw_ref, carry_ref, w_ref, b_ref, lo, width):
        raw = raw_ref[:, lo:lo + width].astype(F32)
        y = _causal_conv_rows(raw, carry_ref[:, lo:lo + width], w_ref.at[:, lo:lo + width], b_ref[:, lo:lo + width],
                              SSD_CONV)
        carry_ref[:, lo:lo + width] = raw[q - SUBLANES:q]
        return y * _sigmoid(y)

    xs = [conv_silu(x_ref, cx_ref, cwx_ref, cbx_ref, g * gw, gw) for g in groups]
    bm_t = [conv_silu(bc_ref, cbc_ref, cwbc_ref, cbbc_ref, g * n, n).T.astype(BF16) for g in groups]
    cmb = [conv_silu(bc_ref, cbc_ref, cwbc_ref, cbbc_ref, SSD_GROUPS * n + g * n, n).astype(BF16) for g in groups]
    cb = [jnp.dot(cmb[g], bm_t[g], preferred_element_type=F32) for g in groups]

    lane_w = lax.broadcasted_iota(jnp.int32, (1, gw), 1)
    xd = [xs[g] * _expand_heads(dt, q, g) for g in groups]
    ys = []
    for g in groups:
        decayed = []
        for h in range(SSD_HPG):
            hh = g * SSD_HPG + h
            seg = cs[:, hh:hh + 1] - cs_t[hh:hh + 1, :]
            decayed.append((cb[g] * jnp.exp(jnp.where(causal, seg, -jnp.inf))).astype(BF16))
        xdb = xd[g].astype(BF16)
        x_heads = [jnp.where((lane_w >= h * SSD_HEAD_DIM) & (lane_w < (h + 1) * SSD_HEAD_DIM), xdb,
                             jnp.zeros_like(xdb)) for h in range(SSD_HPG)]
        ys.append(jnp.dot(jnp.concatenate(decayed, axis=1), jnp.concatenate(x_heads, axis=0),
                          preferred_element_type=F32))
    for g in groups:
        st = st_ref[g]
        ys[g] = ys[g] + (jnp.dot(cmb[g], st.astype(BF16), preferred_element_type=F32)
                         * _expand_heads(from_start, q, g))
        st_ref[g] = (st * _expand_heads(total, 1, g)
                     + jnp.dot(bm_t[g], (xd[g] * _expand_heads(to_end, q, g)).astype(BF16),
                               preferred_element_type=F32))
    for g in groups:
        cols = slice(g * gw, (g + 1) * gw)
        y = ys[g] + xs[g] * dsk_ref[:, cols]
        z = z_ref[:, cols].astype(F32)
        yg = y * (z * _sigmoid(z))
        yn = yg * lax.rsqrt(jnp.mean(yg * yg, axis=-1, keepdims=True) + RMS_EPS) * nw_ref[:, cols]
        o_ref[:, cols] = yn.astype(o_ref.dtype)


def _ssd_mixer(u_zx, u_small, conv_w, conv_b, dt_bias, a_log, d_skip, norm_w):
    rows = u_zx.shape[0]
    q = SSD_CHUNK
    d = SSD_GROUPS * SSD_GROUP_W
    pad_l = lambda v: jnp.pad(v.reshape(1, SSD_HEADS), ((0, 0), (0, LANES - SSD_HEADS)))
    whole = lambda r, w, j: pl.BlockSpec((r, w), lambda c: (0, j))
    return pl.pallas_call(
        _ssd_kernel,
        grid=(rows // q,),
        in_specs=[pl.BlockSpec((q, d), lambda c: (c, 0)),
                  pl.BlockSpec((q, d), lambda c: (c, 1)),
                  pl.BlockSpec((q, d), lambda c: (c, 2)),
                  pl.BlockSpec((q, LANES), lambda c: (c, 0)),
                  whole(SSD_CONV, d, 0), whole(SSD_CONV, d, 1), whole(1, d, 0), whole(1, d, 1),
                  whole(1, LANES, 0), whole(1, LANES, 0), whole(1, d, 0), whole(1, d, 0)],
        out_specs=pl.BlockSpec((q, d), lambda c: (c, 0)),
        out_shape=jax.ShapeDtypeStruct((rows, d), BF16),
        scratch_shapes=[pltpu.VMEM((SSD_GROUPS, SSD_STATE, SSD_GROUP_W), F32), pltpu.VMEM((SUBLANES, d), F32),
                        pltpu.VMEM((SUBLANES, d), F32)],
        compiler_params=_params("arbitrary"),
        name="ssd",
    )(u_zx, u_zx, u_zx, u_small, conv_w, conv_w, conv_b.reshape(1, -1), conv_b.reshape(1, -1),
      pad_l(dt_bias), pad_l(a_log), jnp.repeat(d_skip, SSD_HEAD_DIM).reshape(1, d), norm_w.reshape(1, d))


def _gla_scores_blocked(q, k, gc):
    qc = q.shape[0]
    sub = GLA_SUB
    lane_j = lax.broadcasted_iota(jnp.int32, (sub, qc), 1)
    row_i = lax.broadcasted_iota(jnp.int32, (sub, 1), 0)
    a_rows = []
    for blk in range(qc // sub):
        lo = blk * sub
        q_b = q[lo:lo + sub]
        g_b = gc[lo:lo + sub]
        a_blk = jnp.zeros((sub, qc), F32)
        for j in range(sub):
            k_j = k[lo + j:lo + j + 1, :]
            g_j = gc[lo + j:lo + j + 1, :]
            s_j = jnp.sum(q_b * k_j * jnp.exp(jnp.minimum(g_b - g_j, 0.0)), axis=1, keepdims=True)
            a_blk = jnp.where(lane_j == lo + j, jnp.where(row_i >= j, s_j, 0.0), a_blk)
        if blk > 0:
            g_ref0 = gc[lo:lo + 1, :]
            q_t = (q_b * jnp.exp(g_b - g_ref0)).astype(BF16)
            k_t = (k * jnp.exp(jnp.minimum(g_ref0 - gc, 0.0))).astype(BF16)
            off = lax.dot_general(q_t, k_t, (((1,), (1,)), ((), ())), preferred_element_type=F32)
            a_blk = jnp.where(lane_j < lo, off, a_blk)
        a_rows.append(a_blk)
    return jnp.concatenate(a_rows, axis=0)


def _gla_kernel(q_ref, k_ref, v_ref, r_ref, glr_ref, w2_ref, gb_ref, nw_ref, o_ref, st_ref, a_ref):
    c = pl.program_id(0)
    qc = GLA_CHUNK
    dk, dv = GLA_HEAD_DK, GLA_HEAD_DV
    heads = range(GLA_HEADS)
    nt = (((1,), (1,)), ((), ()))

    @pl.when(c == 0)
    def _():
        st_ref[...] = jnp.zeros_like(st_ref)

    pre = jnp.dot(glr_ref[...], w2_ref[...], precision=HIGHEST, preferred_element_type=F32) + gb_ref[...]
    g = _log_sigmoid(pre) * (1.0 / GLA_GATE_TAU)
    row = lax.broadcasted_iota(jnp.int32, (qc, 1), 0) + c * qc
    g = jnp.where(row >= FRONT_PAD, g, 0.0)
    ri = lax.broadcasted_iota(jnp.int32, (qc, qc), 0)
    ci = lax.broadcasted_iota(jnp.int32, (qc, qc), 1)
    causal = ci <= ri
    gc = jnp.dot(causal.astype(F32), g, precision=HIGHEST, preferred_element_type=F32)
    g_last = gc[qc - 1:qc, :]

    q = q_ref[...].astype(F32) * (GLA_HEAD_DK ** -0.5)
    k = k_ref[...].astype(F32)
    q_dec = (q * jnp.exp(gc)).astype(BF16)
    k_end = (k * jnp.exp(g_last - gc)).astype(BF16)
    safe = jnp.max(-g_last) <= GLA_SAFE_DECAY

    @pl.when(safe)
    def _():
        k_inv = (k * jnp.exp(-gc)).astype(BF16)
        for h in heads:
            s = lax.dot_general(q_dec[:, h * dk:(h + 1) * dk], k_inv[:, h * dk:(h + 1) * dk], nt,
                                preferred_element_type=F32)
            a_ref[h] = jnp.where(causal, s, 0.0)

    @pl.when(jnp.logical_not(safe))
    def _():
        for h in heads:
            a_ref[h] = _gla_scores_blocked(q[:, h * dk:(h + 1) * dk], k[:, h * dk:(h + 1) * dk],
                                           gc[:, h * dk:(h + 1) * dk])

    vb = v_ref[...]
    decay = jnp.exp(g_last)
    outs = []
    for h in heads:
        st = st_ref[h]
        v_h = vb[:, h * dv:(h + 1) * dv]
        o = jnp.dot(a_ref[h].astype(BF16), v_h, preferred_element_type=F32)
        o = o + lax.dot_general(q_dec[:, h * dk:(h + 1) * dk], st.astype(BF16), nt, preferred_element_type=F32)
        st_ref[h] = st * decay[:, h * dk:(h + 1) * dk] + lax.dot_general(
            v_h, k_end[:, h * dk:(h + 1) * dk], (((0,), (0,)), ((), ())), preferred_element_type=F32)
        outs.append(o)
    for h in heads:
        o = outs[h]
        on = o * lax.rsqrt(jnp.mean(o * o, axis=-1, keepdims=True) + RMS_EPS) * nw_ref[:, h * dv:(h + 1) * dv]
        r = r_ref[:, h * dv:(h + 1) * dv].astype(F32)
        o_ref[:, h * dv:(h + 1) * dv] = (on * (r * _sigmoid(r))).astype(o_ref.dtype)


def _gla_mixer(u_qkv, u_r, u_small, gate_w2, gate_b, norm_w):
    rows = u_qkv.shape[0]
    qc = GLA_CHUNK
    dk, dv = GLA_HEAD_DK, GLA_HEAD_DV
    wk, wv = GLA_HEADS * dk, GLA_HEADS * dv
    glr_lane = HYB_GLR_COL % LANES
    w2 = jnp.pad(gate_w2, ((glr_lane, LANES - GLA_GATE_RANK - glr_lane), (0, 0)))
    return pl.pallas_call(
        _gla_kernel,
        grid=(rows // qc,),
        in_specs=[pl.BlockSpec((qc, wk), lambda c: (c, 0)),
                  pl.BlockSpec((qc, wk), lambda c: (c, 1)),
                  pl.BlockSpec((qc, wv), lambda c: (c, 1)),
                  pl.BlockSpec((qc, wv), lambda c: (c, 0)),
                  pl.BlockSpec((qc, LANES), lambda c: (c, 1)),
                  pl.BlockSpec((LANES, wk), lambda c: (0, 0)),
                  pl.BlockSpec((1, wk), lambda c: (0, 0)),
                  pl.BlockSpec((1, wv), lambda c: (0, 0))],
        out_specs=pl.BlockSpec((qc, wv), lambda c: (c, 0)),
        out_shape=jax.ShapeDtypeStruct((rows, wv), BF16),
        scratch_shapes=[pltpu.VMEM((GLA_HEADS, dv, dk), F32), pltpu.VMEM((GLA_HEADS, qc, qc), F32)],
        compiler_params=_params("arbitrary"),
        name="gla",
    )(u_qkv, u_qkv, u_qkv, u_r, u_small, w2, gate_b.reshape(1, -1), norm_w.reshape(1, -1))


def _rope_kernel(freq_ref, cm_ref, sp_ref, sm_ref, *, tm):
    i = pl.program_id(0)
    row = lax.broadcasted_iota(jnp.int32, (tm, LANES), 0) + i * tm
    lane = lax.broadcasted_iota(jnp.int32, (tm, LANES), 1) % SWA_HEAD_DIM
    ang = (row - FRONT_PAD).astype(F32) * freq_ref[...]
    cos = jnp.cos(ang)
    sin = jnp.sin(ang)
    half = ROPE_DIM // 2
    cm_ref[...] = cos
    sp_ref[...] = jnp.where(lane < half, -sin, 0.0)
    sm_ref[...] = jnp.where((lane >= half) & (lane < ROPE_DIM), sin, 0.0)


def _rope_tables(rows):
    half = ROPE_DIM // 2
    inv_freq = ROPE_THETA ** (-jnp.arange(half, dtype=F32) / half)
    per_head = jnp.concatenate([inv_freq, inv_freq, jnp.zeros((SWA_HEAD_DIM - ROPE_DIM,), F32)])
    freq = jnp.tile(per_head, LANES // SWA_HEAD_DIM).reshape(1, LANES)
    tm = _row_tile(rows)
    shp = jax.ShapeDtypeStruct((rows, LANES), F32)
    spec = pl.BlockSpec((tm, LANES), lambda i: (i, 0))
    return pl.pallas_call(
        functools.partial(_rope_kernel, tm=tm),
        grid=(rows // tm,),
        in_specs=[pl.BlockSpec((1, LANES), lambda i: (0, 0))],
        out_specs=[spec, spec, spec],
        out_shape=[shp, shp, shp],
        compiler_params=_params("arbitrary"),
        name="rope_tables",
    )(freq)


def _swa_kernel(sink_ref, q_ref, prev_ref, cur_ref, meta_ref, o_ref):
    n = pl.program_id(0)
    w = SWA_WINDOW
    meta_lo = FRONT_PAD
    kvw = SWA_KV_HEADS * LANES
    nt = (((1,), (1,)), ((), ()))
    lane = lax.broadcasted_iota(jnp.int32, (1, LANES), 1)
    low = lane < SWA_HEAD_DIM
    i = lax.broadcasted_iota(jnp.int32, (SWA_GROUP * w, 1), 0) % w
    hrow = lax.broadcasted_iota(jnp.int32, (SWA_GROUP * w, 1), 0) // w
    on_cur = lane <= i
    valid_band = (on_cur & ((n >= 1) | (lane >= meta_lo))) | (jnp.logical_not(on_cur) & (n >= 2))
    valid_meta = (lane >= meta_lo) & (n >= 1)
    zero = jnp.zeros((w, LANES), q_ref.dtype)
    kv_heads = range(SWA_KV_HEADS)

    def masked_scores(g):
        qa = q_ref[:, 2 * g * LANES:(2 * g + 1) * LANES]
        qb = q_ref[:, (2 * g + 1) * LANES:(2 * g + 2) * LANES]
        qs = jnp.concatenate([jnp.where(low, qa, zero), jnp.where(low, zero, qa),
                              jnp.where(low, qb, zero), jnp.where(low, zero, qb)], axis=0)
        kcols = slice(g * LANES, (g + 1) * LANES)
        kk = jnp.concatenate([prev_ref[:, kcols], cur_ref[:, kcols], meta_ref[:, kcols]], axis=0)
        s = lax.dot_general(qs, kk, nt, preferred_element_type=F32)
        s_band = jnp.where(on_cur, s[:, w:2 * w], s[:, :w])
        return jnp.concatenate([jnp.where(valid_band, s_band, -jnp.inf),
                                jnp.where(valid_meta, s[:, 2 * w:], -jnp.inf)], axis=1)

    s_next = masked_scores(0)
    for g in kv_heads:
        s = s_next
        if g + 1 < SWA_KV_HEADS:
            s_next = masked_scores(g + 1)
        sink = jnp.zeros((SWA_GROUP * w, 1), F32)
        for h in range(SWA_GROUP):
            sink = jnp.where(hrow == h, sink_ref[g * SWA_GROUP + h], sink)
        m = jnp.maximum(jnp.max(s, axis=-1, keepdims=True), sink)
        p = jnp.exp(s - m)
        denom = jnp.sum(p, axis=-1, keepdims=True) + jnp.exp(sink - m)
        p_band = p[:, :w]
        pb = jnp.concatenate([jnp.where(on_cur, 0.0, p_band), jnp.where(on_cur, p_band, 0.0), p[:, w:]],
                             axis=1).astype(BF16)
        vcols = slice(kvw + g * LANES, kvw + (g + 1) * LANES)
        vv = jnp.concatenate([prev_ref[:, vcols], cur_ref[:, vcols], meta_ref[:, vcols]], axis=0)
        o = jnp.dot(pb, vv, preferred_element_type=F32) / denom
        oa = jnp.where(low, o[0:w], o[w:2 * w])
        ob = jnp.where(low, o[2 * w:3 * w], o[3 * w:4 * w])
        o_ref[:, 2 * g * LANES:(2 * g + 2) * LANES] = jnp.concatenate([oa, ob], axis=1).astype(o_ref.dtype)


def _swa_attention(q, kv, sinks):
    rows, qw = q.shape
    w = SWA_WINDOW
    kvw = kv.shape[1]
    return pl.pallas_call(
        _swa_kernel,
        grid=(rows // w,),
        in_specs=[pl.BlockSpec(memory_space=pltpu.SMEM),
                  pl.BlockSpec((w, qw), lambda n: (n, 0)),
                  pl.BlockSpec((w, kvw), lambda n: (jnp.maximum(n - 1, 0), 0)),
                  pl.BlockSpec((w, kvw), lambda n: (n, 0)),
                  pl.BlockSpec((w, kvw), lambda n: (0, 0))],
        out_specs=pl.BlockSpec((w, qw), lambda n: (n, 0)),
        out_shape=jax.ShapeDtypeStruct((rows, qw), BF16),
        compiler_params=_params("arbitrary"),
        name="swa",
    )(sinks, q, kv, kv, kv)


def _embed_kernel(x_ref, meta_ref, hf_ref, hb_ref):
    i = pl.program_id(0)

    @pl.when(i == 0)
    def _():
        hf_ref[...] = jnp.concatenate([jnp.zeros((FRONT_PAD, D_MODEL), F32), meta_ref[...]], axis=0)

    @pl.when(i != 0)
    def _():
        hf_ref[...] = x_ref[...]

    hb_ref[...] = hf_ref[...].astype(BF16)


def _embed(x, meta):
    seq, d = x.shape
    blk = FRONT_PAD + N_META
    assert seq % blk == 0
    rows = blk + seq
    out = pl.BlockSpec((blk, d), lambda i: (i, 0))
    return pl.pallas_call(
        _embed_kernel,
        grid=(rows // blk,),
        in_specs=[pl.BlockSpec((blk, d), lambda i: (jnp.maximum(i - 1, 0), 0)),
                  pl.BlockSpec((N_META, d), lambda i: (0, 0))],
        out_specs=[out, out],
        out_shape=[jax.ShapeDtypeStruct((rows, d), F32), jax.ShapeDtypeStruct((rows, d), BF16)],
        compiler_params=_params("arbitrary"),
        name="embed",
    )(x, meta)


def _pad_halves(t):
    pad = lambda a: jnp.pad(a, ((0, 0), (0, D_FF_PAD - D_FF)))
    return jnp.concatenate([pad(t[:, :D_FF]), pad(t[:, D_FF:])], axis=1)


def _trunk(x, meta_tokens, hyb_w_in, hyb_conv_w, hyb_conv_b, ssd_dt_bias, ssd_a_log, ssd_d, ssd_norm_w,
           gla_gate_w2, gla_gate_b, gla_norm_w, hyb_w_out, swa_w_qkv, swa_sinks, swa_w_out,
           ffn_w_up, ffn_conv_w, ffn_conv_b, ffn_w_down, ln_mix_g, ln_mix_b, ln_ffn_g, ln_ffn_b):
    seq = x.shape[0]
    rows = FRONT_PAD + N_META + seq
    h, hb = _embed(x, meta_tokens.astype(F32))
    tables = _rope_tables(rows)
    hyb_w_in = jnp.pad(hyb_w_in, ((0, 0), (0, 0), (0, -hyb_w_in.shape[2] % LANES))).astype(BF16)
    for layer in range(DEPTH):
        j = layer // 2
        if layer % 2 == 0:
            u_zx = _project(hb, hyb_w_in, j, first_block=0, block_stride=4, tn=512, n_tiles=12)
            u_qkv = _project(hb, hyb_w_in, j, first_block=HYB_QKV_COL // LANES, block_stride=4, tn=512, n_tiles=8,
                             shift=HYB_QKV_COL % LANES)
            u_r = _project(hb, hyb_w_in, j, first_block=HYB_R_COL // LANES, block_stride=4, tn=512, n_tiles=4,
                           shift=HYB_R_COL % LANES)
            u_small = _project(hb, hyb_w_in, j, first_block=HYB_DT_COL // LANES,
                               block_stride=(HYB_GLR_COL - HYB_DT_COL) // LANES, tn=LANES, n_tiles=2, out_dtype=F32)
            y_ssd = _ssd_mixer(u_zx, u_small, hyb_conv_w[j], hyb_conv_b[j], ssd_dt_bias[j], ssd_a_log[j],
                               ssd_d[j], ssd_norm_w[j])
            y_gla = _gla_mixer(u_qkv, u_r, u_small, gla_gate_w2[j], gla_gate_b[j], gla_norm_w[j])
            h, hb = _matmul_residual_ln([y_ssd, y_gla], [y_ssd.shape[1], y_gla.shape[1]], hyb_w_out, j, h,
                                        ln_mix_g[layer], ln_mix_b[layer], nchunks=8)
        else:
            q = _project(hb, swa_w_qkv, j, first_block=0, block_stride=4, tn=512, n_tiles=4,
                         scale=SWA_HEAD_DIM ** -0.5, mode="rope", tables=tables)
            kv = _project(hb, swa_w_qkv, j, first_block=SWA_Q_HEADS * SWA_HEAD_DIM // LANES, block_stride=4, tn=512,
                          n_tiles=2, mode="kv", tables=tables)
            attn = _swa_attention(q, kv, swa_sinks[j])
            h, hb = _matmul_residual_ln([attn], [attn.shape[1]], swa_w_out, j, h, ln_mix_g[layer], ln_mix_b[layer],
                                        nchunks=4)
        act = _ffn_up(hb, ffn_w_up, layer, _pad_halves(ffn_conv_w[layer]),
                      _pad_halves(ffn_conv_b[layer].reshape(1, -1)))
        h, hb = _matmul_residual_ln([act], [D_FF], ffn_w_down, layer, h, ln_ffn_g[layer], ln_ffn_b[layer], nchunks=8)
    return h[FRONT_PAD + N_META:]


def kernel(x, meta_tokens, hyb_w_in, hyb_conv_w, hyb_conv_b, ssd_dt_bias, ssd_a_log, ssd_d, ssd_norm_w,
           gla_gate_w2, gla_gate_b, gla_norm_w, hyb_w_out, swa_w_qkv, swa_sinks, swa_w_out,
           ffn_w_up, ffn_conv_w, ffn_conv_b, ffn_w_down, ln_mix_g, ln_mix_b, ln_ffn_g, ln_ffn_b):
    params = (meta_tokens, hyb_w_in, hyb_conv_w, hyb_conv_b, ssd_dt_bias, ssd_a_log, ssd_d, ssd_norm_w,
              gla_gate_w2, gla_gate_b, gla_norm_w, hyb_w_out, swa_w_qkv, swa_sinks, swa_w_out,
              ffn_w_up, ffn_conv_w, ffn_conv_b, ffn_w_down, ln_mix_g, ln_mix_b, ln_ffn_g, ln_ffn_b)
    return jnp.stack([_trunk(x[b], *params) for b in range(x.shape[0])], axis=0)
```

```python
import functools

import jax
import jax.numpy as jnp
from jax import lax
from jax.experimental import pallas as pl
from jax.experimental.pallas import tpu as pltpu

F32 = jnp.float32
BF16 = jnp.bfloat16
HIGHEST = lax.Precision.HIGHEST

D_MODEL = 2048
DEPTH = 4
N_META = 16
LN_EPS = 1e-5
RMS_EPS = 1e-6
DEEPNORM_ALPHA = (2.0 * DEPTH) ** 0.25

SSD_HEAD_DIM = 64
SSD_HEADS = 32
SSD_GROUPS = 8
SSD_HPG = 4
SSD_STATE = 128
SSD_CONV = 4
SSD_CHUNK = 128
SSD_GROUP_W = SSD_HPG * SSD_HEAD_DIM

GLA_HEADS = 4
GLA_HEAD_DK = 256
GLA_HEAD_DV = 512
GLA_GATE_RANK = 16
GLA_GATE_TAU = 16.0
GLA_CHUNK = 128
GLA_SUB = 16
GLA_SAFE_DECAY = 80.0

SWA_HEAD_DIM = 64
SWA_Q_HEADS = 32
SWA_KV_HEADS = 8
SWA_GROUP = 4
SWA_WINDOW = 128
ROPE_THETA = 500000.0
ROPE_DIM = 16

D_FF = 5504
FFN_CONV = 3

LANES = 128
SUBLANES = 8
MXU_DEPTH = 256
FRONT_PAD = SSD_CHUNK - N_META
D_FF_PAD = 5632
FFN_TN = 512
FFN_SUBTILES = 4
PROJ_SUBTILES = 4
LN_ROW_TILE = 320
LN_SUBTILES = 2
LN_WEIGHT_COPY_BUDGET = 20 * 1024 * 1024
VMEM_LIMIT = 56 * 1024 * 1024

HYB_DT_COL = 6144
HYB_QKV_COL = 6176
HYB_GLR_COL = 10272
HYB_R_COL = 10288


def _row_tile(rows):
    for t in (640, 512, 384, 256, 128):
        if rows % t == 0:
            return t
    raise ValueError(f"row count {rows} is not a multiple of 128")


def _row_tile_big(rows):
    return 1664 if rows % 1664 == 0 else _row_tile(rows)


def _params(*sem):
    return pltpu.CompilerParams(dimension_semantics=sem, vmem_limit_bytes=VMEM_LIMIT)


def _sigmoid(x):
    return 1.0 / (1.0 + jnp.exp(-x))


def _softplus(x):
    return jnp.maximum(x, 0.0) + jnp.log(1.0 + jnp.exp(-jnp.abs(x)))


def _log_sigmoid(x):
    return jnp.minimum(x, 0.0) - jnp.log(1.0 + jnp.exp(-jnp.abs(x)))


def _causal_conv_rows(y, carry, w_ref, b_row, taps):
    top = jnp.concatenate([carry, y[0:SUBLANES]], axis=0)
    w_last = w_ref[taps - 1:taps, :]
    acc = b_row + w_last * y
    acc_top = b_row + w_last * y[0:SUBLANES]
    for s in range(1, taps):
        wk = w_ref[taps - 1 - s:taps - s, :]
        acc = acc + wk * pltpu.roll(y, s, 0)
        acc_top = acc_top + wk * pltpu.roll(top, s, 0)[SUBLANES:2 * SUBLANES]
    return jnp.concatenate([acc_top, acc[SUBLANES:]], axis=0)


def _assemble_weight(w_refs, wb_ref, shift, scale):
    k, tn = wb_ref.shape
    chunk = 256
    for r in range(0, k, chunk):
        w = jnp.concatenate([wr[r:r + chunk, :] for wr in w_refs], axis=1)
        if shift:
            w = pltpu.roll(w.astype(F32), w.shape[1] - shift, 1)
        w = w[:, :tn]
        if scale is not None:
            w = w * scale
        wb_ref[r:r + chunk, :] = w.astype(BF16)


def _weight_block_specs(k, layer, block_fns):
    return [pl.BlockSpec((None, k, LANES), functools.partial(lambda *ids, fn: (layer, 0, fn(*ids)), fn=fn))
            for fn in block_fns]


def _rotate_heads(y, cm, sp, sm):
    half = ROPE_DIM // 2
    out = []
    for c in range(y.shape[1] // LANES):
        yc = y[:, c * LANES:(c + 1) * LANES]
        out.append(yc * cm + pltpu.roll(yc, LANES - half, 1) * sp + pltpu.roll(yc, half, 1) * sm)
    return jnp.concatenate(out, axis=1)


def _duplicate_heads(y):
    low = lax.broadcasted_iota(jnp.int32, (1, LANES), 1) < SWA_HEAD_DIM
    out = []
    for c in range(y.shape[1] // LANES):
        yc = y[:, c * LANES:(c + 1) * LANES]
        rolled = pltpu.roll(yc, SWA_HEAD_DIM, 1)
        out.append(jnp.where(low, yc, rolled))
        out.append(jnp.where(low, rolled, yc))
    return jnp.concatenate(out, axis=1)


def _proj_kernel(*refs, nblk, shift, scale, mode):
    x_ref = refs[0]
    w_refs = refs[1:1 + nblk]
    rest = refs[1 + nblk:]
    if mode != "plain":
        cm_ref, sp_ref, sm_ref = rest[:3]
        rest = rest[3:]
    o_ref, wb_ref = rest
    j = pl.program_id(0)
    i = pl.program_id(1)

    @pl.when(i == 0)
    def _():
        _assemble_weight(w_refs, wb_ref, shift, scale)

    if mode == "plain":
        o_ref[...] = jnp.dot(x_ref[...], wb_ref[...], preferred_element_type=F32).astype(o_ref.dtype)
        return
    tm = x_ref.shape[0]
    ts = tm // PROJ_SUBTILES
    ys = [jnp.dot(x_ref[s * ts:(s + 1) * ts, :], wb_ref[...], preferred_element_type=F32)
          for s in range(PROJ_SUBTILES)]
    rotate = (j == 0) if mode == "kv" else True
    for s, y in enumerate(ys):
        rows = slice(s * ts, (s + 1) * ts)
        cm = jnp.where(rotate, cm_ref[rows, :], 1.0)
        sp = jnp.where(rotate, sp_ref[rows, :], 0.0)
        sm = jnp.where(rotate, sm_ref[rows, :], 0.0)
        y = _rotate_heads(y, cm, sp, sm)
        if mode == "kv":
            y = _duplicate_heads(y)
        o_ref[rows, :] = y.astype(o_ref.dtype)


def _project(xb, w, layer, *, first_block, block_stride, tn, n_tiles, shift=0, scale=None, mode="plain",
             tables=None, out_dtype=BF16):
    m, k = xb.shape
    tm = _row_tile_big(m)
    nb = tn // LANES
    tn_out = 2 * tn if mode == "kv" else tn
    if shift == 0 and block_stride == nb and first_block % nb == 0:
        nblk = 1
        w_specs = [pl.BlockSpec((None, k, tn), lambda j, i: (layer, 0, first_block // nb + j))]
    else:
        nblk = nb + (1 if shift else 0)
        w_specs = _weight_block_specs(
            k, layer, [functools.partial(lambda j, i, b: first_block + j * block_stride + b, b=b) for b in range(nblk)])
    in_specs = [pl.BlockSpec((tm, k), lambda j, i: (i, 0))] + w_specs
    args = [xb] + [w] * nblk
    if mode != "plain":
        in_specs += [pl.BlockSpec((tm, LANES), lambda j, i: (i, 0))] * 3
        args += list(tables)
    return pl.pallas_call(
        functools.partial(_proj_kernel, nblk=nblk, shift=shift, scale=scale, mode=mode),
        grid=(n_tiles, m // tm),
        in_specs=in_specs,
        out_specs=pl.BlockSpec((tm, tn_out), lambda j, i: (i, j)),
        out_shape=jax.ShapeDtypeStruct((m, n_tiles * tn_out), out_dtype),
        scratch_shapes=[pltpu.VMEM((k, tn), BF16)],
        compiler_params=_params("arbitrary", "arbitrary"),
        name="proj_" + mode,
    )(*args)


def _mm_ln_kernel(*refs, nsrc, tm, subtiles, nchunks, ck):
    x_refs = refs[:nsrc]
    w_ref, res_ref, g_ref, b_ref, of_ref, ob_ref, wb_ref = refs[nsrc:]
    step = pl.program_id(0)

    @pl.when(step < nchunks)
    def _():
        wb_ref[pl.ds(pl.multiple_of(step * ck, ck), ck), :] = w_ref[...].astype(BF16)

    @pl.when(step >= nchunks)
    def _():
        i = step - nchunks
        ts = tm // subtiles
        sums = []
        for s in range(subtiles):
            rows = slice(s * ts, (s + 1) * ts)
            acc, k0 = None, 0
            for x_ref in x_refs:
                kw = x_ref.shape[1]
                part = jnp.dot(x_ref[rows, :], wb_ref[k0:k0 + kw, :], preferred_element_type=F32)
                acc = part if acc is None else acc + part
                k0 += kw
            sums.append(acc)
        for s, acc in enumerate(sums):
            rows = slice(s * ts, (s + 1) * ts)
            t = DEEPNORM_ALPHA * res_ref[rows, :] + acc
            mu = jnp.mean(t, axis=-1, keepdims=True)
            d = t - mu
            var = jnp.mean(d * d, axis=-1, keepdims=True)
            y = d * lax.rsqrt(var + LN_EPS) * g_ref[...] + b_ref[...]
            row = lax.broadcasted_iota(jnp.int32, (ts, 1), 0) + (i * tm + s * ts)
            y = jnp.where(row >= FRONT_PAD, y, 0.0)
            of_ref[rows, :] = y
            ob_ref[rows, :] = y.astype(BF16)


def _matmul_residual_ln(xs, k_widths, w, layer, res, gamma, beta, nchunks):
    m = xs[0].shape[0]
    _, kdim, n = w.shape
    assert sum(k_widths) == kdim and m % LN_ROW_TILE == 0 and kdim % (16 * nchunks) == 0
    tm = LN_ROW_TILE
    ck = kdim // nchunks
    subtiles = LN_SUBTILES if 2 * kdim * n * 2 <= LN_WEIGHT_COPY_BUDGET else 1
    tile = lambda s: jnp.maximum(s - nchunks, 0)
    x_specs = [pl.BlockSpec((tm, kw), lambda s: (tile(s), 0)) for kw in k_widths]
    return pl.pallas_call(
        functools.partial(_mm_ln_kernel, nsrc=len(xs), tm=tm, subtiles=subtiles, nchunks=nchunks, ck=ck),
        grid=(nchunks + m // tm,),
        in_specs=x_specs + [pl.BlockSpec((None, ck, n), lambda s: (layer, jnp.minimum(s, nchunks - 1), 0)),
                            pl.BlockSpec((tm, n), lambda s: (tile(s), 0)),
                            pl.BlockSpec((1, n), lambda s: (0, 0)),
                            pl.BlockSpec((1, n), lambda s: (0, 0))],
        out_specs=[pl.BlockSpec((tm, n), lambda s: (tile(s), 0)),
                   pl.BlockSpec((tm, n), lambda s: (tile(s), 0))],
        out_shape=[jax.ShapeDtypeStruct((m, n), F32), jax.ShapeDtypeStruct((m, n), BF16)],
        scratch_shapes=[pltpu.VMEM((kdim, n), BF16)],
        compiler_params=_params("arbitrary"),
        name="proj_ln",
    )(*xs, w, res, gamma.reshape(1, n), beta.reshape(1, n))


def _ffn_up_kernel(*refs, nblk, tm, tn):
    x_ref = refs[0]
    w_refs = refs[1:1 + nblk]
    cw_g, cw_v, cb_g, cb_v, o_ref, wb_ref, cg_ref, cv_ref = refs[1 + nblk:]
    j = pl.program_id(0)
    i = pl.program_id(1)

    @pl.when(i == 0)
    def _():
        _assemble_weight(w_refs, wb_ref, 0, None)
        cg_ref[...] = jnp.zeros_like(cg_ref)
        cv_ref[...] = jnp.zeros_like(cv_ref)

    col = lax.broadcasted_iota(jnp.int32, (1, tn), 1) + j * tn
    ts = tm // FFN_SUBTILES
    ys = []
    for s in range(FFN_SUBTILES):
        y = jnp.dot(x_ref[s * ts:(s + 1) * ts, :], wb_ref[...], preferred_element_type=F32)
        ys.append((y[:, :tn], y[:, tn:]))
    carry_g, carry_v = cg_ref[...], cv_ref[...]
    for s, (yg, yv) in enumerate(ys):
        hg = _causal_conv_rows(yg, carry_g, cw_g, cb_g[...], FFN_CONV)
        hv = _causal_conv_rows(yv, carry_v, cw_v, cb_v[...], FFN_CONV)
        carry_g, carry_v = yg[ts - SUBLANES:ts], yv[ts - SUBLANES:ts]
        o_ref[s * ts:(s + 1) * ts, :] = jnp.where(col < D_FF, hg * _sigmoid(hg) * hv, 0.0).astype(o_ref.dtype)
    cg_ref[...] = carry_g
    cv_ref[...] = carry_v


def _ffn_up(xb, w_up, layer, conv_w, conv_b):
    m, k = xb.shape
    tm = _row_tile_big(m)
    assert (tm // FFN_SUBTILES) % 16 == 0, tm
    tn = FFN_TN
    nj = D_FF_PAD // tn
    nb = tn // LANES
    val0 = D_FF // LANES
    last_blk = 2 * D_FF // LANES - 1
    w_specs = [pl.BlockSpec((None, k, tn), lambda j, i: (layer, 0, j))] + _weight_block_specs(
        k, layer, [functools.partial(lambda j, i, b: jnp.minimum(val0 + j * nb + b, last_blk), b=b) for b in range(nb)])
    nblk = 1 + nb
    return pl.pallas_call(
        functools.partial(_ffn_up_kernel, nblk=nblk, tm=tm, tn=tn),
        grid=(nj, m // tm),
        in_specs=[pl.BlockSpec((tm, k), lambda j, i: (i, 0))] + w_specs
        + [pl.BlockSpec((FFN_CONV, tn), lambda j, i: (0, j)),
           pl.BlockSpec((FFN_CONV, tn), lambda j, i: (0, nj + j)),
           pl.BlockSpec((1, tn), lambda j, i: (0, j)),
           pl.BlockSpec((1, tn), lambda j, i: (0, nj + j))],
        out_specs=pl.BlockSpec((tm, tn), lambda j, i: (i, j)),
        out_shape=jax.ShapeDtypeStruct((m, D_FF_PAD), BF16),
        scratch_shapes=[pltpu.VMEM((k, 2 * tn), BF16), pltpu.VMEM((SUBLANES, tn), F32),
                        pltpu.VMEM((SUBLANES, tn), F32)],
        compiler_params=_params("arbitrary", "arbitrary"),
        name="ffn_up",
    )(xb, *([w_up] * nblk), conv_w, conv_w, conv_b, conv_b)


def _expand_heads(x, rows, g):
    low = lax.broadcasted_iota(jnp.int32, (1, LANES), 1) < SSD_HEAD_DIM
    h0 = g * SSD_HPG
    b = [jnp.broadcast_to(x[:, h0 + h:h0 + h + 1], (rows, LANES)) for h in range(SSD_HPG)]
    return jnp.concatenate([jnp.where(low, b[0], b[1]), jnp.where(low, b[2], b[3])], axis=1)


def _ssd_kernel(z_ref, x_ref, bc_ref, dt_ref, cwx_ref, cwbc_ref, cbx_ref, cbbc_ref, dtb_ref, alog_ref, dsk_ref,
                nw_ref, o_ref, st_ref, cx_ref, cbc_ref):
    c = pl.program_id(0)
    q = SSD_CHUNK
    gw = SSD_GROUP_W
    n = SSD_STATE
    groups = range(SSD_GROUPS)

    @pl.when(c == 0)
    def _():
        st_ref[...] = jnp.zeros_like(st_ref)
        cx_ref[...] = jnp.zeros_like(cx_ref)
        cbc_ref[...] = jnp.zeros_like(cbc_ref)

    row = lax.broadcasted_iota(jnp.int32, (q, 1), 0) + c * q
    lane = lax.broadcasted_iota(jnp.int32, (1, LANES), 1)
    dt = _softplus(dt_ref[...] + dtb_ref[...])
    dt = jnp.where((row >= FRONT_PAD) & (lane < SSD_HEADS), dt, 0.0)
    a = -jnp.exp(alog_ref[...])
    ri = lax.broadcasted_iota(jnp.int32, (q, q), 0)
    ci = lax.broadcasted_iota(jnp.int32, (q, q), 1)
    causal = ci <= ri
    cs = jnp.dot(causal.astype(F32), dt * a, precision=HIGHEST, preferred_element_type=F32)
    cs_t = cs.T
    cs_last = cs[q - 1:q, :]
    from_start = jnp.exp(cs)
    to_end = jnp.exp(cs_last - cs)
    total = jnp.exp(cs_last)

    def conv_silu(raw_ref, carry_ref, w_ref, b_ref, lo, width):
        raw = raw_ref[:, lo:lo + width].astype(F32)
        y = _causal_conv_rows(raw, carry_ref[:, lo:lo + width], w_ref.at[:, lo:lo + width], b_ref[:, lo:lo + width],
                              SSD_CONV)
        carry_ref[:, lo:lo + width] = raw[q - SUBLANES:q]
        return y * _sigmoid(y)

    xs = [conv_silu(x_ref, cx_ref, cwx_ref, cbx_ref, g * gw, gw) for g in groups]
    bm_t = [conv_silu(bc_ref, cbc_ref, cwbc_ref, cbbc_ref, g * n, n).T.astype(BF16) for g in groups]
    cmb = [conv_silu(bc_ref, cbc_ref, cwbc_ref, cbbc_ref, SSD_GROUPS * n + g * n, n).astype(BF16) for g in groups]
    cb = [jnp.dot(cmb[g], bm_t[g], preferred_element_type=F32) for g in groups]

    lane_w = lax.broadcasted_iota(jnp.int32, (1, gw), 1)
    xd = [xs[g] * _expand_heads(dt, q, g) for g in groups]
    ys = []
    for g in groups:
        decayed = []
        for h in range(SSD_HPG):
            hh = g * SSD_HPG + h
            seg = cs[:, hh:hh + 1] - cs_t[hh:hh + 1, :]
            decayed.append((cb[g] * jnp.exp(jnp.where(causal, seg, -jnp.inf))).astype(BF16))
        xdb = xd[g].astype(BF16)
        x_heads = [jnp.where((lane_w >= h * SSD_HEAD_DIM) & (lane_w < (h + 1) * SSD_HEAD_DIM), xdb,
                             jnp.zeros_like(xdb)) for h in range(SSD_HPG)]
        ys.append(jnp.dot(jnp.concatenate(decayed, axis=1), jnp.concatenate(x_heads, axis=0),
                          preferred_element_type=F32))
    for g in groups:
        st = st_ref[g]
        ys[g] = ys[g] + (jnp.dot(cmb[g], st.astype(BF16), preferred_element_type=F32)
                         * _expand_heads(from_start, q, g))
        st_ref[g] = (st * _expand_heads(total, 1, g)
                     + jnp.dot(bm_t[g], (xd[g] * _expand_heads(to_end, q, g)).astype(BF16),
                               preferred_element_type=F32))
    for g in groups:
        cols = slice(g * gw, (g + 1) * gw)
        y = ys[g] + xs[g] * dsk_ref[:, cols]
        z = z_ref[:, cols].astype(F32)
        yg = y * (z * _sigmoid(z))
        yn = yg * lax.rsqrt(jnp.mean(yg * yg, axis=-1, keepdims=True) + RMS_EPS) * nw_ref[:, cols]
        o_ref[:, cols] = yn.astype(o_ref.dtype)


def _ssd_mixer(u_zx, u_small, conv_w, conv_b, dt_bias, a_log, d_skip, norm_w):
    rows = u_zx.shape[0]
    q = SSD_CHUNK
    d = SSD_GROUPS * SSD_GROUP_W
    pad_l = lambda v: jnp.pad(v.reshape(1, SSD_HEADS), ((0, 0), (0, LANES - SSD_HEADS)))
    whole = lambda r, w, j: pl.BlockSpec((r, w), lambda c: (0, j))
    return pl.pallas_call(
        _ssd_kernel,
        grid=(rows // q,),
        in_specs=[pl.BlockSpec((q, d), lambda c: (c, 0)),
                  pl.BlockSpec((q, d), lambda c: (c, 1)),
                  pl.BlockSpec((q, d), lambda c: (c, 2)),
                  pl.BlockSpec((q, LANES), lambda c: (c, 0)),
                  whole(SSD_CONV, d, 0), whole(SSD_CONV, d, 1), whole(1, d, 0), whole(1, d, 1),
                  whole(1, LANES, 0), whole(1, LANES, 0), whole(1, d, 0), whole(1, d, 0)],
        out_specs=pl.BlockSpec((q, d), lambda c: (c, 0)),
        out_shape=jax.ShapeDtypeStruct((rows, d), BF16),
        scratch_shapes=[pltpu.VMEM((SSD_GROUPS, SSD_STATE, SSD_GROUP_W), F32), pltpu.VMEM((SUBLANES, d), F32),
                        pltpu.VMEM((SUBLANES, d), F32)],
        compiler_params=_params("arbitrary"),
        name="ssd",
    )(u_zx, u_zx, u_zx, u_small, conv_w, conv_w, conv_b.reshape(1, -1), conv_b.reshape(1, -1),
      pad_l(dt_bias), pad_l(a_log), jnp.repeat(d_skip, SSD_HEAD_DIM).reshape(1, d), norm_w.reshape(1, d))


def _gla_scores_blocked(q, k, gc):
    qc = q.shape[0]
    sub = GLA_SUB
    lane_j = lax.broadcasted_iota(jnp.int32, (sub, qc), 1)
    row_i = lax.broadcasted_iota(jnp.int32, (sub, 1), 0)
    a_rows = []
    for blk in range(qc // sub):
        lo = blk * sub
        q_b = q[lo:lo + sub]
        g_b = gc[lo:lo + sub]
        a_blk = jnp.zeros((sub, qc), F32)
        for j in range(sub):
            k_j = k[lo + j:lo + j + 1, :]
            g_j = gc[lo + j:lo + j + 1, :]
            s_j = jnp.sum(q_b * k_j * jnp.exp(jnp.minimum(g_b - g_j, 0.0)), axis=1, keepdims=True)
            a_blk = jnp.where(lane_j == lo + j, jnp.where(row_i >= j, s_j, 0.0), a_blk)
        if blk > 0:
            g_ref0 = gc[lo:lo + 1, :]
            q_t = (q_b * jnp.exp(g_b - g_ref0)).astype(BF16)
            k_t = (k * jnp.exp(jnp.minimum(g_ref0 - gc, 0.0))).astype(BF16)
            off = lax.dot_general(q_t, k_t, (((1,), (1,)), ((), ())), preferred_element_type=F32)
            a_blk = jnp.where(lane_j < lo, off, a_blk)
        a_rows.append(a_blk)
    return jnp.concatenate(a_rows, axis=0)


def _gla_kernel(q_ref, k_ref, v_ref, r_ref, glr_ref, w2_ref, gb_ref, nw_ref, o_ref, st_ref, a_ref):
    c = pl.program_id(0)
    qc = GLA_CHUNK
    dk, dv = GLA_HEAD_DK, GLA_HEAD_DV
    heads = range(GLA_HEADS)
    nt = (((1,), (1,)), ((), ()))

    @pl.when(c == 0)
    def _():
        st_ref[...] = jnp.zeros_like(st_ref)

    pre = jnp.dot(glr_ref[...], w2_ref[...], precision=HIGHEST, preferred_element_type=F32) + gb_ref[...]
    g = _log_sigmoid(pre) * (1.0 / GLA_GATE_TAU)
    row = lax.broadcasted_iota(jnp.int32, (qc, 1), 0) + c * qc
    g = jnp.where(row >= FRONT_PAD, g, 0.0)
    ri = lax.broadcasted_iota(jnp.int32, (qc, qc), 0)
    ci = lax.broadcasted_iota(jnp.int32, (qc, qc), 1)
    causal = ci <= ri
    gc = jnp.dot(causal.astype(F32), g, precision=HIGHEST, preferred_element_type=F32)
    g_last = gc[qc - 1:qc, :]

    q = q_ref[...].astype(F32) * (GLA_HEAD_DK ** -0.5)
    k = k_ref[...].astype(F32)
    q_dec = (q * jnp.exp(gc)).astype(BF16)
    k_end = (k * jnp.exp(g_last - gc)).astype(BF16)
    safe = jnp.max(-g_last) <= GLA_SAFE_DECAY

    @pl.when(safe)
    def _():
        k_inv = (k * jnp.exp(-gc)).astype(BF16)
        for h in heads:
            s = lax.dot_general(q_dec[:, h * dk:(h + 1) * dk], k_inv[:, h * dk:(h + 1) * dk], nt,
                                preferred_element_type=F32)
            a_ref[h] = jnp.where(causal, s, 0.0)

    @pl.when(jnp.logical_not(safe))
    def _():
        for h in heads:
            a_ref[h] = _gla_scores_blocked(q[:, h * dk:(h + 1) * dk], k[:, h * dk:(h + 1) * dk],
                                           gc[:, h * dk:(h + 1) * dk])

    vb = v_ref[...]
    decay = jnp.exp(g_last)
    outs = []
    for h in heads:
        st = st_ref[h]
        v_h = vb[:, h * dv:(h + 1) * dv]
        o = jnp.dot(a_ref[h].astype(BF16), v_h, preferred_element_type=F32)
        o = o + lax.dot_general(q_dec[:, h * dk:(h + 1) * dk], st.astype(BF16), nt, preferred_element_type=F32)
        st_ref[h] = st * decay[:, h * dk:(h + 1) * dk] + lax.dot_general(
            v_h, k_end[:, h * dk:(h + 1) * dk], (((0,), (0,)), ((), ())), preferred_element_type=F32)
        outs.append(o)
    for h in heads:
        o = outs[h]
        on = o * lax.rsqrt(jnp.mean(o * o, axis=-1, keepdims=True) + RMS_EPS) * nw_ref[:, h * dv:(h + 1) * dv]
        r = r_ref[:, h * dv:(h + 1) * dv].astype(F32)
        o_ref[:, h * dv:(h + 1) * dv] = (on * (r * _sigmoid(r))).astype(o_ref.dtype)


def _gla_mixer(u_qkv, u_r, u_small, gate_w2, gate_b, norm_w):
    rows = u_qkv.shape[0]
    qc = GLA_CHUNK
    dk, dv = GLA_HEAD_DK, GLA_HEAD_DV
    wk, wv = GLA_HEADS * dk, GLA_HEADS * dv
    glr_lane = HYB_GLR_COL % LANES
    w2 = jnp.pad(gate_w2, ((glr_lane, LANES - GLA_GATE_RANK - glr_lane), (0, 0)))
    return pl.pallas_call(
        _gla_kernel,
        grid=(rows // qc,),
        in_specs=[pl.BlockSpec((qc, wk), lambda c: (c, 0)),
                  pl.BlockSpec((qc, wk), lambda c: (c, 1)),
                  pl.BlockSpec((qc, wv), lambda c: (c, 1)),
                  pl.BlockSpec((qc, wv), lambda c: (c, 0)),
                  pl.BlockSpec((qc, LANES), lambda c: (c, 1)),
                  pl.BlockSpec((LANES, wk), lambda c: (0, 0)),
                  pl.BlockSpec((1, wk), lambda c: (0, 0)),
                  pl.BlockSpec((1, wv), lambda c: (0, 0))],
        out_specs=pl.BlockSpec((qc, wv), lambda c: (c, 0)),
        out_shape=jax.ShapeDtypeStruct((rows, wv), BF16),
        scratch_shapes=[pltpu.VMEM((GLA_HEADS, dv, dk), F32), pltpu.VMEM((GLA_HEADS, qc, qc), F32)],
        compiler_params=_params("arbitrary"),
        name="gla",
    )(u_qkv, u_qkv, u_qkv, u_r, u_small, w2, gate_b.reshape(1, -1), norm_w.reshape(1, -1))


def _rope_kernel(freq_ref, cm_ref, sp_ref, sm_ref, *, tm):
    i = pl.program_id(0)
    row = lax.broadcasted_iota(jnp.int32, (tm, LANES), 0) + i * tm
    lane = lax.broadcasted_iota(jnp.int32, (tm, LANES), 1) % SWA_HEAD_DIM
    ang = (row - FRONT_PAD).astype(F32) * freq_ref[...]
    cos = jnp.cos(ang)
    sin = jnp.sin(ang)
    half = ROPE_DIM // 2
    cm_ref[...] = cos
    sp_ref[...] = jnp.where(lane < half, -sin, 0.0)
    sm_ref[...] = jnp.where((lane >= half) & (lane < ROPE_DIM), sin, 0.0)


def _rope_tables(rows):
    half = ROPE_DIM // 2
    inv_freq = ROPE_THETA ** (-jnp.arange(half, dtype=F32) / half)
    per_head = jnp.concatenate([inv_freq, inv_freq, jnp.zeros((SWA_HEAD_DIM - ROPE_DIM,), F32)])
    freq = jnp.tile(per_head, LANES // SWA_HEAD_DIM).reshape(1, LANES)
    tm = _row_tile(rows)
    shp = jax.ShapeDtypeStruct((rows, LANES), F32)
    spec = pl.BlockSpec((tm, LANES), lambda i: (i, 0))
    return pl.pallas_call(
        functools.partial(_rope_kernel, tm=tm),
        grid=(rows // tm,),
        in_specs=[pl.BlockSpec((1, LANES), lambda i: (0, 0))],
        out_specs=[spec, spec, spec],
        out_shape=[shp, shp, shp],
        compiler_params=_params("arbitrary"),
        name="rope_tables",
    )(freq)


def _swa_kernel(sink_ref, q_ref, prev_ref, cur_ref, meta_ref, o_ref):
    n = pl.program_id(0)
    w = SWA_WINDOW
    meta_lo = FRONT_PAD
    kvw = SWA_KV_HEADS * LANES
    nt = (((1,), (1,)), ((), ()))
    lane = lax.broadcasted_iota(jnp.int32, (1, LANES), 1)
    low = lane < SWA_HEAD_DIM
    i = lax.broadcasted_iota(jnp.int32, (SWA_GROUP * w, 1), 0) % w
    hrow = lax.broadcasted_iota(jnp.int32, (SWA_GROUP * w, 1), 0) // w
    on_cur = lane <= i
    valid_band = (on_cur & ((n >= 1) | (lane >= meta_lo))) | (jnp.logical_not(on_cur) & (n >= 2))
    valid_meta = (lane >= meta_lo) & (n >= 1)
    zero = jnp.zeros((w, LANES), q_ref.dtype)
    kv_heads = range(SWA_KV_HEADS)

    def masked_scores(g):
        qa = q_ref[:, 2 * g * LANES:(2 * g + 1) * LANES]
        qb = q_ref[:, (2 * g + 1) * LANES:(2 * g + 2) * LANES]
        qs = jnp.concatenate([jnp.where(low, qa, zero), jnp.where(low, zero, qa),
                              jnp.where(low, qb, zero), jnp.where(low, zero, qb)], axis=0)
        kcols = slice(g * LANES, (g + 1) * LANES)
        kk = jnp.concatenate([prev_ref[:, kcols], cur_ref[:, kcols], meta_ref[:, kcols]], axis=0)
        s = lax.dot_general(qs, kk, nt, preferred_element_type=F32)
        s_band = jnp.where(on_cur, s[:, w:2 * w], s[:, :w])
        return jnp.concatenate([jnp.where(valid_band, s_band, -jnp.inf),
                                jnp.where(valid_meta, s[:, 2 * w:], -jnp.inf)], axis=1)

    s_next = masked_scores(0)
    for g in kv_heads:
        s = s_next
        if g + 1 < SWA_KV_HEADS:
            s_next = masked_scores(g + 1)
        sink = jnp.zeros((SWA_GROUP * w, 1), F32)
        for h in range(SWA_GROUP):
            sink = jnp.where(hrow == h, sink_ref[g * SWA_GROUP + h], sink)
        m = jnp.maximum(jnp.max(s, axis=-1, keepdims=True), sink)
        p = jnp.exp(s - m)
        denom = jnp.sum(p, axis=-1, keepdims=True) + jnp.exp(sink - m)
        p_band = p[:, :w]
        pb = jnp.concatenate([jnp.where(on_cur, 0.0, p_band), jnp.where(on_cur, p_band, 0.0), p[:, w:]],
                             axis=1).astype(BF16)
        vcols = slice(kvw + g * LANES, kvw + (g + 1) * LANES)
        vv = jnp.concatenate([prev_ref[:, vcols], cur_ref[:, vcols], meta_ref[:, vcols]], axis=0)
        o = jnp.dot(pb, vv, preferred_element_type=F32) / denom
        oa = jnp.where(low, o[0:w], o[w:2 * w])
        ob = jnp.where(low, o[2 * w:3 * w], o[3 * w:4 * w])
        o_ref[:, 2 * g * LANES:(2 * g + 2) * LANES] = jnp.concatenate([oa, ob], axis=1).astype(o_ref.dtype)


def _swa_attention(q, kv, sinks):
    rows, qw = q.shape
    w = SWA_WINDOW
    kvw = kv.shape[1]
    return pl.pallas_call(
        _swa_kernel,
        grid=(rows // w,),
        in_specs=[pl.BlockSpec(memory_space=pltpu.SMEM),
                  pl.BlockSpec((w, qw), lambda n: (n, 0)),
                  pl.BlockSpec((w, kvw), lambda n: (jnp.maximum(n - 1, 0), 0)),
                  pl.BlockSpec((w, kvw), lambda n: (n, 0)),
                  pl.BlockSpec((w, kvw), lambda n: (0, 0))],
        out_specs=pl.BlockSpec((w, qw), lambda n: (n, 0)),
        out_shape=jax.ShapeDtypeStruct((rows, qw), BF16),
        compiler_params=_params("arbitrary"),
        name="swa",
    )(sinks, q, kv, kv, kv)


def _embed_kernel(x_ref, meta_ref, hf_ref, hb_ref, *, tm):
    i = pl.program_id(0)
    head = FRONT_PAD + N_META

    @pl.when(i == 0)
    def _():
        top = jnp.concatenate([jnp.zeros((FRONT_PAD, D_MODEL), F32), meta_ref[...]], axis=0)
        hf_ref[0:head, :] = top
        hb_ref[0:head, :] = top.astype(BF16)
        body = x_ref[0:tm - head, :]
        hf_ref[head:tm, :] = body
        hb_ref[head:tm, :] = body.astype(BF16)

    @pl.when(i != 0)
    def _():
        hf_ref[...] = x_ref[...]
        hb_ref[...] = x_ref[...].astype(BF16)


def _embed(x, meta):
    seq, d = x.shape
    head = FRONT_PAD + N_META
    rows = head + seq
    tm = _row_tile(rows)
    out = pl.BlockSpec((tm, d), lambda i: (i, 0))
    x_rows = lambda i: pl.multiple_of(jnp.maximum(i * tm - head, 0), LANES)
    return pl.pallas_call(
        functools.partial(_embed_kernel, tm=tm),
        grid=(rows // tm,),
        in_specs=[pl.BlockSpec((pl.Element(tm), pl.Element(d)), lambda i: (x_rows(i), 0)),
                  pl.BlockSpec((N_META, d), lambda i: (0, 0))],
        out_specs=[out, out],
        out_shape=[jax.ShapeDtypeStruct((rows, d), F32), jax.ShapeDtypeStruct((rows, d), BF16)],
        compiler_params=_params("arbitrary"),
        name="embed",
    )(x, meta)


def _pad_halves(t):
    pad = lambda a: jnp.pad(a, ((0, 0), (0, D_FF_PAD - D_FF)))
    return jnp.concatenate([pad(t[:, :D_FF]), pad(t[:, D_FF:])], axis=1)


def _trunk(x, meta_tokens, hyb_w_in, hyb_conv_w, hyb_conv_b, ssd_dt_bias, ssd_a_log, ssd_d, ssd_norm_w,
           gla_gate_w2, gla_gate_b, gla_norm_w, hyb_w_out, swa_w_qkv, swa_sinks, swa_w_out,
           ffn_w_up, ffn_conv_w, ffn_conv_b, ffn_w_down, ln_mix_g, ln_mix_b, ln_ffn_g, ln_ffn_b):
    seq = x.shape[0]
    rows = FRONT_PAD + N_META + seq
    h, hb = _embed(x, meta_tokens.astype(F32))
    tables = _rope_tables(rows)
    hyb_w_in = jnp.pad(hyb_w_in, ((0, 0), (0, 0), (0, -hyb_w_in.shape[2] % LANES))).astype(BF16)
    for layer in range(DEPTH):
        j = layer // 2
        if layer % 2 == 0:
            u_zx = _project(hb, hyb_w_in, j, first_block=0, block_stride=4, tn=512, n_tiles=12)
            u_qkv = _project(hb, hyb_w_in, j, first_block=HYB_QKV_COL // LANES, block_stride=4, tn=512, n_tiles=8,
                             shift=HYB_QKV_COL % LANES)
            u_r = _project(hb, hyb_w_in, j, first_block=HYB_R_COL // LANES, block_stride=4, tn=512, n_tiles=4,
                           shift=HYB_R_COL % LANES)
            u_small = _project(hb, hyb_w_in, j, first_block=HYB_DT_COL // LANES,
                               block_stride=(HYB_GLR_COL - HYB_DT_COL) // LANES, tn=LANES, n_tiles=2, out_dtype=F32)
            y_ssd = _ssd_mixer(u_zx, u_small, hyb_conv_w[j], hyb_conv_b[j], ssd_dt_bias[j], ssd_a_log[j],
                               ssd_d[j], ssd_norm_w[j])
            y_gla = _gla_mixer(u_qkv, u_r, u_small, gla_gate_w2[j], gla_gate_b[j], gla_norm_w[j])
            h, hb = _matmul_residual_ln([y_ssd, y_gla], [y_ssd.shape[1], y_gla.shape[1]], hyb_w_out, j, h,
                                        ln_mix_g[layer], ln_mix_b[layer], nchunks=8)
        else:
            q = _project(hb, swa_w_qkv, j, first_block=0, block_stride=4, tn=512, n_tiles=4,
                         scale=SWA_HEAD_DIM ** -0.5, mode="rope", tables=tables)
            kv = _project(hb, swa_w_qkv, j, first_block=SWA_Q_HEADS * SWA_HEAD_DIM // LANES, block_stride=4, tn=512,
                          n_tiles=2, mode="kv", tables=tables)
            attn = _swa_attention(q, kv, swa_sinks[j])
            h, hb = _matmul_residual_ln([attn], [attn.shape[1]], swa_w_out, j, h, ln_mix_g[layer], ln_mix_b[layer],
                                        nchunks=4)
        act = _ffn_up(hb, ffn_w_up, layer, _pad_halves(ffn_conv_w[layer]),
                      _pad_halves(ffn_conv_b[layer].reshape(1, -1)))
        h, hb = _matmul_residual_ln([act], [D_FF], ffn_w_down, layer, h, ln_ffn_g[layer], ln_ffn_b[layer], nchunks=8)
    return h[FRONT_PAD + N_META:]


def kernel(x, meta_tokens, hyb_w_in, hyb_conv_w, hyb_conv_b, ssd_dt_bias, ssd_a_log, ssd_d, ssd_norm_w,
           gla_gate_w2, gla_gate_b, gla_norm_w, hyb_w_out, swa_w_qkv, swa_sinks, swa_w_out,
           ffn_w_up, ffn_conv_w, ffn_conv_b, ffn_w_down, ln_mix_g, ln_mix_b, ln_ffn_g, ln_ffn_b):
    params = (meta_tokens, hyb_w_in, hyb_conv_w, hyb_conv_b, ssd_dt_bias, ssd_a_log, ssd_d, ssd_norm_w,
              gla_gate_w2, gla_gate_b, gla_norm_w, hyb_w_out, swa_w_qkv, swa_sinks, swa_w_out,
              ffn_w_up, ffn_conv_w, ffn_conv_b, ffn_w_down, ln_mix_g, ln_mix_b, ln_ffn_g, ln_ffn_b)
    return jnp.stack([_trunk(x[b], *params) for b in range(x.shape[0])], axis=0)
```

```python
import functools

import jax
import jax.numpy as jnp
from jax import lax
from jax.experimental import pallas as pl
from jax.experimental.pallas import tpu as pltpu

F32 = jnp.float32
BF16 = jnp.bfloat16
HIGHEST = lax.Precision.HIGHEST

D_MODEL = 2048
DEPTH = 4
N_META = 16
LN_EPS = 1e-5
RMS_EPS = 1e-6
DEEPNORM_ALPHA = (2.0 * DEPTH) ** 0.25

SSD_HEAD_DIM = 64
SSD_HEADS = 32
SSD_GROUPS = 8
SSD_HPG = 4
SSD_STATE = 128
SSD_CONV = 4
SSD_CHUNK = 128
SSD_GROUP_W = SSD_HPG * SSD_HEAD_DIM

GLA_HEADS = 4
GLA_HEAD_DK = 256
GLA_HEAD_DV = 512
GLA_GATE_RANK = 16
GLA_GATE_TAU = 16.0
GLA_CHUNK = 128
GLA_SUB = 16
GLA_SAFE_DECAY = 80.0

SWA_HEAD_DIM = 64
SWA_Q_HEADS = 32
SWA_KV_HEADS = 8
SWA_GROUP = 4
SWA_WINDOW = 128
ROPE_THETA = 500000.0
ROPE_DIM = 16

D_FF = 5504
FFN_CONV = 3

LANES = 128
SUBLANES = 8
MXU_DEPTH = 256
FRONT_PAD = SSD_CHUNK - N_META
D_FF_PAD = 5632
FFN_TN = 512
FFN_SUBTILES = 2
PROJ_SUBTILES = 4
LN_ROW_TILE = 320
LN_SUBTILES = 2
LN_WEIGHT_COPY_BUDGET = 20 * 1024 * 1024
VMEM_LIMIT = 56 * 1024 * 1024

HYB_DT_COL = 6144
HYB_QKV_COL = 6176
HYB_GLR_COL = 10272
HYB_R_COL = 10288


def _row_tile(rows):
    for t in (640, 512, 384, 256, 128):
        if rows % t == 0:
            return t
    raise ValueError(f"row count {rows} is not a multiple of 128")


def _row_tile_big(rows):
    return 1664 if rows % 1664 == 0 else _row_tile(rows)


def _params(*sem):
    return pltpu.CompilerParams(dimension_semantics=sem, vmem_limit_bytes=VMEM_LIMIT)


def _sigmoid(x):
    return 1.0 / (1.0 + jnp.exp(-x))


def _softplus(x):
    return jnp.maximum(x, 0.0) + jnp.log(1.0 + jnp.exp(-jnp.abs(x)))


def _log_sigmoid(x):
    return jnp.minimum(x, 0.0) - jnp.log(1.0 + jnp.exp(-jnp.abs(x)))


def _causal_conv_rows(y, carry, w_ref, b_row, taps):
    top = jnp.concatenate([carry, y[0:SUBLANES]], axis=0)
    w_last = w_ref[taps - 1:taps, :]
    acc = b_row + w_last * y
    acc_top = b_row + w_last * y[0:SUBLANES]
    for s in range(1, taps):
        wk = w_ref[taps - 1 - s:taps - s, :]
        acc = acc + wk * pltpu.roll(y, s, 0)
        acc_top = acc_top + wk * pltpu.roll(top, s, 0)[SUBLANES:2 * SUBLANES]
    return jnp.concatenate([acc_top, acc[SUBLANES:]], axis=0)


def _assemble_weight(w_refs, wb_ref, shift, scale):
    k, tn = wb_ref.shape
    chunk = 256
    for r in range(0, k, chunk):
        w = jnp.concatenate([wr[r:r + chunk, :] for wr in w_refs], axis=1)
        if shift:
            w = pltpu.roll(w.astype(F32), w.shape[1] - shift, 1)
        w = w[:, :tn]
        if scale is not None:
            w = w * scale
        wb_ref[r:r + chunk, :] = w.astype(BF16)


def _weight_block_specs(k, layer, block_fns):
    return [pl.BlockSpec((None, k, LANES), functools.partial(lambda *ids, fn: (layer, 0, fn(*ids)), fn=fn))
            for fn in block_fns]


def _rotate_heads(y, cm, sp, sm):
    half = ROPE_DIM // 2
    out = []
    for c in range(y.shape[1] // LANES):
        yc = y[:, c * LANES:(c + 1) * LANES]
        out.append(yc * cm + pltpu.roll(yc, LANES - half, 1) * sp + pltpu.roll(yc, half, 1) * sm)
    return jnp.concatenate(out, axis=1)


def _duplicate_heads(y):
    low = lax.broadcasted_iota(jnp.int32, (1, LANES), 1) < SWA_HEAD_DIM
    out = []
    for c in range(y.shape[1] // LANES):
        yc = y[:, c * LANES:(c + 1) * LANES]
        rolled = pltpu.roll(yc, SWA_HEAD_DIM, 1)
        out.append(jnp.where(low, yc, rolled))
        out.append(jnp.where(low, rolled, yc))
    return jnp.concatenate(out, axis=1)


def _proj_kernel(*refs, nblk, shift, scale, mode):
    x_ref = refs[0]
    w_refs = refs[1:1 + nblk]
    rest = refs[1 + nblk:]
    if mode != "plain":
        cm_ref, sp_ref, sm_ref = rest[:3]
        rest = rest[3:]
    o_ref, wb_ref = rest
    j = pl.program_id(0)
    i = pl.program_id(1)

    @pl.when(i == 0)
    def _():
        _assemble_weight(w_refs, wb_ref, shift, scale)

    if mode == "plain":
        o_ref[...] = jnp.dot(x_ref[...], wb_ref[...], preferred_element_type=F32).astype(o_ref.dtype)
        return
    tm = x_ref.shape[0]
    ts = tm // PROJ_SUBTILES
    ys = [jnp.dot(x_ref[s * ts:(s + 1) * ts, :], wb_ref[...], preferred_element_type=F32)
          for s in range(PROJ_SUBTILES)]
    rotate = (j == 0) if mode == "kv" else True
    for s, y in enumerate(ys):
        rows = slice(s * ts, (s + 1) * ts)
        cm = jnp.where(rotate, cm_ref[rows, :], 1.0)
        sp = jnp.where(rotate, sp_ref[rows, :], 0.0)
        sm = jnp.where(rotate, sm_ref[rows, :], 0.0)
        y = _rotate_heads(y, cm, sp, sm)
        if mode == "kv":
            y = _duplicate_heads(y)
        o_ref[rows, :] = y.astype(o_ref.dtype)


def _project(xb, w, layer, *, first_block, block_stride, tn, n_tiles, shift=0, scale=None, mode="plain",
             tables=None, out_dtype=BF16):
    m, k = xb.shape
    tm = _row_tile_big(m)
    nb = tn // LANES
    tn_out = 2 * tn if mode == "kv" else tn
    if shift == 0 and block_stride == nb and first_block % nb == 0:
        nblk = 1
        w_specs = [pl.BlockSpec((None, k, tn), lambda j, i: (layer, 0, first_block // nb + j))]
    else:
        nblk = nb + (1 if shift else 0)
        w_specs = _weight_block_specs(
            k, layer, [functools.partial(lambda j, i, b: first_block + j * block_stride + b, b=b) for b in range(nblk)])
    in_specs = [pl.BlockSpec((tm, k), lambda j, i: (i, 0))] + w_specs
    args = [xb] + [w] * nblk
    if mode != "plain":
        in_specs += [pl.BlockSpec((tm, LANES), lambda j, i: (i, 0))] * 3
        args += list(tables)
    return pl.pallas_call(
        functools.partial(_proj_kernel, nblk=nblk, shift=shift, scale=scale, mode=mode),
        grid=(n_tiles, m // tm),
        in_specs=in_specs,
        out_specs=pl.BlockSpec((tm, tn_out), lambda j, i: (i, j)),
        out_shape=jax.ShapeDtypeStruct((m, n_tiles * tn_out), out_dtype),
        scratch_shapes=[pltpu.VMEM((k, tn), BF16)],
        compiler_params=_params("arbitrary", "arbitrary"),
        name="proj_" + mode,
    )(*args)


def _mm_ln_kernel(*refs, nsrc, tm, subtiles, nchunks, ck):
    x_refs = refs[:nsrc]
    w_ref, res_ref, g_ref, b_ref, of_ref, ob_ref, wb_ref = refs[nsrc:]
    step = pl.program_id(0)

    @pl.when(step < nchunks)
    def _():
        wb_ref[pl.ds(pl.multiple_of(step * ck, ck), ck), :] = w_ref[...].astype(BF16)

    @pl.when(step >= nchunks)
    def _():
        i = step - nchunks
        ts = tm // subtiles
        sums = []
        for s in range(subtiles):
            rows = slice(s * ts, (s + 1) * ts)
            acc, k0 = None, 0
            for x_ref in x_refs:
                kw = x_ref.shape[1]
                part = jnp.dot(x_ref[rows, :], wb_ref[k0:k0 + kw, :], preferred_element_type=F32)
                acc = part if acc is None else acc + part
                k0 += kw
            sums.append(acc)
        for s, acc in enumerate(sums):
            rows = slice(s * ts, (s + 1) * ts)
            t = DEEPNORM_ALPHA * res_ref[rows, :] + acc
            mu = jnp.mean(t, axis=-1, keepdims=True)
            d = t - mu
            var = jnp.mean(d * d, axis=-1, keepdims=True)
            y = d * lax.rsqrt(var + LN_EPS) * g_ref[...] + b_ref[...]
            row = lax.broadcasted_iota(jnp.int32, (ts, 1), 0) + (i * tm + s * ts)
            y = jnp.where(row >= FRONT_PAD, y, 0.0)
            of_ref[rows, :] = y
            ob_ref[rows, :] = y.astype(BF16)


def _matmul_residual_ln(xs, k_widths, w, layer, res, gamma, beta, nchunks):
    m = xs[0].shape[0]
    _, kdim, n = w.shape
    assert sum(k_widths) == kdim and m % LN_ROW_TILE == 0 and kdim % (16 * nchunks) == 0
    tm = LN_ROW_TILE
    ck = kdim // nchunks
    subtiles = LN_SUBTILES if 2 * kdim * n * 2 <= LN_WEIGHT_COPY_BUDGET else 1
    tile = lambda s: jnp.maximum(s - nchunks, 0)
    x_specs = [pl.BlockSpec((tm, kw), lambda s: (tile(s), 0)) for kw in k_widths]
    return pl.pallas_call(
        functools.partial(_mm_ln_kernel, nsrc=len(xs), tm=tm, subtiles=subtiles, nchunks=nchunks, ck=ck),
        grid=(nchunks + m // tm,),
        in_specs=x_specs + [pl.BlockSpec((None, ck, n), lambda s: (layer, jnp.minimum(s, nchunks - 1), 0)),
                            pl.BlockSpec((tm, n), lambda s: (tile(s), 0)),
                            pl.BlockSpec((1, n), lambda s: (0, 0)),
                            pl.BlockSpec((1, n), lambda s: (0, 0))],
        out_specs=[pl.BlockSpec((tm, n), lambda s: (tile(s), 0)),
                   pl.BlockSpec((tm, n), lambda s: (tile(s), 0))],
        out_shape=[jax.ShapeDtypeStruct((m, n), F32), jax.ShapeDtypeStruct((m, n), BF16)],
        scratch_shapes=[pltpu.VMEM((kdim, n), BF16)],
        compiler_params=_params("arbitrary"),
        name="proj_ln",
    )(*xs, w, res, gamma.reshape(1, n), beta.reshape(1, n))


def _ffn_up_kernel(*refs, nblk, tm, tn):
    x_ref = refs[0]
    w_refs = refs[1:1 + nblk]
    cw_g, cw_v, cb_g, cb_v, o_ref, wb_ref, cg_ref, cv_ref = refs[1 + nblk:]
    j = pl.program_id(0)
    i = pl.program_id(1)

    @pl.when(i == 0)
    def _():
        _assemble_weight(w_refs, wb_ref, 0, None)
        cg_ref[...] = jnp.zeros_like(cg_ref)
        cv_ref[...] = jnp.zeros_like(cv_ref)

    col = lax.broadcasted_iota(jnp.int32, (1, tn), 1) + j * tn
    ts = tm // FFN_SUBTILES
    ys = []
    for s in range(FFN_SUBTILES):
        y = jnp.dot(x_ref[s * ts:(s + 1) * ts, :], wb_ref[...], preferred_element_type=F32)
        ys.append((y[:, :tn], y[:, tn:]))
    carry_g, carry_v = cg_ref[...], cv_ref[...]
    for s, (yg, yv) in enumerate(ys):
        hg = _causal_conv_rows(yg, carry_g, cw_g, cb_g[...], FFN_CONV)
        hv = _causal_conv_rows(yv, carry_v, cw_v, cb_v[...], FFN_CONV)
        carry_g, carry_v = yg[ts - SUBLANES:ts], yv[ts - SUBLANES:ts]
        o_ref[s * ts:(s + 1) * ts, :] = jnp.where(col < D_FF, hg * _sigmoid(hg) * hv, 0.0).astype(o_ref.dtype)
    cg_ref[...] = carry_g
    cv_ref[...] = carry_v


def _ffn_up(xb, w_up, layer, conv_w, conv_b):
    m, k = xb.shape
    tm = _row_tile_big(m)
    assert (tm // FFN_SUBTILES) % 16 == 0, tm
    tn = FFN_TN
    nj = D_FF_PAD // tn
    nb = tn // LANES
    val0 = D_FF // LANES
    last_blk = 2 * D_FF // LANES - 1
    w_specs = [pl.BlockSpec((None, k, tn), lambda j, i: (layer, 0, j))] + _weight_block_specs(
        k, layer, [functools.partial(lambda j, i, b: jnp.minimum(val0 + j * nb + b, last_blk), b=b) for b in range(nb)])
    nblk = 1 + nb
    return pl.pallas_call(
        functools.partial(_ffn_up_kernel, nblk=nblk, tm=tm, tn=tn),
        grid=(nj, m // tm),
        in_specs=[pl.BlockSpec((tm, k), lambda j, i: (i, 0))] + w_specs
        + [pl.BlockSpec((FFN_CONV, tn), lambda j, i: (0, j)),
           pl.BlockSpec((FFN_CONV, tn), lambda j, i: (0, nj + j)),
           pl.BlockSpec((1, tn), lambda j, i: (0, j)),
           pl.BlockSpec((1, tn), lambda j, i: (0, nj + j))],
        out_specs=pl.BlockSpec((tm, tn), lambda j, i: (i, j)),
        out_shape=jax.ShapeDtypeStruct((m, D_FF_PAD), BF16),
        scratch_shapes=[pltpu.VMEM((k, 2 * tn), BF16), pltpu.VMEM((SUBLANES, tn), F32),
                        pltpu.VMEM((SUBLANES, tn), F32)],
        compiler_params=_params("arbitrary", "arbitrary"),
        name="ffn_up",
    )(xb, *([w_up] * nblk), conv_w, conv_w, conv_b, conv_b)


def _expand_heads(x, rows, g):
    low = lax.broadcasted_iota(jnp.int32, (1, LANES), 1) < SSD_HEAD_DIM
    h0 = g * SSD_HPG
    b = [jnp.broadcast_to(x[:, h0 + h:h0 + h + 1], (rows, LANES)) for h in range(SSD_HPG)]
    return jnp.concatenate([jnp.where(low, b[0], b[1]), jnp.where(low, b[2], b[3])], axis=1)


def _ssd_kernel(z_ref, x_ref, bc_ref, dt_ref, cwx_ref, cwbc_ref, cbx_ref, cbbc_ref, dtb_ref, alog_ref, dsk_ref,
                nw_ref, o_ref, st_ref, cx_ref, cbc_ref):
    c = pl.program_id(0)
    q = SSD_CHUNK
    gw = SSD_GROUP_W
    n = SSD_STATE
    groups = range(SSD_GROUPS)

    @pl.when(c == 0)
    def _():
        st_ref[...] = jnp.zeros_like(st_ref)
        cx_ref[...] = jnp.zeros_like(cx_ref)
        cbc_ref[...] = jnp.zeros_like(cbc_ref)

    row = lax.broadcasted_iota(jnp.int32, (q, 1), 0) + c * q
    lane = lax.broadcasted_iota(jnp.int32, (1, LANES), 1)
    dt = _softplus(dt_ref[...] + dtb_ref[...])
    dt = jnp.where((row >= FRONT_PAD) & (lane < SSD_HEADS), dt, 0.0)
    a = -jnp.exp(alog_ref[...])
    ri = lax.broadcasted_iota(jnp.int32, (q, q), 0)
    ci = lax.broadcasted_iota(jnp.int32, (q, q), 1)
    causal = ci <= ri
    cs = jnp.dot(causal.astype(F32), dt * a, precision=HIGHEST, preferred_element_type=F32)
    cs_t = cs.T
    cs_last = cs[q - 1:q, :]
    from_start = jnp.exp(cs)
    to_end = jnp.exp(cs_last - cs)
    total = jnp.exp(cs_last)

    def conv_silu(raw_ref, carry_ref, w_ref, b_ref, lo, width):
        raw = raw_ref[:, lo:lo + width].astype(F32)
        y = _causal_conv_rows(raw, carry_ref[:, lo:lo + width], w_ref.at[:, lo:lo + width], b_ref[:, lo:lo + width],
                              SSD_CONV)
        carry_ref[:, lo:lo + width] = raw[q - SUBLANES:q]
        return y * _sigmoid(y)

    xs = [conv_silu(x_ref, cx_ref, cwx_ref, cbx_ref, g * gw, gw) for g in groups]
    bm_t = [conv_silu(bc_ref, cbc_ref, cwbc_ref, cbbc_ref, g * n, n).T.astype(BF16) for g in groups]
    cmb = [conv_silu(bc_ref, cbc_ref, cwbc_ref, cbbc_ref, SSD_GROUPS * n + g * n, n).astype(BF16) for g in groups]
    cb = [jnp.dot(cmb[g], bm_t[g], preferred_element_type=F32) for g in groups]

    lane_w = lax.broadcasted_iota(jnp.int32, (1, gw), 1)
    xd = [xs[g] * _expand_heads(dt, q, g) for g in groups]
    ys = []
    for g in groups:
        decayed = []
        for h in range(SSD_HPG):
            hh = g * SSD_HPG + h
            seg = cs[:, hh:hh + 1] - cs_t[hh:hh + 1, :]
            decayed.append((cb[g] * jnp.exp(jnp.where(causal, seg, -jnp.inf))).astype(BF16))
        xdb = xd[g].astype(BF16)
        x_heads = [jnp.where((lane_w >= h * SSD_HEAD_DIM) & (lane_w < (h + 1) * SSD_HEAD_DIM), xdb,
                             jnp.zeros_like(xdb)) for h in range(SSD_HPG)]
        ys.append(jnp.dot(jnp.concatenate(decayed, axis=1), jnp.concatenate(x_heads, axis=0),
                          preferred_element_type=F32))
    for g in groups:
        st = st_ref[g]
        ys[g] = ys[g] + (jnp.dot(cmb[g], st.astype(BF16), preferred_element_type=F32)
                         * _expand_heads(from_start, q, g))
        st_ref[g] = (st * _expand_heads(total, 1, g)
                     + jnp.dot(bm_t[g], (xd[g] * _expand_heads(to_end, q, g)).astype(BF16),
                               preferred_element_type=F32))
    for g in groups:
        cols = slice(g * gw, (g + 1) * gw)
        y = ys[g] + xs[g] * dsk_ref[:, cols]
        z = z_ref[:, cols].astype(F32)
        yg = y * (z * _sigmoid(z))
        yn = yg * lax.rsqrt(jnp.mean(yg * yg, axis=-1, keepdims=True) + RMS_EPS) * nw_ref[:, cols]
        o_ref[:, cols] = yn.astype(o_ref.dtype)


def _ssd_mixer(u_zx, u_small, conv_w, conv_b, dt_bias, a_log, d_skip, norm_w):
    rows = u_zx.shape[0]
    q = SSD_CHUNK
    d = SSD_GROUPS * SSD_GROUP_W
    pad_l = lambda v: jnp.pad(v.reshape(1, SSD_HEADS), ((0, 0), (0, LANES - SSD_HEADS)))
    whole = lambda r, w, j: pl.BlockSpec((r, w), lambda c: (0, j))
    return pl.pallas_call(
        _ssd_kernel,
        grid=(rows // q,),
        in_specs=[pl.BlockSpec((q, d), lambda c: (c, 0)),
                  pl.BlockSpec((q, d), lambda c: (c, 1)),
                  pl.BlockSpec((q, d), lambda c: (c, 2)),
                  pl.BlockSpec((q, LANES), lambda c: (c, 0)),
                  whole(SSD_CONV, d, 0), whole(SSD_CONV, d, 1), whole(1, d, 0), whole(1, d, 1),
                  whole(1, LANES, 0), whole(1, LANES, 0), whole(1, d, 0), whole(1, d, 0)],
        out_specs=pl.BlockSpec((q, d), lambda c: (c, 0)),
        out_shape=jax.ShapeDtypeStruct((rows, d), BF16),
        scratch_shapes=[pltpu.VMEM((SSD_GROUPS, SSD_STATE, SSD_GROUP_W), F32), pltpu.VMEM((SUBLANES, d), F32),
                        pltpu.VMEM((SUBLANES, d), F32)],
        compiler_params=_params("arbitrary"),
        name="ssd",
    )(u_zx, u_zx, u_zx, u_small, conv_w, conv_w, conv_b.reshape(1, -1), conv_b.reshape(1, -1),
      pad_l(dt_bias), pad_l(a_log), jnp.repeat(d_skip, SSD_HEAD_DIM).reshape(1, d), norm_w.reshape(1, d))


def _gla_scores_blocked(q, k, gc):
    qc = q.shape[0]
    sub = GLA_SUB
    lane_j = lax.broadcasted_iota(jnp.int32, (sub, qc), 1)
    row_i = lax.broadcasted_iota(jnp.int32, (sub, 1), 0)
    a_rows = []
    for blk in range(qc // sub):
        lo = blk * sub
        q_b = q[lo:lo + sub]
        g_b = gc[lo:lo + sub]
        a_blk = jnp.zeros((sub, qc), F32)
        for j in range(sub):
            k_j = k[lo + j:lo + j + 1, :]
            g_j = gc[lo + j:lo + j + 1, :]
            s_j = jnp.sum(q_b * k_j * jnp.exp(jnp.minimum(g_b - g_j, 0.0)), axis=1, keepdims=True)
            a_blk = jnp.where(lane_j == lo + j, jnp.where(row_i >= j, s_j, 0.0), a_blk)
        if blk > 0:
            g_ref0 = gc[lo:lo + 1, :]
            q_t = (q_b * jnp.exp(g_b - g_ref0)).astype(BF16)
            k_t = (k * jnp.exp(jnp.minimum(g_ref0 - gc, 0.0))).astype(BF16)
            off = lax.dot_general(q_t, k_t, (((1,), (1,)), ((), ())), preferred_element_type=F32)
            a_blk = jnp.where(lane_j < lo, off, a_blk)
        a_rows.append(a_blk)
    return jnp.concatenate(a_rows, axis=0)


def _gla_kernel(q_ref, k_ref, v_ref, r_ref, glr_ref, w2_ref, gb_ref, nw_ref, o_ref, st_ref, a_ref):
    c = pl.program_id(0)
    qc = GLA_CHUNK
    dk, dv = GLA_HEAD_DK, GLA_HEAD_DV
    heads = range(GLA_HEADS)
    nt = (((1,), (1,)), ((), ()))

    @pl.when(c == 0)
    def _():
        st_ref[...] = jnp.zeros_like(st_ref)

    pre = jnp.dot(glr_ref[...], w2_ref[...], precision=HIGHEST, preferred_element_type=F32) + gb_ref[...]
    g = _log_sigmoid(pre) * (1.0 / GLA_GATE_TAU)
    row = lax.broadcasted_iota(jnp.int32, (qc, 1), 0) + c * qc
    g = jnp.where(row >= FRONT_PAD, g, 0.0)
    ri = lax.broadcasted_iota(jnp.int32, (qc, qc), 0)
    ci = lax.broadcasted_iota(jnp.int32, (qc, qc), 1)
    causal = ci <= ri
    gc = jnp.dot(causal.astype(F32), g, precision=HIGHEST, preferred_element_type=F32)
    g_last = gc[qc - 1:qc, :]

    q = q_ref[...].astype(F32) * (GLA_HEAD_DK ** -0.5)
    k = k_ref[...].astype(F32)
    q_dec = (q * jnp.exp(gc)).astype(BF16)
    k_end = (k * jnp.exp(g_last - gc)).astype(BF16)
    safe = jnp.max(-g_last) <= GLA_SAFE_DECAY

    @pl.when(safe)
    def _():
        k_inv = (k * jnp.exp(-gc)).astype(BF16)
        for h in heads:
            s = lax.dot_general(q_dec[:, h * dk:(h + 1) * dk], k_inv[:, h * dk:(h + 1) * dk], nt,
                                preferred_element_type=F32)
            a_ref[h] = jnp.where(causal, s, 0.0)

    @pl.when(jnp.logical_not(safe))
    def _():
        for h in heads:
            a_ref[h] = _gla_scores_blocked(q[:, h * dk:(h + 1) * dk], k[:, h * dk:(h + 1) * dk],
                                           gc[:, h * dk:(h + 1) * dk])

    vb = v_ref[...]
    decay = jnp.exp(g_last)
    outs = []
    for h in heads:
        st = st_ref[h]
        v_h = vb[:, h * dv:(h + 1) * dv]
        o = jnp.dot(a_ref[h].astype(BF16), v_h, preferred_element_type=F32)
        o = o + lax.dot_general(q_dec[:, h * dk:(h + 1) * dk], st.astype(BF16), nt, preferred_element_type=F32)
        st_ref[h] = st * decay[:, h * dk:(h + 1) * dk] + lax.dot_general(
            v_h, k_end[:, h * dk:(h + 1) * dk], (((0,), (0,)), ((), ())), preferred_element_type=F32)
        outs.append(o)
    for h in heads:
        o = outs[h]
        on = o * lax.rsqrt(jnp.mean(o * o, axis=-1, keepdims=True) + RMS_EPS) * nw_ref[:, h * dv:(h + 1) * dv]
        r = r_ref[:, h * dv:(h + 1) * dv].astype(F32)
        o_ref[:, h * dv:(h + 1) * dv] = (on * (r * _sigmoid(r))).astype(o_ref.dtype)


def _gla_mixer(u_qkv, u_r, u_small, gate_w2, gate_b, norm_w):
    rows = u_qkv.shape[0]
    qc = GLA_CHUNK
    dk, dv = GLA_HEAD_DK, GLA_HEAD_DV
    wk, wv = GLA_HEADS * dk, GLA_HEADS * dv
    glr_lane = HYB_GLR_COL % LANES
    w2 = jnp.pad(gate_w2, ((glr_lane, LANES - GLA_GATE_RANK - glr_lane), (0, 0)))
    return pl.pallas_call(
        _gla_kernel,
        grid=(rows // qc,),
        in_specs=[pl.BlockSpec((qc, wk), lambda c: (c, 0)),
                  pl.BlockSpec((qc, wk), lambda c: (c, 1)),
                  pl.BlockSpec((qc, wv), lambda c: (c, 1)),
                  pl.BlockSpec((qc, wv), lambda c: (c, 0)),
                  pl.BlockSpec((qc, LANES), lambda c: (c, 1)),
                  pl.BlockSpec((LANES, wk), lambda c: (0, 0)),
                  pl.BlockSpec((1, wk), lambda c: (0, 0)),
                  pl.BlockSpec((1, wv), lambda c: (0, 0))],
        out_specs=pl.BlockSpec((qc, wv), lambda c: (c, 0)),
        out_shape=jax.ShapeDtypeStruct((rows, wv), BF16),
        scratch_shapes=[pltpu.VMEM((GLA_HEADS, dv, dk), F32), pltpu.VMEM((GLA_HEADS, qc, qc), F32)],
        compiler_params=_params("arbitrary"),
        name="gla",
    )(u_qkv, u_qkv, u_qkv, u_r, u_small, w2, gate_b.reshape(1, -1), norm_w.reshape(1, -1))


def _rope_kernel(freq_ref, cm_ref, sp_ref, sm_ref, *, tm):
    i = pl.program_id(0)
    row = lax.broadcasted_iota(jnp.int32, (tm, LANES), 0) + i * tm
    lane = lax.broadcasted_iota(jnp.int32, (tm, LANES), 1) % SWA_HEAD_DIM
    ang = (row - FRONT_PAD).astype(F32) * freq_ref[...]
    cos = jnp.cos(ang)
    sin = jnp.sin(ang)
    half = ROPE_DIM // 2
    cm_ref[...] = cos
    sp_ref[...] = jnp.where(lane < half, -sin, 0.0)
    sm_ref[...] = jnp.where((lane >= half) & (lane < ROPE_DIM), sin, 0.0)


def _rope_tables(rows):
    half = ROPE_DIM // 2
    inv_freq = ROPE_THETA ** (-jnp.arange(half, dtype=F32) / half)
    per_head = jnp.concatenate([inv_freq, inv_freq, jnp.zeros((SWA_HEAD_DIM - ROPE_DIM,), F32)])
    freq = jnp.tile(per_head, LANES // SWA_HEAD_DIM).reshape(1, LANES)
    tm = _row_tile(rows)
    shp = jax.ShapeDtypeStruct((rows, LANES), F32)
    spec = pl.BlockSpec((tm, LANES), lambda i: (i, 0))
    return pl.pallas_call(
        functools.partial(_rope_kernel, tm=tm),
        grid=(rows // tm,),
        in_specs=[pl.BlockSpec((1, LANES), lambda i: (0, 0))],
        out_specs=[spec, spec, spec],
        out_shape=[shp, shp, shp],
        compiler_params=_params("arbitrary"),
        name="rope_tables",
    )(freq)


def _swa_kernel(sink_ref, q_ref, prev_ref, cur_ref, meta_ref, o_ref):
    n = pl.program_id(0)
    w = SWA_WINDOW
    meta_lo = FRONT_PAD
    kvw = SWA_KV_HEADS * LANES
    nt = (((1,), (1,)), ((), ()))
    lane = lax.broadcasted_iota(jnp.int32, (1, LANES), 1)
    low = lane < SWA_HEAD_DIM
    i = lax.broadcasted_iota(jnp.int32, (SWA_GROUP * w, 1), 0) % w
    hrow = lax.broadcasted_iota(jnp.int32, (SWA_GROUP * w, 1), 0) // w
    on_cur = lane <= i
    valid_band = (on_cur & ((n >= 1) | (lane >= meta_lo))) | (jnp.logical_not(on_cur) & (n >= 2))
    valid_meta = (lane >= meta_lo) & (n >= 1)
    zero = jnp.zeros((w, LANES), q_ref.dtype)
    kv_heads = range(SWA_KV_HEADS)

    def masked_scores(g):
        qa = q_ref[:, 2 * g * LANES:(2 * g + 1) * LANES]
        qb = q_ref[:, (2 * g + 1) * LANES:(2 * g + 2) * LANES]
        qs = jnp.concatenate([jnp.where(low, qa, zero), jnp.where(low, zero, qa),
                              jnp.where(low, qb, zero), jnp.where(low, zero, qb)], axis=0)
        kcols = slice(g * LANES, (g + 1) * LANES)
        kk = jnp.concatenate([prev_ref[:, kcols], cur_ref[:, kcols], meta_ref[:, kcols]], axis=0)
        s = lax.dot_general(qs, kk, nt, preferred_element_type=F32)
        s_band = jnp.where(on_cur, s[:, w:2 * w], s[:, :w])
        return jnp.concatenate([jnp.where(valid_band, s_band, -jnp.inf),
                                jnp.where(valid_meta, s[:, 2 * w:], -jnp.inf)], axis=1)

    s_next = masked_scores(0)
    for g in kv_heads:
        s = s_next
        if g + 1 < SWA_KV_HEADS:
            s_next = masked_scores(g + 1)
        sink = jnp.zeros((SWA_GROUP * w, 1), F32)
        for h in range(SWA_GROUP):
            sink = jnp.where(hrow == h, sink_ref[g * SWA_GROUP + h], sink)
        m = jnp.maximum(jnp.max(s, axis=-1, keepdims=True), sink)
        p = jnp.exp(s - m)
        denom = jnp.sum(p, axis=-1, keepdims=True) + jnp.exp(sink - m)
        p_band = p[:, :w]
        pb = jnp.concatenate([jnp.where(on_cur, 0.0, p_band), jnp.where(on_cur, p_band, 0.0), p[:, w:]],
                             axis=1).astype(BF16)
        vcols = slice(kvw + g * LANES, kvw + (g + 1) * LANES)
        vv = jnp.concatenate([prev_ref[:, vcols], cur_ref[:, vcols], meta_ref[:, vcols]], axis=0)
        o = jnp.dot(pb, vv, preferred_element_type=F32) / denom
        oa = jnp.where(low, o[0:w], o[w:2 * w])
        ob = jnp.where(low, o[2 * w:3 * w], o[3 * w:4 * w])
        o_ref[:, 2 * g * LANES:(2 * g + 2) * LANES] = jnp.concatenate([oa, ob], axis=1).astype(o_ref.dtype)


def _swa_attention(q, kv, sinks):
    rows, qw = q.shape
    w = SWA_WINDOW
    kvw = kv.shape[1]
    return pl.pallas_call(
        _swa_kernel,
        grid=(rows // w,),
        in_specs=[pl.BlockSpec(memory_space=pltpu.SMEM),
                  pl.BlockSpec((w, qw), lambda n: (n, 0)),
                  pl.BlockSpec((w, kvw), lambda n: (jnp.maximum(n - 1, 0), 0)),
                  pl.BlockSpec((w, kvw), lambda n: (n, 0)),
                  pl.BlockSpec((w, kvw), lambda n: (0, 0))],
        out_specs=pl.BlockSpec((w, qw), lambda n: (n, 0)),
        out_shape=jax.ShapeDtypeStruct((rows, qw), BF16),
        compiler_params=_params("arbitrary"),
        name="swa",
    )(sinks, q, kv, kv, kv)


def _embed_kernel(x_ref, meta_ref, hf_ref, hb_ref, *, tm):
    i = pl.program_id(0)
    head = FRONT_PAD + N_META

    @pl.when(i == 0)
    def _():
        top = jnp.concatenate([jnp.zeros((FRONT_PAD, D_MODEL), F32), meta_ref[...]], axis=0)
        hf_ref[0:head, :] = top
        hb_ref[0:head, :] = top.astype(BF16)
        body = x_ref[0:tm - head, :]
        hf_ref[head:tm, :] = body
        hb_ref[head:tm, :] = body.astype(BF16)

    @pl.when(i != 0)
    def _():
        hf_ref[...] = x_ref[...]
        hb_ref[...] = x_ref[...].astype(BF16)


def _embed(x, meta):
    seq, d = x.shape
    head = FRONT_PAD + N_META
    rows = head + seq
    tm = _row_tile(rows)
    out = pl.BlockSpec((tm, d), lambda i: (i, 0))
    x_rows = lambda i: pl.multiple_of(jnp.maximum(i * tm - head, 0), LANES)
    return pl.pallas_call(
        functools.partial(_embed_kernel, tm=tm),
        grid=(rows // tm,),
        in_specs=[pl.BlockSpec((pl.Element(tm), pl.Element(d)), lambda i: (x_rows(i), 0)),
                  pl.BlockSpec((N_META, d), lambda i: (0, 0))],
        out_specs=[out, out],
        out_shape=[jax.ShapeDtypeStruct((rows, d), F32), jax.ShapeDtypeStruct((rows, d), BF16)],
        compiler_params=_params("arbitrary"),
        name="embed",
    )(x, meta)


def _pad_halves(t):
    pad = lambda a: jnp.pad(a, ((0, 0), (0, D_FF_PAD - D_FF)))
    return jnp.concatenate([pad(t[:, :D_FF]), pad(t[:, D_FF:])], axis=1)


def _trunk(x, meta_tokens, hyb_w_in, hyb_conv_w, hyb_conv_b, ssd_dt_bias, ssd_a_log, ssd_d, ssd_norm_w,
           gla_gate_w2, gla_gate_b, gla_norm_w, hyb_w_out, swa_w_qkv, swa_sinks, swa_w_out,
           ffn_w_up, ffn_conv_w, ffn_conv_b, ffn_w_down, ln_mix_g, ln_mix_b, ln_ffn_g, ln_ffn_b):
    seq = x.shape[0]
    rows = FRONT_PAD + N_META + seq
    h, hb = _embed(x, meta_tokens.astype(F32))
    tables = _rope_tables(rows)
    hyb_w_in = jnp.pad(hyb_w_in, ((0, 0), (0, 0), (0, -hyb_w_in.shape[2] % LANES))).astype(BF16)
    for layer in range(DEPTH):
        j = layer // 2
        if layer % 2 == 0:
            u_zx = _project(hb, hyb_w_in, j, first_block=0, block_stride=8, tn=1024, n_tiles=6)
            u_qkv = _project(hb, hyb_w_in, j, first_block=HYB_QKV_COL // LANES, block_stride=8, tn=1024, n_tiles=4,
                             shift=HYB_QKV_COL % LANES)
            u_r = _project(hb, hyb_w_in, j, first_block=HYB_R_COL // LANES, block_stride=8, tn=1024, n_tiles=2,
                           shift=HYB_R_COL % LANES)
            u_small = _project(hb, hyb_w_in, j, first_block=HYB_DT_COL // LANES,
                               block_stride=(HYB_GLR_COL - HYB_DT_COL) // LANES, tn=LANES, n_tiles=2, out_dtype=F32)
            y_ssd = _ssd_mixer(u_zx, u_small, hyb_conv_w[j], hyb_conv_b[j], ssd_dt_bias[j], ssd_a_log[j],
                               ssd_d[j], ssd_norm_w[j])
            y_gla = _gla_mixer(u_qkv, u_r, u_small, gla_gate_w2[j], gla_gate_b[j], gla_norm_w[j])
            h, hb = _matmul_residual_ln([y_ssd, y_gla], [y_ssd.shape[1], y_gla.shape[1]], hyb_w_out, j, h,
                                        ln_mix_g[layer], ln_mix_b[layer], nchunks=8)
        else:
            q = _project(hb, swa_w_qkv, j, first_block=0, block_stride=4, tn=512, n_tiles=4,
                         scale=SWA_HEAD_DIM ** -0.5, mode="rope", tables=tables)
            kv = _project(hb, swa_w_qkv, j, first_block=SWA_Q_HEADS * SWA_HEAD_DIM // LANES, block_stride=4, tn=512,
                          n_tiles=2, mode="kv", tables=tables)
            attn = _swa_attention(q, kv, swa_sinks[j])
            h, hb = _matmul_residual_ln([attn], [attn.shape[1]], swa_w_out, j, h, ln_mix_g[layer], ln_mix_b[layer],
                                        nchunks=4)
        act = _ffn_up(hb, ffn_w_up, layer, _pad_halves(ffn_conv_w[layer]),
                      _pad_halves(ffn_conv_b[layer].reshape(1, -1)))
        h, hb = _matmul_residual_ln([act], [D_FF], ffn_w_down, layer, h, ln_ffn_g[layer], ln_ffn_b[layer], nchunks=8)
    return h[FRONT_PAD + N_META:]


def kernel(x, meta_tokens, hyb_w_in, hyb_conv_w, hyb_conv_b, ssd_dt_bias, ssd_a_log, ssd_d, ssd_norm_w,
           gla_gate_w2, gla_gate_b, gla_norm_w, hyb_w_out, swa_w_qkv, swa_sinks, swa_w_out,
           ffn_w_up, ffn_conv_w, ffn_conv_b, ffn_w_down, ln_mix_g, ln_mix_b, ln_ffn_g, ln_ffn_b):
    params = (meta_tokens, hyb_w_in, hyb_conv_w, hyb_conv_b, ssd_dt_bias, ssd_a_log, ssd_d, ssd_norm_w,
              gla_gate_w2, gla_gate_b, gla_norm_w, hyb_w_out, swa_w_qkv, swa_sinks, swa_w_out,
              ffn_w_up, ffn_conv_w, ffn_conv_b, ffn_w_down, ln_mix_g, ln_mix_b, ln_ffn_g, ln_ffn_b)
    return jnp.stack([_trunk(x[b], *params) for b in range(x.shape[0])], axis=0)
```

```python
import functools

import jax
import jax.numpy as jnp
from jax import lax
from jax.experimental import pallas as pl
from jax.experimental.pallas import tpu as pltpu

F32 = jnp.float32
BF16 = jnp.bfloat16
HIGHEST = lax.Precision.HIGHEST

D_MODEL = 2048
DEPTH = 4
N_META = 16
LN_EPS = 1e-5
RMS_EPS = 1e-6
DEEPNORM_ALPHA = (2.0 * DEPTH) ** 0.25

SSD_HEAD_DIM = 64
SSD_HEADS = 32
SSD_GROUPS = 8
SSD_HPG = 4
SSD_STATE = 128
SSD_CONV = 4
SSD_CHUNK = 128
SSD_GROUP_W = SSD_HPG * SSD_HEAD_DIM

GLA_HEADS = 4
GLA_HEAD_DK = 256
GLA_HEAD_DV = 512
GLA_GATE_RANK = 16
GLA_GATE_TAU = 16.0
GLA_CHUNK = 128
GLA_SUB = 16
GLA_SAFE_DECAY = 80.0

SWA_HEAD_DIM = 64
SWA_Q_HEADS = 32
SWA_KV_HEADS = 8
SWA_GROUP = 4
SWA_WINDOW = 128
ROPE_THETA = 500000.0
ROPE_DIM = 16

D_FF = 5504
FFN_CONV = 3

LANES = 128
SUBLANES = 8
MXU_DEPTH = 256
FRONT_PAD = SSD_CHUNK - N_META
D_FF_PAD = 5632
FFN_TN = 512
FFN_SUBTILES = 4
PROJ_SUBTILES = 4
LN_ROW_TILE = 320
LN_SUBTILES = 2
LN_WEIGHT_COPY_BUDGET = 20 * 1024 * 1024
VMEM_LIMIT = 56 * 1024 * 1024

HYB_DT_COL = 6144
HYB_QKV_COL = 6176
HYB_GLR_COL = 10272
HYB_R_COL = 10288


def _row_tile(rows):
    for t in (640, 512, 384, 256, 128):
        if rows % t == 0:
            return t
    raise ValueError(f"row count {rows} is not a multiple of 128")


def _row_tile_big(rows):
    return 1664 if rows % 1664 == 0 else _row_tile(rows)


def _params(*sem):
    return pltpu.CompilerParams(dimension_semantics=sem, vmem_limit_bytes=VMEM_LIMIT)


def _sigmoid(x):
    return 1.0 / (1.0 + jnp.exp(-x))


def _softplus(x):
    return jnp.maximum(x, 0.0) + jnp.log(1.0 + jnp.exp(-jnp.abs(x)))


def _log_sigmoid(x):
    return jnp.minimum(x, 0.0) - jnp.log(1.0 + jnp.exp(-jnp.abs(x)))


def _causal_conv_rows(y, carry, w_ref, b_row, taps):
    top = jnp.concatenate([carry, y[0:SUBLANES]], axis=0)
    w_last = w_ref[taps - 1:taps, :]
    acc = b_row + w_last * y
    acc_top = b_row + w_last * y[0:SUBLANES]
    for s in range(1, taps):
        wk = w_ref[taps - 1 - s:taps - s, :]
        acc = acc + wk * pltpu.roll(y, s, 0)
        acc_top = acc_top + wk * pltpu.roll(top, s, 0)[SUBLANES:2 * SUBLANES]
    return jnp.concatenate([acc_top, acc[SUBLANES:]], axis=0)


def _assemble_weight(w_refs, wb_ref, shift, scale):
    k, tn = wb_ref.shape
    chunk = 256
    for r in range(0, k, chunk):
        w = jnp.concatenate([wr[r:r + chunk, :] for wr in w_refs], axis=1)
        if shift:
            w = pltpu.roll(w.astype(F32), w.shape[1] - shift, 1)
        w = w[:, :tn]
        if scale is not None:
            w = w * scale
        wb_ref[r:r + chunk, :] = w.astype(BF16)


def _weight_block_specs(k, layer, block_fns):
    return [pl.BlockSpec((None, k, LANES), functools.partial(lambda *ids, fn: (layer, 0, fn(*ids)), fn=fn))
            for fn in block_fns]


def _rotate_heads(y, cm, sp, sm):
    half = ROPE_DIM // 2
    out = []
    for c in range(y.shape[1] // LANES):
        yc = y[:, c * LANES:(c + 1) * LANES]
        out.append(yc * cm + pltpu.roll(yc, LANES - half, 1) * sp + pltpu.roll(yc, half, 1) * sm)
    return jnp.concatenate(out, axis=1)


def _duplicate_heads(y):
    low = lax.broadcasted_iota(jnp.int32, (1, LANES), 1) < SWA_HEAD_DIM
    out = []
    for c in range(y.shape[1] // LANES):
        yc = y[:, c * LANES:(c + 1) * LANES]
        rolled = pltpu.roll(yc, SWA_HEAD_DIM, 1)
        out.append(jnp.where(low, yc, rolled))
        out.append(jnp.where(low, rolled, yc))
    return jnp.concatenate(out, axis=1)


def _proj_kernel(*refs, nblk, shift, scale, mode):
    x_ref = refs[0]
    w_refs = refs[1:1 + nblk]
    rest = refs[1 + nblk:]
    if mode != "plain":
        cm_ref, sp_ref, sm_ref = rest[:3]
        rest = rest[3:]
    o_ref, wb_ref = rest
    j = pl.program_id(0)
    i = pl.program_id(1)

    @pl.when(i == 0)
    def _():
        _assemble_weight(w_refs, wb_ref, shift, scale)

    if mode == "plain":
        o_ref[...] = jnp.dot(x_ref[...], wb_ref[...], preferred_element_type=F32).astype(o_ref.dtype)
        return
    tm = x_ref.shape[0]
    ts = tm // PROJ_SUBTILES
    ys = [jnp.dot(x_ref[s * ts:(s + 1) * ts, :], wb_ref[...], preferred_element_type=F32)
          for s in range(PROJ_SUBTILES)]
    rotate = (j == 0) if mode == "kv" else True
    for s, y in enumerate(ys):
        rows = slice(s * ts, (s + 1) * ts)
        cm = jnp.where(rotate, cm_ref[rows, :], 1.0)
        sp = jnp.where(rotate, sp_ref[rows, :], 0.0)
        sm = jnp.where(rotate, sm_ref[rows, :], 0.0)
        y = _rotate_heads(y, cm, sp, sm)
        if mode == "kv":
            y = _duplicate_heads(y)
        o_ref[rows, :] = y.astype(o_ref.dtype)


def _project(xb, w, layer, *, first_block, block_stride, tn, n_tiles, shift=0, scale=None, mode="plain",
             tables=None, out_dtype=BF16):
    m, k = xb.shape
    tm = _row_tile_big(m)
    nb = tn // LANES
    tn_out = 2 * tn if mode == "kv" else tn
    if shift == 0 and block_stride == nb and first_block % nb == 0:
        nblk = 1
        w_specs = [pl.BlockSpec((None, k, tn), lambda j, i: (layer, 0, first_block // nb + j))]
    else:
        nblk = nb + (1 if shift else 0)
        w_specs = _weight_block_specs(
            k, layer, [functools.partial(lambda j, i, b: first_block + j * block_stride + b, b=b) for b in range(nblk)])
    in_specs = [pl.BlockSpec((tm, k), lambda j, i: (i, 0))] + w_specs
    args = [xb] + [w] * nblk
    if mode != "plain":
        in_specs += [pl.BlockSpec((tm, LANES), lambda j, i: (i, 0))] * 3
        args += list(tables)
    return pl.pallas_call(
        functools.partial(_proj_kernel, nblk=nblk, shift=shift, scale=scale, mode=mode),
        grid=(n_tiles, m // tm),
        in_specs=in_specs,
        out_specs=pl.BlockSpec((tm, tn_out), lambda j, i: (i, j)),
        out_shape=jax.ShapeDtypeStruct((m, n_tiles * tn_out), out_dtype),
        scratch_shapes=[pltpu.VMEM((k, tn), BF16)],
        compiler_params=_params("arbitrary", "arbitrary"),
        name="proj_" + mode,
    )(*args)


def _mm_ln_kernel(*refs, nsrc, tm, subtiles, nchunks, ck):
    x_refs = refs[:nsrc]
    w_ref, res_ref, g_ref, b_ref, of_ref, ob_ref, wb_ref = refs[nsrc:]
    step = pl.program_id(0)

    @pl.when(step < nchunks)
    def _():
        wb_ref[pl.ds(pl.multiple_of(step * ck, ck), ck), :] = w_ref[...].astype(BF16)

    @pl.when(step >= nchunks)
    def _():
        i = step - nchunks
        ts = tm // subtiles
        sums = []
        for s in range(subtiles):
            rows = slice(s * ts, (s + 1) * ts)
            acc, k0 = None, 0
            for x_ref in x_refs:
                kw = x_ref.shape[1]
                part = jnp.dot(x_ref[rows, :], wb_ref[k0:k0 + kw, :], preferred_element_type=F32)
                acc = part if acc is None else acc + part
                k0 += kw
            sums.append(acc)
        for s, acc in enumerate(sums):
            rows = slice(s * ts, (s + 1) * ts)
            t = DEEPNORM_ALPHA * res_ref[rows, :] + acc
            mu = jnp.mean(t, axis=-1, keepdims=True)
            d = t - mu
            var = jnp.mean(d * d, axis=-1, keepdims=True)
            y = d * lax.rsqrt(var + LN_EPS) * g_ref[...] + b_ref[...]
            row = lax.broadcasted_iota(jnp.int32, (ts, 1), 0) + (i * tm + s * ts)
            y = jnp.where(row >= FRONT_PAD, y, 0.0)
            of_ref[rows, :] = y
            ob_ref[rows, :] = y.astype(BF16)


def _matmul_residual_ln(xs, k_widths, w, layer, res, gamma, beta, nchunks):
    m = xs[0].shape[0]
    _, kdim, n = w.shape
    assert sum(k_widths) == kdim and m % LN_ROW_TILE == 0 and kdim % (16 * nchunks) == 0
    tm = LN_ROW_TILE
    ck = kdim // nchunks
    subtiles = LN_SUBTILES if 2 * kdim * n * 2 <= LN_WEIGHT_COPY_BUDGET else 1
    tile = lambda s: jnp.maximum(s - nchunks, 0)
    x_specs = [pl.BlockSpec((tm, kw), lambda s: (tile(s), 0)) for kw in k_widths]
    return pl.pallas_call(
        functools.partial(_mm_ln_kernel, nsrc=len(xs), tm=tm, subtiles=subtiles, nchunks=nchunks, ck=ck),
        grid=(nchunks + m // tm,),
        in_specs=x_specs + [pl.BlockSpec((None, ck, n), lambda s: (layer, jnp.minimum(s, nchunks - 1), 0)),
                            pl.BlockSpec((tm, n), lambda s: (tile(s), 0)),
                            pl.BlockSpec((1, n), lambda s: (0, 0)),
                            pl.BlockSpec((1, n), lambda s: (0, 0))],
        out_specs=[pl.BlockSpec((tm, n), lambda s: (tile(s), 0)),
                   pl.BlockSpec((tm, n), lambda s: (tile(s), 0))],
        out_shape=[jax.ShapeDtypeStruct((m, n), F32), jax.ShapeDtypeStruct((m, n), BF16)],
        scratch_shapes=[pltpu.VMEM((kdim, n), BF16)],
        compiler_params=_params("arbitrary"),
        name="proj_ln",
    )(*xs, w, res, gamma.reshape(1, n), beta.reshape(1, n))


def _ffn_up_kernel(*refs, nblk, tm, tn):
    x_ref = refs[0]
    w_refs = refs[1:1 + nblk]
    cw_g, cw_v, cb_g, cb_v, o_ref, wb_ref, cg_ref, cv_ref = refs[1 + nblk:]
    j = pl.program_id(0)
    i = pl.program_id(1)

    @pl.when(i == 0)
    def _():
        _assemble_weight(w_refs, wb_ref, 0, None)
        cg_ref[...] = jnp.zeros_like(cg_ref)
        cv_ref[...] = jnp.zeros_like(cv_ref)

    col = lax.broadcasted_iota(jnp.int32, (1, tn), 1) + j * tn
    ts = tm // FFN_SUBTILES
    ys = []
    for s in range(FFN_SUBTILES):
        y = jnp.dot(x_ref[s * ts:(s + 1) * ts, :], wb_ref[...], preferred_element_type=F32)
        ys.append((y[:, :tn], y[:, tn:]))
    carry_g, carry_v = cg_ref[...], cv_ref[...]
    for s, (yg, yv) in enumerate(ys):
        hg = _causal_conv_rows(yg, carry_g, cw_g, cb_g[...], FFN_CONV)
        hv = _causal_conv_rows(yv, carry_v, cw_v, cb_v[...], FFN_CONV)
        carry_g, carry_v = yg[ts - SUBLANES:ts], yv[ts - SUBLANES:ts]
        o_ref[s * ts:(s + 1) * ts, :] = jnp.where(col < D_FF, hg * _sigmoid(hg) * hv, 0.0).astype(o_ref.dtype)
    cg_ref[...] = carry_g
    cv_ref[...] = carry_v


def _ffn_up(xb, w_up, layer, conv_w, conv_b):
    m, k = xb.shape
    tm = _row_tile_big(m)
    assert (tm // FFN_SUBTILES) % 16 == 0, tm
    tn = FFN_TN
    nj = D_FF_PAD // tn
    nb = tn // LANES
    val0 = D_FF // LANES
    last_blk = 2 * D_FF // LANES - 1
    w_specs = [pl.BlockSpec((None, k, tn), lambda j, i: (layer, 0, j))] + _weight_block_specs(
        k, layer, [functools.partial(lambda j, i, b: jnp.minimum(val0 + j * nb + b, last_blk), b=b) for b in range(nb)])
    nblk = 1 + nb
    return pl.pallas_call(
        functools.partial(_ffn_up_kernel, nblk=nblk, tm=tm, tn=tn),
        grid=(nj, m // tm),
        in_specs=[pl.BlockSpec((tm, k), lambda j, i: (i, 0))] + w_specs
        + [pl.BlockSpec((FFN_CONV, tn), lambda j, i: (0, j)),
           pl.BlockSpec((FFN_CONV, tn), lambda j, i: (0, nj + j)),
           pl.BlockSpec((1, tn), lambda j, i: (0, j)),
           pl.BlockSpec((1, tn), lambda j, i: (0, nj + j))],
        out_specs=pl.BlockSpec((tm, tn), lambda j, i: (i, j)),
        out_shape=jax.ShapeDtypeStruct((m, D_FF_PAD), BF16),
        scratch_shapes=[pltpu.VMEM((k, 2 * tn), BF16), pltpu.VMEM((SUBLANES, tn), F32),
                        pltpu.VMEM((SUBLANES, tn), F32)],
        compiler_params=_params("arbitrary", "arbitrary"),
        name="ffn_up",
    )(xb, *([w_up] * nblk), conv_w, conv_w, conv_b, conv_b)


def _expand_heads(x, rows, g):
    low = lax.broadcasted_iota(jnp.int32, (1, LANES), 1) < SSD_HEAD_DIM
    h0 = g * SSD_HPG
    b = [jnp.broadcast_to(x[:, h0 + h:h0 + h + 1], (rows, LANES)) for h in range(SSD_HPG)]
    return jnp.concatenate([jnp.where(low, b[0], b[1]), jnp.where(low, b[2], b[3])], axis=1)


def _ssd_kernel(z_ref, x_ref, bc_ref, dt_ref, cwx_ref, cwbc_ref, cbx_ref, cbbc_ref, dtb_ref, alog_ref, dsk_ref,
                nw_ref, o_ref, st_ref, cx_ref, cbc_ref):
    c = pl.program_id(0)
    q = SSD_CHUNK
    gw = SSD_GROUP_W
    n = SSD_STATE
    groups = range(SSD_GROUPS)

    @pl.when(c == 0)
    def _():
        st_ref[...] = jnp.zeros_like(st_ref)
        cx_ref[...] = jnp.zeros_like(cx_ref)
        cbc_ref[...] = jnp.zeros_like(cbc_ref)

    row = lax.broadcasted_iota(jnp.int32, (q, 1), 0) + c * q
    lane = lax.broadcasted_iota(jnp.int32, (1, LANES), 1)
    dt = _softplus(dt_ref[...] + dtb_ref[...])
    dt = jnp.where((row >= FRONT_PAD) & (lane < SSD_HEADS), dt, 0.0)
    a = -jnp.exp(alog_ref[...])
    ri = lax.broadcasted_iota(jnp.int32, (q, q), 0)
    ci = lax.broadcasted_iota(jnp.int32, (q, q), 1)
    causal = ci <= ri
    cs = jnp.dot(causal.astype(F32), dt * a, precision=HIGHEST, preferred_element_type=F32)
    cs_t = cs.T
    cs_last = cs[q - 1:q, :]
    from_start = jnp.exp(cs)
    to_end = jnp.exp(cs_last - cs)
    total = jnp.exp(cs_last)

    def conv_silu(raw_ref, carry_ref, w_ref, b_ref, lo, width):
        raw = raw_ref[:, lo:lo + width].astype(F32)
        y = _causal_conv_rows(raw, carry_ref[:, lo:lo + width], w_ref.at[:, lo:lo + width], b_ref[:, lo:lo + width],
                              SSD_CONV)
        carry_ref[:, lo:lo + width] = raw[q - SUBLANES:q]
        return y * _sigmoid(y)

    xs = [conv_silu(x_ref, cx_ref, cwx_ref, cbx_ref, g * gw, gw) for g in groups]
    bm_t = [conv_silu(bc_ref, cbc_ref, cwbc_ref, cbbc_ref, g * n, n).T.astype(BF16) for g in groups]
    cmb = [conv_silu(bc_ref, cbc_ref, cwbc_ref, cbbc_ref, SSD_GROUPS * n + g * n, n).astype(BF16) for g in groups]
    cb = [jnp.dot(cmb[g], bm_t[g], preferred_element_type=F32) for g in groups]

    lane_w = lax.broadcasted_iota(jnp.int32, (1, gw), 1)
    xd = [xs[g] * _expand_heads(dt, q, g) for g in groups]
    ys = []
    for g in groups:
        decayed = []
        for h in range(SSD_HPG):
            hh = g * SSD_HPG + h
            seg = cs[:, hh:hh + 1] - cs_t[hh:hh + 1, :]
            decayed.append((cb[g] * jnp.exp(jnp.where(causal, seg, -jnp.inf))).astype(BF16))
        xdb = xd[g].astype(BF16)
        x_heads = [jnp.where((lane_w >= h * SSD_HEAD_DIM) & (lane_w < (h + 1) * SSD_HEAD_DIM), xdb,
                             jnp.zeros_like(xdb)) for h in range(SSD_HPG)]
        ys.append(jnp.dot(jnp.concatenate(decayed, axis=1), jnp.concatenate(x_heads, axis=0),
                          preferred_element_type=F32))
    for g in groups:
        st = st_ref[g]
        ys[g] = ys[g] + (jnp.dot(cmb[g], st.astype(BF16), preferred_element_type=F32)
                         * _expand_heads(from_start, q, g))
        st_ref[g] = (st * _expand_heads(total, 1, g)
                     + jnp.dot(bm_t[g], (xd[g] * _expand_heads(to_end, q, g)).astype(BF16),
                               preferred_element_type=F32))
    for g in groups:
        cols = slice(g * gw, (g + 1) * gw)
        y = ys[g] + xs[g] * dsk_ref[:, cols]
        z = z_ref[:, cols].astype(F32)
        yg = y * (z * _sigmoid(z))
        yn = yg * lax.rsqrt(jnp.mean(yg * yg, axis=-1, keepdims=True) + RMS_EPS) * nw_ref[:, cols]
        o_ref[:, cols] = yn.astype(o_ref.dtype)


def _ssd_mixer(u_zx, u_small, conv_w, conv_b, dt_bias, a_log, d_skip, norm_w):
    rows = u_zx.shape[0]
    q = SSD_CHUNK
    d = SSD_GROUPS * SSD_GROUP_W
    pad_l = lambda v: jnp.pad(v.reshape(1, SSD_HEADS), ((0, 0), (0, LANES - SSD_HEADS)))
    whole = lambda r, w, j: pl.BlockSpec((r, w), lambda c: (0, j))
    return pl.pallas_call(
        _ssd_kernel,
        grid=(rows // q,),
        in_specs=[pl.BlockSpec((q, d), lambda c: (c, 0)),
                  pl.BlockSpec((q, d), lambda c: (c, 1)),
                  pl.BlockSpec((q, d), lambda c: (c, 2)),
                  pl.BlockSpec((q, LANES), lambda c: (c, 0)),
                  whole(SSD_CONV, d, 0), whole(SSD_CONV, d, 1), whole(1, d, 0), whole(1, d, 1),
                  whole(1, LANES, 0), whole(1, LANES, 0), whole(1, d, 0), whole(1, d, 0)],
        out_specs=pl.BlockSpec((q, d), lambda c: (c, 0)),
        out_shape=jax.ShapeDtypeStruct((rows, d), BF16),
        scratch_shapes=[pltpu.VMEM((SSD_GROUPS, SSD_STATE, SSD_GROUP_W), F32), pltpu.VMEM((SUBLANES, d), F32),
                        pltpu.VMEM((SUBLANES, d), F32)],
        compiler_params=_params("arbitrary"),
        name="ssd",
    )(u_zx, u_zx, u_zx, u_small, conv_w, conv_w, conv_b.reshape(1, -1), conv_b.reshape(1, -1),
      pad_l(dt_bias), pad_l(a_log), jnp.repeat(d_skip, SSD_HEAD_DIM).reshape(1, d), norm_w.reshape(1, d))


def _gla_scores_blocked(q, k, gc):
    qc = q.shape[0]
    sub = GLA_SUB
    lane_j = lax.broadcasted_iota(jnp.int32, (sub, qc), 1)
    row_i = lax.broadcasted_iota(jnp.int32, (sub, 1), 0)
    a_rows = []
    for blk in range(qc // sub):
        lo = blk * sub
        q_b = q[lo:lo + sub]
        g_b = gc[lo:lo + sub]
        a_blk = jnp.zeros((sub, qc), F32)
        for j in range(sub):
            k_j = k[lo + j:lo + j + 1, :]
            g_j = gc[lo + j:lo + j + 1, :]
            s_j = jnp.sum(q_b * k_j * jnp.exp(jnp.minimum(g_b - g_j, 0.0)), axis=1, keepdims=True)
            a_blk = jnp.where(lane_j == lo + j, jnp.where(row_i >= j, s_j, 0.0), a_blk)
        if blk > 0:
            g_ref0 = gc[lo:lo + 1, :]
            q_t = (q_b * jnp.exp(g_b - g_ref0)).astype(BF16)
            k_t = (k * jnp.exp(jnp.minimum(g_ref0 - gc, 0.0))).astype(BF16)
            off = lax.dot_general(q_t, k_t, (((1,), (1,)), ((), ())), preferred_element_type=F32)
            a_blk = jnp.where(lane_j < lo, off, a_blk)
        a_rows.append(a_blk)
    return jnp.concatenate(a_rows, axis=0)


def _gla_kernel(q_ref, k_ref, v_ref, r_ref, glr_ref, w2_ref, gb_ref, nw_ref, o_ref, st_ref, a_ref):
    c = pl.program_id(0)
    qc = GLA_CHUNK
    dk, dv = GLA_HEAD_DK, GLA_HEAD_DV
    heads = range(GLA_HEADS)
    nt = (((1,), (1,)), ((), ()))

    @pl.when(c == 0)
    def _():
        st_ref[...] = jnp.zeros_like(st_ref)

    pre = jnp.dot(glr_ref[...], w2_ref[...], precision=HIGHEST, preferred_element_type=F32) + gb_ref[...]
    g = _log_sigmoid(pre) * (1.0 / GLA_GATE_TAU)
    row = lax.broadcasted_iota(jnp.int32, (qc, 1), 0) + c * qc
    g = jnp.where(row >= FRONT_PAD, g, 0.0)
    ri = lax.broadcasted_iota(jnp.int32, (qc, qc), 0)
    ci = lax.broadcasted_iota(jnp.int32, (qc, qc), 1)
    causal = ci <= ri
    gc = jnp.dot(causal.astype(F32), g, precision=HIGHEST, preferred_element_type=F32)
    g_last = gc[qc - 1:qc, :]

    q = q_ref[...].astype(F32) * (GLA_HEAD_DK ** -0.5)
    k = k_ref[...].astype(F32)
    q_dec = (q * jnp.exp(gc)).astype(BF16)
    k_end = (k * jnp.exp(g_last - gc)).astype(BF16)
    safe = jnp.max(-g_last) <= GLA_SAFE_DECAY

    @pl.when(safe)
    def _():
        k_inv = (k * jnp.exp(-gc)).astype(BF16)
        for h in heads:
            s = lax.dot_general(q_dec[:, h * dk:(h + 1) * dk], k_inv[:, h * dk:(h + 1) * dk], nt,
                                preferred_element_type=F32)
            a_ref[h] = jnp.where(causal, s, 0.0)

    @pl.when(jnp.logical_not(safe))
    def _():
        for h in heads:
            a_ref[h] = _gla_scores_blocked(q[:, h * dk:(h + 1) * dk], k[:, h * dk:(h + 1) * dk],
                                           gc[:, h * dk:(h + 1) * dk])

    vb = v_ref[...]
    decay = jnp.exp(g_last)
    outs = []
    for h in heads:
        st = st_ref[h]
        v_h = vb[:, h * dv:(h + 1) * dv]
        o = jnp.dot(a_ref[h].astype(BF16), v_h, preferred_element_type=F32)
        o = o + lax.dot_general(q_dec[:, h * dk:(h + 1) * dk], st.astype(BF16), nt, preferred_element_type=F32)
        st_ref[h] = st * decay[:, h * dk:(h + 1) * dk] + lax.dot_general(
            v_h, k_end[:, h * dk:(h + 1) * dk], (((0,), (0,)), ((), ())), preferred_element_type=F32)
        outs.append(o)
    for h in heads:
        o = outs[h]
        on = o * lax.rsqrt(jnp.mean(o * o, axis=-1, keepdims=True) + RMS_EPS) * nw_ref[:, h * dv:(h + 1) * dv]
        r = r_ref[:, h * dv:(h + 1) * dv].astype(F32)
        o_ref[:, h * dv:(h + 1) * dv] = (on * (r * _sigmoid(r))).astype(o_ref.dtype)


def _gla_mixer(u_qkv, u_r, u_small, gate_w2, gate_b, norm_w):
    rows = u_qkv.shape[0]
    qc = GLA_CHUNK
    dk, dv = GLA_HEAD_DK, GLA_HEAD_DV
    wk, wv = GLA_HEADS * dk, GLA_HEADS * dv
    glr_lane = HYB_GLR_COL % LANES
    w2 = jnp.pad(gate_w2, ((glr_lane, LANES - GLA_GATE_RANK - glr_lane), (0, 0)))
    return pl.pallas_call(
        _gla_kernel,
        grid=(rows // qc,),
        in_specs=[pl.BlockSpec((qc, wk), lambda c: (c, 0)),
                  pl.BlockSpec((qc, wk), lambda c: (c, 1)),
                  pl.BlockSpec((qc, wv), lambda c: (c, 1)),
                  pl.BlockSpec((qc, wv), lambda c: (c, 0)),
                  pl.BlockSpec((qc, LANES), lambda c: (c, 1)),
                  pl.BlockSpec((LANES, wk), lambda c: (0, 0)),
                  pl.BlockSpec((1, wk), lambda c: (0, 0)),
                  pl.BlockSpec((1, wv), lambda c: (0, 0))],
        out_specs=pl.BlockSpec((qc, wv), lambda c: (c, 0)),
        out_shape=jax.ShapeDtypeStruct((rows, wv), BF16),
        scratch_shapes=[pltpu.VMEM((GLA_HEADS, dv, dk), F32), pltpu.VMEM((GLA_HEADS, qc, qc), F32)],
        compiler_params=_params("arbitrary"),
        name="gla",
    )(u_qkv, u_qkv, u_qkv, u_r, u_small, w2, gate_b.reshape(1, -1), norm_w.reshape(1, -1))


def _rope_kernel(freq_ref, cm_ref, sp_ref, sm_ref, *, tm):
    i = pl.program_id(0)
    row = lax.broadcasted_iota(jnp.int32, (tm, LANES), 0) + i * tm
    lane = lax.broadcasted_iota(jnp.int32, (tm, LANES), 1) % SWA_HEAD_DIM
    ang = (row - FRONT_PAD).astype(F32) * freq_ref[...]
    cos = jnp.cos(ang)
    sin = jnp.sin(ang)
    half = ROPE_DIM // 2
    cm_ref[...] = cos
    sp_ref[...] = jnp.where(lane < half, -sin, 0.0)
    sm_ref[...] = jnp.where((lane >= half) & (lane < ROPE_DIM), sin, 0.0)


def _rope_tables(rows):
    half = ROPE_DIM // 2
    inv_freq = ROPE_THETA ** (-jnp.arange(half, dtype=F32) / half)
    per_head = jnp.concatenate([inv_freq, inv_freq, jnp.zeros((SWA_HEAD_DIM - ROPE_DIM,), F32)])
    freq = jnp.tile(per_head, LANES // SWA_HEAD_DIM).reshape(1, LANES)
    tm = _row_tile(rows)
    shp = jax.ShapeDtypeStruct((rows, LANES), F32)
    spec = pl.BlockSpec((tm, LANES), lambda i: (i, 0))
    return pl.pallas_call(
        functools.partial(_rope_kernel, tm=tm),
        grid=(rows // tm,),
        in_specs=[pl.BlockSpec((1, LANES), lambda i: (0, 0))],
        out_specs=[spec, spec, spec],
        out_shape=[shp, shp, shp],
        compiler_params=_params("arbitrary"),
        name="rope_tables",
    )(freq)


def _swa_kernel(sink_ref, q_ref, prev_ref, cur_ref, meta_ref, o_ref):
    n = pl.program_id(0)
    w = SWA_WINDOW
    meta_lo = FRONT_PAD
    kvw = SWA_KV_HEADS * LANES
    nt = (((1,), (1,)), ((), ()))
    lane = lax.broadcasted_iota(jnp.int32, (1, LANES), 1)
    low = lane < SWA_HEAD_DIM
    i = lax.broadcasted_iota(jnp.int32, (w, 1), 0)
    on_cur = lane <= i
    valid_band = (on_cur & ((n >= 1) | (lane >= meta_lo))) | (jnp.logical_not(on_cur) & (n >= 2))
    valid_meta = (lane >= meta_lo) & (n >= 1)
    zero = jnp.zeros((w, LANES), q_ref.dtype)
    kv_heads = range(SWA_KV_HEADS)

    def scores(g):
        qa = q_ref[:, 2 * g * LANES:(2 * g + 1) * LANES]
        qb = q_ref[:, (2 * g + 1) * LANES:(2 * g + 2) * LANES]
        qs = jnp.concatenate([jnp.where(low, qa, zero), jnp.where(low, zero, qa),
                              jnp.where(low, qb, zero), jnp.where(low, zero, qb)], axis=0)
        kcols = slice(g * LANES, (g + 1) * LANES)
        kk = jnp.concatenate([prev_ref[:, kcols], cur_ref[:, kcols], meta_ref[:, kcols]], axis=0)
        return lax.dot_general(qs, kk, nt, preferred_element_type=F32)

    s_next = scores(0)
    for g in kv_heads:
        s_all = s_next
        if g + 1 < SWA_KV_HEADS:
            s_next = scores(g + 1)
        probs, denoms = [], []
        for h in range(SWA_GROUP):
            s = s_all[h * w:(h + 1) * w]
            s_band = jnp.where(valid_band, jnp.where(on_cur, s[:, w:2 * w], s[:, :w]), -jnp.inf)
            s_meta = jnp.where(valid_meta, s[:, 2 * w:], -jnp.inf)
            sink = sink_ref[g * SWA_GROUP + h]
            m = jnp.maximum(jnp.max(jnp.maximum(s_band, s_meta), axis=-1, keepdims=True), sink)
            p_band = jnp.exp(s_band - m)
            p_meta = jnp.exp(s_meta - m)
            denoms.append(jnp.sum(p_band + p_meta, axis=-1, keepdims=True) + jnp.exp(sink - m))
            probs.append(jnp.concatenate([jnp.where(on_cur, 0.0, p_band), jnp.where(on_cur, p_band, 0.0), p_meta],
                                         axis=1).astype(BF16))
        vcols = slice(kvw + g * LANES, kvw + (g + 1) * LANES)
        vv = jnp.concatenate([prev_ref[:, vcols], cur_ref[:, vcols], meta_ref[:, vcols]], axis=0)
        o = jnp.dot(jnp.concatenate(probs, axis=0), vv, preferred_element_type=F32)
        o = [o[h * w:(h + 1) * w] / denoms[h] for h in range(SWA_GROUP)]
        oa = jnp.where(low, o[0], o[1])
        ob = jnp.where(low, o[2], o[3])
        o_ref[:, 2 * g * LANES:(2 * g + 2) * LANES] = jnp.concatenate([oa, ob], axis=1).astype(o_ref.dtype)


def _swa_attention(q, kv, sinks):
    rows, qw = q.shape
    w = SWA_WINDOW
    kvw = kv.shape[1]
    return pl.pallas_call(
        _swa_kernel,
        grid=(rows // w,),
        in_specs=[pl.BlockSpec(memory_space=pltpu.SMEM),
                  pl.BlockSpec((w, qw), lambda n: (n, 0)),
                  pl.BlockSpec((w, kvw), lambda n: (jnp.maximum(n - 1, 0), 0)),
                  pl.BlockSpec((w, kvw), lambda n: (n, 0)),
                  pl.BlockSpec((w, kvw), lambda n: (0, 0))],
        out_specs=pl.BlockSpec((w, qw), lambda n: (n, 0)),
        out_shape=jax.ShapeDtypeStruct((rows, qw), BF16),
        compiler_params=_params("arbitrary"),
        name="swa",
    )(sinks, q, kv, kv, kv)


def _embed_kernel(x_ref, meta_ref, hf_ref, hb_ref, *, tm):
    i = pl.program_id(0)
    head = FRONT_PAD + N_META

    @pl.when(i == 0)
    def _():
        top = jnp.concatenate([jnp.zeros((FRONT_PAD, D_MODEL), F32), meta_ref[...]], axis=0)
        hf_ref[0:head, :] = top
        hb_ref[0:head, :] = top.astype(BF16)
        body = x_ref[0:tm - head, :]
        hf_ref[head:tm, :] = body
        hb_ref[head:tm, :] = body.astype(BF16)

    @pl.when(i != 0)
    def _():
        hf_ref[...] = x_ref[...]
        hb_ref[...] = x_ref[...].astype(BF16)


def _embed(x, meta):
    seq, d = x.shape
    head = FRONT_PAD + N_META
    rows = head + seq
    tm = _row_tile(rows)
    out = pl.BlockSpec((tm, d), lambda i: (i, 0))
    x_rows = lambda i: pl.multiple_of(jnp.maximum(i * tm - head, 0), LANES)
    return pl.pallas_call(
        functools.partial(_embed_kernel, tm=tm),
        grid=(rows // tm,),
        in_specs=[pl.BlockSpec((pl.Element(tm), pl.Element(d)), lambda i: (x_rows(i), 0)),
                  pl.BlockSpec((N_META, d), lambda i: (0, 0))],
        out_specs=[out, out],
        out_shape=[jax.ShapeDtypeStruct((rows, d), F32), jax.ShapeDtypeStruct((rows, d), BF16)],
        compiler_params=_params("arbitrary"),
        name="embed",
    )(x, meta)


def _pad_halves(t):
    pad = lambda a: jnp.pad(a, ((0, 0), (0, D_FF_PAD - D_FF)))
    return jnp.concatenate([pad(t[:, :D_FF]), pad(t[:, D_FF:])], axis=1)


def _trunk(x, meta_tokens, hyb_w_in, hyb_conv_w, hyb_conv_b, ssd_dt_bias, ssd_a_log, ssd_d, ssd_norm_w,
           gla_gate_w2, gla_gate_b, gla_norm_w, hyb_w_out, swa_w_qkv, swa_sinks, swa_w_out,
           ffn_w_up, ffn_conv_w, ffn_conv_b, ffn_w_down, ln_mix_g, ln_mix_b, ln_ffn_g, ln_ffn_b):
    seq = x.shape[0]
    rows = FRONT_PAD + N_META + seq
    h, hb = _embed(x, meta_tokens.astype(F32))
    tables = _rope_tables(rows)
    hyb_w_in = jnp.pad(hyb_w_in, ((0, 0), (0, 0), (0, -hyb_w_in.shape[2] % LANES))).astype(BF16)
    for layer in range(DEPTH):
        j = layer // 2
        if layer % 2 == 0:
            u_zx = _project(hb, hyb_w_in, j, first_block=0, block_stride=8, tn=1024, n_tiles=6)
            u_qkv = _project(hb, hyb_w_in, j, first_block=HYB_QKV_COL // LANES, block_stride=8, tn=1024, n_tiles=4,
                             shift=HYB_QKV_COL % LANES)
            u_r = _project(hb, hyb_w_in, j, first_block=HYB_R_COL // LANES, block_stride=8, tn=1024, n_tiles=2,
                           shift=HYB_R_COL % LANES)
            u_small = _project(hb, hyb_w_in, j, first_block=HYB_DT_COL // LANES,
                               block_stride=(HYB_GLR_COL - HYB_DT_COL) // LANES, tn=LANES, n_tiles=2, out_dtype=F32)
            y_ssd = _ssd_mixer(u_zx, u_small, hyb_conv_w[j], hyb_conv_b[j], ssd_dt_bias[j], ssd_a_log[j],
                               ssd_d[j], ssd_norm_w[j])
            y_gla = _gla_mixer(u_qkv, u_r, u_small, gla_gate_w2[j], gla_gate_b[j], gla_norm_w[j])
            h, hb = _matmul_residual_ln([y_ssd, y_gla], [y_ssd.shape[1], y_gla.shape[1]], hyb_w_out, j, h,
                                        ln_mix_g[layer], ln_mix_b[layer], nchunks=8)
        else:
            q = _project(hb, swa_w_qkv, j, first_block=0, block_stride=4, tn=512, n_tiles=4,
                         scale=SWA_HEAD_DIM ** -0.5, mode="rope", tables=tables)
            kv = _project(hb, swa_w_qkv, j, first_block=SWA_Q_HEADS * SWA_HEAD_DIM // LANES, block_stride=4, tn=512,
                          n_tiles=2, mode="kv", tables=tables)
            attn = _swa_attention(q, kv, swa_sinks[j])
            h, hb = _matmul_residual_ln([attn], [attn.shape[1]], swa_w_out, j, h, ln_mix_g[layer], ln_mix_b[layer],
                                        nchunks=4)
        act = _ffn_up(hb, ffn_w_up, layer, _pad_halves(ffn_conv_w[layer]),
                      _pad_halves(ffn_conv_b[layer].reshape(1, -1)))
        h, hb = _matmul_residual_ln([act], [D_FF], ffn_w_down, layer, h, ln_ffn_g[layer], ln_ffn_b[layer], nchunks=8)
    return h[FRONT_PAD + N_META:]


def kernel(x, meta_tokens, hyb_w_in, hyb_conv_w, hyb_conv_b, ssd_dt_bias, ssd_a_log, ssd_d, ssd_norm_w,
           gla_gate_w2, gla_gate_b, gla_norm_w, hyb_w_out, swa_w_qkv, swa_sinks, swa_w_out,
           ffn_w_up, ffn_conv_w, ffn_conv_b, ffn_w_down, ln_mix_g, ln_mix_b, ln_ffn_g, ln_ffn_b):
    params = (meta_tokens, hyb_w_in, hyb_conv_w, hyb_conv_b, ssd_dt_bias, ssd_a_log, ssd_d, ssd_norm_w,
              gla_gate_w2, gla_gate_b, gla_norm_w, hyb_w_out, swa_w_qkv, swa_sinks, swa_w_out,
              ffn_w_up, ffn_conv_w, ffn_conv_b, ffn_w_down, ln_mix_g, ln_mix_b, ln_ffn_g, ln_ffn_b)
    return jnp.stack([_trunk(x[b], *params) for b in range(x.shape[0])], axis=0)
```

```python
import functools

import jax
import jax.numpy as jnp
from jax import lax
from jax.experimental import pallas as pl
from jax.experimental.pallas import tpu as pltpu

F32 = jnp.float32
BF16 = jnp.bfloat16
HIGHEST = lax.Precision.HIGHEST

D_MODEL = 2048
DEPTH = 4
N_META = 16
LN_EPS = 1e-5
RMS_EPS = 1e-6
DEEPNORM_ALPHA = (2.0 * DEPTH) ** 0.25

SSD_HEAD_DIM = 64
SSD_HEADS = 32
SSD_GROUPS = 8
SSD_HPG = 4
SSD_STATE = 128
SSD_CONV = 4
SSD_CHUNK = 128
SSD_GROUP_W = SSD_HPG * SSD_HEAD_DIM

GLA_HEADS = 4
GLA_HEAD_DK = 256
GLA_HEAD_DV = 512
GLA_GATE_RANK = 16
GLA_GATE_TAU = 16.0
GLA_CHUNK = 128
GLA_SUB = 16
GLA_SAFE_DECAY = 80.0

SWA_HEAD_DIM = 64
SWA_Q_HEADS = 32
SWA_KV_HEADS = 8
SWA_GROUP = 4
SWA_WINDOW = 128
ROPE_THETA = 500000.0
ROPE_DIM = 16

D_FF = 5504
FFN_CONV = 3

LANES = 128
SUBLANES = 8
FRONT_PAD = SSD_CHUNK - N_META
D_FF_PAD = 5632
FFN_TN = 512
FFN_SUBTILES = 4
PROJ_SUBTILES = 4
LN_ROW_TILE = 320
LN_SUBTILES = 2
LN_WEIGHT_COPY_BUDGET = 20 * 1024 * 1024
VMEM_LIMIT = 56 * 1024 * 1024

HYB_DT_COL = D_MODEL + (D_MODEL + 2 * SSD_GROUPS * SSD_STATE)
HYB_QKV_COL = HYB_DT_COL + SSD_HEADS
HYB_GLR_COL = HYB_QKV_COL + GLA_HEADS * (2 * GLA_HEAD_DK + GLA_HEAD_DV)
HYB_R_COL = HYB_GLR_COL + GLA_GATE_RANK


def _row_tile(rows):
    for t in (640, 512, 384, 256, 128):
        if rows % t == 0:
            return t
    raise ValueError(f"row count {rows} is not a multiple of 128")


def _row_tile_big(rows):
    return 1664 if rows % 1664 == 0 else _row_tile(rows)


def _params(*sem):
    return pltpu.CompilerParams(dimension_semantics=sem, vmem_limit_bytes=VMEM_LIMIT)


def _sigmoid(x):
    return 1.0 / (1.0 + jnp.exp(-x))


def _softplus(x):
    return jnp.maximum(x, 0.0) + jnp.log(1.0 + jnp.exp(-jnp.abs(x)))


def _log_sigmoid(x):
    return jnp.minimum(x, 0.0) - jnp.log(1.0 + jnp.exp(-jnp.abs(x)))


def _causal_conv_rows(y, carry, w_ref, b_row, taps):
    top = jnp.concatenate([carry, y[0:SUBLANES]], axis=0)
    w_last = w_ref[taps - 1:taps, :]
    acc = b_row + w_last * y
    acc_top = b_row + w_last * y[0:SUBLANES]
    for s in range(1, taps):
        wk = w_ref[taps - 1 - s:taps - s, :]
        acc = acc + wk * pltpu.roll(y, s, 0)
        acc_top = acc_top + wk * pltpu.roll(top, s, 0)[SUBLANES:2 * SUBLANES]
    return jnp.concatenate([acc_top, acc[SUBLANES:]], axis=0)


def _assemble_weight(w_refs, wb_ref, shift, scale):
    k, tn = wb_ref.shape
    chunk = 256
    for r in range(0, k, chunk):
        w = jnp.concatenate([wr[r:r + chunk, :] for wr in w_refs], axis=1)
        if shift:
            w = pltpu.roll(w.astype(F32), w.shape[1] - shift, 1)
        w = w[:, :tn]
        if scale is not None:
            w = w * scale
        wb_ref[r:r + chunk, :] = w.astype(BF16)


def _weight_block_specs(k, layer, block_fns):
    return [pl.BlockSpec((None, k, LANES), functools.partial(lambda *ids, fn: (layer, 0, fn(*ids)), fn=fn))
            for fn in block_fns]


def _rotate_heads(y, cm, sp, sm):
    half = ROPE_DIM // 2
    out = []
    for c in range(y.shape[1] // LANES):
        yc = y[:, c * LANES:(c + 1) * LANES]
        out.append(yc * cm + pltpu.roll(yc, LANES - half, 1) * sp + pltpu.roll(yc, half, 1) * sm)
    return jnp.concatenate(out, axis=1)


def _duplicate_heads(y):
    low = lax.broadcasted_iota(jnp.int32, (1, LANES), 1) < SWA_HEAD_DIM
    out = []
    for c in range(y.shape[1] // LANES):
        yc = y[:, c * LANES:(c + 1) * LANES]
        rolled = pltpu.roll(yc, SWA_HEAD_DIM, 1)
        out.append(jnp.where(low, yc, rolled))
        out.append(jnp.where(low, rolled, yc))
    return jnp.concatenate(out, axis=1)


def _proj_kernel(*refs, nblk, shift, scale, mode):
    x_ref = refs[0]
    w_refs = refs[1:1 + nblk]
    rest = refs[1 + nblk:]
    if mode != "plain":
        cm_ref, sp_ref, sm_ref = rest[:3]
        rest = rest[3:]
    o_ref, wb_ref = rest
    j = pl.program_id(0)
    i = pl.program_id(1)

    @pl.when(i == 0)
    def _():
        _assemble_weight(w_refs, wb_ref, shift, scale)

    if mode == "plain":
        o_ref[...] = jnp.dot(x_ref[...], wb_ref[...], preferred_element_type=F32).astype(o_ref.dtype)
        return
    tm = x_ref.shape[0]
    ts = tm // PROJ_SUBTILES
    ys = [jnp.dot(x_ref[s * ts:(s + 1) * ts, :], wb_ref[...], preferred_element_type=F32)
          for s in range(PROJ_SUBTILES)]
    rotate = (j == 0) if mode == "kv" else True
    for s, y in enumerate(ys):
        rows = slice(s * ts, (s + 1) * ts)
        cm = jnp.where(rotate, cm_ref[rows, :], 1.0)
        sp = jnp.where(rotate, sp_ref[rows, :], 0.0)
        sm = jnp.where(rotate, sm_ref[rows, :], 0.0)
        y = _rotate_heads(y, cm, sp, sm)
        if mode == "kv":
            y = _duplicate_heads(y)
        o_ref[rows, :] = y.astype(o_ref.dtype)


def _project(xb, w, layer, *, first_block, block_stride, tn, n_tiles, shift=0, scale=None, mode="plain",
             tables=None, out_dtype=BF16):
    m, k = xb.shape
    tm = _row_tile_big(m)
    nb = tn // LANES
    tn_out = 2 * tn if mode == "kv" else tn
    if shift == 0 and block_stride == nb and first_block % nb == 0:
        nblk = 1
        w_specs = [pl.BlockSpec((None, k, tn), lambda j, i: (layer, 0, first_block // nb + j))]
    else:
        nblk = nb + (1 if shift else 0)
        w_specs = _weight_block_specs(
            k, layer, [functools.partial(lambda j, i, b: first_block + j * block_stride + b, b=b) for b in range(nblk)])
    in_specs = [pl.BlockSpec((tm, k), lambda j, i: (i, 0))] + w_specs
    args = [xb] + [w] * nblk
    if mode != "plain":
        in_specs += [pl.BlockSpec((tm, LANES), lambda j, i: (i, 0))] * 3
        args += list(tables)
    return pl.pallas_call(
        functools.partial(_proj_kernel, nblk=nblk, shift=shift, scale=scale, mode=mode),
        grid=(n_tiles, m // tm),
        in_specs=in_specs,
        out_specs=pl.BlockSpec((tm, tn_out), lambda j, i: (i, j)),
        out_shape=jax.ShapeDtypeStruct((m, n_tiles * tn_out), out_dtype),
        scratch_shapes=[pltpu.VMEM((k, tn), BF16)],
        compiler_params=_params("arbitrary", "arbitrary"),
        name="proj_" + mode,
    )(*args)


def _mm_ln_kernel(*refs, nsrc, tm, subtiles, nchunks, ck):
    x_refs = refs[:nsrc]
    w_ref, res_ref, g_ref, b_ref, of_ref, ob_ref, wb_ref = refs[nsrc:]
    step = pl.program_id(0)

    @pl.when(step < nchunks)
    def _():
        wb_ref[pl.ds(pl.multiple_of(step * ck, ck), ck), :] = w_ref[...].astype(BF16)

    @pl.when(step >= nchunks)
    def _():
        i = step - nchunks
        ts = tm // subtiles
        sums = []
        for s in range(subtiles):
            rows = slice(s * ts, (s + 1) * ts)
            acc, k0 = None, 0
            for x_ref in x_refs:
                kw = x_ref.shape[1]
                part = jnp.dot(x_ref[rows, :], wb_ref[k0:k0 + kw, :], preferred_element_type=F32)
                acc = part if acc is None else acc + part
                k0 += kw
            sums.append(acc)
        for s, acc in enumerate(sums):
            rows = slice(s * ts, (s + 1) * ts)
            t = DEEPNORM_ALPHA * res_ref[rows, :] + acc
            mu = jnp.mean(t, axis=-1, keepdims=True)
            d = t - mu
            var = jnp.mean(d * d, axis=-1, keepdims=True)
            y = d * lax.rsqrt(var + LN_EPS) * g_ref[...] + b_ref[...]
            row = lax.broadcasted_iota(jnp.int32, (ts, 1), 0) + (i * tm + s * ts)
            y = jnp.where(row >= FRONT_PAD, y, 0.0)
            of_ref[rows, :] = y
            ob_ref[rows, :] = y.astype(BF16)


def _matmul_residual_ln(xs, k_widths, w, layer, res, gamma, beta, nchunks):
    m = xs[0].shape[0]
    _, kdim, n = w.shape
    assert sum(k_widths) == kdim and m % LN_ROW_TILE == 0 and kdim % (16 * nchunks) == 0
    tm = LN_ROW_TILE
    ck = kdim // nchunks
    subtiles = LN_SUBTILES if 2 * kdim * n * 2 <= LN_WEIGHT_COPY_BUDGET else 1
    tile = lambda s: jnp.maximum(s - nchunks, 0)
    x_specs = [pl.BlockSpec((tm, kw), lambda s: (tile(s), 0)) for kw in k_widths]
    return pl.pallas_call(
        functools.partial(_mm_ln_kernel, nsrc=len(xs), tm=tm, subtiles=subtiles, nchunks=nchunks, ck=ck),
        grid=(nchunks + m // tm,),
        in_specs=x_specs + [pl.BlockSpec((None, ck, n), lambda s: (layer, jnp.minimum(s, nchunks - 1), 0)),
                            pl.BlockSpec((tm, n), lambda s: (tile(s), 0)),
                            pl.BlockSpec((1, n), lambda s: (0, 0)),
                            pl.BlockSpec((1, n), lambda s: (0, 0))],
        out_specs=[pl.BlockSpec((tm, n), lambda s: (tile(s), 0)),
                   pl.BlockSpec((tm, n), lambda s: (tile(s), 0))],
        out_shape=[jax.ShapeDtypeStruct((m, n), F32), jax.ShapeDtypeStruct((m, n), BF16)],
        scratch_shapes=[pltpu.VMEM((kdim, n), BF16)],
        compiler_params=_params("arbitrary"),
        name="proj_ln",
    )(*xs, w, res, gamma.reshape(1, n), beta.reshape(1, n))


def _ffn_up_kernel(*refs, nblk, tm, tn):
    x_ref = refs[0]
    w_refs = refs[1:1 + nblk]
    cw_g, cw_v, cb_g, cb_v, o_ref, wb_ref, cg_ref, cv_ref = refs[1 + nblk:]
    j = pl.program_id(0)
    i = pl.program_id(1)

    @pl.when(i == 0)
    def _():
        _assemble_weight(w_refs, wb_ref, 0, None)
        cg_ref[...] = jnp.zeros_like(cg_ref)
        cv_ref[...] = jnp.zeros_like(cv_ref)

    col = lax.broadcasted_iota(jnp.int32, (1, tn), 1) + j * tn
    ts = tm // FFN_SUBTILES
    ys = []
    for s in range(FFN_SUBTILES):
        y = jnp.dot(x_ref[s * ts:(s + 1) * ts, :], wb_ref[...], preferred_element_type=F32)
        ys.append((y[:, :tn], y[:, tn:]))
    carry_g, carry_v = cg_ref[...], cv_ref[...]
    for s, (yg, yv) in enumerate(ys):
        hg = _causal_conv_rows(yg, carry_g, cw_g, cb_g[...], FFN_CONV)
        hv = _causal_conv_rows(yv, carry_v, cw_v, cb_v[...], FFN_CONV)
        carry_g, carry_v = yg[ts - SUBLANES:ts], yv[ts - SUBLANES:ts]
        o_ref[s * ts:(s + 1) * ts, :] = jnp.where(col < D_FF, hg * _sigmoid(hg) * hv, 0.0).astype(o_ref.dtype)
    cg_ref[...] = carry_g
    cv_ref[...] = carry_v


def _ffn_up(xb, w_up, layer, conv_w, conv_b):
    m, k = xb.shape
    tm = _row_tile_big(m)
    assert (tm // FFN_SUBTILES) % 16 == 0, tm
    tn = FFN_TN
    nj = D_FF_PAD // tn
    nb = tn // LANES
    val0 = D_FF // LANES
    last_blk = 2 * D_FF // LANES - 1
    w_specs = [pl.BlockSpec((None, k, tn), lambda j, i: (layer, 0, j))] + _weight_block_specs(
        k, layer, [functools.partial(lambda j, i, b: jnp.minimum(val0 + j * nb + b, last_blk), b=b) for b in range(nb)])
    nblk = 1 + nb
    return pl.pallas_call(
        functools.partial(_ffn_up_kernel, nblk=nblk, tm=tm, tn=tn),
        grid=(nj, m // tm),
        in_specs=[pl.BlockSpec((tm, k), lambda j, i: (i, 0))] + w_specs
        + [pl.BlockSpec((FFN_CONV, tn), lambda j, i: (0, j)),
           pl.BlockSpec((FFN_CONV, tn), lambda j, i: (0, nj + j)),
           pl.BlockSpec((1, tn), lambda j, i: (0, j)),
           pl.BlockSpec((1, tn), lambda j, i: (0, nj + j))],
        out_specs=pl.BlockSpec((tm, tn), lambda j, i: (i, j)),
        out_shape=jax.ShapeDtypeStruct((m, D_FF_PAD), BF16),
        scratch_shapes=[pltpu.VMEM((k, 2 * tn), BF16), pltpu.VMEM((SUBLANES, tn), F32),
                        pltpu.VMEM((SUBLANES, tn), F32)],
        compiler_params=_params("arbitrary", "arbitrary"),
        name="ffn_up",
    )(xb, *([w_up] * nblk), conv_w, conv_w, conv_b, conv_b)


def _expand_heads(x, rows, g):
    low = lax.broadcasted_iota(jnp.int32, (1, LANES), 1) < SSD_HEAD_DIM
    h0 = g * SSD_HPG
    b = [jnp.broadcast_to(x[:, h0 + h:h0 + h + 1], (rows, LANES)) for h in range(SSD_HPG)]
    return jnp.concatenate([jnp.where(low, b[0], b[1]), jnp.where(low, b[2], b[3])], axis=1)


def _ssd_kernel(z_ref, x_ref, bc_ref, dt_ref, cwx_ref, cwbc_ref, cbx_ref, cbbc_ref, dtb_ref, alog_ref, dsk_ref,
                nw_ref, o_ref, st_ref, cx_ref, cbc_ref):
    c = pl.program_id(0)
    q = SSD_CHUNK
    gw = SSD_GROUP_W
    n = SSD_STATE
    groups = range(SSD_GROUPS)

    @pl.when(c == 0)
    def _():
        st_ref[...] = jnp.zeros_like(st_ref)
        cx_ref[...] = jnp.zeros_like(cx_ref)
        cbc_ref[...] = jnp.zeros_like(cbc_ref)

    row = lax.broadcasted_iota(jnp.int32, (q, 1), 0) + c * q
    lane = lax.broadcasted_iota(jnp.int32, (1, LANES), 1)
    dt = _softplus(dt_ref[...] + dtb_ref[...])
    dt = jnp.where((row >= FRONT_PAD) & (lane < SSD_HEADS), dt, 0.0)
    a = -jnp.exp(alog_ref[...])
    ri = lax.broadcasted_iota(jnp.int32, (q, q), 0)
    ci = lax.broadcasted_iota(jnp.int32, (q, q), 1)
    causal = ci <= ri
    cs = jnp.dot(causal.astype(F32), dt * a, precision=HIGHEST, preferred_element_type=F32)
    cs_t = cs.T
    cs_last = cs[q - 1:q, :]
    from_start = jnp.exp(cs)
    to_end = jnp.exp(cs_last - cs)
    total = jnp.exp(cs_last)

    def conv_silu(raw_ref, carry_ref, w_ref, b_ref, lo, width):
        raw = raw_ref[:, lo:lo + width].astype(F32)
        y = _causal_conv_rows(raw, carry_ref[:, lo:lo + width], w_ref.at[:, lo:lo + width], b_ref[:, lo:lo + width],
                              SSD_CONV)
        carry_ref[:, lo:lo + width] = raw[q - SUBLANES:q]
        return y * _sigmoid(y)

    xs = [conv_silu(x_ref, cx_ref, cwx_ref, cbx_ref, g * gw, gw) for g in groups]
    bmb = [conv_silu(bc_ref, cbc_ref, cwbc_ref, cbbc_ref, g * n, n).astype(BF16) for g in groups]
    cmb = [conv_silu(bc_ref, cbc_ref, cwbc_ref, cbbc_ref, SSD_GROUPS * n + g * n, n).astype(BF16) for g in groups]
    cb = [lax.dot_general(cmb[g], bmb[g], (((1,), (1,)), ((), ())), preferred_element_type=F32)
          for g in groups]

    lane_w = lax.broadcasted_iota(jnp.int32, (1, gw), 1)
    xd = [xs[g] * _expand_heads(dt, q, g) for g in groups]
    ys = []
    for g in groups:
        decayed = []
        for h in range(SSD_HPG):
            hh = g * SSD_HPG + h
            seg = cs[:, hh:hh + 1] - cs_t[hh:hh + 1, :]
            decayed.append((cb[g] * jnp.exp(jnp.where(causal, seg, -jnp.inf))).astype(BF16))
        xdb = xd[g].astype(BF16)
        x_heads = [jnp.where((lane_w >= h * SSD_HEAD_DIM) & (lane_w < (h + 1) * SSD_HEAD_DIM), xdb,
                             jnp.zeros_like(xdb)) for h in range(SSD_HPG)]
        ys.append(jnp.dot(jnp.concatenate(decayed, axis=1), jnp.concatenate(x_heads, axis=0),
                          preferred_element_type=F32))
    for g in groups:
        st = st_ref[g]
        ys[g] = ys[g] + (jnp.dot(cmb[g], st.astype(BF16), preferred_element_type=F32)
                         * _expand_heads(from_start, q, g))
        st_ref[g] = (st * _expand_heads(total, 1, g)
                     + lax.dot_general(bmb[g], (xd[g] * _expand_heads(to_end, q, g)).astype(BF16),
                                       (((0,), (0,)), ((), ())), preferred_element_type=F32))
    for g in groups:
        cols = slice(g * gw, (g + 1) * gw)
        y = ys[g] + xs[g] * dsk_ref[:, cols]
        z = z_ref[:, cols].astype(F32)
        yg = y * (z * _sigmoid(z))
        yn = yg * lax.rsqrt(jnp.mean(yg * yg, axis=-1, keepdims=True) + RMS_EPS) * nw_ref[:, cols]
        o_ref[:, cols] = yn.astype(o_ref.dtype)


def _ssd_mixer(u_zx, u_small, conv_w, conv_b, dt_bias, a_log, d_skip, norm_w):
    rows = u_zx.shape[0]
    q = SSD_CHUNK
    d = SSD_GROUPS * SSD_GROUP_W
    pad_l = lambda v: jnp.pad(v.reshape(1, SSD_HEADS), ((0, 0), (0, LANES - SSD_HEADS)))
    whole = lambda r, w, j: pl.BlockSpec((r, w), lambda c: (0, j))
    return pl.pallas_call(
        _ssd_kernel,
        grid=(rows // q,),
        in_specs=[pl.BlockSpec((q, d), lambda c: (c, 0)),
                  pl.BlockSpec((q, d), lambda c: (c, 1)),
                  pl.BlockSpec((q, d), lambda c: (c, 2)),
                  pl.BlockSpec((q, LANES), lambda c: (c, 0)),
                  whole(SSD_CONV, d, 0), whole(SSD_CONV, d, 1), whole(1, d, 0), whole(1, d, 1),
                  whole(1, LANES, 0), whole(1, LANES, 0), whole(1, d, 0), whole(1, d, 0)],
        out_specs=pl.BlockSpec((q, d), lambda c: (c, 0)),
        out_shape=jax.ShapeDtypeStruct((rows, d), BF16),
        scratch_shapes=[pltpu.VMEM((SSD_GROUPS, SSD_STATE, SSD_GROUP_W), F32), pltpu.VMEM((SUBLANES, d), F32),
                        pltpu.VMEM((SUBLANES, d), F32)],
        compiler_params=_params("arbitrary"),
        name="ssd",
    )(u_zx, u_zx, u_zx, u_small, conv_w, conv_w, conv_b.reshape(1, -1), conv_b.reshape(1, -1),
      pad_l(dt_bias), pad_l(a_log), jnp.repeat(d_skip, SSD_HEAD_DIM).reshape(1, d), norm_w.reshape(1, d))


def _gla_scores_blocked(q, k, gc):
    qc = q.shape[0]
    sub = GLA_SUB
    lane_j = lax.broadcasted_iota(jnp.int32, (sub, qc), 1)
    row_i = lax.broadcasted_iota(jnp.int32, (sub, 1), 0)
    a_rows = []
    for blk in range(qc // sub):
        lo = blk * sub
        q_b = q[lo:lo + sub]
        g_b = gc[lo:lo + sub]
        a_blk = jnp.zeros((sub, qc), F32)
        for j in range(sub):
            k_j = k[lo + j:lo + j + 1, :]
            g_j = gc[lo + j:lo + j + 1, :]
            s_j = jnp.sum(q_b * k_j * jnp.exp(jnp.minimum(g_b - g_j, 0.0)), axis=1, keepdims=True)
            a_blk = jnp.where(lane_j == lo + j, jnp.where(row_i >= j, s_j, 0.0), a_blk)
        if blk > 0:
            g_ref0 = gc[lo:lo + 1, :]
            q_t = (q_b * jnp.exp(g_b - g_ref0)).astype(BF16)
            k_t = (k * jnp.exp(jnp.minimum(g_ref0 - gc, 0.0))).astype(BF16)
            off = lax.dot_general(q_t, k_t, (((1,), (1,)), ((), ())), preferred_element_type=F32)
            a_blk = jnp.where(lane_j < lo, off, a_blk)
        a_rows.append(a_blk)
    return jnp.concatenate(a_rows, axis=0)


def _gla_kernel(q_ref, k_ref, v_ref, r_ref, glr_ref, w2_ref, gb_ref, nw_ref, o_ref, st_ref, a_ref):
    c = pl.program_id(0)
    qc = GLA_CHUNK
    dk, dv = GLA_HEAD_DK, GLA_HEAD_DV
    heads = range(GLA_HEADS)
    nt = (((1,), (1,)), ((), ()))

    @pl.when(c == 0)
    def _():
        st_ref[...] = jnp.zeros_like(st_ref)

    pre = jnp.dot(glr_ref[...], w2_ref[...], precision=HIGHEST, preferred_element_type=F32) + gb_ref[...]
    g = _log_sigmoid(pre) * (1.0 / GLA_GATE_TAU)
    row = lax.broadcasted_iota(jnp.int32, (qc, 1), 0) + c * qc
    g = jnp.where(row >= FRONT_PAD, g, 0.0)
    ri = lax.broadcasted_iota(jnp.int32, (qc, qc), 0)
    ci = lax.broadcasted_iota(jnp.int32, (qc, qc), 1)
    causal = ci <= ri
    gc = jnp.dot(causal.astype(F32), g, precision=HIGHEST, preferred_element_type=F32)
    g_last = gc[qc - 1:qc, :]

    q = q_ref[...].astype(F32) * (GLA_HEAD_DK ** -0.5)
    k = k_ref[...].astype(F32)
    q_dec = (q * jnp.exp(gc)).astype(BF16)
    k_end = (k * jnp.exp(g_last - gc)).astype(BF16)
    safe = jnp.max(-g_last) <= GLA_SAFE_DECAY

    @pl.when(safe)
    def _():
        k_inv = (k * jnp.exp(-gc)).astype(BF16)
        for h in heads:
            s = lax.dot_general(q_dec[:, h * dk:(h + 1) * dk], k_inv[:, h * dk:(h + 1) * dk], nt,
                                preferred_element_type=F32)
            a_ref[h] = jnp.where(causal, s, 0.0)

    @pl.when(jnp.logical_not(safe))
    def _():
        for h in heads:
            a_ref[h] = _gla_scores_blocked(q[:, h * dk:(h + 1) * dk], k[:, h * dk:(h + 1) * dk],
                                           gc[:, h * dk:(h + 1) * dk])

    vb = v_ref[...]
    decay = jnp.exp(g_last)
    outs = []
    for h in heads:
        st = st_ref[h]
        v_h = vb[:, h * dv:(h + 1) * dv]
        o = jnp.dot(a_ref[h].astype(BF16), v_h, preferred_element_type=F32)
        o = o + lax.dot_general(q_dec[:, h * dk:(h + 1) * dk], st.astype(BF16), nt, preferred_element_type=F32)
        st_ref[h] = st * decay[:, h * dk:(h + 1) * dk] + lax.dot_general(
            v_h, k_end[:, h * dk:(h + 1) * dk], (((0,), (0,)), ((), ())), preferred_element_type=F32)
        outs.append(o)
    for h in heads:
        o = outs[h]
        on = o * lax.rsqrt(jnp.mean(o * o, axis=-1, keepdims=True) + RMS_EPS) * nw_ref[:, h * dv:(h + 1) * dv]
        r = r_ref[:, h * dv:(h + 1) * dv].astype(F32)
        o_ref[:, h * dv:(h + 1) * dv] = (on * (r * _sigmoid(r))).astype(o_ref.dtype)


def _gla_mixer(u_qkv, u_r, u_small, gate_w2, gate_b, norm_w):
    rows = u_qkv.shape[0]
    qc = GLA_CHUNK
    dk, dv = GLA_HEAD_DK, GLA_HEAD_DV
    wk, wv = GLA_HEADS * dk, GLA_HEADS * dv
    glr_lane = HYB_GLR_COL % LANES
    w2 = jnp.pad(gate_w2, ((glr_lane, LANES - GLA_GATE_RANK - glr_lane), (0, 0)))
    return pl.pallas_call(
        _gla_kernel,
        grid=(rows // qc,),
        in_specs=[pl.BlockSpec((qc, wk), lambda c: (c, 0)),
                  pl.BlockSpec((qc, wk), lambda c: (c, 1)),
                  pl.BlockSpec((qc, wv), lambda c: (c, 1)),
                  pl.BlockSpec((qc, wv), lambda c: (c, 0)),
                  pl.BlockSpec((qc, LANES), lambda c: (c, 1)),
                  pl.BlockSpec((LANES, wk), lambda c: (0, 0)),
                  pl.BlockSpec((1, wk), lambda c: (0, 0)),
                  pl.BlockSpec((1, wv), lambda c: (0, 0))],
        out_specs=pl.BlockSpec((qc, wv), lambda c: (c, 0)),
        out_shape=jax.ShapeDtypeStruct((rows, wv), BF16),
        scratch_shapes=[pltpu.VMEM((GLA_HEADS, dv, dk), F32), pltpu.VMEM((GLA_HEADS, qc, qc), F32)],
        compiler_params=_params("arbitrary"),
        name="gla",
    )(u_qkv, u_qkv, u_qkv, u_r, u_small, w2, gate_b.reshape(1, -1), norm_w.reshape(1, -1))


def _rope_kernel(freq_ref, cm_ref, sp_ref, sm_ref, *, tm):
    i = pl.program_id(0)
    row = lax.broadcasted_iota(jnp.int32, (tm, LANES), 0) + i * tm
    lane = lax.broadcasted_iota(jnp.int32, (tm, LANES), 1) % SWA_HEAD_DIM
    ang = (row - FRONT_PAD).astype(F32) * freq_ref[...]
    cos = jnp.cos(ang)
    sin = jnp.sin(ang)
    half = ROPE_DIM // 2
    cm_ref[...] = cos
    sp_ref[...] = jnp.where(lane < half, -sin, 0.0)
    sm_ref[...] = jnp.where((lane >= half) & (lane < ROPE_DIM), sin, 0.0)


def _rope_tables(rows):
    half = ROPE_DIM // 2
    inv_freq = ROPE_THETA ** (-jnp.arange(half, dtype=F32) / half)
    per_head = jnp.concatenate([inv_freq, inv_freq, jnp.zeros((SWA_HEAD_DIM - ROPE_DIM,), F32)])
    freq = jnp.tile(per_head, LANES // SWA_HEAD_DIM).reshape(1, LANES)
    tm = _row_tile(rows)
    shp = jax.ShapeDtypeStruct((rows, LANES), F32)
    spec = pl.BlockSpec((tm, LANES), lambda i: (i, 0))
    return pl.pallas_call(
        functools.partial(_rope_kernel, tm=tm),
        grid=(rows // tm,),
        in_specs=[pl.BlockSpec((1, LANES), lambda i: (0, 0))],
        out_specs=[spec, spec, spec],
        out_shape=[shp, shp, shp],
        compiler_params=_params("arbitrary"),
        name="rope_tables",
    )(freq)


def _swa_kernel(sink_ref, q_ref, prev_ref, cur_ref, meta_ref, o_ref):
    n = pl.program_id(0)
    w = SWA_WINDOW
    meta_lo = FRONT_PAD
    kvw = SWA_KV_HEADS * LANES
    nt = (((1,), (1,)), ((), ()))
    lane = lax.broadcasted_iota(jnp.int32, (1, LANES), 1)
    low = lane < SWA_HEAD_DIM
    i = lax.broadcasted_iota(jnp.int32, (w, 1), 0)
    on_cur = lane <= i
    valid_band = (on_cur & ((n >= 1) | (lane >= meta_lo))) | (jnp.logical_not(on_cur) & (n >= 2))
    valid_meta = (lane >= meta_lo) & (n >= 1)
    zero = jnp.zeros((w, LANES), q_ref.dtype)
    kv_heads = range(SWA_KV_HEADS)

    def scores(g):
        qa = q_ref[:, 2 * g * LANES:(2 * g + 1) * LANES]
        qb = q_ref[:, (2 * g + 1) * LANES:(2 * g + 2) * LANES]
        qs = jnp.concatenate([jnp.where(low, qa, zero), jnp.where(low, zero, qa),
                              jnp.where(low, qb, zero), jnp.where(low, zero, qb)], axis=0)
        kcols = slice(g * LANES, (g + 1) * LANES)
        kk = jnp.concatenate([prev_ref[:, kcols], cur_ref[:, kcols], meta_ref[:, kcols]], axis=0)
        return lax.dot_general(qs, kk, nt, preferred_element_type=F32)

    s_next = scores(0)
    for g in kv_heads:
        s_all = s_next
        if g + 1 < SWA_KV_HEADS:
            s_next = scores(g + 1)
        probs, denoms = [], []
        for h in range(SWA_GROUP):
            s = s_all[h * w:(h + 1) * w]
            s_band = jnp.where(valid_band, jnp.where(on_cur, s[:, w:2 * w], s[:, :w]), -jnp.inf)
            s_meta = jnp.where(valid_meta, s[:, 2 * w:], -jnp.inf)
            sink = sink_ref[g * SWA_GROUP + h]
            m = jnp.maximum(jnp.max(jnp.maximum(s_band, s_meta), axis=-1, keepdims=True), sink)
            p_band = jnp.exp(s_band - m)
            p_meta = jnp.exp(s_meta - m)
            denoms.append(jnp.sum(p_band + p_meta, axis=-1, keepdims=True) + jnp.exp(sink - m))
            probs.append(jnp.concatenate([jnp.where(on_cur, 0.0, p_band), jnp.where(on_cur, p_band, 0.0), p_meta],
                                         axis=1).astype(BF16))
        vcols = slice(kvw + g * LANES, kvw + (g + 1) * LANES)
        vv = jnp.concatenate([prev_ref[:, vcols], cur_ref[:, vcols], meta_ref[:, vcols]], axis=0)
        o = jnp.dot(jnp.concatenate(probs, axis=0), vv, preferred_element_type=F32)
        o = [o[h * w:(h + 1) * w] / denoms[h] for h in range(SWA_GROUP)]
        oa = jnp.where(low, o[0], o[1])
        ob = jnp.where(low, o[2], o[3])
        o_ref[:, 2 * g * LANES:(2 * g + 2) * LANES] = jnp.concatenate([oa, ob], axis=1).astype(o_ref.dtype)


def _swa_attention(q, kv, sinks):
    rows, qw = q.shape
    w = SWA_WINDOW
    kvw = kv.shape[1]
    return pl.pallas_call(
        _swa_kernel,
        grid=(rows // w,),
        in_specs=[pl.BlockSpec(memory_space=pltpu.SMEM),
                  pl.BlockSpec((w, qw), lambda n: (n, 0)),
                  pl.BlockSpec((w, kvw), lambda n: (jnp.maximum(n - 1, 0), 0)),
                  pl.BlockSpec((w, kvw), lambda n: (n, 0)),
                  pl.BlockSpec((w, kvw), lambda n: (0, 0))],
        out_specs=pl.BlockSpec((w, qw), lambda n: (n, 0)),
        out_shape=jax.ShapeDtypeStruct((rows, qw), BF16),
        compiler_params=_params("arbitrary"),
        name="swa",
    )(sinks, q, kv, kv, kv)


def _embed_kernel(x_ref, meta_ref, hf_ref, hb_ref, *, tm):
    i = pl.program_id(0)
    head = FRONT_PAD + N_META

    @pl.when(i == 0)
    def _():
        top = jnp.concatenate([jnp.zeros((FRONT_PAD, D_MODEL), F32), meta_ref[...]], axis=0)
        hf_ref[0:head, :] = top
        hb_ref[0:head, :] = top.astype(BF16)
        body = x_ref[0:tm - head, :]
        hf_ref[head:tm, :] = body
        hb_ref[head:tm, :] = body.astype(BF16)

    @pl.when(i != 0)
    def _():
        hf_ref[...] = x_ref[...]
        hb_ref[...] = x_ref[...].astype(BF16)


def _embed(x, meta):
    seq, d = x.shape
    head = FRONT_PAD + N_META
    rows = head + seq
    tm = _row_tile(rows)
    out = pl.BlockSpec((tm, d), lambda i: (i, 0))
    x_rows = lambda i: pl.multiple_of(jnp.maximum(i * tm - head, 0), LANES)
    return pl.pallas_call(
        functools.partial(_embed_kernel, tm=tm),
        grid=(rows // tm,),
        in_specs=[pl.BlockSpec((pl.Element(tm), pl.Element(d)), lambda i: (x_rows(i), 0)),
                  pl.BlockSpec((N_META, d), lambda i: (0, 0))],
        out_specs=[out, out],
        out_shape=[jax.ShapeDtypeStruct((rows, d), F32), jax.ShapeDtypeStruct((rows, d), BF16)],
        compiler_params=_params("arbitrary"),
        name="embed",
    )(x, meta)


def _pad_halves(t):
    pad = lambda a: jnp.pad(a, ((0, 0), (0, D_FF_PAD - D_FF)))
    return jnp.concatenate([pad(t[:, :D_FF]), pad(t[:, D_FF:])], axis=1)


def _trunk(x, meta_tokens, hyb_w_in, hyb_conv_w, hyb_conv_b, ssd_dt_bias, ssd_a_log, ssd_d, ssd_norm_w,
           gla_gate_w2, gla_gate_b, gla_norm_w, hyb_w_out, swa_w_qkv, swa_sinks, swa_w_out,
           ffn_w_up, ffn_conv_w, ffn_conv_b, ffn_w_down, ln_mix_g, ln_mix_b, ln_ffn_g, ln_ffn_b):
    seq = x.shape[0]
    rows = FRONT_PAD + N_META + seq
    h, hb = _embed(x, meta_tokens.astype(F32))
    tables = _rope_tables(rows)
    hyb_w_in = jnp.pad(hyb_w_in, ((0, 0), (0, 0), (0, -hyb_w_in.shape[2] % LANES))).astype(BF16)
    for layer in range(DEPTH):
        j = layer // 2
        if layer % 2 == 0:
            u_zx = _project(hb, hyb_w_in, j, first_block=0, block_stride=8, tn=1024, n_tiles=6)
            u_qkv = _project(hb, hyb_w_in, j, first_block=HYB_QKV_COL // LANES, block_stride=8, tn=1024, n_tiles=4,
                             shift=HYB_QKV_COL % LANES)
            u_r = _project(hb, hyb_w_in, j, first_block=HYB_R_COL // LANES, block_stride=8, tn=1024, n_tiles=2,
                           shift=HYB_R_COL % LANES)
            u_small = _project(hb, hyb_w_in, j, first_block=HYB_DT_COL // LANES,
                               block_stride=(HYB_GLR_COL - HYB_DT_COL) // LANES, tn=LANES, n_tiles=2, out_dtype=F32)
            y_ssd = _ssd_mixer(u_zx, u_small, hyb_conv_w[j], hyb_conv_b[j], ssd_dt_bias[j], ssd_a_log[j],
                               ssd_d[j], ssd_norm_w[j])
            y_gla = _gla_mixer(u_qkv, u_r, u_small, gla_gate_w2[j], gla_gate_b[j], gla_norm_w[j])
            h, hb = _matmul_residual_ln([y_ssd, y_gla], [y_ssd.shape[1], y_gla.shape[1]], hyb_w_out, j, h,
                                        ln_mix_g[layer], ln_mix_b[layer], nchunks=8)
        else:
            q = _project(hb, swa_w_qkv, j, first_block=0, block_stride=8, tn=1024, n_tiles=2,
                         scale=SWA_HEAD_DIM ** -0.5, mode="rope", tables=tables)
            kv = _project(hb, swa_w_qkv, j, first_block=SWA_Q_HEADS * SWA_HEAD_DIM // LANES, block_stride=4, tn=512,
                          n_tiles=2, mode="kv", tables=tables)
            attn = _swa_attention(q, kv, swa_sinks[j])
            h, hb = _matmul_residual_ln([attn], [attn.shape[1]], swa_w_out, j, h, ln_mix_g[layer], ln_mix_b[layer],
                                        nchunks=4)
        act = _ffn_up(hb, ffn_w_up, layer, _pad_halves(ffn_conv_w[layer]),
                      _pad_halves(ffn_conv_b[layer].reshape(1, -1)))
        h, hb = _matmul_residual_ln([act], [D_FF], ffn_w_down, layer, h, ln_ffn_g[layer], ln_ffn_b[layer], nchunks=8)
    return h[FRONT_PAD + N_META:]


def kernel(x, meta_tokens, hyb_w_in, hyb_conv_w, hyb_conv_b, ssd_dt_bias, ssd_a_log, ssd_d, ssd_norm_w,
           gla_gate_w2, gla_gate_b, gla_norm_w, hyb_w_out, swa_w_qkv, swa_sinks, swa_w_out,
           ffn_w_up, ffn_conv_w, ffn_conv_b, ffn_w_down, ln_mix_g, ln_mix_b, ln_ffn_g, ln_ffn_b):
    params = (meta_tokens, hyb_w_in, hyb_conv_w, hyb_conv_b, ssd_dt_bias, ssd_a_log, ssd_d, ssd_norm_w,
              gla_gate_w2, gla_gate_b, gla_norm_w, hyb_w_out, swa_w_qkv, swa_sinks, swa_w_out,
              ffn_w_up, ffn_conv_w, ffn_conv_b, ffn_w_down, ln_mix_g, ln_mix_b, ln_ffn_g, ln_ffn_b)
    return jnp.stack([_trunk(x[b], *params) for b in range(x.shape[0])], axis=0)
```

```python
import functools

import jax
import jax.numpy as jnp
from jax import lax
from jax.experimental import pallas as pl
from jax.experimental.pallas import tpu as pltpu

F32 = jnp.float32
BF16 = jnp.bfloat16
HIGHEST = lax.Precision.HIGHEST

D_MODEL = 2048
DEPTH = 4
N_META = 16
LN_EPS = 1e-5
RMS_EPS = 1e-6
DEEPNORM_ALPHA = (2.0 * DEPTH) ** 0.25

SSD_HEAD_DIM = 64
SSD_HEADS = 32
SSD_GROUPS = 8
SSD_HPG = 4
SSD_STATE = 128
SSD_CONV = 4
SSD_CHUNK = 128
SSD_GROUP_W = SSD_HPG * SSD_HEAD_DIM

GLA_HEADS = 4
GLA_HEAD_DK = 256
GLA_HEAD_DV = 512
GLA_GATE_RANK = 16
GLA_GATE_TAU = 16.0
GLA_CHUNK = 128
GLA_SUB = 16
GLA_SAFE_DECAY = 80.0

SWA_HEAD_DIM = 64
SWA_Q_HEADS = 32
SWA_KV_HEADS = 8
SWA_GROUP = 4
SWA_WINDOW = 128
ROPE_THETA = 500000.0
ROPE_DIM = 16

D_FF = 5504
FFN_CONV = 3

LANES = 128
SUBLANES = 8
FRONT_PAD = SSD_CHUNK - N_META
D_FF_PAD = 5632
FFN_TN = 512
FFN_SUBTILES = 4
PROJ_SUBTILES = 4
LN_ROW_TILE = 320
LN_SUBTILES = 2
LN_WEIGHT_COPY_BUDGET = 20 * 1024 * 1024
VMEM_LIMIT = 56 * 1024 * 1024

HYB_DT_COL = D_MODEL + (D_MODEL + 2 * SSD_GROUPS * SSD_STATE)
HYB_QKV_COL = HYB_DT_COL + SSD_HEADS
HYB_GLR_COL = HYB_QKV_COL + GLA_HEADS * (2 * GLA_HEAD_DK + GLA_HEAD_DV)
HYB_R_COL = HYB_GLR_COL + GLA_GATE_RANK


def _row_tile(rows):
    for t in (640, 512, 384, 256, 128):
        if rows % t == 0:
            return t
    raise ValueError(f"row count {rows} is not a multiple of 128")


def _row_tile_big(rows):
    return 1664 if rows % 1664 == 0 else _row_tile(rows)


def _params(*sem):
    return pltpu.CompilerParams(dimension_semantics=sem, vmem_limit_bytes=VMEM_LIMIT)


def _sigmoid(x):
    return 1.0 / (1.0 + jnp.exp(-x))


def _softplus(x):
    return jnp.maximum(x, 0.0) + jnp.log(1.0 + jnp.exp(-jnp.abs(x)))


def _log_sigmoid(x):
    return jnp.minimum(x, 0.0) - jnp.log(1.0 + jnp.exp(-jnp.abs(x)))


def _causal_conv_rows(y, carry, w_ref, b_row, taps):
    top = jnp.concatenate([carry, y[0:SUBLANES]], axis=0)
    w_last = w_ref[taps - 1:taps, :]
    acc = b_row + w_last * y
    acc_top = b_row + w_last * y[0:SUBLANES]
    for s in range(1, taps):
        wk = w_ref[taps - 1 - s:taps - s, :]
        acc = acc + wk * pltpu.roll(y, s, 0)
        acc_top = acc_top + wk * pltpu.roll(top, s, 0)[SUBLANES:2 * SUBLANES]
    return jnp.concatenate([acc_top, acc[SUBLANES:]], axis=0)


def _assemble_weight(w_refs, wb_ref, shift, scale):
    k, tn = wb_ref.shape
    chunk = 256
    for r in range(0, k, chunk):
        w = jnp.concatenate([wr[r:r + chunk, :] for wr in w_refs], axis=1)
        if shift:
            w = pltpu.roll(w.astype(F32), w.shape[1] - shift, 1)
        w = w[:, :tn]
        if scale is not None:
            w = w * scale
        wb_ref[r:r + chunk, :] = w.astype(BF16)


def _weight_block_specs(k, layer, block_fns):
    return [pl.BlockSpec((None, k, LANES), functools.partial(lambda *ids, fn: (layer, 0, fn(*ids)), fn=fn))
            for fn in block_fns]


def _rotate_heads(y, cm, sp, sm):
    half = ROPE_DIM // 2
    out = []
    for c in range(y.shape[1] // LANES):
        yc = y[:, c * LANES:(c + 1) * LANES]
        out.append(yc * cm + pltpu.roll(yc, LANES - half, 1) * sp + pltpu.roll(yc, half, 1) * sm)
    return jnp.concatenate(out, axis=1)


def _duplicate_heads(y):
    low = lax.broadcasted_iota(jnp.int32, (1, LANES), 1) < SWA_HEAD_DIM
    out = []
    for c in range(y.shape[1] // LANES):
        yc = y[:, c * LANES:(c + 1) * LANES]
        rolled = pltpu.roll(yc, SWA_HEAD_DIM, 1)
        out.append(jnp.where(low, yc, rolled))
        out.append(jnp.where(low, rolled, yc))
    return jnp.concatenate(out, axis=1)


def _proj_kernel(*refs, nblk, shift, scale, mode):
    x_ref = refs[0]
    w_refs = refs[1:1 + nblk]
    rest = refs[1 + nblk:]
    if mode != "plain":
        cm_ref, sp_ref, sm_ref = rest[:3]
        rest = rest[3:]
    o_ref, wb_ref = rest
    j = pl.program_id(0)
    i = pl.program_id(1)

    @pl.when(i == 0)
    def _():
        _assemble_weight(w_refs, wb_ref, shift, scale)

    if mode == "plain":
        o_ref[...] = jnp.dot(x_ref[...], wb_ref[...], preferred_element_type=F32).astype(o_ref.dtype)
        return
    tm = x_ref.shape[0]
    ts = tm // PROJ_SUBTILES
    ys = [jnp.dot(x_ref[s * ts:(s + 1) * ts, :], wb_ref[...], preferred_element_type=F32)
          for s in range(PROJ_SUBTILES)]
    rotate = (j == 0) if mode == "kv" else True
    for s, y in enumerate(ys):
        rows = slice(s * ts, (s + 1) * ts)
        cm = jnp.where(rotate, cm_ref[rows, :], 1.0)
        sp = jnp.where(rotate, sp_ref[rows, :], 0.0)
        sm = jnp.where(rotate, sm_ref[rows, :], 0.0)
        y = _rotate_heads(y, cm, sp, sm)
        if mode == "kv":
            y = _duplicate_heads(y)
        o_ref[rows, :] = y.astype(o_ref.dtype)


def _project(xb, w, layer, *, first_block, block_stride, tn, n_tiles, shift=0, scale=None, mode="plain",
             tables=None, out_dtype=BF16, blocks=None):
    m, k = xb.shape
    tm = _row_tile_big(m)
    nb = tn // LANES
    tn_out = 2 * tn if mode == "kv" else tn
    if blocks is not None:
        assert n_tiles == 1 and shift == 0 and len(blocks) == nb
        nblk = nb
        w_specs = _weight_block_specs(k, layer, [functools.partial(lambda j, i, b: b, b=b) for b in blocks])
    elif shift == 0 and block_stride == nb and first_block % nb == 0:
        nblk = 1
        w_specs = [pl.BlockSpec((None, k, tn), lambda j, i: (layer, 0, first_block // nb + j))]
    else:
        nblk = nb + (1 if shift else 0)
        w_specs = _weight_block_specs(
            k, layer, [functools.partial(lambda j, i, b: first_block + j * block_stride + b, b=b) for b in range(nblk)])
    in_specs = [pl.BlockSpec((tm, k), lambda j, i: (i, 0))] + w_specs
    args = [xb] + [w] * nblk
    if mode != "plain":
        in_specs += [pl.BlockSpec((tm, LANES), lambda j, i: (i, 0))] * 3
        args += list(tables)
    return pl.pallas_call(
        functools.partial(_proj_kernel, nblk=nblk, shift=shift, scale=scale, mode=mode),
        grid=(n_tiles, m // tm),
        in_specs=in_specs,
        out_specs=pl.BlockSpec((tm, tn_out), lambda j, i: (i, j)),
        out_shape=jax.ShapeDtypeStruct((m, n_tiles * tn_out), out_dtype),
        scratch_shapes=[pltpu.VMEM((k, tn), BF16)],
        compiler_params=_params("arbitrary", "arbitrary"),
        name="proj_" + mode,
    )(*args)


def _mm_ln_kernel(*refs, nsrc, tm, subtiles, nchunks, ck):
    x_refs = refs[:nsrc]
    w_ref, res_ref, g_ref, b_ref, of_ref, ob_ref, wb_ref = refs[nsrc:]
    step = pl.program_id(0)

    @pl.when(step < nchunks)
    def _():
        wb_ref[pl.ds(pl.multiple_of(step * ck, ck), ck), :] = w_ref[...].astype(BF16)

    @pl.when(step >= nchunks)
    def _():
        i = step - nchunks
        ts = tm // subtiles
        sums = []
        for s in range(subtiles):
            rows = slice(s * ts, (s + 1) * ts)
            acc, k0 = None, 0
            for x_ref in x_refs:
                kw = x_ref.shape[1]
                part = jnp.dot(x_ref[rows, :], wb_ref[k0:k0 + kw, :], preferred_element_type=F32)
                acc = part if acc is None else acc + part
                k0 += kw
            sums.append(acc)
        for s, acc in enumerate(sums):
            rows = slice(s * ts, (s + 1) * ts)
            t = DEEPNORM_ALPHA * res_ref[rows, :] + acc
            mu = jnp.mean(t, axis=-1, keepdims=True)
            d = t - mu
            var = jnp.mean(d * d, axis=-1, keepdims=True)
            y = d * lax.rsqrt(var + LN_EPS) * g_ref[...] + b_ref[...]
            row = lax.broadcasted_iota(jnp.int32, (ts, 1), 0) + (i * tm + s * ts)
            y = jnp.where(row >= FRONT_PAD, y, 0.0)
            of_ref[rows, :] = y
            ob_ref[rows, :] = y.astype(BF16)


def _matmul_residual_ln(xs, k_widths, w, layer, res, gamma, beta, nchunks):
    m = xs[0].shape[0]
    _, kdim, n = w.shape
    assert sum(k_widths) == kdim and m % LN_ROW_TILE == 0 and kdim % (16 * nchunks) == 0
    tm = LN_ROW_TILE
    ck = kdim // nchunks
    subtiles = LN_SUBTILES if 2 * kdim * n * 2 <= LN_WEIGHT_COPY_BUDGET else 1
    tile = lambda s: jnp.maximum(s - nchunks, 0)
    x_specs = [pl.BlockSpec((tm, kw), lambda s: (tile(s), 0)) for kw in k_widths]
    return pl.pallas_call(
        functools.partial(_mm_ln_kernel, nsrc=len(xs), tm=tm, subtiles=subtiles, nchunks=nchunks, ck=ck),
        grid=(nchunks + m // tm,),
        in_specs=x_specs + [pl.BlockSpec((None, ck, n), lambda s: (layer, jnp.minimum(s, nchunks - 1), 0)),
                            pl.BlockSpec((tm, n), lambda s: (tile(s), 0)),
                            pl.BlockSpec((1, n), lambda s: (0, 0)),
                            pl.BlockSpec((1, n), lambda s: (0, 0))],
        out_specs=[pl.BlockSpec((tm, n), lambda s: (tile(s), 0)),
                   pl.BlockSpec((tm, n), lambda s: (tile(s), 0))],
        out_shape=[jax.ShapeDtypeStruct((m, n), F32), jax.ShapeDtypeStruct((m, n), BF16)],
        scratch_shapes=[pltpu.VMEM((kdim, n), BF16)],
        compiler_params=_params("arbitrary"),
        name="proj_ln",
    )(*xs, w, res, gamma.reshape(1, n), beta.reshape(1, n))


def _ffn_up_kernel(*refs, nblk, tm, tn):
    x_ref = refs[0]
    w_refs = refs[1:1 + nblk]
    cw_g, cw_v, cb_g, cb_v, o_ref, wb_ref, cg_ref, cv_ref = refs[1 + nblk:]
    j = pl.program_id(0)
    i = pl.program_id(1)

    @pl.when(i == 0)
    def _():
        _assemble_weight(w_refs, wb_ref, 0, None)
        cg_ref[...] = jnp.zeros_like(cg_ref)
        cv_ref[...] = jnp.zeros_like(cv_ref)

    col = lax.broadcasted_iota(jnp.int32, (1, tn), 1) + j * tn
    ts = tm // FFN_SUBTILES
    ys = []
    for s in range(FFN_SUBTILES):
        y = jnp.dot(x_ref[s * ts:(s + 1) * ts, :], wb_ref[...], preferred_element_type=F32)
        ys.append((y[:, :tn], y[:, tn:]))
    carry_g, carry_v = cg_ref[...], cv_ref[...]
    for s, (yg, yv) in enumerate(ys):
        hg = _causal_conv_rows(yg, carry_g, cw_g, cb_g[...], FFN_CONV)
        hv = _causal_conv_rows(yv, carry_v, cw_v, cb_v[...], FFN_CONV)
        carry_g, carry_v = yg[ts - SUBLANES:ts], yv[ts - SUBLANES:ts]
        o_ref[s * ts:(s + 1) * ts, :] = jnp.where(col < D_FF, hg * _sigmoid(hg) * hv, 0.0).astype(o_ref.dtype)
    cg_ref[...] = carry_g
    cv_ref[...] = carry_v


def _ffn_up(xb, w_up, layer, conv_w, conv_b):
    m, k = xb.shape
    tm = _row_tile_big(m)
    assert (tm // FFN_SUBTILES) % 16 == 0, tm
    tn = FFN_TN
    nj = D_FF_PAD // tn
    nb = tn // LANES
    val0 = D_FF // LANES
    last_blk = 2 * D_FF // LANES - 1
    w_specs = [pl.BlockSpec((None, k, tn), lambda j, i: (layer, 0, j))] + _weight_block_specs(
        k, layer, [functools.partial(lambda j, i, b: jnp.minimum(val0 + j * nb + b, last_blk), b=b) for b in range(nb)])
    nblk = 1 + nb
    return pl.pallas_call(
        functools.partial(_ffn_up_kernel, nblk=nblk, tm=tm, tn=tn),
        grid=(nj, m // tm),
        in_specs=[pl.BlockSpec((tm, k), lambda j, i: (i, 0))] + w_specs
        + [pl.BlockSpec((FFN_CONV, tn), lambda j, i: (0, j)),
           pl.BlockSpec((FFN_CONV, tn), lambda j, i: (0, nj + j)),
           pl.BlockSpec((1, tn), lambda j, i: (0, j)),
           pl.BlockSpec((1, tn), lambda j, i: (0, nj + j))],
        out_specs=pl.BlockSpec((tm, tn), lambda j, i: (i, j)),
        out_shape=jax.ShapeDtypeStruct((m, D_FF_PAD), BF16),
        scratch_shapes=[pltpu.VMEM((k, 2 * tn), BF16), pltpu.VMEM((SUBLANES, tn), F32),
                        pltpu.VMEM((SUBLANES, tn), F32)],
        compiler_params=_params("arbitrary", "arbitrary"),
        name="ffn_up",
    )(xb, *([w_up] * nblk), conv_w, conv_w, conv_b, conv_b)


def _expand_heads(x, rows, g):
    low = lax.broadcasted_iota(jnp.int32, (1, LANES), 1) < SSD_HEAD_DIM
    h0 = g * SSD_HPG
    b = [jnp.broadcast_to(x[:, h0 + h:h0 + h + 1], (rows, LANES)) for h in range(SSD_HPG)]
    return jnp.concatenate([jnp.where(low, b[0], b[1]), jnp.where(low, b[2], b[3])], axis=1)


def _ssd_kernel(z_ref, x_ref, bc_ref, dt_ref, cwx_ref, cwbc_ref, cbx_ref, cbbc_ref, dtb_ref, alog_ref, dsk_ref,
                nw_ref, o_ref, st_ref, cx_ref, cbc_ref):
    c = pl.program_id(0)
    q = SSD_CHUNK
    gw = SSD_GROUP_W
    n = SSD_STATE
    groups = range(SSD_GROUPS)

    @pl.when(c == 0)
    def _():
        st_ref[...] = jnp.zeros_like(st_ref)
        cx_ref[...] = jnp.zeros_like(cx_ref)
        cbc_ref[...] = jnp.zeros_like(cbc_ref)

    row = lax.broadcasted_iota(jnp.int32, (q, 1), 0) + c * q
    lane = lax.broadcasted_iota(jnp.int32, (1, LANES), 1)
    dt = _softplus(dt_ref[...] + dtb_ref[...])
    dt = jnp.where((row >= FRONT_PAD) & (lane < SSD_HEADS), dt, 0.0)
    a = -jnp.exp(alog_ref[...])
    ri = lax.broadcasted_iota(jnp.int32, (q, q), 0)
    ci = lax.broadcasted_iota(jnp.int32, (q, q), 1)
    causal = ci <= ri
    cs = jnp.dot(causal.astype(F32), dt * a, precision=HIGHEST, preferred_element_type=F32)
    cs_t = cs.T
    cs_last = cs[q - 1:q, :]
    from_start = jnp.exp(cs)
    to_end = jnp.exp(cs_last - cs)
    total = jnp.exp(cs_last)

    def conv_silu(raw_ref, carry_ref, w_ref, b_ref, lo, width):
        raw = raw_ref[:, lo:lo + width].astype(F32)
        y = _causal_conv_rows(raw, carry_ref[:, lo:lo + width], w_ref.at[:, lo:lo + width], b_ref[:, lo:lo + width],
                              SSD_CONV)
        carry_ref[:, lo:lo + width] = raw[q - SUBLANES:q]
        return y * _sigmoid(y)

    xs = [conv_silu(x_ref, cx_ref, cwx_ref, cbx_ref, g * gw, gw) for g in groups]
    bmb = [conv_silu(bc_ref, cbc_ref, cwbc_ref, cbbc_ref, g * n, n).astype(BF16) for g in groups]
    cmb = [conv_silu(bc_ref, cbc_ref, cwbc_ref, cbbc_ref, SSD_GROUPS * n + g * n, n).astype(BF16) for g in groups]
    cb = [lax.dot_general(cmb[g], bmb[g], (((1,), (1,)), ((), ())), preferred_element_type=F32)
          for g in groups]

    lane_w = lax.broadcasted_iota(jnp.int32, (1, gw), 1)
    xd = [xs[g] * _expand_heads(dt, q, g) for g in groups]
    ys = []
    for g in groups:
        decayed = []
        for h in range(SSD_HPG):
            hh = g * SSD_HPG + h
            seg = cs[:, hh:hh + 1] - cs_t[hh:hh + 1, :]
            decayed.append((cb[g] * jnp.exp(jnp.where(causal, seg, -jnp.inf))).astype(BF16))
        xdb = xd[g].astype(BF16)
        x_heads = [jnp.where((lane_w >= h * SSD_HEAD_DIM) & (lane_w < (h + 1) * SSD_HEAD_DIM), xdb,
                             jnp.zeros_like(xdb)) for h in range(SSD_HPG)]
        ys.append(jnp.dot(jnp.concatenate(decayed, axis=1), jnp.concatenate(x_heads, axis=0),
                          preferred_element_type=F32))
    for g in groups:
        st = st_ref[g]
        ys[g] = ys[g] + (jnp.dot(cmb[g], st.astype(BF16), preferred_element_type=F32)
                         * _expand_heads(from_start, q, g))
        st_ref[g] = (st * _expand_heads(total, 1, g)
                     + lax.dot_general(bmb[g], (xd[g] * _expand_heads(to_end, q, g)).astype(BF16),
                                       (((0,), (0,)), ((), ())), preferred_element_type=F32))
    for g in groups:
        cols = slice(g * gw, (g + 1) * gw)
        y = ys[g] + xs[g] * dsk_ref[:, cols]
        z = z_ref[:, cols].astype(F32)
        yg = y * (z * _sigmoid(z))
        yn = yg * lax.rsqrt(jnp.mean(yg * yg, axis=-1, keepdims=True) + RMS_EPS) * nw_ref[:, cols]
        o_ref[:, cols] = yn.astype(o_ref.dtype)


def _ssd_mixer(u_zx, u_small, conv_w, conv_b, dt_bias, a_log, d_skip, norm_w):
    rows = u_zx.shape[0]
    q = SSD_CHUNK
    d = SSD_GROUPS * SSD_GROUP_W
    pad_l = lambda v: jnp.pad(v.reshape(1, SSD_HEADS), ((0, 0), (0, LANES - SSD_HEADS)))
    whole = lambda r, w, j: pl.BlockSpec((r, w), lambda c: (0, j))
    return pl.pallas_call(
        _ssd_kernel,
        grid=(rows // q,),
        in_specs=[pl.BlockSpec((q, d), lambda c: (c, 0)),
                  pl.BlockSpec((q, d), lambda c: (c, 1)),
                  pl.BlockSpec((q, d), lambda c: (c, 2)),
                  pl.BlockSpec((q, LANES), lambda c: (c, 0)),
                  whole(SSD_CONV, d, 0), whole(SSD_CONV, d, 1), whole(1, d, 0), whole(1, d, 1),
                  whole(1, LANES, 0), whole(1, LANES, 0), whole(1, d, 0), whole(1, d, 0)],
        out_specs=pl.BlockSpec((q, d), lambda c: (c, 0)),
        out_shape=jax.ShapeDtypeStruct((rows, d), BF16),
        scratch_shapes=[pltpu.VMEM((SSD_GROUPS, SSD_STATE, SSD_GROUP_W), F32), pltpu.VMEM((SUBLANES, d), F32),
                        pltpu.VMEM((SUBLANES, d), F32)],
        compiler_params=_params("arbitrary"),
        name="ssd",
    )(u_zx, u_zx, u_zx, u_small, conv_w, conv_w, conv_b.reshape(1, -1), conv_b.reshape(1, -1),
      pad_l(dt_bias), pad_l(a_log), jnp.repeat(d_skip, SSD_HEAD_DIM).reshape(1, d), norm_w.reshape(1, d))


def _gla_scores_blocked(q, k, gc):
    qc = q.shape[0]
    sub = GLA_SUB
    lane_j = lax.broadcasted_iota(jnp.int32, (sub, qc), 1)
    row_i = lax.broadcasted_iota(jnp.int32, (sub, 1), 0)
    a_rows = []
    for blk in range(qc // sub):
        lo = blk * sub
        q_b = q[lo:lo + sub]
        g_b = gc[lo:lo + sub]
        a_blk = jnp.zeros((sub, qc), F32)
        for j in range(sub):
            k_j = k[lo + j:lo + j + 1, :]
            g_j = gc[lo + j:lo + j + 1, :]
            s_j = jnp.sum(q_b * k_j * jnp.exp(jnp.minimum(g_b - g_j, 0.0)), axis=1, keepdims=True)
            a_blk = jnp.where(lane_j == lo + j, jnp.where(row_i >= j, s_j, 0.0), a_blk)
        if blk > 0:
            g_ref0 = gc[lo:lo + 1, :]
            q_t = (q_b * jnp.exp(g_b - g_ref0)).astype(BF16)
            k_t = (k * jnp.exp(jnp.minimum(g_ref0 - gc, 0.0))).astype(BF16)
            off = lax.dot_general(q_t, k_t, (((1,), (1,)), ((), ())), preferred_element_type=F32)
            a_blk = jnp.where(lane_j < lo, off, a_blk)
        a_rows.append(a_blk)
    return jnp.concatenate(a_rows, axis=0)


def _gla_kernel(q_ref, k_ref, v_ref, r_ref, glr_ref, w2_ref, gb_ref, nw_ref, o_ref, st_ref, a_ref):
    c = pl.program_id(0)
    qc = GLA_CHUNK
    dk, dv = GLA_HEAD_DK, GLA_HEAD_DV
    heads = range(GLA_HEADS)
    nt = (((1,), (1,)), ((), ()))

    @pl.when(c == 0)
    def _():
        st_ref[...] = jnp.zeros_like(st_ref)

    pre = jnp.dot(glr_ref[...], w2_ref[...], precision=HIGHEST, preferred_element_type=F32) + gb_ref[...]
    g = _log_sigmoid(pre) * (1.0 / GLA_GATE_TAU)
    row = lax.broadcasted_iota(jnp.int32, (qc, 1), 0) + c * qc
    g = jnp.where(row >= FRONT_PAD, g, 0.0)
    ri = lax.broadcasted_iota(jnp.int32, (qc, qc), 0)
    ci = lax.broadcasted_iota(jnp.int32, (qc, qc), 1)
    causal = ci <= ri
    gc = jnp.dot(causal.astype(F32), g, precision=HIGHEST, preferred_element_type=F32)
    g_last = gc[qc - 1:qc, :]

    q = q_ref[...].astype(F32) * (GLA_HEAD_DK ** -0.5)
    k = k_ref[...].astype(F32)
    q_dec = (q * jnp.exp(gc)).astype(BF16)
    k_end = (k * jnp.exp(g_last - gc)).astype(BF16)
    safe = jnp.max(-g_last) <= GLA_SAFE_DECAY

    @pl.when(safe)
    def _():
        k_inv = (k * jnp.exp(-gc)).astype(BF16)
        for h in heads:
            s = lax.dot_general(q_dec[:, h * dk:(h + 1) * dk], k_inv[:, h * dk:(h + 1) * dk], nt,
                                preferred_element_type=F32)
            a_ref[h] = jnp.where(causal, s, 0.0)

    @pl.when(jnp.logical_not(safe))
    def _():
        for h in heads:
            a_ref[h] = _gla_scores_blocked(q[:, h * dk:(h + 1) * dk], k[:, h * dk:(h + 1) * dk],
                                           gc[:, h * dk:(h + 1) * dk])

    vb = v_ref[...]
    decay = jnp.exp(g_last)
    outs = []
    for h in heads:
        st = st_ref[h]
        v_h = vb[:, h * dv:(h + 1) * dv]
        o = jnp.dot(a_ref[h].astype(BF16), v_h, preferred_element_type=F32)
        o = o + lax.dot_general(q_dec[:, h * dk:(h + 1) * dk], st.astype(BF16), nt, preferred_element_type=F32)
        st_ref[h] = st * decay[:, h * dk:(h + 1) * dk] + lax.dot_general(
            v_h, k_end[:, h * dk:(h + 1) * dk], (((0,), (0,)), ((), ())), preferred_element_type=F32)
        outs.append(o)
    for h in heads:
        o = outs[h]
        on = o * lax.rsqrt(jnp.mean(o * o, axis=-1, keepdims=True) + RMS_EPS) * nw_ref[:, h * dv:(h + 1) * dv]
        r = r_ref[:, h * dv:(h + 1) * dv].astype(F32)
        o_ref[:, h * dv:(h + 1) * dv] = (on * (r * _sigmoid(r))).astype(o_ref.dtype)


def _gla_mixer(u_qkv, u_r, u_small, gate_w2, gate_b, norm_w):
    rows = u_qkv.shape[0]
    qc = GLA_CHUNK
    dk, dv = GLA_HEAD_DK, GLA_HEAD_DV
    wk, wv = GLA_HEADS * dk, GLA_HEADS * dv
    glr_lane = HYB_GLR_COL % LANES
    w2 = jnp.pad(gate_w2, ((glr_lane, LANES - GLA_GATE_RANK - glr_lane), (0, 0)))
    return pl.pallas_call(
        _gla_kernel,
        grid=(rows // qc,),
        in_specs=[pl.BlockSpec((qc, wk), lambda c: (c, 0)),
                  pl.BlockSpec((qc, wk), lambda c: (c, 1)),
                  pl.BlockSpec((qc, wv), lambda c: (c, 1)),
                  pl.BlockSpec((qc, wv), lambda c: (c, 0)),
                  pl.BlockSpec((qc, LANES), lambda c: (c, 1)),
                  pl.BlockSpec((LANES, wk), lambda c: (0, 0)),
                  pl.BlockSpec((1, wk), lambda c: (0, 0)),
                  pl.BlockSpec((1, wv), lambda c: (0, 0))],
        out_specs=pl.BlockSpec((qc, wv), lambda c: (c, 0)),
        out_shape=jax.ShapeDtypeStruct((rows, wv), BF16),
        scratch_shapes=[pltpu.VMEM((GLA_HEADS, dv, dk), F32), pltpu.VMEM((GLA_HEADS, qc, qc), F32)],
        compiler_params=_params("arbitrary"),
        name="gla",
    )(u_qkv, u_qkv, u_qkv, u_r, u_small, w2, gate_b.reshape(1, -1), norm_w.reshape(1, -1))


def _rope_kernel(freq_ref, cm_ref, sp_ref, sm_ref, *, tm):
    i = pl.program_id(0)
    row = lax.broadcasted_iota(jnp.int32, (tm, LANES), 0) + i * tm
    lane = lax.broadcasted_iota(jnp.int32, (tm, LANES), 1) % SWA_HEAD_DIM
    ang = (row - FRONT_PAD).astype(F32) * freq_ref[...]
    cos = jnp.cos(ang)
    sin = jnp.sin(ang)
    half = ROPE_DIM // 2
    cm_ref[...] = cos
    sp_ref[...] = jnp.where(lane < half, -sin, 0.0)
    sm_ref[...] = jnp.where((lane >= half) & (lane < ROPE_DIM), sin, 0.0)


def _rope_tables(rows):
    half = ROPE_DIM // 2
    inv_freq = ROPE_THETA ** (-jnp.arange(half, dtype=F32) / half)
    per_head = jnp.concatenate([inv_freq, inv_freq, jnp.zeros((SWA_HEAD_DIM - ROPE_DIM,), F32)])
    freq = jnp.tile(per_head, LANES // SWA_HEAD_DIM).reshape(1, LANES)
    tm = _row_tile(rows)
    shp = jax.ShapeDtypeStruct((rows, LANES), F32)
    spec = pl.BlockSpec((tm, LANES), lambda i: (i, 0))
    return pl.pallas_call(
        functools.partial(_rope_kernel, tm=tm),
        grid=(rows // tm,),
        in_specs=[pl.BlockSpec((1, LANES), lambda i: (0, 0))],
        out_specs=[spec, spec, spec],
        out_shape=[shp, shp, shp],
        compiler_params=_params("arbitrary"),
        name="rope_tables",
    )(freq)


def _swa_kernel(sink_ref, q_ref, prev_ref, cur_ref, meta_ref, o_ref):
    n = pl.program_id(0)
    w = SWA_WINDOW
    meta_lo = FRONT_PAD
    kvw = SWA_KV_HEADS * LANES
    nt = (((1,), (1,)), ((), ()))
    lane = lax.broadcasted_iota(jnp.int32, (1, LANES), 1)
    low = lane < SWA_HEAD_DIM
    i = lax.broadcasted_iota(jnp.int32, (w, 1), 0)
    on_cur = lane <= i
    valid_band = (on_cur & ((n >= 1) | (lane >= meta_lo))) | (jnp.logical_not(on_cur) & (n >= 2))
    valid_meta = (lane >= meta_lo) & (n >= 1)
    zero = jnp.zeros((w, LANES), q_ref.dtype)
    kv_heads = range(SWA_KV_HEADS)

    def scores(g):
        qa = q_ref[:, 2 * g * LANES:(2 * g + 1) * LANES]
        qb = q_ref[:, (2 * g + 1) * LANES:(2 * g + 2) * LANES]
        qs = jnp.concatenate([jnp.where(low, qa, zero), jnp.where(low, zero, qa),
                              jnp.where(low, qb, zero), jnp.where(low, zero, qb)], axis=0)
        kcols = slice(g * LANES, (g + 1) * LANES)
        kk = jnp.concatenate([prev_ref[:, kcols], cur_ref[:, kcols], meta_ref[:, kcols]], axis=0)
        return lax.dot_general(qs, kk, nt, preferred_element_type=F32)

    s_next = scores(0)
    for g in kv_heads:
        s_all = s_next
        if g + 1 < SWA_KV_HEADS:
            s_next = scores(g + 1)
        probs, denoms = [], []
        for h in range(SWA_GROUP):
            s = s_all[h * w:(h + 1) * w]
            s_band = jnp.where(valid_band, jnp.where(on_cur, s[:, w:2 * w], s[:, :w]), -jnp.inf)
            s_meta = jnp.where(valid_meta, s[:, 2 * w:], -jnp.inf)
            sink = sink_ref[g * SWA_GROUP + h]
            m = jnp.maximum(jnp.max(jnp.maximum(s_band, s_meta), axis=-1, keepdims=True), sink)
            p_band = jnp.exp(s_band - m)
            p_meta = jnp.exp(s_meta - m)
            denoms.append(jnp.sum(p_band + p_meta, axis=-1, keepdims=True) + jnp.exp(sink - m))
            probs.append(jnp.concatenate([jnp.where(on_cur, 0.0, p_band), jnp.where(on_cur, p_band, 0.0), p_meta],
                                         axis=1).astype(BF16))
        vcols = slice(kvw + g * LANES, kvw + (g + 1) * LANES)
        vv = jnp.concatenate([prev_ref[:, vcols], cur_ref[:, vcols], meta_ref[:, vcols]], axis=0)
        o = jnp.dot(jnp.concatenate(probs, axis=0), vv, preferred_element_type=F32)
        o = [o[h * w:(h + 1) * w] / denoms[h] for h in range(SWA_GROUP)]
        oa = jnp.where(low, o[0], o[1])
        ob = jnp.where(low, o[2], o[3])
        o_ref[:, 2 * g * LANES:(2 * g + 2) * LANES] = jnp.concatenate([oa, ob], axis=1).astype(o_ref.dtype)


def _swa_attention(q, kv, sinks):
    rows, qw = q.shape
    w = SWA_WINDOW
    kvw = kv.shape[1]
    return pl.pallas_call(
        _swa_kernel,
        grid=(rows // w,),
        in_specs=[pl.BlockSpec(memory_space=pltpu.SMEM),
                  pl.BlockSpec((w, qw), lambda n: (n, 0)),
                  pl.BlockSpec((w, kvw), lambda n: (jnp.maximum(n - 1, 0), 0)),
                  pl.BlockSpec((w, kvw), lambda n: (n, 0)),
                  pl.BlockSpec((w, kvw), lambda n: (0, 0))],
        out_specs=pl.BlockSpec((w, qw), lambda n: (n, 0)),
        out_shape=jax.ShapeDtypeStruct((rows, qw), BF16),
        compiler_params=_params("arbitrary"),
        name="swa",
    )(sinks, q, kv, kv, kv)


def _embed_kernel(x_ref, meta_ref, hf_ref, hb_ref, *, tm):
    i = pl.program_id(0)
    head = FRONT_PAD + N_META

    @pl.when(i == 0)
    def _():
        top = jnp.concatenate([jnp.zeros((FRONT_PAD, D_MODEL), F32), meta_ref[...]], axis=0)
        hf_ref[0:head, :] = top
        hb_ref[0:head, :] = top.astype(BF16)
        body = x_ref[0:tm - head, :]
        hf_ref[head:tm, :] = body
        hb_ref[head:tm, :] = body.astype(BF16)

    @pl.when(i != 0)
    def _():
        hf_ref[...] = x_ref[...]
        hb_ref[...] = x_ref[...].astype(BF16)


def _embed(x, meta):
    seq, d = x.shape
    head = FRONT_PAD + N_META
    rows = head + seq
    tm = _row_tile(rows)
    out = pl.BlockSpec((tm, d), lambda i: (i, 0))
    x_rows = lambda i: pl.multiple_of(jnp.maximum(i * tm - head, 0), LANES)
    return pl.pallas_call(
        functools.partial(_embed_kernel, tm=tm),
        grid=(rows // tm,),
        in_specs=[pl.BlockSpec((pl.Element(tm), pl.Element(d)), lambda i: (x_rows(i), 0)),
                  pl.BlockSpec((N_META, d), lambda i: (0, 0))],
        out_specs=[out, out],
        out_shape=[jax.ShapeDtypeStruct((rows, d), F32), jax.ShapeDtypeStruct((rows, d), BF16)],
        compiler_params=_params("arbitrary"),
        name="embed",
    )(x, meta)


def _pad_halves(t):
    pad = lambda a: jnp.pad(a, ((0, 0), (0, D_FF_PAD - D_FF)))
    return jnp.concatenate([pad(t[:, :D_FF]), pad(t[:, D_FF:])], axis=1)


def _trunk(x, meta_tokens, hyb_w_in, hyb_conv_w, hyb_conv_b, ssd_dt_bias, ssd_a_log, ssd_d, ssd_norm_w,
           gla_gate_w2, gla_gate_b, gla_norm_w, hyb_w_out, swa_w_qkv, swa_sinks, swa_w_out,
           ffn_w_up, ffn_conv_w, ffn_conv_b, ffn_w_down, ln_mix_g, ln_mix_b, ln_ffn_g, ln_ffn_b):
    seq = x.shape[0]
    rows = FRONT_PAD + N_META + seq
    h, hb = _embed(x, meta_tokens.astype(F32))
    tables = _rope_tables(rows)
    hyb_w_in = jnp.pad(hyb_w_in, ((0, 0), (0, 0), (0, -hyb_w_in.shape[2] % LANES))).astype(BF16)
    for layer in range(DEPTH):
        j = layer // 2
        if layer % 2 == 0:
            u_zx = _project(hb, hyb_w_in, j, first_block=0, block_stride=8, tn=1024, n_tiles=6)
            u_qkv = _project(hb, hyb_w_in, j, first_block=HYB_QKV_COL // LANES, block_stride=8, tn=1024, n_tiles=4,
                             shift=HYB_QKV_COL % LANES)
            u_r = _project(hb, hyb_w_in, j, first_block=HYB_R_COL // LANES, block_stride=8, tn=1024, n_tiles=2,
                           shift=HYB_R_COL % LANES)
            u_small = _project(hb, hyb_w_in, j, first_block=0, block_stride=0, tn=2 * LANES, n_tiles=1,
                               blocks=[HYB_DT_COL // LANES, HYB_GLR_COL // LANES], out_dtype=F32)
            y_ssd = _ssd_mixer(u_zx, u_small, hyb_conv_w[j], hyb_conv_b[j], ssd_dt_bias[j], ssd_a_log[j],
                               ssd_d[j], ssd_norm_w[j])
            y_gla = _gla_mixer(u_qkv, u_r, u_small, gla_gate_w2[j], gla_gate_b[j], gla_norm_w[j])
            h, hb = _matmul_residual_ln([y_ssd, y_gla], [y_ssd.shape[1], y_gla.shape[1]], hyb_w_out, j, h,
                                        ln_mix_g[layer], ln_mix_b[layer], nchunks=8)
        else:
            q = _project(hb, swa_w_qkv, j, first_block=0, block_stride=8, tn=1024, n_tiles=2,
                         scale=SWA_HEAD_DIM ** -0.5, mode="rope", tables=tables)
            kv = _project(hb, swa_w_qkv, j, first_block=SWA_Q_HEADS * SWA_HEAD_DIM // LANES, block_stride=4, tn=512,
                          n_tiles=2, mode="kv", tables=tables)
            attn = _swa_attention(q, kv, swa_sinks[j])
            h, hb = _matmul_residual_ln([attn], [attn.shape[1]], swa_w_out, j, h, ln_mix_g[layer], ln_mix_b[layer],
                                        nchunks=4)
        act = _ffn_up(hb, ffn_w_up, layer, _pad_halves(ffn_conv_w[layer]),
                      _pad_halves(ffn_conv_b[layer].reshape(1, -1)))
        h, hb = _matmul_residual_ln([act], [D_FF], ffn_w_down, layer, h, ln_ffn_g[layer], ln_ffn_b[layer], nchunks=8)
    return h[FRONT_PAD + N_META:]


def kernel(x, meta_tokens, hyb_w_in, hyb_conv_w, hyb_conv_b, ssd_dt_bias, ssd_a_log, ssd_d, ssd_norm_w,
           gla_gate_w2, gla_gate_b, gla_norm_w, hyb_w_out, swa_w_qkv, swa_sinks, swa_w_out,
           ffn_w_up, ffn_conv_w, ffn_conv_b, ffn_w_down, ln_mix_g, ln_mix_b, ln_ffn_g, ln_ffn_b):
    params = (meta_tokens, hyb_w_in, hyb_conv_w, hyb_conv_b, ssd_dt_bias, ssd_a_log, ssd_d, ssd_norm_w,
              gla_gate_w2, gla_gate_b, gla_norm_w, hyb_w_out, swa_w_qkv, swa_sinks, swa_w_out,
              ffn_w_up, ffn_conv_w, ffn_conv_b, ffn_w_down, ln_mix_g, ln_mix_b, ln_ffn_g, ln_ffn_b)
    return jnp.stack([_trunk(x[b], *params) for b in range(x.shape[0])], axis=0)
```

```python
import functools

import jax
import jax.numpy as jnp
from jax import lax
from jax.experimental import pallas as pl
from jax.experimental.pallas import tpu as pltpu

F32 = jnp.float32
BF16 = jnp.bfloat16
HIGHEST = lax.Precision.HIGHEST

D_MODEL = 2048
DEPTH = 4
N_META = 16
LN_EPS = 1e-5
RMS_EPS = 1e-6
DEEPNORM_ALPHA = (2.0 * DEPTH) ** 0.25

SSD_HEAD_DIM = 64
SSD_HEADS = 32
SSD_GROUPS = 8
SSD_HPG = 4
SSD_STATE = 128
SSD_CONV = 4
SSD_CHUNK = 128
SSD_GROUP_W = SSD_HPG * SSD_HEAD_DIM

GLA_HEADS = 4
GLA_HEAD_DK = 256
GLA_HEAD_DV = 512
GLA_GATE_RANK = 16
GLA_GATE_TAU = 16.0
GLA_CHUNK = 128
GLA_SUB = 16
GLA_SAFE_DECAY = 60.0

SWA_HEAD_DIM = 64
SWA_Q_HEADS = 32
SWA_KV_HEADS = 8
SWA_GROUP = 4
SWA_WINDOW = 128
ROPE_THETA = 500000.0
ROPE_DIM = 16

D_FF = 5504
FFN_CONV = 3

LANES = 128
SUBLANES = 8
FRONT_PAD = SSD_CHUNK - N_META
D_FF_PAD = 5632
FFN_TN = 512
FFN_SUBTILES = 4
PROJ_SUBTILES = 4
LN_ROW_TILE = 320
LN_SUBTILES = 2
LN_WEIGHT_COPY_BUDGET = 20 * 1024 * 1024
VMEM_LIMIT = 56 * 1024 * 1024

HYB_DT_COL = D_MODEL + (D_MODEL + 2 * SSD_GROUPS * SSD_STATE)
HYB_QKV_COL = HYB_DT_COL + SSD_HEADS
HYB_GLR_COL = HYB_QKV_COL + GLA_HEADS * (2 * GLA_HEAD_DK + GLA_HEAD_DV)
HYB_R_COL = HYB_GLR_COL + GLA_GATE_RANK


def _row_tile(rows):
    for t in (640, 512, 384, 256, 128):
        if rows % t == 0:
            return t
    raise ValueError(f"row count {rows} is not a multiple of 128")


def _row_tile_big(rows):
    return 1664 if rows % 1664 == 0 else _row_tile(rows)


def _params(*sem):
    return pltpu.CompilerParams(dimension_semantics=sem, vmem_limit_bytes=VMEM_LIMIT)


def _sigmoid(x):
    return 1.0 / (1.0 + jnp.exp(-x))


def _softplus(x):
    return jnp.maximum(x, 0.0) + jnp.log(1.0 + jnp.exp(-jnp.abs(x)))


def _log_sigmoid(x):
    return jnp.minimum(x, 0.0) - jnp.log(1.0 + jnp.exp(-jnp.abs(x)))


def _causal_conv_rows(y, carry, w_ref, b_row, taps):
    top = jnp.concatenate([carry, y[0:SUBLANES]], axis=0)
    w_last = w_ref[taps - 1:taps, :]
    acc = b_row + w_last * y
    acc_top = b_row + w_last * y[0:SUBLANES]
    for s in range(1, taps):
        wk = w_ref[taps - 1 - s:taps - s, :]
        acc = acc + wk * pltpu.roll(y, s, 0)
        acc_top = acc_top + wk * pltpu.roll(top, s, 0)[SUBLANES:2 * SUBLANES]
    return jnp.concatenate([acc_top, acc[SUBLANES:]], axis=0)


def _assemble_weight(w_refs, wb_ref, shift, scale):
    k, tn = wb_ref.shape
    chunk = 256
    for r in range(0, k, chunk):
        w = jnp.concatenate([wr[r:r + chunk, :] for wr in w_refs], axis=1)
        if shift:
            w = pltpu.roll(w.astype(F32), w.shape[1] - shift, 1)
        w = w[:, :tn]
        if scale is not None:
            w = w * scale
        wb_ref[r:r + chunk, :] = w.astype(BF16)


def _weight_block_specs(k, layer, block_fns):
    return [pl.BlockSpec((None, k, LANES), functools.partial(lambda *ids, fn: (layer, 0, fn(*ids)), fn=fn))
            for fn in block_fns]


def _rotate_heads(y, cm, sp, sm):
    half = ROPE_DIM // 2
    out = []
    for c in range(y.shape[1] // LANES):
        yc = y[:, c * LANES:(c + 1) * LANES]
        out.append(yc * cm + pltpu.roll(yc, LANES - half, 1) * sp + pltpu.roll(yc, half, 1) * sm)
    return jnp.concatenate(out, axis=1)


def _duplicate_heads(y):
    low = lax.broadcasted_iota(jnp.int32, (1, LANES), 1) < SWA_HEAD_DIM
    out = []
    for c in range(y.shape[1] // LANES):
        yc = y[:, c * LANES:(c + 1) * LANES]
        rolled = pltpu.roll(yc, SWA_HEAD_DIM, 1)
        out.append(jnp.where(low, yc, rolled))
        out.append(jnp.where(low, rolled, yc))
    return jnp.concatenate(out, axis=1)


def _proj_kernel(*refs, nblk, shift, scale, mode):
    x_ref = refs[0]
    w_refs = refs[1:1 + nblk]
    rest = refs[1 + nblk:]
    if mode != "plain":
        cm_ref, sp_ref, sm_ref = rest[:3]
        rest = rest[3:]
    o_ref, wb_ref = rest
    j = pl.program_id(0)
    i = pl.program_id(1)

    @pl.when(i == 0)
    def _():
        _assemble_weight(w_refs, wb_ref, shift, scale)

    if mode == "plain":
        o_ref[...] = jnp.dot(x_ref[...], wb_ref[...], preferred_element_type=F32).astype(o_ref.dtype)
        return
    tm = x_ref.shape[0]
    ts = tm // PROJ_SUBTILES
    ys = [jnp.dot(x_ref[s * ts:(s + 1) * ts, :], wb_ref[...], preferred_element_type=F32)
          for s in range(PROJ_SUBTILES)]
    rotate = (j == 0) if mode == "kv" else True
    for s, y in enumerate(ys):
        rows = slice(s * ts, (s + 1) * ts)
        cm = jnp.where(rotate, cm_ref[rows, :], 1.0)
        sp = jnp.where(rotate, sp_ref[rows, :], 0.0)
        sm = jnp.where(rotate, sm_ref[rows, :], 0.0)
        y = _rotate_heads(y, cm, sp, sm)
        if mode == "kv":
            y = _duplicate_heads(y)
        o_ref[rows, :] = y.astype(o_ref.dtype)


def _project(xb, w, layer, *, first_block, block_stride, tn, n_tiles, shift=0, scale=None, mode="plain",
             tables=None, out_dtype=BF16, blocks=None):
    m, k = xb.shape
    tm = _row_tile_big(m)
    nb = tn // LANES
    tn_out = 2 * tn if mode == "kv" else tn
    if blocks is not None:
        assert n_tiles == 1 and shift == 0 and len(blocks) == nb
        nblk = nb
        w_specs = _weight_block_specs(k, layer, [functools.partial(lambda j, i, b: b, b=b) for b in blocks])
    elif shift == 0 and block_stride == nb and first_block % nb == 0:
        nblk = 1
        w_specs = [pl.BlockSpec((None, k, tn), lambda j, i: (layer, 0, first_block // nb + j))]
    else:
        nblk = nb + (1 if shift else 0)
        w_specs = _weight_block_specs(
            k, layer, [functools.partial(lambda j, i, b: first_block + j * block_stride + b, b=b) for b in range(nblk)])
    in_specs = [pl.BlockSpec((tm, k), lambda j, i: (i, 0))] + w_specs
    args = [xb] + [w] * nblk
    if mode != "plain":
        in_specs += [pl.BlockSpec((tm, LANES), lambda j, i: (i, 0))] * 3
        args += list(tables)
    return pl.pallas_call(
        functools.partial(_proj_kernel, nblk=nblk, shift=shift, scale=scale, mode=mode),
        grid=(n_tiles, m // tm),
        in_specs=in_specs,
        out_specs=pl.BlockSpec((tm, tn_out), lambda j, i: (i, j)),
        out_shape=jax.ShapeDtypeStruct((m, n_tiles * tn_out), out_dtype),
        scratch_shapes=[pltpu.VMEM((k, tn), BF16)],
        compiler_params=_params("arbitrary", "arbitrary"),
        name="proj_" + mode,
    )(*args)


def _mm_ln_kernel(*refs, nsrc, tm, subtiles, nchunks, ck):
    x_refs = refs[:nsrc]
    w_ref, res_ref, g_ref, b_ref, of_ref, ob_ref, wb_ref = refs[nsrc:]
    step = pl.program_id(0)

    @pl.when(step < nchunks)
    def _():
        wb_ref[pl.ds(pl.multiple_of(step * ck, ck), ck), :] = w_ref[...].astype(BF16)

    @pl.when(step >= nchunks)
    def _():
        i = step - nchunks
        ts = tm // subtiles
        sums = []
        for s in range(subtiles):
            rows = slice(s * ts, (s + 1) * ts)
            acc, k0 = None, 0
            for x_ref in x_refs:
                kw = x_ref.shape[1]
                part = jnp.dot(x_ref[rows, :], wb_ref[k0:k0 + kw, :], preferred_element_type=F32)
                acc = part if acc is None else acc + part
                k0 += kw
            sums.append(acc)
        for s, acc in enumerate(sums):
            rows = slice(s * ts, (s + 1) * ts)
            t = DEEPNORM_ALPHA * res_ref[rows, :] + acc
            mu = jnp.mean(t, axis=-1, keepdims=True)
            d = t - mu
            var = jnp.mean(d * d, axis=-1, keepdims=True)
            y = d * lax.rsqrt(var + LN_EPS) * g_ref[...] + b_ref[...]
            row = lax.broadcasted_iota(jnp.int32, (ts, 1), 0) + (i * tm + s * ts)
            y = jnp.where(row >= FRONT_PAD, y, 0.0)
            of_ref[rows, :] = y
            ob_ref[rows, :] = y.astype(BF16)


def _matmul_residual_ln(xs, k_widths, w, layer, res, gamma, beta, nchunks):
    m = xs[0].shape[0]
    _, kdim, n = w.shape
    assert sum(k_widths) == kdim and m % LN_ROW_TILE == 0 and kdim % (16 * nchunks) == 0
    tm = LN_ROW_TILE
    ck = kdim // nchunks
    subtiles = LN_SUBTILES if 2 * kdim * n * 2 <= LN_WEIGHT_COPY_BUDGET else 1
    tile = lambda s: jnp.maximum(s - nchunks, 0)
    x_specs = [pl.BlockSpec((tm, kw), lambda s: (tile(s), 0)) for kw in k_widths]
    return pl.pallas_call(
        functools.partial(_mm_ln_kernel, nsrc=len(xs), tm=tm, subtiles=subtiles, nchunks=nchunks, ck=ck),
        grid=(nchunks + m // tm,),
        in_specs=x_specs + [pl.BlockSpec((None, ck, n), lambda s: (layer, jnp.minimum(s, nchunks - 1), 0)),
                            pl.BlockSpec((tm, n), lambda s: (tile(s), 0)),
                            pl.BlockSpec((1, n), lambda s: (0, 0)),
                            pl.BlockSpec((1, n), lambda s: (0, 0))],
        out_specs=[pl.BlockSpec((tm, n), lambda s: (tile(s), 0)),
                   pl.BlockSpec((tm, n), lambda s: (tile(s), 0))],
        out_shape=[jax.ShapeDtypeStruct((m, n), F32), jax.ShapeDtypeStruct((m, n), BF16)],
        scratch_shapes=[pltpu.VMEM((kdim, n), BF16)],
        compiler_params=_params("arbitrary"),
        name="proj_ln",
    )(*xs, w, res, gamma.reshape(1, n), beta.reshape(1, n))


def _ffn_up_kernel(*refs, nblk, tm, tn):
    x_ref = refs[0]
    w_refs = refs[1:1 + nblk]
    cw_g, cw_v, cb_g, cb_v, o_ref, wb_ref, cg_ref, cv_ref = refs[1 + nblk:]
    j = pl.program_id(0)
    i = pl.program_id(1)

    @pl.when(i == 0)
    def _():
        _assemble_weight(w_refs, wb_ref, 0, None)
        cg_ref[...] = jnp.zeros_like(cg_ref)
        cv_ref[...] = jnp.zeros_like(cv_ref)

    col = lax.broadcasted_iota(jnp.int32, (1, tn), 1) + j * tn
    ts = tm // FFN_SUBTILES
    ys = []
    for s in range(FFN_SUBTILES):
        y = jnp.dot(x_ref[s * ts:(s + 1) * ts, :], wb_ref[...], preferred_element_type=F32)
        ys.append((y[:, :tn], y[:, tn:]))
    carry_g, carry_v = cg_ref[...], cv_ref[...]
    for s, (yg, yv) in enumerate(ys):
        hg = _causal_conv_rows(yg, carry_g, cw_g, cb_g[...], FFN_CONV)
        hv = _causal_conv_rows(yv, carry_v, cw_v, cb_v[...], FFN_CONV)
        carry_g, carry_v = yg[ts - SUBLANES:ts], yv[ts - SUBLANES:ts]
        o_ref[s * ts:(s + 1) * ts, :] = jnp.where(col < D_FF, hg * _sigmoid(hg) * hv, 0.0).astype(o_ref.dtype)
    cg_ref[...] = carry_g
    cv_ref[...] = carry_v


def _ffn_up(xb, w_up, layer, conv_w, conv_b):
    m, k = xb.shape
    tm = _row_tile_big(m)
    assert (tm // FFN_SUBTILES) % 16 == 0, tm
    tn = FFN_TN
    nj = D_FF_PAD // tn
    nb = tn // LANES
    val0 = D_FF // LANES
    last_blk = 2 * D_FF // LANES - 1
    w_specs = [pl.BlockSpec((None, k, tn), lambda j, i: (layer, 0, j))] + _weight_block_specs(
        k, layer, [functools.partial(lambda j, i, b: jnp.minimum(val0 + j * nb + b, last_blk), b=b) for b in range(nb)])
    nblk = 1 + nb
    return pl.pallas_call(
        functools.partial(_ffn_up_kernel, nblk=nblk, tm=tm, tn=tn),
        grid=(nj, m // tm),
        in_specs=[pl.BlockSpec((tm, k), lambda j, i: (i, 0))] + w_specs
        + [pl.BlockSpec((FFN_CONV, tn), lambda j, i: (0, j)),
           pl.BlockSpec((FFN_CONV, tn), lambda j, i: (0, nj + j)),
           pl.BlockSpec((1, tn), lambda j, i: (0, j)),
           pl.BlockSpec((1, tn), lambda j, i: (0, nj + j))],
        out_specs=pl.BlockSpec((tm, tn), lambda j, i: (i, j)),
        out_shape=jax.ShapeDtypeStruct((m, D_FF_PAD), BF16),
        scratch_shapes=[pltpu.VMEM((k, 2 * tn), BF16), pltpu.VMEM((SUBLANES, tn), F32),
                        pltpu.VMEM((SUBLANES, tn), F32)],
        compiler_params=_params("arbitrary", "arbitrary"),
        name="ffn_up",
    )(xb, *([w_up] * nblk), conv_w, conv_w, conv_b, conv_b)


def _expand_heads(x, rows, g):
    low = lax.broadcasted_iota(jnp.int32, (1, LANES), 1) < SSD_HEAD_DIM
    h0 = g * SSD_HPG
    b = [jnp.broadcast_to(x[:, h0 + h:h0 + h + 1], (rows, LANES)) for h in range(SSD_HPG)]
    return jnp.concatenate([jnp.where(low, b[0], b[1]), jnp.where(low, b[2], b[3])], axis=1)


def _ssd_kernel(z_ref, x_ref, bc_ref, dt_ref, cwx_ref, cwbc_ref, cbx_ref, cbbc_ref, dtb_ref, alog_ref, dsk_ref,
                nw_ref, o_ref, st_ref, cx_ref, cbc_ref):
    c = pl.program_id(0)
    q = SSD_CHUNK
    gw = SSD_GROUP_W
    n = SSD_STATE
    groups = range(SSD_GROUPS)

    @pl.when(c == 0)
    def _():
        st_ref[...] = jnp.zeros_like(st_ref)
        cx_ref[...] = jnp.zeros_like(cx_ref)
        cbc_ref[...] = jnp.zeros_like(cbc_ref)

    row = lax.broadcasted_iota(jnp.int32, (q, 1), 0) + c * q
    lane = lax.broadcasted_iota(jnp.int32, (1, LANES), 1)
    dt = _softplus(dt_ref[...] + dtb_ref[...])
    dt = jnp.where((row >= FRONT_PAD) & (lane < SSD_HEADS), dt, 0.0)
    a = -jnp.exp(alog_ref[...])
    ri = lax.broadcasted_iota(jnp.int32, (q, q), 0)
    ci = lax.broadcasted_iota(jnp.int32, (q, q), 1)
    causal = ci <= ri
    cs = jnp.dot(causal.astype(F32), dt * a, precision=HIGHEST, preferred_element_type=F32)
    cs_t = cs.T
    cs_last = cs[q - 1:q, :]
    from_start = jnp.exp(cs)
    to_end = jnp.exp(cs_last - cs)
    total = jnp.exp(cs_last)

    def conv_silu(raw_ref, carry_ref, w_ref, b_ref, lo, width):
        raw = raw_ref[:, lo:lo + width].astype(F32)
        y = _causal_conv_rows(raw, carry_ref[:, lo:lo + width], w_ref.at[:, lo:lo + width], b_ref[:, lo:lo + width],
                              SSD_CONV)
        carry_ref[:, lo:lo + width] = raw[q - SUBLANES:q]
        return y * _sigmoid(y)

    xs = [conv_silu(x_ref, cx_ref, cwx_ref, cbx_ref, g * gw, gw) for g in groups]
    bmb = [conv_silu(bc_ref, cbc_ref, cwbc_ref, cbbc_ref, g * n, n).astype(BF16) for g in groups]
    cmb = [conv_silu(bc_ref, cbc_ref, cwbc_ref, cbbc_ref, SSD_GROUPS * n + g * n, n).astype(BF16) for g in groups]
    cb = [lax.dot_general(cmb[g], bmb[g], (((1,), (1,)), ((), ())), preferred_element_type=F32)
          for g in groups]

    lane_w = lax.broadcasted_iota(jnp.int32, (1, gw), 1)
    xd = [xs[g] * _expand_heads(dt, q, g) for g in groups]
    ys = []
    for g in groups:
        decayed = []
        for h in range(SSD_HPG):
            hh = g * SSD_HPG + h
            seg = cs[:, hh:hh + 1] - cs_t[hh:hh + 1, :]
            decayed.append((cb[g] * jnp.exp(jnp.where(causal, seg, -jnp.inf))).astype(BF16))
        xdb = xd[g].astype(BF16)
        x_heads = [jnp.where((lane_w >= h * SSD_HEAD_DIM) & (lane_w < (h + 1) * SSD_HEAD_DIM), xdb,
                             jnp.zeros_like(xdb)) for h in range(SSD_HPG)]
        ys.append(jnp.dot(jnp.concatenate(decayed, axis=1), jnp.concatenate(x_heads, axis=0),
                          preferred_element_type=F32))
    for g in groups:
        st = st_ref[g]
        ys[g] = ys[g] + (jnp.dot(cmb[g], st.astype(BF16), preferred_element_type=F32)
                         * _expand_heads(from_start, q, g))
        st_ref[g] = (st * _expand_heads(total, 1, g)
                     + lax.dot_general(bmb[g], (xd[g] * _expand_heads(to_end, q, g)).astype(BF16),
                                       (((0,), (0,)), ((), ())), preferred_element_type=F32))
    for g in groups:
        cols = slice(g * gw, (g + 1) * gw)
        y = ys[g] + xs[g] * dsk_ref[:, cols]
        z = z_ref[:, cols].astype(F32)
        yg = y * (z * _sigmoid(z))
        yn = yg * lax.rsqrt(jnp.mean(yg * yg, axis=-1, keepdims=True) + RMS_EPS) * nw_ref[:, cols]
        o_ref[:, cols] = yn.astype(o_ref.dtype)


def _ssd_mixer(u_zx, u_small, conv_w, conv_b, dt_bias, a_log, d_skip, norm_w):
    rows = u_zx.shape[0]
    q = SSD_CHUNK
    d = SSD_GROUPS * SSD_GROUP_W
    pad_l = lambda v: jnp.pad(v.reshape(1, SSD_HEADS), ((0, 0), (0, LANES - SSD_HEADS)))
    whole = lambda r, w, j: pl.BlockSpec((r, w), lambda c: (0, j))
    return pl.pallas_call(
        _ssd_kernel,
        grid=(rows // q,),
        in_specs=[pl.BlockSpec((q, d), lambda c: (c, 0)),
                  pl.BlockSpec((q, d), lambda c: (c, 1)),
                  pl.BlockSpec((q, d), lambda c: (c, 2)),
                  pl.BlockSpec((q, LANES), lambda c: (c, 0)),
                  whole(SSD_CONV, d, 0), whole(SSD_CONV, d, 1), whole(1, d, 0), whole(1, d, 1),
                  whole(1, LANES, 0), whole(1, LANES, 0), whole(1, d, 0), whole(1, d, 0)],
        out_specs=pl.BlockSpec((q, d), lambda c: (c, 0)),
        out_shape=jax.ShapeDtypeStruct((rows, d), BF16),
        scratch_shapes=[pltpu.VMEM((SSD_GROUPS, SSD_STATE, SSD_GROUP_W), F32), pltpu.VMEM((SUBLANES, d), F32),
                        pltpu.VMEM((SUBLANES, d), F32)],
        compiler_params=_params("arbitrary"),
        name="ssd",
    )(u_zx, u_zx, u_zx, u_small, conv_w, conv_w, conv_b.reshape(1, -1), conv_b.reshape(1, -1),
      pad_l(dt_bias), pad_l(a_log), jnp.repeat(d_skip, SSD_HEAD_DIM).reshape(1, d), norm_w.reshape(1, d))


def _gla_scores_blocked(q, k, gc):
    qc = q.shape[0]
    sub = GLA_SUB
    lane_j = lax.broadcasted_iota(jnp.int32, (sub, qc), 1)
    row_i = lax.broadcasted_iota(jnp.int32, (sub, 1), 0)
    a_rows = []
    for blk in range(qc // sub):
        lo = blk * sub
        q_b = q[lo:lo + sub]
        g_b = gc[lo:lo + sub]
        a_blk = jnp.zeros((sub, qc), F32)
        for j in range(sub):
            k_j = k[lo + j:lo + j + 1, :]
            g_j = gc[lo + j:lo + j + 1, :]
            s_j = jnp.sum(q_b * k_j * jnp.exp(jnp.minimum(g_b - g_j, 0.0)), axis=1, keepdims=True)
            a_blk = jnp.where(lane_j == lo + j, jnp.where(row_i >= j, s_j, 0.0), a_blk)
        if blk > 0:
            g_ref0 = gc[lo:lo + 1, :]
            q_t = (q_b * jnp.exp(g_b - g_ref0)).astype(BF16)
            k_t = (k * jnp.exp(jnp.minimum(g_ref0 - gc, 0.0))).astype(BF16)
            off = lax.dot_general(q_t, k_t, (((1,), (1,)), ((), ())), preferred_element_type=F32)
            a_blk = jnp.where(lane_j < lo, off, a_blk)
        a_rows.append(a_blk)
    return jnp.concatenate(a_rows, axis=0)


def _gla_kernel(q_ref, k_ref, v_ref, r_ref, glr_ref, w2_ref, gb_ref, nw_ref, o_ref, st_ref, a_ref):
    c = pl.program_id(0)
    qc = GLA_CHUNK
    dk, dv = GLA_HEAD_DK, GLA_HEAD_DV
    heads = range(GLA_HEADS)
    nt = (((1,), (1,)), ((), ()))

    @pl.when(c == 0)
    def _():
        st_ref[...] = jnp.zeros_like(st_ref)

    pre = jnp.dot(glr_ref[...], w2_ref[...], precision=HIGHEST, preferred_element_type=F32) + gb_ref[...]
    g = _log_sigmoid(pre) * (1.0 / GLA_GATE_TAU)
    row = lax.broadcasted_iota(jnp.int32, (qc, 1), 0) + c * qc
    g = jnp.where(row >= FRONT_PAD, g, 0.0)
    ri = lax.broadcasted_iota(jnp.int32, (qc, qc), 0)
    ci = lax.broadcasted_iota(jnp.int32, (qc, qc), 1)
    causal = ci <= ri
    gc = jnp.dot(causal.astype(F32), g, precision=HIGHEST, preferred_element_type=F32)
    g_last = gc[qc - 1:qc, :]

    q = q_ref[...].astype(F32) * (GLA_HEAD_DK ** -0.5)
    k = k_ref[...].astype(F32)
    q_dec = (q * jnp.exp(gc)).astype(BF16)
    k_end = (k * jnp.exp(g_last - gc)).astype(BF16)
    safe = jnp.max(-g_last) <= GLA_SAFE_DECAY

    @pl.when(safe)
    def _():
        k_inv = (k * jnp.exp(-gc)).astype(BF16)
        for h in heads:
            s = lax.dot_general(q_dec[:, h * dk:(h + 1) * dk], k_inv[:, h * dk:(h + 1) * dk], nt,
                                preferred_element_type=F32)
            a_ref[h] = jnp.where(causal, s, 0.0)

    @pl.when(jnp.logical_not(safe))
    def _():
        for h in heads:
            a_ref[h] = _gla_scores_blocked(q[:, h * dk:(h + 1) * dk], k[:, h * dk:(h + 1) * dk],
                                           gc[:, h * dk:(h + 1) * dk])

    vb = v_ref[...]
    decay = jnp.exp(g_last)
    outs = []
    for h in heads:
        st = st_ref[h]
        v_h = vb[:, h * dv:(h + 1) * dv]
        o = jnp.dot(a_ref[h].astype(BF16), v_h, preferred_element_type=F32)
        o = o + lax.dot_general(q_dec[:, h * dk:(h + 1) * dk], st.astype(BF16), nt, preferred_element_type=F32)
        st_ref[h] = st * decay[:, h * dk:(h + 1) * dk] + lax.dot_general(
            v_h, k_end[:, h * dk:(h + 1) * dk], (((0,), (0,)), ((), ())), preferred_element_type=F32)
        outs.append(o)
    for h in heads:
        o = outs[h]
        on = o * lax.rsqrt(jnp.mean(o * o, axis=-1, keepdims=True) + RMS_EPS) * nw_ref[:, h * dv:(h + 1) * dv]
        r = r_ref[:, h * dv:(h + 1) * dv].astype(F32)
        o_ref[:, h * dv:(h + 1) * dv] = (on * (r * _sigmoid(r))).astype(o_ref.dtype)


def _gla_mixer(u_qkv, u_r, u_small, gate_w2, gate_b, norm_w):
    rows = u_qkv.shape[0]
    qc = GLA_CHUNK
    dk, dv = GLA_HEAD_DK, GLA_HEAD_DV
    wk, wv = GLA_HEADS * dk, GLA_HEADS * dv
    glr_lane = HYB_GLR_COL % LANES
    w2 = jnp.pad(gate_w2, ((glr_lane, LANES - GLA_GATE_RANK - glr_lane), (0, 0)))
    return pl.pallas_call(
        _gla_kernel,
        grid=(rows // qc,),
        in_specs=[pl.BlockSpec((qc, wk), lambda c: (c, 0)),
                  pl.BlockSpec((qc, wk), lambda c: (c, 1)),
                  pl.BlockSpec((qc, wv), lambda c: (c, 1)),
                  pl.BlockSpec((qc, wv), lambda c: (c, 0)),
                  pl.BlockSpec((qc, LANES), lambda c: (c, 1)),
                  pl.BlockSpec((LANES, wk), lambda c: (0, 0)),
                  pl.BlockSpec((1, wk), lambda c: (0, 0)),
                  pl.BlockSpec((1, wv), lambda c: (0, 0))],
        out_specs=pl.BlockSpec((qc, wv), lambda c: (c, 0)),
        out_shape=jax.ShapeDtypeStruct((rows, wv), BF16),
        scratch_shapes=[pltpu.VMEM((GLA_HEADS, dv, dk), F32), pltpu.VMEM((GLA_HEADS, qc, qc), F32)],
        compiler_params=_params("arbitrary"),
        name="gla",
    )(u_qkv, u_qkv, u_qkv, u_r, u_small, w2, gate_b.reshape(1, -1), norm_w.reshape(1, -1))


def _rope_kernel(freq_ref, cm_ref, sp_ref, sm_ref, *, tm):
    i = pl.program_id(0)
    row = lax.broadcasted_iota(jnp.int32, (tm, LANES), 0) + i * tm
    lane = lax.broadcasted_iota(jnp.int32, (tm, LANES), 1) % SWA_HEAD_DIM
    ang = (row - FRONT_PAD).astype(F32) * freq_ref[...]
    cos = jnp.cos(ang)
    sin = jnp.sin(ang)
    half = ROPE_DIM // 2
    cm_ref[...] = cos
    sp_ref[...] = jnp.where(lane < half, -sin, 0.0)
    sm_ref[...] = jnp.where((lane >= half) & (lane < ROPE_DIM), sin, 0.0)


def _rope_tables(rows):
    half = ROPE_DIM // 2
    inv_freq = ROPE_THETA ** (-jnp.arange(half, dtype=F32) / half)
    per_head = jnp.concatenate([inv_freq, inv_freq, jnp.zeros((SWA_HEAD_DIM - ROPE_DIM,), F32)])
    freq = jnp.tile(per_head, LANES // SWA_HEAD_DIM).reshape(1, LANES)
    tm = _row_tile(rows)
    shp = jax.ShapeDtypeStruct((rows, LANES), F32)
    spec = pl.BlockSpec((tm, LANES), lambda i: (i, 0))
    return pl.pallas_call(
        functools.partial(_rope_kernel, tm=tm),
        grid=(rows // tm,),
        in_specs=[pl.BlockSpec((1, LANES), lambda i: (0, 0))],
        out_specs=[spec, spec, spec],
        out_shape=[shp, shp, shp],
        compiler_params=_params("arbitrary"),
        name="rope_tables",
    )(freq)


def _swa_kernel(sink_ref, q_ref, prev_ref, cur_ref, meta_ref, o_ref):
    n = pl.program_id(0)
    w = SWA_WINDOW
    meta_lo = FRONT_PAD
    kvw = SWA_KV_HEADS * LANES
    nt = (((1,), (1,)), ((), ()))
    lane = lax.broadcasted_iota(jnp.int32, (1, LANES), 1)
    low = lane < SWA_HEAD_DIM
    i = lax.broadcasted_iota(jnp.int32, (w, 1), 0)
    on_cur = lane <= i
    valid_band = (on_cur & ((n >= 1) | (lane >= meta_lo))) | (jnp.logical_not(on_cur) & (n >= 2))
    valid_meta = (lane >= meta_lo) & (n >= 1)
    zero = jnp.zeros((w, LANES), q_ref.dtype)
    kv_heads = range(SWA_KV_HEADS)

    def scores(g):
        qa = q_ref[:, 2 * g * LANES:(2 * g + 1) * LANES]
        qb = q_ref[:, (2 * g + 1) * LANES:(2 * g + 2) * LANES]
        qs = jnp.concatenate([jnp.where(low, qa, zero), jnp.where(low, zero, qa),
                              jnp.where(low, qb, zero), jnp.where(low, zero, qb)], axis=0)
        kcols = slice(g * LANES, (g + 1) * LANES)
        kk = jnp.concatenate([prev_ref[:, kcols], cur_ref[:, kcols], meta_ref[:, kcols]], axis=0)
        return lax.dot_general(qs, kk, nt, preferred_element_type=F32)

    s_next = scores(0)
    for g in kv_heads:
        s_all = s_next
        if g + 1 < SWA_KV_HEADS:
            s_next = scores(g + 1)
        probs, denoms = [], []
        for h in range(SWA_GROUP):
            s = s_all[h * w:(h + 1) * w]
            s_band = jnp.where(valid_band, jnp.where(on_cur, s[:, w:2 * w], s[:, :w]), -jnp.inf)
            s_meta = jnp.where(valid_meta, s[:, 2 * w:], -jnp.inf)
            sink = sink_ref[g * SWA_GROUP + h]
            m = jnp.maximum(jnp.max(jnp.maximum(s_band, s_meta), axis=-1, keepdims=True), sink)
            p_band = jnp.exp(s_band - m)
            p_meta = jnp.exp(s_meta - m)
            denoms.append(jnp.sum(p_band + p_meta, axis=-1, keepdims=True) + jnp.exp(sink - m))
            probs.append(jnp.concatenate([jnp.where(on_cur, 0.0, p_band), jnp.where(on_cur, p_band, 0.0), p_meta],
                                         axis=1).astype(BF16))
        vcols = slice(kvw + g * LANES, kvw + (g + 1) * LANES)
        vv = jnp.concatenate([prev_ref[:, vcols], cur_ref[:, vcols], meta_ref[:, vcols]], axis=0)
        o = jnp.dot(jnp.concatenate(probs, axis=0), vv, preferred_element_type=F32)
        o = [o[h * w:(h + 1) * w] / denoms[h] for h in range(SWA_GROUP)]
        oa = jnp.where(low, o[0], o[1])
        ob = jnp.where(low, o[2], o[3])
        o_ref[:, 2 * g * LANES:(2 * g + 2) * LANES] = jnp.concatenate([oa, ob], axis=1).astype(o_ref.dtype)


def _swa_attention(q, kv, sinks):
    rows, qw = q.shape
    w = SWA_WINDOW
    kvw = kv.shape[1]
    return pl.pallas_call(
        _swa_kernel,
        grid=(rows // w,),
        in_specs=[pl.BlockSpec(memory_space=pltpu.SMEM),
                  pl.BlockSpec((w, qw), lambda n: (n, 0)),
                  pl.BlockSpec((w, kvw), lambda n: (jnp.maximum(n - 1, 0), 0)),
                  pl.BlockSpec((w, kvw), lambda n: (n, 0)),
                  pl.BlockSpec((w, kvw), lambda n: (0, 0))],
        out_specs=pl.BlockSpec((w, qw), lambda n: (n, 0)),
        out_shape=jax.ShapeDtypeStruct((rows, qw), BF16),
        compiler_params=_params("arbitrary"),
        name="swa",
    )(sinks, q, kv, kv, kv)


def _embed_kernel(x_ref, meta_ref, hf_ref, hb_ref, *, tm):
    i = pl.program_id(0)
    head = FRONT_PAD + N_META

    @pl.when(i == 0)
    def _():
        top = jnp.concatenate([jnp.zeros((FRONT_PAD, D_MODEL), F32), meta_ref[...]], axis=0)
        hf_ref[0:head, :] = top
        hb_ref[0:head, :] = top.astype(BF16)
        body = x_ref[0:tm - head, :]
        hf_ref[head:tm, :] = body
        hb_ref[head:tm, :] = body.astype(BF16)

    @pl.when(i != 0)
    def _():
        hf_ref[...] = x_ref[...]
        hb_ref[...] = x_ref[...].astype(BF16)


def _embed(x, meta):
    seq, d = x.shape
    head = FRONT_PAD + N_META
    rows = head + seq
    tm = _row_tile(rows)
    out = pl.BlockSpec((tm, d), lambda i: (i, 0))
    x_rows = lambda i: pl.multiple_of(jnp.maximum(i * tm - head, 0), LANES)
    return pl.pallas_call(
        functools.partial(_embed_kernel, tm=tm),
        grid=(rows // tm,),
        in_specs=[pl.BlockSpec((pl.Element(tm), pl.Element(d)), lambda i: (x_rows(i), 0)),
                  pl.BlockSpec((N_META, d), lambda i: (0, 0))],
        out_specs=[out, out],
        out_shape=[jax.ShapeDtypeStruct((rows, d), F32), jax.ShapeDtypeStruct((rows, d), BF16)],
        compiler_params=_params("arbitrary"),
        name="embed",
    )(x, meta)


def _pad_halves(t):
    pad = lambda a: jnp.pad(a, ((0, 0), (0, D_FF_PAD - D_FF)))
    return jnp.concatenate([pad(t[:, :D_FF]), pad(t[:, D_FF:])], axis=1)


def _trunk(x, meta_tokens, hyb_w_in, hyb_conv_w, hyb_conv_b, ssd_dt_bias, ssd_a_log, ssd_d, ssd_norm_w,
           gla_gate_w2, gla_gate_b, gla_norm_w, hyb_w_out, swa_w_qkv, swa_sinks, swa_w_out,
           ffn_w_up, ffn_conv_w, ffn_conv_b, ffn_w_down, ln_mix_g, ln_mix_b, ln_ffn_g, ln_ffn_b):
    seq = x.shape[0]
    rows = FRONT_PAD + N_META + seq
    h, hb = _embed(x, meta_tokens.astype(F32))
    tables = _rope_tables(rows)
    hyb_w_in = jnp.pad(hyb_w_in, ((0, 0), (0, 0), (0, -hyb_w_in.shape[2] % LANES))).astype(BF16)
    for layer in range(DEPTH):
        j = layer // 2
        if layer % 2 == 0:
            u_zx = _project(hb, hyb_w_in, j, first_block=0, block_stride=8, tn=1024, n_tiles=6)
            u_qkv = _project(hb, hyb_w_in, j, first_block=HYB_QKV_COL // LANES, block_stride=8, tn=1024, n_tiles=4,
                             shift=HYB_QKV_COL % LANES)
            u_r = _project(hb, hyb_w_in, j, first_block=HYB_R_COL // LANES, block_stride=8, tn=1024, n_tiles=2,
                           shift=HYB_R_COL % LANES)
            u_small = _project(hb, hyb_w_in, j, first_block=0, block_stride=0, tn=2 * LANES, n_tiles=1,
                               blocks=[HYB_DT_COL // LANES, HYB_GLR_COL // LANES], out_dtype=F32)
            y_ssd = _ssd_mixer(u_zx, u_small, hyb_conv_w[j], hyb_conv_b[j], ssd_dt_bias[j], ssd_a_log[j],
                               ssd_d[j], ssd_norm_w[j])
            y_gla = _gla_mixer(u_qkv, u_r, u_small, gla_gate_w2[j], gla_gate_b[j], gla_norm_w[j])
            h, hb = _matmul_residual_ln([y_ssd, y_gla], [y_ssd.shape[1], y_gla.shape[1]], hyb_w_out, j, h,
                                        ln_mix_g[layer], ln_mix_b[layer], nchunks=8)
        else:
            q = _project(hb, swa_w_qkv, j, first_block=0, block_stride=8, tn=1024, n_tiles=2,
                         scale=SWA_HEAD_DIM ** -0.5, mode="rope", tables=tables)
            kv = _project(hb, swa_w_qkv, j, first_block=SWA_Q_HEADS * SWA_HEAD_DIM // LANES, block_stride=4, tn=512,
                          n_tiles=2, mode="kv", tables=tables)
            attn = _swa_attention(q, kv, swa_sinks[j])
            h, hb = _matmul_residual_ln([attn], [attn.shape[1]], swa_w_out, j, h, ln_mix_g[layer], ln_mix_b[layer],
                                        nchunks=4)
        act = _ffn_up(hb, ffn_w_up, layer, _pad_halves(ffn_conv_w[layer]),
                      _pad_halves(ffn_conv_b[layer].reshape(1, -1)))
        h, hb = _matmul_residual_ln([act], [D_FF], ffn_w_down, layer, h, ln_ffn_g[layer], ln_ffn_b[layer], nchunks=8)
    return h[FRONT_PAD + N_META:]


def kernel(x, meta_tokens, hyb_w_in, hyb_conv_w, hyb_conv_b, ssd_dt_bias, ssd_a_log, ssd_d, ssd_norm_w,
           gla_gate_w2, gla_gate_b, gla_norm_w, hyb_w_out, swa_w_qkv, swa_sinks, swa_w_out,
           ffn_w_up, ffn_conv_w, ffn_conv_b, ffn_w_down, ln_mix_g, ln_mix_b, ln_ffn_g, ln_ffn_b):
    params = (meta_tokens, hyb_w_in, hyb_conv_w, hyb_conv_b, ssd_dt_bias, ssd_a_log, ssd_d, ssd_norm_w,
              gla_gate_w2, gla_gate_b, gla_norm_w, hyb_w_out, swa_w_qkv, swa_sinks, swa_w_out,
              ffn_w_up, ffn_conv_w, ffn_conv_b, ffn_w_down, ln_mix_g, ln_mix_b, ln_ffn_g, ln_ffn_b)
    return jnp.stack([_trunk(x[b], *params) for b in range(x.shape[0])], axis=0)
```

```python
import functools

import jax
import jax.numpy as jnp
from jax import lax
from jax.experimental import pallas as pl
from jax.experimental.pallas import tpu as pltpu

F32 = jnp.float32
BF16 = jnp.bfloat16
HIGHEST = lax.Precision.HIGHEST

D_MODEL = 2048
DEPTH = 4
N_META = 16
LN_EPS = 1e-5
RMS_EPS = 1e-6
DEEPNORM_ALPHA = (2.0 * DEPTH) ** 0.25

SSD_HEAD_DIM = 64
SSD_HEADS = 32
SSD_GROUPS = 8
SSD_HPG = 4
SSD_STATE = 128
SSD_CONV = 4
SSD_CHUNK = 128
SSD_GROUP_W = SSD_HPG * SSD_HEAD_DIM

GLA_HEADS = 4
GLA_HEAD_DK = 256
GLA_HEAD_DV = 512
GLA_GATE_RANK = 16
GLA_GATE_TAU = 16.0
GLA_CHUNK = 128
GLA_SUB = 16
GLA_SAFE_DECAY = 60.0

SWA_HEAD_DIM = 64
SWA_Q_HEADS = 32
SWA_KV_HEADS = 8
SWA_GROUP = 4
SWA_WINDOW = 128
ROPE_THETA = 500000.0
ROPE_DIM = 16

D_FF = 5504
FFN_CONV = 3

LANES = 128
SUBLANES = 8
FRONT_PAD = SSD_CHUNK - N_META
D_FF_PAD = 5632
FFN_TN = 512
FFN_SUBTILES = 4
PROJ_SUBTILES = 4
LN_ROW_TILE = 320
LN_SUBTILES = 2
LN_WEIGHT_COPY_BUDGET = 20 * 1024 * 1024
VMEM_LIMIT = 56 * 1024 * 1024

HYB_DT_COL = D_MODEL + (D_MODEL + 2 * SSD_GROUPS * SSD_STATE)
HYB_QKV_COL = HYB_DT_COL + SSD_HEADS
HYB_GLR_COL = HYB_QKV_COL + GLA_HEADS * (2 * GLA_HEAD_DK + GLA_HEAD_DV)
HYB_R_COL = HYB_GLR_COL + GLA_GATE_RANK


def _row_tile(rows):
    for t in (640, 512, 384, 256, 128):
        if rows % t == 0:
            return t
    raise ValueError(f"row count {rows} is not a multiple of 128")


def _row_tile_big(rows):
    return 1664 if rows % 1664 == 0 else _row_tile(rows)


def _params(*sem):
    return pltpu.CompilerParams(dimension_semantics=sem, vmem_limit_bytes=VMEM_LIMIT)


def _sigmoid(x):
    return 1.0 / (1.0 + jnp.exp(-x))


def _softplus(x):
    return jnp.maximum(x, 0.0) + jnp.log(1.0 + jnp.exp(-jnp.abs(x)))


def _log_sigmoid(x):
    return jnp.minimum(x, 0.0) - jnp.log(1.0 + jnp.exp(-jnp.abs(x)))


def _causal_conv_rows(y, carry, w_ref, b_row, taps):
    top = jnp.concatenate([carry, y[0:SUBLANES]], axis=0)
    w_last = w_ref[taps - 1:taps, :]
    acc = b_row + w_last * y
    acc_top = b_row + w_last * y[0:SUBLANES]
    for s in range(1, taps):
        wk = w_ref[taps - 1 - s:taps - s, :]
        acc = acc + wk * pltpu.roll(y, s, 0)
        acc_top = acc_top + wk * pltpu.roll(top, s, 0)[SUBLANES:2 * SUBLANES]
    return jnp.concatenate([acc_top, acc[SUBLANES:]], axis=0)


def _assemble_weight(w_refs, wb_ref, shift, scale):
    k, tn = wb_ref.shape
    chunk = 256
    for r in range(0, k, chunk):
        w = jnp.concatenate([wr[r:r + chunk, :] for wr in w_refs], axis=1)
        if shift:
            w = pltpu.roll(w.astype(F32), w.shape[1] - shift, 1)
        w = w[:, :tn]
        if scale is not None:
            w = w * scale
        wb_ref[r:r + chunk, :] = w.astype(BF16)


def _weight_block_specs(k, layer, block_fns):
    return [pl.BlockSpec((None, k, LANES), functools.partial(lambda *ids, fn: (layer, 0, fn(*ids)), fn=fn))
            for fn in block_fns]


def _rotate_heads(y, cm, sp, sm):
    half = ROPE_DIM // 2
    out = []
    for c in range(y.shape[1] // LANES):
        yc = y[:, c * LANES:(c + 1) * LANES]
        out.append(yc * cm + pltpu.roll(yc, LANES - half, 1) * sp + pltpu.roll(yc, half, 1) * sm)
    return jnp.concatenate(out, axis=1)


def _duplicate_heads(y):
    low = lax.broadcasted_iota(jnp.int32, (1, LANES), 1) < SWA_HEAD_DIM
    out = []
    for c in range(y.shape[1] // LANES):
        yc = y[:, c * LANES:(c + 1) * LANES]
        rolled = pltpu.roll(yc, SWA_HEAD_DIM, 1)
        out.append(jnp.where(low, yc, rolled))
        out.append(jnp.where(low, rolled, yc))
    return jnp.concatenate(out, axis=1)


def _proj_kernel(*refs, nblk, shift, scale, mode):
    x_ref = refs[0]
    w_refs = refs[1:1 + nblk]
    rest = refs[1 + nblk:]
    if mode != "plain":
        cm_ref, sp_ref, sm_ref = rest[:3]
        rest = rest[3:]
    o_ref, wb_ref = rest
    i = pl.program_id(1)

    @pl.when(i == 0)
    def _():
        _assemble_weight(w_refs, wb_ref, shift, scale)

    if mode == "plain":
        o_ref[...] = jnp.dot(x_ref[...], wb_ref[...], preferred_element_type=F32).astype(o_ref.dtype)
        return
    tm = x_ref.shape[0]
    ts = tm // PROJ_SUBTILES
    ys = [jnp.dot(x_ref[s * ts:(s + 1) * ts, :], wb_ref[...], preferred_element_type=F32)
          for s in range(PROJ_SUBTILES)]
    for s, y in enumerate(ys):
        rows = slice(s * ts, (s + 1) * ts)
        n_rot = y.shape[1] // 2 if mode == "kv" else y.shape[1]
        rot = _rotate_heads(y[:, :n_rot], cm_ref[rows, :], sp_ref[rows, :], sm_ref[rows, :])
        if mode == "kv":
            rot = _duplicate_heads(jnp.concatenate([rot, y[:, n_rot:]], axis=1))
        o_ref[rows, :] = rot.astype(o_ref.dtype)


def _project(xb, w, layer, *, first_block, block_stride, tn, n_tiles, shift=0, scale=None, mode="plain",
             tables=None, out_dtype=BF16, blocks=None):
    m, k = xb.shape
    tm = _row_tile_big(m)
    nb = tn // LANES
    tn_out = 2 * tn if mode == "kv" else tn
    if blocks is not None:
        assert n_tiles == 1 and shift == 0 and len(blocks) == nb
        nblk = nb
        w_specs = _weight_block_specs(k, layer, [functools.partial(lambda j, i, b: b, b=b) for b in blocks])
    elif shift == 0 and block_stride == nb and first_block % nb == 0:
        nblk = 1
        w_specs = [pl.BlockSpec((None, k, tn), lambda j, i: (layer, 0, first_block // nb + j))]
    else:
        nblk = nb + (1 if shift else 0)
        w_specs = _weight_block_specs(
            k, layer, [functools.partial(lambda j, i, b: first_block + j * block_stride + b, b=b) for b in range(nblk)])
    in_specs = [pl.BlockSpec((tm, k), lambda j, i: (i, 0))] + w_specs
    args = [xb] + [w] * nblk
    if mode != "plain":
        in_specs += [pl.BlockSpec((tm, LANES), lambda j, i: (i, 0))] * 3
        args += list(tables)
    return pl.pallas_call(
        functools.partial(_proj_kernel, nblk=nblk, shift=shift, scale=scale, mode=mode),
        grid=(n_tiles, m // tm),
        in_specs=in_specs,
        out_specs=pl.BlockSpec((tm, tn_out), lambda j, i: (i, j)),
        out_shape=jax.ShapeDtypeStruct((m, n_tiles * tn_out), out_dtype),
        scratch_shapes=[pltpu.VMEM((k, tn), BF16)],
        compiler_params=_params("arbitrary", "arbitrary"),
        name="proj_" + mode,
    )(*args)


def _mm_ln_kernel(*refs, nsrc, tm, subtiles, nchunks, ck):
    x_refs = refs[:nsrc]
    w_ref, res_ref, g_ref, b_ref, of_ref, ob_ref, wb_ref = refs[nsrc:]
    step = pl.program_id(0)

    @pl.when(step < nchunks)
    def _():
        wb_ref[pl.ds(pl.multiple_of(step * ck, ck), ck), :] = w_ref[...].astype(BF16)

    @pl.when(step >= nchunks)
    def _():
        i = step - nchunks
        ts = tm // subtiles
        sums = []
        for s in range(subtiles):
            rows = slice(s * ts, (s + 1) * ts)
            acc, k0 = None, 0
            for x_ref in x_refs:
                kw = x_ref.shape[1]
                part = jnp.dot(x_ref[rows, :], wb_ref[k0:k0 + kw, :], preferred_element_type=F32)
                acc = part if acc is None else acc + part
                k0 += kw
            sums.append(acc)
        for s, acc in enumerate(sums):
            rows = slice(s * ts, (s + 1) * ts)
            t = DEEPNORM_ALPHA * res_ref[rows, :] + acc
            mu = jnp.mean(t, axis=-1, keepdims=True)
            d = t - mu
            var = jnp.mean(d * d, axis=-1, keepdims=True)
            y = d * lax.rsqrt(var + LN_EPS) * g_ref[...] + b_ref[...]
            row = lax.broadcasted_iota(jnp.int32, (ts, 1), 0) + (i * tm + s * ts)
            y = jnp.where(row >= FRONT_PAD, y, 0.0)
            of_ref[rows, :] = y
            ob_ref[rows, :] = y.astype(BF16)


def _matmul_residual_ln(xs, k_widths, w, layer, res, gamma, beta, nchunks):
    m = xs[0].shape[0]
    _, kdim, n = w.shape
    assert sum(k_widths) == kdim and m % LN_ROW_TILE == 0 and kdim % (16 * nchunks) == 0
    tm = LN_ROW_TILE
    ck = kdim // nchunks
    subtiles = LN_SUBTILES if 2 * kdim * n * 2 <= LN_WEIGHT_COPY_BUDGET else 1
    tile = lambda s: jnp.maximum(s - nchunks, 0)
    x_specs = [pl.BlockSpec((tm, kw), lambda s: (tile(s), 0)) for kw in k_widths]
    return pl.pallas_call(
        functools.partial(_mm_ln_kernel, nsrc=len(xs), tm=tm, subtiles=subtiles, nchunks=nchunks, ck=ck),
        grid=(nchunks + m // tm,),
        in_specs=x_specs + [pl.BlockSpec((None, ck, n), lambda s: (layer, jnp.minimum(s, nchunks - 1), 0)),
                            pl.BlockSpec((tm, n), lambda s: (tile(s), 0)),
                            pl.BlockSpec((1, n), lambda s: (0, 0)),
                            pl.BlockSpec((1, n), lambda s: (0, 0))],
        out_specs=[pl.BlockSpec((tm, n), lambda s: (tile(s), 0)),
                   pl.BlockSpec((tm, n), lambda s: (tile(s), 0))],
        out_shape=[jax.ShapeDtypeStruct((m, n), F32), jax.ShapeDtypeStruct((m, n), BF16)],
        scratch_shapes=[pltpu.VMEM((kdim, n), BF16)],
        compiler_params=_params("arbitrary"),
        name="proj_ln",
    )(*xs, w, res, gamma.reshape(1, n), beta.reshape(1, n))


def _ffn_up_kernel(*refs, nblk, tm, tn):
    x_ref = refs[0]
    w_refs = refs[1:1 + nblk]
    cw_g, cw_v, cb_g, cb_v, o_ref, wb_ref, cg_ref, cv_ref = refs[1 + nblk:]
    j = pl.program_id(0)
    i = pl.program_id(1)

    @pl.when(i == 0)
    def _():
        _assemble_weight(w_refs, wb_ref, 0, None)
        cg_ref[...] = jnp.zeros_like(cg_ref)
        cv_ref[...] = jnp.zeros_like(cv_ref)

    col = lax.broadcasted_iota(jnp.int32, (1, tn), 1) + j * tn
    ts = tm // FFN_SUBTILES
    ys = []
    for s in range(FFN_SUBTILES):
        y = jnp.dot(x_ref[s * ts:(s + 1) * ts, :], wb_ref[...], preferred_element_type=F32)
        ys.append((y[:, :tn], y[:, tn:]))
    carry_g, carry_v = cg_ref[...], cv_ref[...]
    for s, (yg, yv) in enumerate(ys):
        hg = _causal_conv_rows(yg, carry_g, cw_g, cb_g[...], FFN_CONV)
        hv = _causal_conv_rows(yv, carry_v, cw_v, cb_v[...], FFN_CONV)
        carry_g, carry_v = yg[ts - SUBLANES:ts], yv[ts - SUBLANES:ts]
        o_ref[s * ts:(s + 1) * ts, :] = jnp.where(col < D_FF, hg * _sigmoid(hg) * hv, 0.0).astype(o_ref.dtype)
    cg_ref[...] = carry_g
    cv_ref[...] = carry_v


def _ffn_up(xb, w_up, layer, conv_w, conv_b):
    m, k = xb.shape
    tm = _row_tile_big(m)
    assert (tm // FFN_SUBTILES) % 16 == 0, tm
    tn = FFN_TN
    nj = D_FF_PAD // tn
    nb = tn // LANES
    val0 = D_FF // LANES
    last_blk = 2 * D_FF // LANES - 1
    w_specs = [pl.BlockSpec((None, k, tn), lambda j, i: (layer, 0, j))] + _weight_block_specs(
        k, layer, [functools.partial(lambda j, i, b: jnp.minimum(val0 + j * nb + b, last_blk), b=b) for b in range(nb)])
    nblk = 1 + nb
    return pl.pallas_call(
        functools.partial(_ffn_up_kernel, nblk=nblk, tm=tm, tn=tn),
        grid=(nj, m // tm),
        in_specs=[pl.BlockSpec((tm, k), lambda j, i: (i, 0))] + w_specs
        + [pl.BlockSpec((FFN_CONV, tn), lambda j, i: (0, j)),
           pl.BlockSpec((FFN_CONV, tn), lambda j, i: (0, nj + j)),
           pl.BlockSpec((1, tn), lambda j, i: (0, j)),
           pl.BlockSpec((1, tn), lambda j, i: (0, nj + j))],
        out_specs=pl.BlockSpec((tm, tn), lambda j, i: (i, j)),
        out_shape=jax.ShapeDtypeStruct((m, D_FF_PAD), BF16),
        scratch_shapes=[pltpu.VMEM((k, 2 * tn), BF16), pltpu.VMEM((SUBLANES, tn), F32),
                        pltpu.VMEM((SUBLANES, tn), F32)],
        compiler_params=_params("arbitrary", "arbitrary"),
        name="ffn_up",
    )(xb, *([w_up] * nblk), conv_w, conv_w, conv_b, conv_b)


def _expand_heads(x, rows, g):
    low = lax.broadcasted_iota(jnp.int32, (1, LANES), 1) < SSD_HEAD_DIM
    h0 = g * SSD_HPG
    b = [jnp.broadcast_to(x[:, h0 + h:h0 + h + 1], (rows, LANES)) for h in range(SSD_HPG)]
    return jnp.concatenate([jnp.where(low, b[0], b[1]), jnp.where(low, b[2], b[3])], axis=1)


def _ssd_kernel(z_ref, x_ref, bc_ref, dt_ref, cwx_ref, cwbc_ref, cbx_ref, cbbc_ref, dtb_ref, alog_ref, dsk_ref,
                nw_ref, o_ref, st_ref, cx_ref, cbc_ref):
    c = pl.program_id(0)
    q = SSD_CHUNK
    gw = SSD_GROUP_W
    n = SSD_STATE
    groups = range(SSD_GROUPS)

    @pl.when(c == 0)
    def _():
        st_ref[...] = jnp.zeros_like(st_ref)
        cx_ref[...] = jnp.zeros_like(cx_ref)
        cbc_ref[...] = jnp.zeros_like(cbc_ref)

    row = lax.broadcasted_iota(jnp.int32, (q, 1), 0) + c * q
    lane = lax.broadcasted_iota(jnp.int32, (1, LANES), 1)
    dt = _softplus(dt_ref[...] + dtb_ref[...])
    dt = jnp.where((row >= FRONT_PAD) & (lane < SSD_HEADS), dt, 0.0)
    a = -jnp.exp(alog_ref[...])
    ri = lax.broadcasted_iota(jnp.int32, (q, q), 0)
    ci = lax.broadcasted_iota(jnp.int32, (q, q), 1)
    causal = ci <= ri
    cs = jnp.dot(causal.astype(F32), dt * a, precision=HIGHEST, preferred_element_type=F32)
    cs_t = cs.T
    cs_last = cs[q - 1:q, :]
    from_start = jnp.exp(cs)
    to_end = jnp.exp(cs_last - cs)
    total = jnp.exp(cs_last)

    def conv_silu(raw_ref, carry_ref, w_ref, b_ref, lo, width):
        raw = raw_ref[:, lo:lo + width].astype(F32)
        y = _causal_conv_rows(raw, carry_ref[:, lo:lo + width], w_ref.at[:, lo:lo + width], b_ref[:, lo:lo + width],
                              SSD_CONV)
        carry_ref[:, lo:lo + width] = raw[q - SUBLANES:q]
        return y * _sigmoid(y)

    xs = [conv_silu(x_ref, cx_ref, cwx_ref, cbx_ref, g * gw, gw) for g in groups]
    bmb = [conv_silu(bc_ref, cbc_ref, cwbc_ref, cbbc_ref, g * n, n).astype(BF16) for g in groups]
    cmb = [conv_silu(bc_ref, cbc_ref, cwbc_ref, cbbc_ref, SSD_GROUPS * n + g * n, n).astype(BF16) for g in groups]
    cb = [lax.dot_general(cmb[g], bmb[g], (((1,), (1,)), ((), ())), preferred_element_type=F32)
          for g in groups]

    lane_w = lax.broadcasted_iota(jnp.int32, (1, gw), 1)
    xd = [xs[g] * _expand_heads(dt, q, g) for g in groups]
    ys = []
    for g in groups:
        decayed = []
        for h in range(SSD_HPG):
            hh = g * SSD_HPG + h
            seg = cs[:, hh:hh + 1] - cs_t[hh:hh + 1, :]
            decayed.append((cb[g] * jnp.exp(jnp.where(causal, seg, -jnp.inf))).astype(BF16))
        xdb = xd[g].astype(BF16)
        x_heads = [jnp.where((lane_w >= h * SSD_HEAD_DIM) & (lane_w < (h + 1) * SSD_HEAD_DIM), xdb,
                             jnp.zeros_like(xdb)) for h in range(SSD_HPG)]
        ys.append(jnp.dot(jnp.concatenate(decayed, axis=1), jnp.concatenate(x_heads, axis=0),
                          preferred_element_type=F32))
    for g in groups:
        st = st_ref[g]
        ys[g] = ys[g] + (jnp.dot(cmb[g], st.astype(BF16), preferred_element_type=F32)
                         * _expand_heads(from_start, q, g))
        st_ref[g] = (st * _expand_heads(total, 1, g)
                     + lax.dot_general(bmb[g], (xd[g] * _expand_heads(to_end, q, g)).astype(BF16),
                                       (((0,), (0,)), ((), ())), preferred_element_type=F32))
    for g in groups:
        cols = slice(g * gw, (g + 1) * gw)
        y = ys[g] + xs[g] * dsk_ref[:, cols]
        z = z_ref[:, cols].astype(F32)
        yg = y * (z * _sigmoid(z))
        yn = yg * lax.rsqrt(jnp.mean(yg * yg, axis=-1, keepdims=True) + RMS_EPS) * nw_ref[:, cols]
        o_ref[:, cols] = yn.astype(o_ref.dtype)


def _ssd_mixer(u_zx, u_small, conv_w, conv_b, dt_bias, a_log, d_skip, norm_w):
    rows = u_zx.shape[0]
    q = SSD_CHUNK
    d = SSD_GROUPS * SSD_GROUP_W
    pad_l = lambda v: jnp.pad(v.reshape(1, SSD_HEADS), ((0, 0), (0, LANES - SSD_HEADS)))
    whole = lambda r, w, j: pl.BlockSpec((r, w), lambda c: (0, j))
    return pl.pallas_call(
        _ssd_kernel,
        grid=(rows // q,),
        in_specs=[pl.BlockSpec((q, d), lambda c: (c, 0)),
                  pl.BlockSpec((q, d), lambda c: (c, 1)),
                  pl.BlockSpec((q, d), lambda c: (c, 2)),
                  pl.BlockSpec((q, LANES), lambda c: (c, 0)),
                  whole(SSD_CONV, d, 0), whole(SSD_CONV, d, 1), whole(1, d, 0), whole(1, d, 1),
                  whole(1, LANES, 0), whole(1, LANES, 0), whole(1, d, 0), whole(1, d, 0)],
        out_specs=pl.BlockSpec((q, d), lambda c: (c, 0)),
        out_shape=jax.ShapeDtypeStruct((rows, d), BF16),
        scratch_shapes=[pltpu.VMEM((SSD_GROUPS, SSD_STATE, SSD_GROUP_W), F32), pltpu.VMEM((SUBLANES, d), F32),
                        pltpu.VMEM((SUBLANES, d), F32)],
        compiler_params=_params("arbitrary"),
        name="ssd",
    )(u_zx, u_zx, u_zx, u_small, conv_w, conv_w, conv_b.reshape(1, -1), conv_b.reshape(1, -1),
      pad_l(dt_bias), pad_l(a_log), jnp.repeat(d_skip, SSD_HEAD_DIM).reshape(1, d), norm_w.reshape(1, d))


def _gla_scores_blocked(q, k, gc):
    qc = q.shape[0]
    sub = GLA_SUB
    lane_j = lax.broadcasted_iota(jnp.int32, (sub, qc), 1)
    row_i = lax.broadcasted_iota(jnp.int32, (sub, 1), 0)
    a_rows = []
    for blk in range(qc // sub):
        lo = blk * sub
        q_b = q[lo:lo + sub]
        g_b = gc[lo:lo + sub]
        a_blk = jnp.zeros((sub, qc), F32)
        for j in range(sub):
            k_j = k[lo + j:lo + j + 1, :]
            g_j = gc[lo + j:lo + j + 1, :]
            s_j = jnp.sum(q_b * k_j * jnp.exp(jnp.minimum(g_b - g_j, 0.0)), axis=1, keepdims=True)
            a_blk = jnp.where(lane_j == lo + j, jnp.where(row_i >= j, s_j, 0.0), a_blk)
        if blk > 0:
            g_ref0 = gc[lo:lo + 1, :]
            q_t = (q_b * jnp.exp(g_b - g_ref0)).astype(BF16)
            k_t = (k * jnp.exp(jnp.minimum(g_ref0 - gc, 0.0))).astype(BF16)
            off = lax.dot_general(q_t, k_t, (((1,), (1,)), ((), ())), preferred_element_type=F32)
            a_blk = jnp.where(lane_j < lo, off, a_blk)
        a_rows.append(a_blk)
    return jnp.concatenate(a_rows, axis=0)


def _gla_kernel(q_ref, k_ref, v_ref, r_ref, glr_ref, w2_ref, gb_ref, nw_ref, o_ref, st_ref, a_ref):
    c = pl.program_id(0)
    qc = GLA_CHUNK
    dk, dv = GLA_HEAD_DK, GLA_HEAD_DV
    heads = range(GLA_HEADS)
    nt = (((1,), (1,)), ((), ()))

    @pl.when(c == 0)
    def _():
        st_ref[...] = jnp.zeros_like(st_ref)

    pre = jnp.dot(glr_ref[...], w2_ref[...], precision=HIGHEST, preferred_element_type=F32) + gb_ref[...]
    g = _log_sigmoid(pre) * (1.0 / GLA_GATE_TAU)
    row = lax.broadcasted_iota(jnp.int32, (qc, 1), 0) + c * qc
    g = jnp.where(row >= FRONT_PAD, g, 0.0)
    ri = lax.broadcasted_iota(jnp.int32, (qc, qc), 0)
    ci = lax.broadcasted_iota(jnp.int32, (qc, qc), 1)
    causal = ci <= ri
    gc = jnp.dot(causal.astype(F32), g, precision=HIGHEST, preferred_element_type=F32)
    g_last = gc[qc - 1:qc, :]

    q = q_ref[...].astype(F32) * (GLA_HEAD_DK ** -0.5)
    k = k_ref[...].astype(F32)
    q_dec = (q * jnp.exp(gc)).astype(BF16)
    k_end = (k * jnp.exp(g_last - gc)).astype(BF16)
    safe = jnp.max(-g_last) <= GLA_SAFE_DECAY

    @pl.when(safe)
    def _():
        k_inv = (k * jnp.exp(-gc)).astype(BF16)
        for h in heads:
            s = lax.dot_general(q_dec[:, h * dk:(h + 1) * dk], k_inv[:, h * dk:(h + 1) * dk], nt,
                                preferred_element_type=F32)
            a_ref[h] = jnp.where(causal, s, 0.0)

    @pl.when(jnp.logical_not(safe))
    def _():
        for h in heads:
            a_ref[h] = _gla_scores_blocked(q[:, h * dk:(h + 1) * dk], k[:, h * dk:(h + 1) * dk],
                                           gc[:, h * dk:(h + 1) * dk])

    vb = v_ref[...]
    decay = jnp.exp(g_last)
    outs = []
    for h in heads:
        st = st_ref[h]
        v_h = vb[:, h * dv:(h + 1) * dv]
        o = jnp.dot(a_ref[h].astype(BF16), v_h, preferred_element_type=F32)
        o = o + lax.dot_general(q_dec[:, h * dk:(h + 1) * dk], st.astype(BF16), nt, preferred_element_type=F32)
        st_ref[h] = st * decay[:, h * dk:(h + 1) * dk] + lax.dot_general(
            v_h, k_end[:, h * dk:(h + 1) * dk], (((0,), (0,)), ((), ())), preferred_element_type=F32)
        outs.append(o)
    for h in heads:
        o = outs[h]
        on = o * lax.rsqrt(jnp.mean(o * o, axis=-1, keepdims=True) + RMS_EPS) * nw_ref[:, h * dv:(h + 1) * dv]
        r = r_ref[:, h * dv:(h + 1) * dv].astype(F32)
        o_ref[:, h * dv:(h + 1) * dv] = (on * (r * _sigmoid(r))).astype(o_ref.dtype)


def _gla_mixer(u_qkv, u_r, u_small, gate_w2, gate_b, norm_w):
    rows = u_qkv.shape[0]
    qc = GLA_CHUNK
    dk, dv = GLA_HEAD_DK, GLA_HEAD_DV
    wk, wv = GLA_HEADS * dk, GLA_HEADS * dv
    glr_lane = HYB_GLR_COL % LANES
    w2 = jnp.pad(gate_w2, ((glr_lane, LANES - GLA_GATE_RANK - glr_lane), (0, 0)))
    return pl.pallas_call(
        _gla_kernel,
        grid=(rows // qc,),
        in_specs=[pl.BlockSpec((qc, wk), lambda c: (c, 0)),
                  pl.BlockSpec((qc, wk), lambda c: (c, 1)),
                  pl.BlockSpec((qc, wv), lambda c: (c, 1)),
                  pl.BlockSpec((qc, wv), lambda c: (c, 0)),
                  pl.BlockSpec((qc, LANES), lambda c: (c, 1)),
                  pl.BlockSpec((LANES, wk), lambda c: (0, 0)),
                  pl.BlockSpec((1, wk), lambda c: (0, 0)),
                  pl.BlockSpec((1, wv), lambda c: (0, 0))],
        out_specs=pl.BlockSpec((qc, wv), lambda c: (c, 0)),
        out_shape=jax.ShapeDtypeStruct((rows, wv), BF16),
        scratch_shapes=[pltpu.VMEM((GLA_HEADS, dv, dk), F32), pltpu.VMEM((GLA_HEADS, qc, qc), F32)],
        compiler_params=_params("arbitrary"),
        name="gla",
    )(u_qkv, u_qkv, u_qkv, u_r, u_small, w2, gate_b.reshape(1, -1), norm_w.reshape(1, -1))


def _rope_kernel(freq_ref, cm_ref, sp_ref, sm_ref, *, tm):
    i = pl.program_id(0)
    row = lax.broadcasted_iota(jnp.int32, (tm, LANES), 0) + i * tm
    lane = lax.broadcasted_iota(jnp.int32, (tm, LANES), 1) % SWA_HEAD_DIM
    ang = (row - FRONT_PAD).astype(F32) * freq_ref[...]
    cos = jnp.cos(ang)
    sin = jnp.sin(ang)
    half = ROPE_DIM // 2
    cm_ref[...] = cos
    sp_ref[...] = jnp.where(lane < half, -sin, 0.0)
    sm_ref[...] = jnp.where((lane >= half) & (lane < ROPE_DIM), sin, 0.0)


def _rope_tables(rows):
    half = ROPE_DIM // 2
    inv_freq = ROPE_THETA ** (-jnp.arange(half, dtype=F32) / half)
    per_head = jnp.concatenate([inv_freq, inv_freq, jnp.zeros((SWA_HEAD_DIM - ROPE_DIM,), F32)])
    freq = jnp.tile(per_head, LANES // SWA_HEAD_DIM).reshape(1, LANES)
    tm = _row_tile(rows)
    shp = jax.ShapeDtypeStruct((rows, LANES), F32)
    spec = pl.BlockSpec((tm, LANES), lambda i: (i, 0))
    return pl.pallas_call(
        functools.partial(_rope_kernel, tm=tm),
        grid=(rows // tm,),
        in_specs=[pl.BlockSpec((1, LANES), lambda i: (0, 0))],
        out_specs=[spec, spec, spec],
        out_shape=[shp, shp, shp],
        compiler_params=_params("arbitrary"),
        name="rope_tables",
    )(freq)


def _swa_kernel(sink_ref, q_ref, prev_ref, cur_ref, meta_ref, o_ref):
    n = pl.program_id(0)
    w = SWA_WINDOW
    meta_lo = FRONT_PAD
    kvw = SWA_KV_HEADS * LANES
    nt = (((1,), (1,)), ((), ()))
    lane = lax.broadcasted_iota(jnp.int32, (1, LANES), 1)
    low = lane < SWA_HEAD_DIM
    i = lax.broadcasted_iota(jnp.int32, (w, 1), 0)
    on_cur = lane <= i
    valid_band = (on_cur & ((n >= 1) | (lane >= meta_lo))) | (jnp.logical_not(on_cur) & (n >= 2))
    valid_meta = (lane >= meta_lo) & (n >= 1)
    zero = jnp.zeros((w, LANES), q_ref.dtype)
    kv_heads = range(SWA_KV_HEADS)

    def scores(g):
        qa = q_ref[:, 2 * g * LANES:(2 * g + 1) * LANES]
        qb = q_ref[:, (2 * g + 1) * LANES:(2 * g + 2) * LANES]
        qs = jnp.concatenate([jnp.where(low, qa, zero), jnp.where(low, zero, qa),
                              jnp.where(low, qb, zero), jnp.where(low, zero, qb)], axis=0)
        kcols = slice(g * LANES, (g + 1) * LANES)
        kk = jnp.concatenate([prev_ref[:, kcols], cur_ref[:, kcols], meta_ref[:, kcols]], axis=0)
        return lax.dot_general(qs, kk, nt, preferred_element_type=F32)

    s_next = scores(0)
    for g in kv_heads:
        s_all = s_next
        if g + 1 < SWA_KV_HEADS:
            s_next = scores(g + 1)
        probs, denoms = [], []
        for h in range(SWA_GROUP):
            s = s_all[h * w:(h + 1) * w]
            s_band = jnp.where(valid_band, jnp.where(on_cur, s[:, w:2 * w], s[:, :w]), -jnp.inf)
            s_meta = jnp.where(valid_meta, s[:, 2 * w:], -jnp.inf)
            sink = sink_ref[g * SWA_GROUP + h]
            m = jnp.maximum(jnp.max(jnp.maximum(s_band, s_meta), axis=-1, keepdims=True), sink)
            p_band = jnp.exp(s_band - m)
            p_meta = jnp.exp(s_meta - m)
            denoms.append(jnp.sum(p_band + p_meta, axis=-1, keepdims=True) + jnp.exp(sink - m))
            probs.append(jnp.concatenate([jnp.where(on_cur, 0.0, p_band), jnp.where(on_cur, p_band, 0.0), p_meta],
                                         axis=1).astype(BF16))
        vcols = slice(kvw + g * LANES, kvw + (g + 1) * LANES)
        vv = jnp.concatenate([prev_ref[:, vcols], cur_ref[:, vcols], meta_ref[:, vcols]], axis=0)
        o = jnp.dot(jnp.concatenate(probs, axis=0), vv, preferred_element_type=F32)
        o = [o[h * w:(h + 1) * w] / denoms[h] for h in range(SWA_GROUP)]
        oa = jnp.where(low, o[0], o[1])
        ob = jnp.where(low, o[2], o[3])
        o_ref[:, 2 * g * LANES:(2 * g + 2) * LANES] = jnp.concatenate([oa, ob], axis=1).astype(o_ref.dtype)


def _swa_attention(q, kv, sinks):
    rows, qw = q.shape
    w = SWA_WINDOW
    kvw = kv.shape[1]
    return pl.pallas_call(
        _swa_kernel,
        grid=(rows // w,),
        in_specs=[pl.BlockSpec(memory_space=pltpu.SMEM),
                  pl.BlockSpec((w, qw), lambda n: (n, 0)),
                  pl.BlockSpec((w, kvw), lambda n: (jnp.maximum(n - 1, 0), 0)),
                  pl.BlockSpec((w, kvw), lambda n: (n, 0)),
                  pl.BlockSpec((w, kvw), lambda n: (0, 0))],
        out_specs=pl.BlockSpec((w, qw), lambda n: (n, 0)),
        out_shape=jax.ShapeDtypeStruct((rows, qw), BF16),
        compiler_params=_params("arbitrary"),
        name="swa",
    )(sinks, q, kv, kv, kv)


def _embed_kernel(x_ref, meta_ref, hf_ref, hb_ref, *, tm):
    i = pl.program_id(0)
    head = FRONT_PAD + N_META

    @pl.when(i == 0)
    def _():
        top = jnp.concatenate([jnp.zeros((FRONT_PAD, D_MODEL), F32), meta_ref[...]], axis=0)
        hf_ref[0:head, :] = top
        hb_ref[0:head, :] = top.astype(BF16)
        body = x_ref[0:tm - head, :]
        hf_ref[head:tm, :] = body
        hb_ref[head:tm, :] = body.astype(BF16)

    @pl.when(i != 0)
    def _():
        hf_ref[...] = x_ref[...]
        hb_ref[...] = x_ref[...].astype(BF16)


def _embed(x, meta):
    seq, d = x.shape
    head = FRONT_PAD + N_META
    rows = head + seq
    tm = _row_tile(rows)
    out = pl.BlockSpec((tm, d), lambda i: (i, 0))
    x_rows = lambda i: pl.multiple_of(jnp.maximum(i * tm - head, 0), LANES)
    return pl.pallas_call(
        functools.partial(_embed_kernel, tm=tm),
        grid=(rows // tm,),
        in_specs=[pl.BlockSpec((pl.Element(tm), pl.Element(d)), lambda i: (x_rows(i), 0)),
                  pl.BlockSpec((N_META, d), lambda i: (0, 0))],
        out_specs=[out, out],
        out_shape=[jax.ShapeDtypeStruct((rows, d), F32), jax.ShapeDtypeStruct((rows, d), BF16)],
        compiler_params=_params("arbitrary"),
        name="embed",
    )(x, meta)


def _pad_halves(t):
    pad = lambda a: jnp.pad(a, ((0, 0), (0, D_FF_PAD - D_FF)))
    return jnp.concatenate([pad(t[:, :D_FF]), pad(t[:, D_FF:])], axis=1)


def _trunk(x, meta_tokens, hyb_w_in, hyb_conv_w, hyb_conv_b, ssd_dt_bias, ssd_a_log, ssd_d, ssd_norm_w,
           gla_gate_w2, gla_gate_b, gla_norm_w, hyb_w_out, swa_w_qkv, swa_sinks, swa_w_out,
           ffn_w_up, ffn_conv_w, ffn_conv_b, ffn_w_down, ln_mix_g, ln_mix_b, ln_ffn_g, ln_ffn_b):
    seq = x.shape[0]
    rows = FRONT_PAD + N_META + seq
    h, hb = _embed(x, meta_tokens.astype(F32))
    tables = _rope_tables(rows)
    hyb_w_in = jnp.pad(hyb_w_in, ((0, 0), (0, 0), (0, -hyb_w_in.shape[2] % LANES))).astype(BF16)
    for layer in range(DEPTH):
        j = layer // 2
        if layer % 2 == 0:
            u_zx = _project(hb, hyb_w_in, j, first_block=0, block_stride=8, tn=1024, n_tiles=6)
            u_qkv = _project(hb, hyb_w_in, j, first_block=HYB_QKV_COL // LANES, block_stride=8, tn=1024, n_tiles=4,
                             shift=HYB_QKV_COL % LANES)
            u_r = _project(hb, hyb_w_in, j, first_block=HYB_R_COL // LANES, block_stride=8, tn=1024, n_tiles=2,
                           shift=HYB_R_COL % LANES)
            u_small = _project(hb, hyb_w_in, j, first_block=0, block_stride=0, tn=2 * LANES, n_tiles=1,
                               blocks=[HYB_DT_COL // LANES, HYB_GLR_COL // LANES], out_dtype=F32)
            y_ssd = _ssd_mixer(u_zx, u_small, hyb_conv_w[j], hyb_conv_b[j], ssd_dt_bias[j], ssd_a_log[j],
                               ssd_d[j], ssd_norm_w[j])
            y_gla = _gla_mixer(u_qkv, u_r, u_small, gla_gate_w2[j], gla_gate_b[j], gla_norm_w[j])
            h, hb = _matmul_residual_ln([y_ssd, y_gla], [y_ssd.shape[1], y_gla.shape[1]], hyb_w_out, j, h,
                                        ln_mix_g[layer], ln_mix_b[layer], nchunks=8)
        else:
            q = _project(hb, swa_w_qkv, j, first_block=0, block_stride=8, tn=1024, n_tiles=2,
                         scale=SWA_HEAD_DIM ** -0.5, mode="rope", tables=tables)
            kv = _project(hb, swa_w_qkv, j, first_block=SWA_Q_HEADS * SWA_HEAD_DIM // LANES, block_stride=8, tn=1024,
                          n_tiles=1, mode="kv", tables=tables)
            attn = _swa_attention(q, kv, swa_sinks[j])
            h, hb = _matmul_residual_ln([attn], [attn.shape[1]], swa_w_out, j, h, ln_mix_g[layer], ln_mix_b[layer],
                                        nchunks=4)
        act = _ffn_up(hb, ffn_w_up, layer, _pad_halves(ffn_conv_w[layer]),
                      _pad_halves(ffn_conv_b[layer].reshape(1, -1)))
        h, hb = _matmul_residual_ln([act], [D_FF], ffn_w_down, layer, h, ln_ffn_g[layer], ln_ffn_b[layer], nchunks=8)
    return h[FRONT_PAD + N_META:]


def kernel(x, meta_tokens, hyb_w_in, hyb_conv_w, hyb_conv_b, ssd_dt_bias, ssd_a_log, ssd_d, ssd_norm_w,
           gla_gate_w2, gla_gate_b, gla_norm_w, hyb_w_out, swa_w_qkv, swa_sinks, swa_w_out,
           ffn_w_up, ffn_conv_w, ffn_conv_b, ffn_w_down, ln_mix_g, ln_mix_b, ln_ffn_g, ln_ffn_b):
    params = (meta_tokens, hyb_w_in, hyb_conv_w, hyb_conv_b, ssd_dt_bias, ssd_a_log, ssd_d, ssd_norm_w,
              gla_gate_w2, gla_gate_b, gla_norm_w, hyb_w_out, swa_w_qkv, swa_sinks, swa_w_out,
              ffn_w_up, ffn_conv_w, ffn_conv_b, ffn_w_down, ln_mix_g, ln_mix_b, ln_ffn_g, ln_ffn_b)
    return jnp.stack([_trunk(x[b], *params) for b in range(x.shape[0])], axis=0)
```

```python
import functools

import jax
import jax.numpy as jnp
from jax import lax
from jax.experimental import pallas as pl
from jax.experimental.pallas import tpu as pltpu

F32 = jnp.float32
BF16 = jnp.bfloat16
HIGHEST = lax.Precision.HIGHEST

D_MODEL = 2048
DEPTH = 4
N_META = 16
LN_EPS = 1e-5
RMS_EPS = 1e-6
DEEPNORM_ALPHA = (2.0 * DEPTH) ** 0.25

SSD_HEAD_DIM = 64
SSD_HEADS = 32
SSD_GROUPS = 8
SSD_HPG = 4
SSD_STATE = 128
SSD_CONV = 4
SSD_CHUNK = 128
SSD_GROUP_W = SSD_HPG * SSD_HEAD_DIM

GLA_HEADS = 4
GLA_HEAD_DK = 256
GLA_HEAD_DV = 512
GLA_GATE_RANK = 16
GLA_GATE_TAU = 16.0
GLA_CHUNK = 128
GLA_SUB = 16
GLA_SAFE_DECAY = 60.0

SWA_HEAD_DIM = 64
SWA_Q_HEADS = 32
SWA_KV_HEADS = 8
SWA_GROUP = 4
SWA_WINDOW = 128
ROPE_THETA = 500000.0
ROPE_DIM = 16

D_FF = 5504
FFN_CONV = 3

LANES = 128
SUBLANES = 8
FRONT_PAD = SSD_CHUNK - N_META
D_FF_PAD = 5632
FFN_TN = 512
FFN_SUBTILES = 4
FFN_STRIP = 256
PROJ_SUBTILES = 4
LN_ROW_TILE = 320
LN_SUBTILES = 2
LN_WEIGHT_COPY_BUDGET = 20 * 1024 * 1024
VMEM_LIMIT = 56 * 1024 * 1024

HYB_DT_COL = D_MODEL + (D_MODEL + 2 * SSD_GROUPS * SSD_STATE)
HYB_QKV_COL = HYB_DT_COL + SSD_HEADS
HYB_GLR_COL = HYB_QKV_COL + GLA_HEADS * (2 * GLA_HEAD_DK + GLA_HEAD_DV)
HYB_R_COL = HYB_GLR_COL + GLA_GATE_RANK


def _row_tile(rows):
    for t in (640, 512, 384, 256, 128):
        if rows % t == 0:
            return t
    raise ValueError(f"row count {rows} is not a multiple of 128")


def _row_tile_big(rows):
    return 1664 if rows % 1664 == 0 else _row_tile(rows)


def _params(*sem):
    return pltpu.CompilerParams(dimension_semantics=sem, vmem_limit_bytes=VMEM_LIMIT)


def _sigmoid(x):
    return 1.0 / (1.0 + jnp.exp(-x))


def _softplus(x):
    return jnp.maximum(x, 0.0) + jnp.log(1.0 + jnp.exp(-jnp.abs(x)))


def _log_sigmoid(x):
    return jnp.minimum(x, 0.0) - jnp.log(1.0 + jnp.exp(-jnp.abs(x)))


def _causal_conv_rows(y, carry, w_ref, b_row, taps):
    top = jnp.concatenate([carry, y[0:SUBLANES]], axis=0)
    w_last = w_ref[taps - 1:taps, :]
    acc = b_row + w_last * y
    acc_top = b_row + w_last * y[0:SUBLANES]
    for s in range(1, taps):
        wk = w_ref[taps - 1 - s:taps - s, :]
        acc = acc + wk * pltpu.roll(y, s, 0)
        acc_top = acc_top + wk * pltpu.roll(top, s, 0)[SUBLANES:2 * SUBLANES]
    return jnp.concatenate([acc_top, acc[SUBLANES:]], axis=0)


def _assemble_weight(w_refs, wb_ref, shift, scale):
    k, tn = wb_ref.shape
    chunk = 256
    for r in range(0, k, chunk):
        w = jnp.concatenate([wr[r:r + chunk, :] for wr in w_refs], axis=1)
        if shift:
            w = pltpu.roll(w.astype(F32), w.shape[1] - shift, 1)
        w = w[:, :tn]
        if scale is not None:
            w = w * scale
        wb_ref[r:r + chunk, :] = w.astype(BF16)


def _weight_block_specs(k, layer, block_fns):
    return [pl.BlockSpec((None, k, LANES), functools.partial(lambda *ids, fn: (layer, 0, fn(*ids)), fn=fn))
            for fn in block_fns]


def _rotate_heads(y, cm, sp, sm):
    half = ROPE_DIM // 2
    out = []
    for c in range(y.shape[1] // LANES):
        yc = y[:, c * LANES:(c + 1) * LANES]
        out.append(yc * cm + pltpu.roll(yc, LANES - half, 1) * sp + pltpu.roll(yc, half, 1) * sm)
    return jnp.concatenate(out, axis=1)


def _duplicate_heads(y):
    low = lax.broadcasted_iota(jnp.int32, (1, LANES), 1) < SWA_HEAD_DIM
    out = []
    for c in range(y.shape[1] // LANES):
        yc = y[:, c * LANES:(c + 1) * LANES]
        rolled = pltpu.roll(yc, SWA_HEAD_DIM, 1)
        out.append(jnp.where(low, yc, rolled))
        out.append(jnp.where(low, rolled, yc))
    return jnp.concatenate(out, axis=1)


def _proj_kernel(*refs, nblk, shift, scale, mode):
    x_ref = refs[0]
    w_refs = refs[1:1 + nblk]
    rest = refs[1 + nblk:]
    if mode != "plain":
        cm_ref, sp_ref, sm_ref = rest[:3]
        rest = rest[3:]
    o_ref, wb_ref = rest
    j = pl.program_id(0)
    i = pl.program_id(1)

    @pl.when(i == 0)
    def _():
        _assemble_weight(w_refs, wb_ref, shift, scale)

    if mode == "plain":
        o_ref[...] = jnp.dot(x_ref[...], wb_ref[...], preferred_element_type=F32).astype(o_ref.dtype)
        return
    tm = x_ref.shape[0]
    ts = tm // PROJ_SUBTILES
    ys = [jnp.dot(x_ref[s * ts:(s + 1) * ts, :], wb_ref[...], preferred_element_type=F32)
          for s in range(PROJ_SUBTILES)]
    rotate = (j == 0) if mode == "kv" else True
    for s, y in enumerate(ys):
        rows = slice(s * ts, (s + 1) * ts)
        cm = jnp.where(rotate, cm_ref[rows, :], 1.0)
        sp = jnp.where(rotate, sp_ref[rows, :], 0.0)
        sm = jnp.where(rotate, sm_ref[rows, :], 0.0)
        y = _rotate_heads(y, cm, sp, sm)
        if mode == "kv":
            y = _duplicate_heads(y)
        o_ref[rows, :] = y.astype(o_ref.dtype)


def _project(xb, w, layer, *, first_block, block_stride, tn, n_tiles, shift=0, scale=None, mode="plain",
             tables=None, out_dtype=BF16, blocks=None):
    m, k = xb.shape
    tm = _row_tile_big(m)
    nb = tn // LANES
    tn_out = 2 * tn if mode == "kv" else tn
    if blocks is not None:
        assert n_tiles == 1 and shift == 0 and len(blocks) == nb
        nblk = nb
        w_specs = _weight_block_specs(k, layer, [functools.partial(lambda j, i, b: b, b=b) for b in blocks])
    elif shift == 0 and block_stride == nb and first_block % nb == 0:
        nblk = 1
        w_specs = [pl.BlockSpec((None, k, tn), lambda j, i: (layer, 0, first_block // nb + j))]
    else:
        nblk = nb + (1 if shift else 0)
        w_specs = _weight_block_specs(
            k, layer, [functools.partial(lambda j, i, b: first_block + j * block_stride + b, b=b) for b in range(nblk)])
    in_specs = [pl.BlockSpec((tm, k), lambda j, i: (i, 0))] + w_specs
    args = [xb] + [w] * nblk
    if mode != "plain":
        in_specs += [pl.BlockSpec((tm, LANES), lambda j, i: (i, 0))] * 3
        args += list(tables)
    return pl.pallas_call(
        functools.partial(_proj_kernel, nblk=nblk, shift=shift, scale=scale, mode=mode),
        grid=(n_tiles, m // tm),
        in_specs=in_specs,
        out_specs=pl.BlockSpec((tm, tn_out), lambda j, i: (i, j)),
        out_shape=jax.ShapeDtypeStruct((m, n_tiles * tn_out), out_dtype),
        scratch_shapes=[pltpu.VMEM((k, tn), BF16)],
        compiler_params=_params("arbitrary", "arbitrary"),
        name="proj_" + mode,
    )(*args)


def _mm_ln_kernel(*refs, nsrc, tm, subtiles, nchunks, ck):
    x_refs = refs[:nsrc]
    w_ref, res_ref, g_ref, b_ref, of_ref, ob_ref, wb_ref = refs[nsrc:]
    step = pl.program_id(0)

    @pl.when(step < nchunks)
    def _():
        wb_ref[pl.ds(pl.multiple_of(step * ck, ck), ck), :] = w_ref[...].astype(BF16)

    @pl.when(step >= nchunks)
    def _():
        i = step - nchunks
        ts = tm // subtiles
        sums = []
        for s in range(subtiles):
            rows = slice(s * ts, (s + 1) * ts)
            acc, k0 = None, 0
            for x_ref in x_refs:
                kw = x_ref.shape[1]
                part = jnp.dot(x_ref[rows, :], wb_ref[k0:k0 + kw, :], preferred_element_type=F32)
                acc = part if acc is None else acc + part
                k0 += kw
            sums.append(acc)
        for s, acc in enumerate(sums):
            rows = slice(s * ts, (s + 1) * ts)
            t = DEEPNORM_ALPHA * res_ref[rows, :] + acc
            mu = jnp.mean(t, axis=-1, keepdims=True)
            d = t - mu
            var = jnp.mean(d * d, axis=-1, keepdims=True)
            y = d * lax.rsqrt(var + LN_EPS) * g_ref[...] + b_ref[...]
            row = lax.broadcasted_iota(jnp.int32, (ts, 1), 0) + (i * tm + s * ts)
            y = jnp.where(row >= FRONT_PAD, y, 0.0)
            of_ref[rows, :] = y
            ob_ref[rows, :] = y.astype(BF16)


def _matmul_residual_ln(xs, k_widths, w, layer, res, gamma, beta, nchunks):
    m = xs[0].shape[0]
    _, kdim, n = w.shape
    assert sum(k_widths) == kdim and m % LN_ROW_TILE == 0 and kdim % (16 * nchunks) == 0
    tm = LN_ROW_TILE
    ck = kdim // nchunks
    subtiles = LN_SUBTILES if 2 * kdim * n * 2 <= LN_WEIGHT_COPY_BUDGET else 1
    tile = lambda s: jnp.maximum(s - nchunks, 0)
    x_specs = [pl.BlockSpec((tm, kw), lambda s: (tile(s), 0)) for kw in k_widths]
    return pl.pallas_call(
        functools.partial(_mm_ln_kernel, nsrc=len(xs), tm=tm, subtiles=subtiles, nchunks=nchunks, ck=ck),
        grid=(nchunks + m // tm,),
        in_specs=x_specs + [pl.BlockSpec((None, ck, n), lambda s: (layer, jnp.minimum(s, nchunks - 1), 0)),
                            pl.BlockSpec((tm, n), lambda s: (tile(s), 0)),
                            pl.BlockSpec((1, n), lambda s: (0, 0)),
                            pl.BlockSpec((1, n), lambda s: (0, 0))],
        out_specs=[pl.BlockSpec((tm, n), lambda s: (tile(s), 0)),
                   pl.BlockSpec((tm, n), lambda s: (tile(s), 0))],
        out_shape=[jax.ShapeDtypeStruct((m, n), F32), jax.ShapeDtypeStruct((m, n), BF16)],
        scratch_shapes=[pltpu.VMEM((kdim, n), BF16)],
        compiler_params=_params("arbitrary"),
        name="proj_ln",
    )(*xs, w, res, gamma.reshape(1, n), beta.reshape(1, n))


def _ffn_up_kernel(*refs, nblk, tm, tn):
    x_ref = refs[0]
    w_refs = refs[1:1 + nblk]
    cw_g, cw_v, cb_g, cb_v, o_ref, wb_ref, cg_ref, cv_ref = refs[1 + nblk:]
    j = pl.program_id(0)
    i = pl.program_id(1)

    @pl.when(i == 0)
    def _():
        _assemble_weight(w_refs, wb_ref, 0, None)
        cg_ref[...] = jnp.zeros_like(cg_ref)
        cv_ref[...] = jnp.zeros_like(cv_ref)

    col = lax.broadcasted_iota(jnp.int32, (1, tn), 1) + j * tn
    ts = tm // FFN_SUBTILES
    ys = []
    for s in range(FFN_SUBTILES):
        y = jnp.dot(x_ref[s * ts:(s + 1) * ts, :], wb_ref[...], preferred_element_type=F32)
        ys.append((y[:, :tn], y[:, tn:]))
    carry_g, carry_v = cg_ref[...], cv_ref[...]
    for s, (yg, yv) in enumerate(ys):
        for c0 in range(0, tn, FFN_STRIP):
            cs = slice(c0, c0 + FFN_STRIP)
            hg = _causal_conv_rows(yg[:, cs], carry_g[:, cs], cw_g.at[:, cs], cb_g[:, cs], FFN_CONV)
            hv = _causal_conv_rows(yv[:, cs], carry_v[:, cs], cw_v.at[:, cs], cb_v[:, cs], FFN_CONV)
            o_ref[s * ts:(s + 1) * ts, cs] = jnp.where(col[:, cs] < D_FF, hg * _sigmoid(hg) * hv,
                                                       0.0).astype(o_ref.dtype)
        carry_g, carry_v = yg[ts - SUBLANES:ts], yv[ts - SUBLANES:ts]
    cg_ref[...] = carry_g
    cv_ref[...] = carry_v


def _ffn_up(xb, w_up, layer, conv_w, conv_b):
    m, k = xb.shape
    tm = _row_tile_big(m)
    assert (tm // FFN_SUBTILES) % 16 == 0, tm
    tn = FFN_TN
    nj = D_FF_PAD // tn
    nb = tn // LANES
    val0 = D_FF // LANES
    last_blk = 2 * D_FF // LANES - 1
    w_specs = [pl.BlockSpec((None, k, tn), lambda j, i: (layer, 0, j))] + _weight_block_specs(
        k, layer, [functools.partial(lambda j, i, b: jnp.minimum(val0 + j * nb + b, last_blk), b=b) for b in range(nb)])
    nblk = 1 + nb
    return pl.pallas_call(
        functools.partial(_ffn_up_kernel, nblk=nblk, tm=tm, tn=tn),
        grid=(nj, m // tm),
        in_specs=[pl.BlockSpec((tm, k), lambda j, i: (i, 0))] + w_specs
        + [pl.BlockSpec((FFN_CONV, tn), lambda j, i: (0, j)),
           pl.BlockSpec((FFN_CONV, tn), lambda j, i: (0, nj + j)),
           pl.BlockSpec((1, tn), lambda j, i: (0, j)),
           pl.BlockSpec((1, tn), lambda j, i: (0, nj + j))],
        out_specs=pl.BlockSpec((tm, tn), lambda j, i: (i, j)),
        out_shape=jax.ShapeDtypeStruct((m, D_FF_PAD), BF16),
        scratch_shapes=[pltpu.VMEM((k, 2 * tn), BF16), pltpu.VMEM((SUBLANES, tn), F32),
                        pltpu.VMEM((SUBLANES, tn), F32)],
        compiler_params=_params("arbitrary", "arbitrary"),
        name="ffn_up",
    )(xb, *([w_up] * nblk), conv_w, conv_w, conv_b, conv_b)


def _expand_heads(x, rows, g):
    low = lax.broadcasted_iota(jnp.int32, (1, LANES), 1) < SSD_HEAD_DIM
    h0 = g * SSD_HPG
    b = [jnp.broadcast_to(x[:, h0 + h:h0 + h + 1], (rows, LANES)) for h in range(SSD_HPG)]
    return jnp.concatenate([jnp.where(low, b[0], b[1]), jnp.where(low, b[2], b[3])], axis=1)


def _ssd_kernel(z_ref, x_ref, bc_ref, dt_ref, cwx_ref, cwbc_ref, cbx_ref, cbbc_ref, dtb_ref, alog_ref, dsk_ref,
                nw_ref, o_ref, st_ref, cx_ref, cbc_ref):
    c = pl.program_id(0)
    q = SSD_CHUNK
    gw = SSD_GROUP_W
    n = SSD_STATE
    groups = range(SSD_GROUPS)

    @pl.when(c == 0)
    def _():
        st_ref[...] = jnp.zeros_like(st_ref)
        cx_ref[...] = jnp.zeros_like(cx_ref)
        cbc_ref[...] = jnp.zeros_like(cbc_ref)

    row = lax.broadcasted_iota(jnp.int32, (q, 1), 0) + c * q
    lane = lax.broadcasted_iota(jnp.int32, (1, LANES), 1)
    dt = _softplus(dt_ref[...] + dtb_ref[...])
    dt = jnp.where((row >= FRONT_PAD) & (lane < SSD_HEADS), dt, 0.0)
    a = -jnp.exp(alog_ref[...])
    ri = lax.broadcasted_iota(jnp.int32, (q, q), 0)
    ci = lax.broadcasted_iota(jnp.int32, (q, q), 1)
    causal = ci <= ri
    cs = jnp.dot(causal.astype(F32), dt * a, precision=HIGHEST, preferred_element_type=F32)
    cs_t = cs.T
    cs_last = cs[q - 1:q, :]
    from_start = jnp.exp(cs)
    to_end = jnp.exp(cs_last - cs)
    total = jnp.exp(cs_last)

    def conv_silu(raw_ref, carry_ref, w_ref, b_ref, lo, width):
        raw = raw_ref[:, lo:lo + width].astype(F32)
        y = _causal_conv_rows(raw, carry_ref[:, lo:lo + width], w_ref.at[:, lo:lo + width], b_ref[:, lo:lo + width],
                              SSD_CONV)
        carry_ref[:, lo:lo + width] = raw[q - SUBLANES:q]
        return y * _sigmoid(y)

    xs = [conv_silu(x_ref, cx_ref, cwx_ref, cbx_ref, g * gw, gw) for g in groups]
    bmb = [conv_silu(bc_ref, cbc_ref, cwbc_ref, cbbc_ref, g * n, n).astype(BF16) for g in groups]
    cmb = [conv_silu(bc_ref, cbc_ref, cwbc_ref, cbbc_ref, SSD_GROUPS * n + g * n, n).astype(BF16) for g in groups]
    cb = [lax.dot_general(cmb[g], bmb[g], (((1,), (1,)), ((), ())), preferred_element_type=F32)
          for g in groups]

    lane_w = lax.broadcasted_iota(jnp.int32, (1, gw), 1)
    xd = [xs[g] * _expand_heads(dt, q, g) for g in groups]
    ys = []
    for g in groups:
        decayed = []
        for h in range(SSD_HPG):
            hh = g * SSD_HPG + h
            seg = cs[:, hh:hh + 1] - cs_t[hh:hh + 1, :]
            decayed.append((cb[g] * jnp.exp(jnp.where(causal, seg, -jnp.inf))).astype(BF16))
        xdb = xd[g].astype(BF16)
        x_heads = [jnp.where((lane_w >= h * SSD_HEAD_DIM) & (lane_w < (h + 1) * SSD_HEAD_DIM), xdb,
                             jnp.zeros_like(xdb)) for h in range(SSD_HPG)]
        ys.append(jnp.dot(jnp.concatenate(decayed, axis=1), jnp.concatenate(x_heads, axis=0),
                          preferred_element_type=F32))
    for g in groups:
        st = st_ref[g]
        ys[g] = ys[g] + (jnp.dot(cmb[g], st.astype(BF16), preferred_element_type=F32)
                         * _expand_heads(from_start, q, g))
        st_ref[g] = (st * _expand_heads(total, 1, g)
                     + lax.dot_general(bmb[g], (xd[g] * _expand_heads(to_end, q, g)).astype(BF16),
                                       (((0,), (0,)), ((), ())), preferred_element_type=F32))
    for g in groups:
        cols = slice(g * gw, (g + 1) * gw)
        y = ys[g] + xs[g] * dsk_ref[:, cols]
        z = z_ref[:, cols].astype(F32)
        yg = y * (z * _sigmoid(z))
        yn = yg * lax.rsqrt(jnp.mean(yg * yg, axis=-1, keepdims=True) + RMS_EPS) * nw_ref[:, cols]
        o_ref[:, cols] = yn.astype(o_ref.dtype)


def _ssd_mixer(u_zx, u_small, conv_w, conv_b, dt_bias, a_log, d_skip, norm_w):
    rows = u_zx.shape[0]
    q = SSD_CHUNK
    d = SSD_GROUPS * SSD_GROUP_W
    pad_l = lambda v: jnp.pad(v.reshape(1, SSD_HEADS), ((0, 0), (0, LANES - SSD_HEADS)))
    whole = lambda r, w, j: pl.BlockSpec((r, w), lambda c: (0, j))
    return pl.pallas_call(
        _ssd_kernel,
        grid=(rows // q,),
        in_specs=[pl.BlockSpec((q, d), lambda c: (c, 0)),
                  pl.BlockSpec((q, d), lambda c: (c, 1)),
                  pl.BlockSpec((q, d), lambda c: (c, 2)),
                  pl.BlockSpec((q, LANES), lambda c: (c, 0)),
                  whole(SSD_CONV, d, 0), whole(SSD_CONV, d, 1), whole(1, d, 0), whole(1, d, 1),
                  whole(1, LANES, 0), whole(1, LANES, 0), whole(1, d, 0), whole(1, d, 0)],
        out_specs=pl.BlockSpec((q, d), lambda c: (c, 0)),
        out_shape=jax.ShapeDtypeStruct((rows, d), BF16),
        scratch_shapes=[pltpu.VMEM((SSD_GROUPS, SSD_STATE, SSD_GROUP_W), F32), pltpu.VMEM((SUBLANES, d), F32),
                        pltpu.VMEM((SUBLANES, d), F32)],
        compiler_params=_params("arbitrary"),
        name="ssd",
    )(u_zx, u_zx, u_zx, u_small, conv_w, conv_w, conv_b.reshape(1, -1), conv_b.reshape(1, -1),
      pad_l(dt_bias), pad_l(a_log), jnp.repeat(d_skip, SSD_HEAD_DIM).reshape(1, d), norm_w.reshape(1, d))


def _gla_scores_blocked(q, k, gc):
    qc = q.shape[0]
    sub = GLA_SUB
    lane_j = lax.broadcasted_iota(jnp.int32, (sub, qc), 1)
    row_i = lax.broadcasted_iota(jnp.int32, (sub, 1), 0)
    a_rows = []
    for blk in range(qc // sub):
        lo = blk * sub
        q_b = q[lo:lo + sub]
        g_b = gc[lo:lo + sub]
        a_blk = jnp.zeros((sub, qc), F32)
        for j in range(sub):
            k_j = k[lo + j:lo + j + 1, :]
            g_j = gc[lo + j:lo + j + 1, :]
            s_j = jnp.sum(q_b * k_j * jnp.exp(jnp.minimum(g_b - g_j, 0.0)), axis=1, keepdims=True)
            a_blk = jnp.where(lane_j == lo + j, jnp.where(row_i >= j, s_j, 0.0), a_blk)
        if blk > 0:
            g_ref0 = gc[lo:lo + 1, :]
            q_t = (q_b * jnp.exp(g_b - g_ref0)).astype(BF16)
            k_t = (k * jnp.exp(jnp.minimum(g_ref0 - gc, 0.0))).astype(BF16)
            off = lax.dot_general(q_t, k_t, (((1,), (1,)), ((), ())), preferred_element_type=F32)
            a_blk = jnp.where(lane_j < lo, off, a_blk)
        a_rows.append(a_blk)
    return jnp.concatenate(a_rows, axis=0)


def _gla_kernel(q_ref, k_ref, v_ref, r_ref, glr_ref, w2_ref, gb_ref, nw_ref, o_ref, st_ref, a_ref):
    c = pl.program_id(0)
    qc = GLA_CHUNK
    dk, dv = GLA_HEAD_DK, GLA_HEAD_DV
    heads = range(GLA_HEADS)
    nt = (((1,), (1,)), ((), ()))

    @pl.when(c == 0)
    def _():
        st_ref[...] = jnp.zeros_like(st_ref)

    pre = jnp.dot(glr_ref[...], w2_ref[...], precision=HIGHEST, preferred_element_type=F32) + gb_ref[...]
    g = _log_sigmoid(pre) * (1.0 / GLA_GATE_TAU)
    row = lax.broadcasted_iota(jnp.int32, (qc, 1), 0) + c * qc
    g = jnp.where(row >= FRONT_PAD, g, 0.0)
    ri = lax.broadcasted_iota(jnp.int32, (qc, qc), 0)
    ci = lax.broadcasted_iota(jnp.int32, (qc, qc), 1)
    causal = ci <= ri
    gc = jnp.dot(causal.astype(F32), g, precision=HIGHEST, preferred_element_type=F32)
    g_last = gc[qc - 1:qc, :]

    q = q_ref[...].astype(F32) * (GLA_HEAD_DK ** -0.5)
    k = k_ref[...].astype(F32)
    q_dec = (q * jnp.exp(gc)).astype(BF16)
    k_end = (k * jnp.exp(g_last - gc)).astype(BF16)
    safe = jnp.max(-g_last) <= GLA_SAFE_DECAY

    @pl.when(safe)
    def _():
        k_inv = (k * jnp.exp(-gc)).astype(BF16)
        for h in heads:
            s = lax.dot_general(q_dec[:, h * dk:(h + 1) * dk], k_inv[:, h * dk:(h + 1) * dk], nt,
                                preferred_element_type=F32)
            a_ref[h] = jnp.where(causal, s, 0.0)

    @pl.when(jnp.logical_not(safe))
    def _():
        for h in heads:
            a_ref[h] = _gla_scores_blocked(q[:, h * dk:(h + 1) * dk], k[:, h * dk:(h + 1) * dk],
                                           gc[:, h * dk:(h + 1) * dk])

    vb = v_ref[...]
    decay = jnp.exp(g_last)
    outs = []
    for h in heads:
        st = st_ref[h]
        v_h = vb[:, h * dv:(h + 1) * dv]
        o = jnp.dot(a_ref[h].astype(BF16), v_h, preferred_element_type=F32)
        o = o + lax.dot_general(q_dec[:, h * dk:(h + 1) * dk], st.astype(BF16), nt, preferred_element_type=F32)
        st_ref[h] = st * decay[:, h * dk:(h + 1) * dk] + lax.dot_general(
            v_h, k_end[:, h * dk:(h + 1) * dk], (((0,), (0,)), ((), ())), preferred_element_type=F32)
        outs.append(o)
    for h in heads:
        o = outs[h]
        on = o * lax.rsqrt(jnp.mean(o * o, axis=-1, keepdims=True) + RMS_EPS) * nw_ref[:, h * dv:(h + 1) * dv]
        r = r_ref[:, h * dv:(h + 1) * dv].astype(F32)
        o_ref[:, h * dv:(h + 1) * dv] = (on * (r * _sigmoid(r))).astype(o_ref.dtype)


def _gla_mixer(u_qkv, u_r, u_small, gate_w2, gate_b, norm_w):
    rows = u_qkv.shape[0]
    qc = GLA_CHUNK
    dk, dv = GLA_HEAD_DK, GLA_HEAD_DV
    wk, wv = GLA_HEADS * dk, GLA_HEADS * dv
    glr_lane = HYB_GLR_COL % LANES
    w2 = jnp.pad(gate_w2, ((glr_lane, LANES - GLA_GATE_RANK - glr_lane), (0, 0)))
    return pl.pallas_call(
        _gla_kernel,
        grid=(rows // qc,),
        in_specs=[pl.BlockSpec((qc, wk), lambda c: (c, 0)),
                  pl.BlockSpec((qc, wk), lambda c: (c, 1)),
                  pl.BlockSpec((qc, wv), lambda c: (c, 1)),
                  pl.BlockSpec((qc, wv), lambda c: (c, 0)),
                  pl.BlockSpec((qc, LANES), lambda c: (c, 1)),
                  pl.BlockSpec((LANES, wk), lambda c: (0, 0)),
                  pl.BlockSpec((1, wk), lambda c: (0, 0)),
                  pl.BlockSpec((1, wv), lambda c: (0, 0))],
        out_specs=pl.BlockSpec((qc, wv), lambda c: (c, 0)),
        out_shape=jax.ShapeDtypeStruct((rows, wv), BF16),
        scratch_shapes=[pltpu.VMEM((GLA_HEADS, dv, dk), F32), pltpu.VMEM((GLA_HEADS, qc, qc), F32)],
        compiler_params=_params("arbitrary"),
        name="gla",
    )(u_qkv, u_qkv, u_qkv, u_r, u_small, w2, gate_b.reshape(1, -1), norm_w.reshape(1, -1))


def _rope_kernel(freq_ref, cm_ref, sp_ref, sm_ref, *, tm):
    i = pl.program_id(0)
    row = lax.broadcasted_iota(jnp.int32, (tm, LANES), 0) + i * tm
    lane = lax.broadcasted_iota(jnp.int32, (tm, LANES), 1) % SWA_HEAD_DIM
    ang = (row - FRONT_PAD).astype(F32) * freq_ref[...]
    cos = jnp.cos(ang)
    sin = jnp.sin(ang)
    half = ROPE_DIM // 2
    cm_ref[...] = cos
    sp_ref[...] = jnp.where(lane < half, -sin, 0.0)
    sm_ref[...] = jnp.where((lane >= half) & (lane < ROPE_DIM), sin, 0.0)


def _rope_tables(rows):
    half = ROPE_DIM // 2
    inv_freq = ROPE_THETA ** (-jnp.arange(half, dtype=F32) / half)
    per_head = jnp.concatenate([inv_freq, inv_freq, jnp.zeros((SWA_HEAD_DIM - ROPE_DIM,), F32)])
    freq = jnp.tile(per_head, LANES // SWA_HEAD_DIM).reshape(1, LANES)
    tm = _row_tile(rows)
    shp = jax.ShapeDtypeStruct((rows, LANES), F32)
    spec = pl.BlockSpec((tm, LANES), lambda i: (i, 0))
    return pl.pallas_call(
        functools.partial(_rope_kernel, tm=tm),
        grid=(rows // tm,),
        in_specs=[pl.BlockSpec((1, LANES), lambda i: (0, 0))],
        out_specs=[spec, spec, spec],
        out_shape=[shp, shp, shp],
        compiler_params=_params("arbitrary"),
        name="rope_tables",
    )(freq)


def _swa_kernel(sink_ref, q_ref, prev_ref, cur_ref, meta_ref, o_ref):
    n = pl.program_id(0)
    w = SWA_WINDOW
    meta_lo = FRONT_PAD
    kvw = SWA_KV_HEADS * LANES
    nt = (((1,), (1,)), ((), ()))
    lane = lax.broadcasted_iota(jnp.int32, (1, LANES), 1)
    low = lane < SWA_HEAD_DIM
    i = lax.broadcasted_iota(jnp.int32, (w, 1), 0)
    on_cur = lane <= i
    valid_band = (on_cur & ((n >= 1) | (lane >= meta_lo))) | (jnp.logical_not(on_cur) & (n >= 2))
    valid_meta = (lane >= meta_lo) & (n >= 1)
    zero = jnp.zeros((w, LANES), q_ref.dtype)
    kv_heads = range(SWA_KV_HEADS)

    def scores(g):
        qa = q_ref[:, 2 * g * LANES:(2 * g + 1) * LANES]
        qb = q_ref[:, (2 * g + 1) * LANES:(2 * g + 2) * LANES]
        qs = jnp.concatenate([jnp.where(low, qa, zero), jnp.where(low, zero, qa),
                              jnp.where(low, qb, zero), jnp.where(low, zero, qb)], axis=0)
        kcols = slice(g * LANES, (g + 1) * LANES)
        kk = jnp.concatenate([prev_ref[:, kcols], cur_ref[:, kcols], meta_ref[:, kcols]], axis=0)
        return lax.dot_general(qs, kk, nt, preferred_element_type=F32)

    s_next = scores(0)
    for g in kv_heads:
        s_all = s_next
        if g + 1 < SWA_KV_HEADS:
            s_next = scores(g + 1)
        probs, denoms = [], []
        for h in range(SWA_GROUP):
            s = s_all[h * w:(h + 1) * w]
            s_band = jnp.where(valid_band, jnp.where(on_cur, s[:, w:2 * w], s[:, :w]), -jnp.inf)
            s_meta = jnp.where(valid_meta, s[:, 2 * w:], -jnp.inf)
            sink = sink_ref[g * SWA_GROUP + h]
            m = jnp.maximum(jnp.max(jnp.maximum(s_band, s_meta), axis=-1, keepdims=True), sink)
            p_band = jnp.exp(s_band - m)
            p_meta = jnp.exp(s_meta - m)
            denoms.append(jnp.sum(p_band + p_meta, axis=-1, keepdims=True) + jnp.exp(sink - m))
            probs.append(jnp.concatenate([jnp.where(on_cur, 0.0, p_band), jnp.where(on_cur, p_band, 0.0), p_meta],
                                         axis=1).astype(BF16))
        vcols = slice(kvw + g * LANES, kvw + (g + 1) * LANES)
        vv = jnp.concatenate([prev_ref[:, vcols], cur_ref[:, vcols], meta_ref[:, vcols]], axis=0)
        o = jnp.dot(jnp.concatenate(probs, axis=0), vv, preferred_element_type=F32)
        o = [o[h * w:(h + 1) * w] / denoms[h] for h in range(SWA_GROUP)]
        oa = jnp.where(low, o[0], o[1])
        ob = jnp.where(low, o[2], o[3])
        o_ref[:, 2 * g * LANES:(2 * g + 2) * LANES] = jnp.concatenate([oa, ob], axis=1).astype(o_ref.dtype)


def _swa_attention(q, kv, sinks):
    rows, qw = q.shape
    w = SWA_WINDOW
    kvw = kv.shape[1]
    return pl.pallas_call(
        _swa_kernel,
        grid=(rows // w,),
        in_specs=[pl.BlockSpec(memory_space=pltpu.SMEM),
                  pl.BlockSpec((w, qw), lambda n: (n, 0)),
                  pl.BlockSpec((w, kvw), lambda n: (jnp.maximum(n - 1, 0), 0)),
                  pl.BlockSpec((w, kvw), lambda n: (n, 0)),
                  pl.BlockSpec((w, kvw), lambda n: (0, 0))],
        out_specs=pl.BlockSpec((w, qw), lambda n: (n, 0)),
        out_shape=jax.ShapeDtypeStruct((rows, qw), BF16),
        compiler_params=_params("arbitrary"),
        name="swa",
    )(sinks, q, kv, kv, kv)


def _embed_kernel(x_ref, meta_ref, hf_ref, hb_ref, *, tm):
    i = pl.program_id(0)
    head = FRONT_PAD + N_META

    @pl.when(i == 0)
    def _():
        top = jnp.concatenate([jnp.zeros((FRONT_PAD, D_MODEL), F32), meta_ref[...]], axis=0)
        hf_ref[0:head, :] = top
        hb_ref[0:head, :] = top.astype(BF16)
        body = x_ref[0:tm - head, :]
        hf_ref[head:tm, :] = body
        hb_ref[head:tm, :] = body.astype(BF16)

    @pl.when(i != 0)
    def _():
        hf_ref[...] = x_ref[...]
        hb_ref[...] = x_ref[...].astype(BF16)


def _embed(x, meta):
    seq, d = x.shape
    head = FRONT_PAD + N_META
    rows = head + seq
    tm = _row_tile(rows)
    out = pl.BlockSpec((tm, d), lambda i: (i, 0))
    x_rows = lambda i: pl.multiple_of(jnp.maximum(i * tm - head, 0), LANES)
    return pl.pallas_call(
        functools.partial(_embed_kernel, tm=tm),
        grid=(rows // tm,),
        in_specs=[pl.BlockSpec((pl.Element(tm), pl.Element(d)), lambda i: (x_rows(i), 0)),
                  pl.BlockSpec((N_META, d), lambda i: (0, 0))],
        out_specs=[out, out],
        out_shape=[jax.ShapeDtypeStruct((rows, d), F32), jax.ShapeDtypeStruct((rows, d), BF16)],
        compiler_params=_params("arbitrary"),
        name="embed",
    )(x, meta)


def _pad_halves(t):
    pad = lambda a: jnp.pad(a, ((0, 0), (0, D_FF_PAD - D_FF)))
    return jnp.concatenate([pad(t[:, :D_FF]), pad(t[:, D_FF:])], axis=1)


def _trunk(x, meta_tokens, hyb_w_in, hyb_conv_w, hyb_conv_b, ssd_dt_bias, ssd_a_log, ssd_d, ssd_norm_w,
           gla_gate_w2, gla_gate_b, gla_norm_w, hyb_w_out, swa_w_qkv, swa_sinks, swa_w_out,
           ffn_w_up, ffn_conv_w, ffn_conv_b, ffn_w_down, ln_mix_g, ln_mix_b, ln_ffn_g, ln_ffn_b):
    seq = x.shape[0]
    rows = FRONT_PAD + N_META + seq
    h, hb = _embed(x, meta_tokens.astype(F32))
    tables = _rope_tables(rows)
    hyb_w_in = jnp.pad(hyb_w_in, ((0, 0), (0, 0), (0, -hyb_w_in.shape[2] % LANES))).astype(BF16)
    for layer in range(DEPTH):
        j = layer // 2
        if layer % 2 == 0:
            u_zx = _project(hb, hyb_w_in, j, first_block=0, block_stride=8, tn=1024, n_tiles=6)
            u_qkv = _project(hb, hyb_w_in, j, first_block=HYB_QKV_COL // LANES, block_stride=8, tn=1024, n_tiles=4,
                             shift=HYB_QKV_COL % LANES)
            u_r = _project(hb, hyb_w_in, j, first_block=HYB_R_COL // LANES, block_stride=8, tn=1024, n_tiles=2,
                           shift=HYB_R_COL % LANES)
            u_small = _project(hb, hyb_w_in, j, first_block=0, block_stride=0, tn=2 * LANES, n_tiles=1,
                               blocks=[HYB_DT_COL // LANES, HYB_GLR_COL // LANES], out_dtype=F32)
            y_ssd = _ssd_mixer(u_zx, u_small, hyb_conv_w[j], hyb_conv_b[j], ssd_dt_bias[j], ssd_a_log[j],
                               ssd_d[j], ssd_norm_w[j])
            y_gla = _gla_mixer(u_qkv, u_r, u_small, gla_gate_w2[j], gla_gate_b[j], gla_norm_w[j])
            h, hb = _matmul_residual_ln([y_ssd, y_gla], [y_ssd.shape[1], y_gla.shape[1]], hyb_w_out, j, h,
                                        ln_mix_g[layer], ln_mix_b[layer], nchunks=8)
        else:
            q = _project(hb, swa_w_qkv, j, first_block=0, block_stride=8, tn=1024, n_tiles=2,
                         scale=SWA_HEAD_DIM ** -0.5, mode="rope", tables=tables)
            kv = _project(hb, swa_w_qkv, j, first_block=SWA_Q_HEADS * SWA_HEAD_DIM // LANES, block_stride=4, tn=512,
                          n_tiles=2, mode="kv", tables=tables)
            attn = _swa_attention(q, kv, swa_sinks[j])
            h, hb = _matmul_residual_ln([attn], [attn.shape[1]], swa_w_out, j, h, ln_mix_g[layer], ln_mix_b[layer],
                                        nchunks=4)
        act = _ffn_up(hb, ffn_w_up, layer, _pad_halves(ffn_conv_w[layer]),
                      _pad_halves(ffn_conv_b[layer].reshape(1, -1)))
        h, hb = _matmul_residual_ln([act], [D_FF], ffn_w_down, layer, h, ln_ffn_g[layer], ln_ffn_b[layer], nchunks=8)
    return h[FRONT_PAD + N_META:]


def kernel(x, meta_tokens, hyb_w_in, hyb_conv_w, hyb_conv_b, ssd_dt_bias, ssd_a_log, ssd_d, ssd_norm_w,
           gla_gate_w2, gla_gate_b, gla_norm_w, hyb_w_out, swa_w_qkv, swa_sinks, swa_w_out,
           ffn_w_up, ffn_conv_w, ffn_conv_b, ffn_w_down, ln_mix_g, ln_mix_b, ln_ffn_g, ln_ffn_b):
    params = (meta_tokens, hyb_w_in, hyb_conv_w, hyb_conv_b, ssd_dt_bias, ssd_a_log, ssd_d, ssd_norm_w,
              gla_gate_w2, gla_gate_b, gla_norm_w, hyb_w_out, swa_w_qkv, swa_sinks, swa_w_out,
              ffn_w_up, ffn_conv_w, ffn_conv_b, ffn_w_down, ln_mix_g, ln_mix_b, ln_ffn_g, ln_ffn_b)
    return jnp.stack([_trunk(x[b], *params) for b in range(x.shape[0])], axis=0)
```
